```python
import jax, jax.numpy as jnp
from jax import lax
import numpy as np

D_MODEL = 1024
BATCH = 8
SEQ = 8192
DEPTH = 4

HEAD_DIM = 64
A_Q_HEADS = 8
A_KV_HEADS = 2
B_GROUP_CFG = ((128, 1), (512, 4), (2048, 16))
B_HEADS_PER_GROUP = 4
B_GROUPS = len(B_GROUP_CFG)
B_HEADS = B_GROUPS * B_HEADS_PER_GROUP
POOL_WINDOWS = (2, 4, 8, 16)
POOL_GROUPS = len(POOL_WINDOWS)
POOL_GROUP_WIDTH = 128
POOL_WIDTH = POOL_GROUPS * POOL_GROUP_WIDTH
N_BRANCHES = 3
D_FF = ((-(-8 * D_MODEL // 3) + 255) // 256) * 256
GRID_W = 64
ROPE_THETA = 10000.0
BLOCK = 128
EPS = 1e-6
NEG_INF = -1e30

A_Q_WIDTH = A_Q_HEADS * HEAD_DIM
A_KV_WIDTH = A_KV_HEADS * HEAD_DIM
B_WIDTH = B_HEADS * HEAD_DIM
B_OUT_WIDTH = B_HEADS_PER_GROUP * HEAD_DIM
GATE_WIDTH = N_BRANCHES * D_MODEL
IN_WIDTH = A_Q_WIDTH + 2 * A_KV_WIDTH + 3 * B_WIDTH + POOL_WIDTH + GATE_WIDTH
SPLIT_POINTS = (
    A_Q_WIDTH,
    A_Q_WIDTH + A_KV_WIDTH,
    A_Q_WIDTH + 2 * A_KV_WIDTH,
    A_Q_WIDTH + 2 * A_KV_WIDTH + B_WIDTH,
    A_Q_WIDTH + 2 * A_KV_WIDTH + 2 * B_WIDTH,
    A_Q_WIDTH + 2 * A_KV_WIDTH + 3 * B_WIDTH,
    A_Q_WIDTH + 2 * A_KV_WIDTH + 3 * B_WIDTH + POOL_WIDTH,
)
RESID_SCALE = (2 * DEPTH) ** -0.5

kernel_name = 'hybrid_gated_axial_dilated_pool_encoder'


def rms_norm(x, g):
    xf = x.astype(jnp.float32)
    y = xf * lax.rsqrt(jnp.mean(xf * xf, axis=-1, keepdims=True) + EPS)
    return (y * g.astype(jnp.float32)).astype(x.dtype)


def rope_angles(pos, dim):
    inv = ROPE_THETA ** (-jnp.arange(0, dim, 2, dtype=jnp.float32) / dim)
    return pos.astype(jnp.float32)[:, None] * inv[None, :]


def apply_rope(x, ang):
    xf = x.astype(jnp.float32)
    x1, x2 = jnp.split(xf, 2, axis=-1)
    c = jnp.cos(ang)[None, :, None, :]
    s = jnp.sin(ang)[None, :, None, :]
    return jnp.concatenate([x1 * c - x2 * s, x2 * c + x1 * s], axis=-1).astype(x.dtype)


def apply_axial_rope(x, ang_row, ang_col):
    xr, xc = jnp.split(x, 2, axis=-1)
    return jnp.concatenate([apply_rope(xr, ang_row), apply_rope(xc, ang_col)], axis=-1)


def dense_gqa_blocked(q, k, v):
    b, s, hq, dh = q.shape
    hkv = k.shape[2]
    g = hq // hkv
    nb = s // BLOCK
    scale = dh ** -0.5
    qb = q.reshape(b, nb, BLOCK, hkv, g, dh).transpose(1, 0, 2, 3, 4, 5)

    def one_block(q_blk):
        sc = jnp.einsum('bqkgd,bskd->bkgqs', q_blk, k, preferred_element_type=jnp.float32) * scale
        p = jax.nn.softmax(sc, axis=-1)
        return jnp.einsum('bkgqs,bskd->bqkgd', p.astype(v.dtype), v)

    o = lax.map(one_block, qb)
    return o.transpose(1, 0, 2, 3, 4, 5).reshape(b, s, hq * dh)


def dilated_window_attention(q, k, v, dilation, half_span):
    b, s, h, dh = q.shape
    L = s // dilation
    nb = -(-L // BLOCK)
    lp = nb * BLOCK
    bd = b * dilation

    def to_sub(t):
        t = t.reshape(b, L, dilation, h, dh).transpose(0, 2, 1, 3, 4).reshape(bd, L, h, dh)
        return jnp.pad(t, ((0, 0), (0, lp - L), (0, 0), (0, 0)))

    def band(t):
        tp = jnp.pad(t, ((0, 0), (BLOCK, BLOCK), (0, 0), (0, 0)))
        return jnp.concatenate(
            [tp[:, i * BLOCK:i * BLOCK + lp].reshape(bd, nb, BLOCK, h, dh) for i in range(3)], axis=2)

    qb = to_sub(q).reshape(bd, nb, BLOCK, h, dh)
    kb = band(to_sub(k))
    vb = band(to_sub(v))
    sc = jnp.einsum('znqhd,znkhd->znhqk', qb, kb, preferred_element_type=jnp.float32) * (dh ** -0.5)
    blk = jnp.arange(nb)[:, None, None] * BLOCK
    qpos = blk + jnp.arange(BLOCK)[None, :, None]
    kpos = blk - BLOCK + jnp.arange(3 * BLOCK)[None, None, :]
    valid = (jnp.abs(qpos - kpos) <= half_span) & (kpos >= 0) & (kpos < L)
    sc = jnp.where(valid[None, :, None], sc, NEG_INF)
    m = jnp.max(sc, axis=-1, keepdims=True)
    p = jnp.exp(sc - m)
    den = jnp.sum(p, axis=-1, keepdims=True)
    o = jnp.einsum('znhqk,znkhd->znqhd', (p / den).astype(v.dtype), vb)
    lse = (m + jnp.log(den))[..., 0]
    o = o.reshape(b, dilation, lp, h, dh)[:, :, :L].transpose(0, 2, 1, 3, 4).reshape(b, s, h, dh)
    lse = lse.transpose(0, 1, 3, 2).reshape(b, dilation, lp, h)[:, :, :L]
    lse = lse.transpose(0, 2, 1, 3).reshape(b, s, h)
    return o, lse


def multiscale_pool(u, lin, scale):
    b, s, _ = u.shape
    uf = u.astype(jnp.float32).reshape(b, s, POOL_GROUPS, POOL_GROUP_WIDTH)
    cs = jnp.pad(lax.cumsum(uf, axis=1), ((0, 0), (1, 0), (0, 0), (0, 0)))
    t = jnp.arange(s)
    outs = []
    for gi, w in enumerate(POOL_WINDOWS):
        lo = jnp.clip(t - w // 2, 0, s)
        hi = jnp.clip(t + w - w // 2, 0, s)
        mean = (cs[:, hi, gi] - cs[:, lo, gi]) / (hi - lo).astype(jnp.float32)[None, :, None]
        outs.append(mean - uf[:, :, gi])
    pooled = jnp.stack(outs, axis=2).astype(u.dtype)
    mixed = jnp.einsum('bsgc,gcd->bsgd', pooled, lin)
    return mixed.reshape(b, s, POOL_WIDTH) * scale


def _fwd_setup_inputs(seed: int = 0) -> dict:
    key = jax.random.key(seed)
    ks = jax.random.split(key, 19)
    f32 = jnp.float32

    def nrm(k, shape, fan_in):
        return jax.random.normal(k, shape, f32) * (fan_in ** -0.5)

    def gain(k, shape):
        return 1.0 + 0.02 * jax.random.normal(k, shape, f32)

    return {
        'x': jax.random.normal(ks[0], (BATCH, SEQ, D_MODEL), f32),
        'norm_mix': gain(ks[1], (DEPTH, D_MODEL)),
        'w_in': nrm(ks[2], (DEPTH, D_MODEL, IN_WIDTH), D_MODEL),
        'b_gate': 0.02 * jax.random.normal(ks[3], (DEPTH, GATE_WIDTH), f32),
        'qn_a': gain(ks[4], (DEPTH, HEAD_DIM)),
        'kn_a': gain(ks[5], (DEPTH, HEAD_DIM)),
        'qn_b': gain(ks[6], (DEPTH, HEAD_DIM)),
        'kn_b': gain(ks[7], (DEPTH, HEAD_DIM)),
        'pool_lin': nrm(ks[8], (DEPTH, POOL_GROUPS, POOL_GROUP_WIDTH, POOL_GROUP_WIDTH), POOL_GROUP_WIDTH),
        'pool_scale': gain(ks[9], (DEPTH, POOL_WIDTH)),
        'w_branch_a': nrm(ks[10], (DEPTH, A_Q_WIDTH, D_MODEL), A_Q_WIDTH),
        'w_branch_b': nrm(ks[11], (DEPTH, B_OUT_WIDTH, D_MODEL), B_OUT_WIDTH),
        'w_branch_c': nrm(ks[12], (DEPTH, POOL_WIDTH, D_MODEL), POOL_WIDTH),
        'w_out': nrm(ks[13], (DEPTH, D_MODEL, D_MODEL), D_MODEL) * RESID_SCALE,
        'norm_ffn': gain(ks[14], (DEPTH, D_MODEL)),
        'w_ffn_gate': nrm(ks[15], (DEPTH, D_MODEL, D_FF), D_MODEL),
        'w_ffn_up': nrm(ks[16], (DEPTH, D_MODEL, D_FF), D_MODEL),
        'w_ffn_down': nrm(ks[17], (DEPTH, D_FF, D_MODEL), D_FF) * RESID_SCALE,
    }


def _fwd_reference(x, norm_mix, w_in, b_gate, qn_a, kn_a, qn_b, kn_b, pool_lin, pool_scale,
              w_branch_a, w_branch_b, w_branch_c, w_out, norm_ffn, w_ffn_gate, w_ffn_up, w_ffn_down):
    b, s, _ = x.shape
    rows = s // GRID_W
    row_idx = jnp.repeat(jnp.arange(rows), GRID_W)
    col_idx = jnp.tile(jnp.arange(GRID_W), rows)
    ang_row = rope_angles(row_idx, HEAD_DIM // 2)
    ang_col = rope_angles(col_idx, HEAD_DIM // 2)
    ang_seq = rope_angles(jnp.arange(s), HEAD_DIM)

    for l in range(DEPTH):
        h = rms_norm(x, norm_mix[l])
        z = h @ w_in[l]
        qa, ka, va, qb, kb, vb, uc, gz = jnp.split(z, SPLIT_POINTS, axis=-1)

        qa = apply_axial_rope(rms_norm(qa.reshape(b, s, A_Q_HEADS, HEAD_DIM), qn_a[l]), ang_row, ang_col)
        ka = apply_axial_rope(rms_norm(ka.reshape(b, s, A_KV_HEADS, HEAD_DIM), kn_a[l]), ang_row, ang_col)
        va = va.reshape(b, s, A_KV_HEADS, HEAD_DIM)
        ya = dense_gqa_blocked(qa, ka, va) @ w_branch_a[l]

        qb = apply_rope(rms_norm(qb.reshape(b, s, B_HEADS, HEAD_DIM), qn_b[l]), ang_seq)
        kb = apply_rope(rms_norm(kb.reshape(b, s, B_HEADS, HEAD_DIM), kn_b[l]), ang_seq)
        qb = qb.reshape(b, s, B_GROUPS, B_HEADS_PER_GROUP, HEAD_DIM)
        kb = kb.reshape(b, s, B_GROUPS, B_HEADS_PER_GROUP, HEAD_DIM)
        vb = vb.reshape(b, s, B_GROUPS, B_HEADS_PER_GROUP, HEAD_DIM)
        o_list, lse_list = [], []
        for gi, (window, dil) in enumerate(B_GROUP_CFG):
            o_g, lse_g = dilated_window_attention(qb[:, :, gi], kb[:, :, gi], vb[:, :, gi], dil, window // (2 * dil))
            o_list.append(o_g)
            lse_list.append(lse_g)
        wts = jax.nn.softmax(jnp.stack(lse_list, axis=0), axis=0)
        ob = jnp.einsum('gbsh,gbshd->bshd', wts.astype(x.dtype), jnp.stack(o_list, axis=0))
        yb = ob.reshape(b, s, B_OUT_WIDTH) @ w_branch_b[l]

        yc = multiscale_pool(uc, pool_lin[l], pool_scale[l]) @ w_branch_c[l]

        gates = jax.nn.sigmoid(gz + b_gate[l]).reshape(b, s, N_BRANCHES, D_MODEL)
        merged = gates[:, :, 0] * ya + gates[:, :, 1] * yb + gates[:, :, 2] * yc
        x = x + merged @ w_out[l]

        h2 = rms_norm(x, norm_ffn[l])
        x = x + (jax.nn.silu(h2 @ w_ffn_gate[l]) * (h2 @ w_ffn_up[l])) @ w_ffn_down[l]
    return x


import jax as _jax
import jax.numpy as _jnp

TWIN_FORMAT = 'train_step'
FWD_PARAMS = ['x', 'norm_mix', 'w_in', 'b_gate', 'qn_a', 'kn_a', 'qn_b', 'kn_b', 'pool_lin', 'pool_scale', 'w_branch_a', 'w_branch_b', 'w_branch_c', 'w_out', 'norm_ffn', 'w_ffn_gate', 'w_ffn_up', 'w_ffn_down']
TWIN_WEIGHTS = ['norm_mix', 'w_in', 'b_gate', 'qn_a', 'kn_a', 'qn_b', 'kn_b', 'pool_lin', 'pool_scale', 'w_branch_a', 'w_branch_b', 'w_branch_c', 'w_out', 'norm_ffn', 'w_ffn_gate', 'w_ffn_up', 'w_ffn_down']
TWIN_DIFF_INPUT = 'x'
TWIN_INPUTS = ['x', 'norm_mix', 'w_in', 'b_gate', 'qn_a', 'kn_a', 'qn_b', 'kn_b', 'pool_lin', 'pool_scale', 'w_branch_a', 'w_branch_b', 'w_branch_c', 'w_out', 'norm_ffn', 'w_ffn_gate', 'w_ffn_up', 'w_ffn_down', 'loss_target', 'm_norm_mix', 'm_w_in', 'm_b_gate', 'm_qn_a', 'm_kn_a', 'm_qn_b', 'm_kn_b', 'm_pool_lin', 'm_pool_scale', 'm_w_branch_a', 'm_w_branch_b', 'm_w_branch_c', 'm_w_out', 'm_norm_ffn', 'm_w_ffn_gate', 'm_w_ffn_up', 'm_w_ffn_down', 'v_norm_mix', 'v_w_in', 'v_b_gate', 'v_qn_a', 'v_kn_a', 'v_qn_b', 'v_kn_b', 'v_pool_lin', 'v_pool_scale', 'v_w_branch_a', 'v_w_branch_b', 'v_w_branch_c', 'v_w_out', 'v_norm_ffn', 'v_w_ffn_gate', 'v_w_ffn_up', 'v_w_ffn_down']
TWIN_OUTPUTS = ['loss', 'grad_x', 'grad_norm_mix', 'grad_w_in', 'grad_b_gate', 'grad_qn_a', 'grad_kn_a', 'grad_qn_b', 'grad_kn_b', 'grad_pool_lin', 'grad_pool_scale', 'grad_w_branch_a', 'grad_w_branch_b', 'grad_w_branch_c', 'grad_w_out', 'grad_norm_ffn', 'grad_w_ffn_gate', 'grad_w_ffn_up', 'grad_w_ffn_down', 'delta_norm_mix', 'delta_w_in', 'delta_b_gate', 'delta_qn_a', 'delta_kn_a', 'delta_qn_b', 'delta_kn_b', 'delta_pool_lin', 'delta_pool_scale', 'delta_w_branch_a', 'delta_w_branch_b', 'delta_w_branch_c', 'delta_w_out', 'delta_norm_ffn', 'delta_w_ffn_gate', 'delta_w_ffn_up', 'delta_w_ffn_down', 'new_m_norm_mix', 'new_m_w_in', 'new_m_b_gate', 'new_m_qn_a', 'new_m_kn_a', 'new_m_qn_b', 'new_m_kn_b', 'new_m_pool_lin', 'new_m_pool_scale', 'new_m_w_branch_a', 'new_m_w_branch_b', 'new_m_w_branch_c', 'new_m_w_out', 'new_m_norm_ffn', 'new_m_w_ffn_gate', 'new_m_w_ffn_up', 'new_m_w_ffn_down', 'new_v_norm_mix', 'new_v_w_in', 'new_v_b_gate', 'new_v_qn_a', 'new_v_kn_a', 'new_v_qn_b', 'new_v_kn_b', 'new_v_pool_lin', 'new_v_pool_scale', 'new_v_w_branch_a', 'new_v_w_branch_b', 'new_v_w_branch_c', 'new_v_w_out', 'new_v_norm_ffn', 'new_v_w_ffn_gate', 'new_v_w_ffn_up', 'new_v_w_ffn_down']
TWIN_LEAF_KINDS = {'loss': 'loss', 'grad_x': 'grad_x', 'grad_norm_mix': 'grad_w', 'grad_w_in': 'grad_w', 'grad_b_gate': 'grad_w', 'grad_qn_a': 'grad_w', 'grad_kn_a': 'grad_w', 'grad_qn_b': 'grad_w', 'grad_kn_b': 'grad_w', 'grad_pool_lin': 'grad_w', 'grad_pool_scale': 'grad_w', 'grad_w_branch_a': 'grad_w', 'grad_w_branch_b': 'grad_w', 'grad_w_branch_c': 'grad_w', 'grad_w_out': 'grad_w', 'grad_norm_ffn': 'grad_w', 'grad_w_ffn_gate': 'grad_w', 'grad_w_ffn_up': 'grad_w', 'grad_w_ffn_down': 'grad_w', 'delta_norm_mix': 'delta_w', 'delta_w_in': 'delta_w', 'delta_b_gate': 'delta_w', 'delta_qn_a': 'delta_w', 'delta_kn_a': 'delta_w', 'delta_qn_b': 'delta_w', 'delta_kn_b': 'delta_w', 'delta_pool_lin': 'delta_w', 'delta_pool_scale': 'delta_w', 'delta_w_branch_a': 'delta_w', 'delta_w_branch_b': 'delta_w', 'delta_w_branch_c': 'delta_w', 'delta_w_out': 'delta_w', 'delta_norm_ffn': 'delta_w', 'delta_w_ffn_gate': 'delta_w', 'delta_w_ffn_up': 'delta_w', 'delta_w_ffn_down': 'delta_w', 'new_m_norm_mix': 'new_m', 'new_m_w_in': 'new_m', 'new_m_b_gate': 'new_m', 'new_m_qn_a': 'new_m', 'new_m_kn_a': 'new_m', 'new_m_qn_b': 'new_m', 'new_m_kn_b': 'new_m', 'new_m_pool_lin': 'new_m', 'new_m_pool_scale': 'new_m', 'new_m_w_branch_a': 'new_m', 'new_m_w_branch_b': 'new_m', 'new_m_w_branch_c': 'new_m', 'new_m_w_out': 'new_m', 'new_m_norm_ffn': 'new_m', 'new_m_w_ffn_gate': 'new_m', 'new_m_w_ffn_up': 'new_m', 'new_m_w_ffn_down': 'new_m', 'new_v_norm_mix': 'new_v', 'new_v_w_in': 'new_v', 'new_v_b_gate': 'new_v', 'new_v_qn_a': 'new_v', 'new_v_kn_a': 'new_v', 'new_v_qn_b': 'new_v', 'new_v_kn_b': 'new_v', 'new_v_pool_lin': 'new_v', 'new_v_pool_scale': 'new_v', 'new_v_w_branch_a': 'new_v', 'new_v_w_branch_b': 'new_v', 'new_v_w_branch_c': 'new_v', 'new_v_w_out': 'new_v', 'new_v_norm_ffn': 'new_v', 'new_v_w_ffn_gate': 'new_v', 'new_v_w_ffn_up': 'new_v', 'new_v_w_ffn_down': 'new_v'}


def _forward(args):
    return _fwd_reference(*[args[k] for k in FWD_PARAMS])


def _output_shape():
    def fwd():
        inp = _fwd_setup_inputs(0)
        return _fwd_reference(*[inp[k] for k in FWD_PARAMS])
    out = _jax.eval_shape(fwd)
    return out.shape, out.dtype

N_MICROBATCH = 1
ADAM_LR = 0.001
ADAM_B1 = 0.9
ADAM_B2 = 0.999
ADAM_EPS = 1e-08
ADAM_WD = 0.01
ADAM_STEP = 10
PER_EXAMPLE_BATCH_AXIS = {'x': 0, 'loss_target': 0}
SHARED_INPUTS = []
_WEIGHT_DTYPES = {'norm_mix': _jnp.float32, 'w_in': _jnp.float32, 'b_gate': _jnp.float32, 'qn_a': _jnp.float32, 'kn_a': _jnp.float32, 'qn_b': _jnp.float32, 'kn_b': _jnp.float32, 'pool_lin': _jnp.float32, 'pool_scale': _jnp.float32, 'w_branch_a': _jnp.float32, 'w_branch_b': _jnp.float32, 'w_branch_c': _jnp.float32, 'w_out': _jnp.float32, 'norm_ffn': _jnp.float32, 'w_ffn_gate': _jnp.float32, 'w_ffn_up': _jnp.float32, 'w_ffn_down': _jnp.float32}
MOMENT_SCALE = {'norm_mix': 1.982353e+00, 'w_in': 9.918093e-02, 'b_gate': 3.692004e-01, 'qn_a': 6.740044e-02, 'kn_a': 6.659404e-02, 'qn_b': 1.486365e-01, 'kn_b': 1.495281e-01, 'pool_lin': 4.727277e-01, 'pool_scale': 3.747833e+00, 'w_branch_a': 9.280854e-03, 'w_branch_b': 9.912136e-03, 'w_branch_c': 2.760664e-01, 'w_out': 7.278820e-01, 'norm_ffn': 6.183173e+00, 'w_ffn_gate': 5.968604e-02, 'w_ffn_up': 8.196926e-02, 'w_ffn_down': 3.819798e-01}


def _to_microbatches(a, axis):
    t = _jnp.moveaxis(a, axis, 0)
    t = t.reshape((N_MICROBATCH, t.shape[0] // N_MICROBATCH) + t.shape[1:])
    return _jnp.moveaxis(t, 1, axis + 1)


def setup_inputs(seed: int = 0) -> dict:
    inp = _fwd_setup_inputs(seed)
    key = _jax.random.fold_in(_jax.random.key(seed), 7919)
    shape, _ = _output_shape()
    out = dict(inp)
    out["loss_target"] = _jax.random.normal(_jax.random.fold_in(key, 0), shape, _jnp.float32)
    for i, name in enumerate(TWIN_WEIGHTS):
        w = inp[name].astype(_jnp.float32)
        if MOMENT_SCALE is None:
            s = _jnp.sqrt(_jnp.mean(_jnp.square(w)) + 1e-30)
        else:
            s = MOMENT_SCALE[name]
        km, kv = _jax.random.split(_jax.random.fold_in(key, i + 1))
        out[name] = w
        out["m_" + name] = s * _jax.random.normal(km, w.shape, _jnp.float32)
        out["v_" + name] = (s * s) * _jax.random.uniform(kv, w.shape, _jnp.float32, 0.5, 1.5)
    if N_MICROBATCH > 1:
        for name, axis in PER_EXAMPLE_BATCH_AXIS.items():
            out[name] = _to_microbatches(out[name], axis)
    return {'x': out['x'], 'norm_mix': out['norm_mix'], 'w_in': out['w_in'], 'b_gate': out['b_gate'], 'qn_a': out['qn_a'], 'kn_a': out['kn_a'], 'qn_b': out['qn_b'], 'kn_b': out['kn_b'], 'pool_lin': out['pool_lin'], 'pool_scale': out['pool_scale'], 'w_branch_a': out['w_branch_a'], 'w_branch_b': out['w_branch_b'], 'w_branch_c': out['w_branch_c'], 'w_out': out['w_out'], 'norm_ffn': out['norm_ffn'], 'w_ffn_gate': out['w_ffn_gate'], 'w_ffn_up': out['w_ffn_up'], 'w_ffn_down': out['w_ffn_down'], 'loss_target': out['loss_target'], 'm_norm_mix': out['m_norm_mix'], 'm_w_in': out['m_w_in'], 'm_b_gate': out['m_b_gate'], 'm_qn_a': out['m_qn_a'], 'm_kn_a': out['m_kn_a'], 'm_qn_b': out['m_qn_b'], 'm_kn_b': out['m_kn_b'], 'm_pool_lin': out['m_pool_lin'], 'm_pool_scale': out['m_pool_scale'], 'm_w_branch_a': out['m_w_branch_a'], 'm_w_branch_b': out['m_w_branch_b'], 'm_w_branch_c': out['m_w_branch_c'], 'm_w_out': out['m_w_out'], 'm_norm_ffn': out['m_norm_ffn'], 'm_w_ffn_gate': out['m_w_ffn_gate'], 'm_w_ffn_up': out['m_w_ffn_up'], 'm_w_ffn_down': out['m_w_ffn_down'], 'v_norm_mix': out['v_norm_mix'], 'v_w_in': out['v_w_in'], 'v_b_gate': out['v_b_gate'], 'v_qn_a': out['v_qn_a'], 'v_kn_a': out['v_kn_a'], 'v_qn_b': out['v_qn_b'], 'v_kn_b': out['v_kn_b'], 'v_pool_lin': out['v_pool_lin'], 'v_pool_scale': out['v_pool_scale'], 'v_w_branch_a': out['v_w_branch_a'], 'v_w_branch_b': out['v_w_branch_b'], 'v_w_branch_c': out['v_w_branch_c'], 'v_w_out': out['v_w_out'], 'v_norm_ffn': out['v_norm_ffn'], 'v_w_ffn_gate': out['v_w_ffn_gate'], 'v_w_ffn_up': out['v_w_ffn_up'], 'v_w_ffn_down': out['v_w_ffn_down']}


def _loss(weights, diff, rest, loss_target):
    with _jax.named_scope("forward"):
        args = {**rest, TWIN_DIFF_INPUT: diff, **{k: w.astype(_WEIGHT_DTYPES[k]) for k, w in weights.items()}}
        y = _forward(args)
    with _jax.named_scope("loss_head"):
        err = _jnp.square(y.astype(_jnp.float32) - loss_target)
        return 0.5 * _jnp.sum(_jnp.mean(err, axis=-1)) if err.ndim else 0.5 * err


def _adamw(w, g, m, v):
    m = ADAM_B1 * m + (1.0 - ADAM_B1) * g
    v = ADAM_B2 * v + (1.0 - ADAM_B2) * _jnp.square(g)
    m_hat = m / (1.0 - ADAM_B1 ** ADAM_STEP)
    v_hat = v / (1.0 - ADAM_B2 ** ADAM_STEP)
    delta = -ADAM_LR * (m_hat / (_jnp.sqrt(v_hat) + ADAM_EPS) + ADAM_WD * w)
    return delta, m, v


def reference(x, norm_mix, w_in, b_gate, qn_a, kn_a, qn_b, kn_b, pool_lin, pool_scale, w_branch_a, w_branch_b, w_branch_c, w_out, norm_ffn, w_ffn_gate, w_ffn_up, w_ffn_down, loss_target, m_norm_mix, m_w_in, m_b_gate, m_qn_a, m_kn_a, m_qn_b, m_kn_b, m_pool_lin, m_pool_scale, m_w_branch_a, m_w_branch_b, m_w_branch_c, m_w_out, m_norm_ffn, m_w_ffn_gate, m_w_ffn_up, m_w_ffn_down, v_norm_mix, v_w_in, v_b_gate, v_qn_a, v_kn_a, v_qn_b, v_kn_b, v_pool_lin, v_pool_scale, v_w_branch_a, v_w_branch_b, v_w_branch_c, v_w_out, v_norm_ffn, v_w_ffn_gate, v_w_ffn_up, v_w_ffn_down):
    given = dict(x=x, norm_mix=norm_mix, w_in=w_in, b_gate=b_gate, qn_a=qn_a, kn_a=kn_a, qn_b=qn_b, kn_b=kn_b, pool_lin=pool_lin, pool_scale=pool_scale, w_branch_a=w_branch_a, w_branch_b=w_branch_b, w_branch_c=w_branch_c, w_out=w_out, norm_ffn=norm_ffn, w_ffn_gate=w_ffn_gate, w_ffn_up=w_ffn_up, w_ffn_down=w_ffn_down, loss_target=loss_target, m_norm_mix=m_norm_mix, m_w_in=m_w_in, m_b_gate=m_b_gate, m_qn_a=m_qn_a, m_kn_a=m_kn_a, m_qn_b=m_qn_b, m_kn_b=m_kn_b, m_pool_lin=m_pool_lin, m_pool_scale=m_pool_scale, m_w_branch_a=m_w_branch_a, m_w_branch_b=m_w_branch_b, m_w_branch_c=m_w_branch_c, m_w_out=m_w_out, m_norm_ffn=m_norm_ffn, m_w_ffn_gate=m_w_ffn_gate, m_w_ffn_up=m_w_ffn_up, m_w_ffn_down=m_w_ffn_down, v_norm_mix=v_norm_mix, v_w_in=v_w_in, v_b_gate=v_b_gate, v_qn_a=v_qn_a, v_kn_a=v_kn_a, v_qn_b=v_qn_b, v_kn_b=v_kn_b, v_pool_lin=v_pool_lin, v_pool_scale=v_pool_scale, v_w_branch_a=v_w_branch_a, v_w_branch_b=v_w_branch_b, v_w_branch_c=v_w_branch_c, v_w_out=v_w_out, v_norm_ffn=v_norm_ffn, v_w_ffn_gate=v_w_ffn_gate, v_w_ffn_up=v_w_ffn_up, v_w_ffn_down=v_w_ffn_down)
    weights = {n: given[n] for n in TWIN_WEIGHTS}
    shared = {n: given[n] for n in SHARED_INPUTS}
    per_example = {n: given[n] for n in ['x']}
    grad_fn = _jax.value_and_grad(_loss, argnums=(0, 1))

    def one_microbatch(ex, loss_target):
        ex = dict(ex)
        diff = ex.pop(TWIN_DIFF_INPUT)
        return grad_fn(weights, diff, {**shared, **ex}, loss_target)

    if N_MICROBATCH == 1:
        loss, (grad_w, grad_x) = one_microbatch(per_example, given["loss_target"])
    else:
        def body(carry, xs):
            loss_sum, grad_sum = carry
            l_k, (gw_k, gx_k) = one_microbatch(xs[0], xs[1])
            with _jax.named_scope("update"):
                return (loss_sum + l_k, _jax.tree.map(_jnp.add, grad_sum, gw_k)), gx_k

        init = (_jnp.zeros((), _jnp.float32), _jax.tree.map(_jnp.zeros_like, weights))
        (loss, grad_w), grad_x = _jax.lax.scan(body, init, (per_example, given["loss_target"]))
    with _jax.named_scope("update"):
        delta_w, new_m, new_v = {}, {}, {}
        for n in TWIN_WEIGHTS:
            delta_w[n], new_m[n], new_v[n] = _adamw(weights[n], grad_w[n], given["m_" + n], given["v_" + n])
    return (loss, grad_x, *[grad_w[n] for n in TWIN_WEIGHTS], *[delta_w[n] for n in TWIN_WEIGHTS],
            *[new_m[n] for n in TWIN_WEIGHTS], *[new_v[n] for n in TWIN_WEIGHTS])
```

```python
import functools

import jax
import jax.numpy as jnp
from jax import lax
from jax.experimental import pallas as pl
from jax.experimental.pallas import tpu as pltpu

f32, bf16 = jnp.float32, jnp.bfloat16
SDS = jax.ShapeDtypeStruct

DEPTH = 4
D_MODEL = 1024
HEAD_DIM = 64
LANES = 128
GRID_W = 64
ROPE_THETA = 10000.0
EPS = 1e-6
NEG_INF = -1e30
SCALE = HEAD_DIM ** -0.5
B_GROUP_CFG = ((128, 1), (512, 4), (2048, 16))
POOL_WINDOWS = (2, 4, 8, 16)
N_CHIPS = 4
N_DEV = 8
C_QA, C_KA, C_VA, C_QB, C_KB, C_VB, C_UC, C_GZ = 0, 512, 640, 768, 1536, 2304, 3072, 3584
IN_WIDTH = 6656
FF_SHARD = 704
ADAM_LR, ADAM_B1, ADAM_B2, ADAM_EPS, ADAM_WD, ADAM_STEP = 0.001, 0.9, 0.999, 1e-08, 0.01, 10
VMEM_LIMIT = 56 * 1024 * 1024
MESH = pl.DeviceIdType.MESH

BIG = ("w_in", "w_branch_a", "w_branch_b", "w_branch_c", "w_out", "w_ffn_gate", "w_ffn_up", "w_ffn_down")
SMALL = ("norm_mix", "b_gate", "qn_a", "kn_a", "qn_b", "kn_b", "pool_lin", "pool_scale", "norm_ffn")
SMALL_SIZES = (1024, 3072, 64, 64, 64, 64, 65536, 512, 1024)
SMALL_ROWS = sum(SMALL_SIZES) // LANES
WEIGHT_ORDER = ("norm_mix", "w_in", "b_gate", "qn_a", "kn_a", "qn_b", "kn_b", "pool_lin", "pool_scale",
                "w_branch_a", "w_branch_b", "w_branch_c", "w_out", "norm_ffn", "w_ffn_gate", "w_ffn_up", "w_ffn_down")


def _cparams():
    return pltpu.CompilerParams(vmem_limit_bytes=VMEM_LIMIT)


def _lo_mask(shape):
    return lax.broadcasted_iota(jnp.int32, shape, len(shape) - 1) < HEAD_DIM


def _dot(a, b, dims):
    dn = {"nn": (((1,), (0,)), ((), ())), "nt": (((1,), (1,)), ((), ())), "tn": (((0,), (0,)), ((), ()))}[dims]
    return lax.dot_general(a, b, dn, preferred_element_type=f32)


def _mm(name, ins, in_specs, out_shape, out_spec, grid, nk, dims, acc_shape, epilogue=None, n_extra=0, aliases=None):
    multi = isinstance(out_shape, (list, tuple))
    n_out = len(out_shape) if multi else 1

    def body(*refs):
        a_ref, b_ref = refs[0], refs[1]
        extra = refs[2:2 + n_extra]
        outs = refs[2 + n_extra:2 + n_extra + n_out]
        k = pl.program_id(len(grid) - 1)
        part = _dot(a_ref[...].astype(bf16), b_ref[...].astype(bf16), dims)

        def finish(acc):
            res = epilogue(acc, *[e[...] for e in extra]) if epilogue is not None else acc
            res = res if isinstance(res, (list, tuple)) else (res,)
            for o, r in zip(outs, res):
                o[...] = r.astype(o.dtype)

        if nk == 1:
            finish(part)
        else:
            acc_ref = refs[-1]

            @pl.when(k == 0)
            def _():
                acc_ref[...] = part

            @pl.when(k > 0)
            def _():
                acc_ref[...] += part

            @pl.when(k == nk - 1)
            def _():
                finish(acc_ref[...])

    return pl.pallas_call(
        body, name=name, grid=grid, in_specs=in_specs,
        out_specs=list(out_spec) if multi else out_spec,
        out_shape=list(out_shape) if multi else out_shape,
        scratch_shapes=[] if nk == 1 else [pltpu.VMEM(acc_shape, f32)],
        input_output_aliases=aliases or {}, compiler_params=_cparams())(*ins)


def _ew(name, fn, grid, ins, in_specs, out_shapes, out_specs, n_acc=0, aliases=None):
    n_in = len(ins)
    n_out = len(out_shapes) - n_acc
    n_alias = len(aliases or {})

    def body(*refs):
        in_refs = refs[:n_in - n_alias] if n_alias else refs[:n_in]
        out_refs = refs[n_in:n_in + n_out]
        acc_refs = refs[n_in + n_out:]
        i = pl.program_id(len(grid) - 1)
        res = fn(*[r[...] for r in in_refs])
        res = res if isinstance(res, (list, tuple)) else (res,)
        for o, r in zip(out_refs, res[:n_out]):
            o[...] = r.astype(o.dtype)
        for a, r in zip(acc_refs, res[n_out:]):
            @pl.when(i == 0)
            def _(a=a, r=r):
                a[...] = r.astype(a.dtype)

            @pl.when(i > 0)
            def _(a=a, r=r):
                a[...] += r.astype(a.dtype)

    return pl.pallas_call(body, name=name, grid=grid, in_specs=in_specs, out_specs=out_specs, out_shape=out_shapes,
                          input_output_aliases=aliases or {}, compiler_params=_cparams())(*ins)


EW_BLOCK_ELEMS = 128 * 1024


def _row_tile(rows, cols):
    return max(t for t in range(8, rows + 1, 8) if rows % t == 0 and (t * cols <= EW_BLOCK_ELEMS or t == 8))


def _rope_tables(s):
    t = jnp.arange(s)
    inv_ax = ROPE_THETA ** (-jnp.arange(0, HEAD_DIM // 2, 2, dtype=f32) / (HEAD_DIM // 2))
    inv_sq = ROPE_THETA ** (-jnp.arange(0, HEAD_DIM, 2, dtype=f32) / HEAD_DIM)
    ang_row = (t // GRID_W).astype(f32)[:, None] * inv_ax[None, :]
    ang_col = (t % GRID_W).astype(f32)[:, None] * inv_ax[None, :]
    ang_seq = t.astype(f32)[:, None] * inv_sq[None, :]
    a_ax = jnp.concatenate([ang_row, ang_row, ang_col, ang_col], axis=1)
    sg_ax = jnp.concatenate([-jnp.ones(16), jnp.ones(16), -jnp.ones(16), jnp.ones(16)]).astype(f32)
    a_sq = jnp.concatenate([ang_seq, ang_seq], axis=1)
    sg_sq = jnp.concatenate([-jnp.ones(32), jnp.ones(32)]).astype(f32)
    two = lambda a: jnp.concatenate([a, a], axis=1)
    return (two(jnp.cos(a_ax)), two(jnp.sin(a_ax) * sg_ax), two(jnp.cos(a_sq)), two(jnp.sin(a_sq) * sg_sq))


def _head_stat(v, lo):
    s0 = jnp.sum(jnp.where(lo, v, 0.0), axis=1, keepdims=True)
    s1 = jnp.sum(jnp.where(lo, 0.0, v), axis=1, keepdims=True)
    return jnp.where(lo, s0, s1)


def _partner(y, off):
    lane = lax.broadcasted_iota(jnp.int32, y.shape, 1)
    first = (lane & (2 * off - 1)) < off
    return jnp.where(first, pltpu.roll(y, LANES - off, 1), pltpu.roll(y, off, 1))


def _normrope(xc, g, cos, sin, off):
    lo = _lo_mask(xc.shape)
    r = lax.rsqrt(_head_stat(xc * xc, lo) * (1.0 / HEAD_DIM) + EPS)
    y = xc * r * g
    return y * cos + _partner(y, off) * sin


def _normrope_bwd(xc, g, cos, sin, off, drot):
    lo = _lo_mask(xc.shape)
    r = lax.rsqrt(_head_stat(xc * xc, lo) * (1.0 / HEAD_DIM) + EPS)
    n = xc * r
    dy = drot * cos + _partner(drot * sin, off)
    dg = jnp.sum(dy * n, axis=0, keepdims=True)
    dn = dy * g
    dx = r * (dn - n * (_head_stat(dn * n, lo) * (1.0 / HEAD_DIM)))
    return dx, dg


def _norm_fwd(x, g, s):
    ts = 256

    def fn(xt, gt):
        r = lax.rsqrt(jnp.mean(xt * xt, axis=1, keepdims=True) + EPS)
        return xt * r * gt

    return _ew("norm_fwd", fn, (s // ts,), [x, g],
               [pl.BlockSpec((ts, D_MODEL), lambda i: (i, 0)), pl.BlockSpec((1, D_MODEL), lambda i: (0, 0))],
               [SDS((s, D_MODEL), bf16)], [pl.BlockSpec((ts, D_MODEL), lambda i: (i, 0))])[0]


def _norm_bwd(x, g, dh, dres, s):
    ts = 256

    def fn(xt, gt, dht, drt):
        r = lax.rsqrt(jnp.mean(xt * xt, axis=1, keepdims=True) + EPS)
        n = xt * r
        dn = dht * gt
        dx = drt + r * (dn - n * jnp.mean(dn * n, axis=1, keepdims=True))
        return dx, jnp.sum(dht * n, axis=0, keepdims=True)

    row = pl.BlockSpec((ts, D_MODEL), lambda i: (i, 0))
    one = pl.BlockSpec((1, D_MODEL), lambda i: (0, 0))
    return _ew("norm_bwd", fn, (s // ts,), [x, g, dh, dres], [row, one, row, row],
               [SDS((s, D_MODEL), f32), SDS((1, D_MODEL), f32)], [row, one], n_acc=1)


def _prep_fwd(z, tabs, gains, s):
    ts = 256
    cos_a, sin_a, cos_b, sin_b = tabs

    def body(za_ref, zq_ref, zk_ref, zv_ref, ca, sa, cb, sb, g_ref, qa_o, kd_o, vd_o, qb_o, kb_o, vb_o):
        lo = _lo_mask((ts, LANES))
        g = g_ref[...]
        ca_, sa_, cb_, sb_ = ca[...], sa[...], cb[...], sb[...]
        for c in range(4):
            qa_o[:, c * LANES:(c + 1) * LANES] = _normrope(za_ref[:, c * LANES:(c + 1) * LANES], g[0:1], ca_, sa_, 16).astype(bf16)
        k = _normrope(za_ref[:, C_KA:C_KA + LANES], g[1:2], ca_, sa_, 16)
        kr = pltpu.roll(k, HEAD_DIM, 1)
        kd_o[0] = jnp.where(lo, k, kr).astype(bf16)
        kd_o[1] = jnp.where(lo, kr, k).astype(bf16)
        v = za_ref[:, C_VA:C_VA + LANES]
        vr = pltpu.roll(v, HEAD_DIM, 1)
        vd_o[0] = jnp.where(lo, v, vr).astype(bf16)
        vd_o[1] = jnp.where(lo, vr, v).astype(bf16)
        for c in range(6):
            sl = slice(c * LANES, (c + 1) * LANES)
            qb_o[:, sl] = _normrope(zq_ref[:, sl], g[2:3], cb_, sb_, 32).astype(bf16)
            kb_o[:, sl] = _normrope(zk_ref[:, sl], g[3:4], cb_, sb_, 32).astype(bf16)
        vb_o[...] = zv_ref[...].astype(bf16)

    w = 768
    zspec = lambda cb: pl.BlockSpec((ts, w), lambda i: (i, cb))
    tab = pl.BlockSpec((ts, LANES), lambda i: (i, 0))
    dup = pl.BlockSpec((2, ts, LANES), lambda i: (0, i, 0))
    return pl.pallas_call(
        body, name="prep_fwd", grid=(s // ts,),
        in_specs=[zspec(0), zspec(1), zspec(2), zspec(3), tab, tab, tab, tab, pl.BlockSpec((4, LANES), lambda i: (0, 0))],
        out_specs=[pl.BlockSpec((ts, 512), lambda i: (i, 0)), dup, dup, zspec(0), zspec(0), zspec(0)],
        out_shape=[SDS((s, 512), bf16), SDS((2, s, LANES), bf16), SDS((2, s, LANES), bf16),
                   SDS((s, w), bf16), SDS((s, w), bf16), SDS((s, w), bf16)],
        compiler_params=_cparams())(z, z, z, z, cos_a, sin_a, cos_b, sin_b, gains)


def _prep_bwd(z, tabs, gains, dqa, dkd, dvd, dqb, dkb, dvb, dz, s):
    ts = 256
    cos_a, sin_a, cos_b, sin_b = tabs

    def body(za_ref, zq_ref, zk_ref, ca, sa, cb, sb, g_ref, dqa_r, dkd_r, dvd_r,
             dq0, dq1, dq2, dk0, dk1, dk2, dv0, dv1, dv2, dz_in, dz_o, dg_o):
        i = pl.program_id(0)
        lo = _lo_mask((ts, LANES))
        g = g_ref[...]
        ca_, sa_, cb_, sb_ = ca[...], sa[...], cb[...], sb[...]
        dg = [jnp.zeros((1, LANES), f32) for _ in range(4)]
        for c in range(4):
            sl = slice(c * LANES, (c + 1) * LANES)
            dx, d = _normrope_bwd(za_ref[:, sl], g[0:1], ca_, sa_, 16, dqa_r[:, sl])
            dz_o[:, sl] = dx.astype(bf16)
            dg[0] += d
        dk = jnp.where(lo, dkd_r[0], dkd_r[1])
        dx, d = _normrope_bwd(za_ref[:, C_KA:C_KA + LANES], g[1:2], ca_, sa_, 16, dk)
        dz_o[:, C_KA:C_KA + LANES] = dx.astype(bf16)
        dg[1] += d
        dz_o[:, C_VA:C_VA + LANES] = jnp.where(lo, dvd_r[0], dvd_r[1]).astype(bf16)
        dqs, dks, dvs = (dq0, dq1, dq2), (dk0, dk1, dk2), (dv0, dv1, dv2)
        for c in range(6):
            sl = slice(c * LANES, (c + 1) * LANES)
            gsl = slice((c % 2) * LANES, (c % 2 + 1) * LANES)
            dx, d = _normrope_bwd(zq_ref[:, sl], g[2:3], cb_, sb_, 32, dqs[c // 2][:, gsl])
            dz_o[:, C_QB + c * LANES:C_QB + (c + 1) * LANES] = dx.astype(bf16)
            dg[2] += d
            dx, d = _normrope_bwd(zk_ref[:, sl], g[3:4], cb_, sb_, 32, dks[c // 2][:, gsl])
            dz_o[:, C_KB + c * LANES:C_KB + (c + 1) * LANES] = dx.astype(bf16)
            dg[3] += d
            dz_o[:, C_VB + c * LANES:C_VB + (c + 1) * LANES] = dvs[c // 2][:, gsl].astype(bf16)
        dgs = jnp.concatenate(dg + [jnp.zeros((4, LANES), f32)], axis=0)

        @pl.when(i == 0)
        def _():
            dg_o[...] = dgs

        @pl.when(i > 0)
        def _():
            dg_o[...] += dgs

    w = 768
    zspec = lambda cb: pl.BlockSpec((ts, w), lambda i: (i, cb))
    tab = pl.BlockSpec((ts, LANES), lambda i: (i, 0))
    dup = pl.BlockSpec((2, ts, LANES), lambda i: (0, i, 0))
    grp = pl.BlockSpec((ts, 256), lambda i: (i, 0))
    dzo, dgo = pl.pallas_call(
        body, name="prep_bwd", grid=(s // ts,),
        in_specs=[zspec(0), zspec(1), zspec(2), tab, tab, tab, tab, pl.BlockSpec((4, LANES), lambda i: (0, 0)),
                  pl.BlockSpec((ts, 512), lambda i: (i, 0)), dup, dup] + [grp] * 9 + [pl.BlockSpec(memory_space=pl.ANY)],
        out_specs=[pl.BlockSpec((ts, C_UC), lambda i: (i, 0)), pl.BlockSpec((8, LANES), lambda i: (0, 0))],
        out_shape=[SDS((s, IN_WIDTH), bf16), SDS((8, LANES), f32)],
        input_output_aliases={20: 0}, compiler_params=_cparams())(
            z, z, z, cos_a, sin_a, cos_b, sin_b, gains, dqa, dkd, dvd, *dqb, *dkb, *dvb, dz)
    return dzo, dgo


def _stack_heads(x, lo, dtype):
    z = jnp.zeros_like(x)
    return jnp.concatenate([jnp.where(lo, x, z), jnp.where(lo, z, x)], axis=0).astype(dtype)


def _rows_of(v, lo, kind):
    if kind == "max":
        a = jnp.max(jnp.where(lo, v, NEG_INF * 10), axis=1, keepdims=True)
        b = jnp.max(jnp.where(lo, NEG_INF * 10, v), axis=1, keepdims=True)
    else:
        a = jnp.sum(jnp.where(lo, v, 0.0), axis=1, keepdims=True) * (1.0 / HEAD_DIM)
        b = jnp.sum(jnp.where(lo, 0.0, v), axis=1, keepdims=True) * (1.0 / HEAD_DIM)
    return jnp.concatenate([a, b], axis=0)


def _flash_a_fwd(qa, kd, vd, s):
    tq, tk = 256, 512
    nj = s // tk

    def body(q_ref, k_ref, v_ref, o_ref, lse_ref, qs, m_s, l_s, acc):
        j = pl.program_id(2)
        lo = _lo_mask((tq, LANES))

        @pl.when(j == 0)
        def _():
            qs[...] = _stack_heads(q_ref[...], lo, bf16)
            m_s[...] = jnp.full(m_s.shape, NEG_INF, f32)
            l_s[...] = jnp.zeros(l_s.shape, f32)
            acc[...] = jnp.zeros(acc.shape, f32)

        sc = _dot(qs[...], k_ref[...], "nt") * SCALE
        m_prev = m_s[...]
        m_new = jnp.maximum(m_prev, jnp.max(sc, axis=1, keepdims=True))
        alpha = jnp.exp(m_prev - m_new)
        p = jnp.exp(sc - m_new)
        l_s[...] = alpha * l_s[...] + jnp.sum(p, axis=1, keepdims=True)
        acc[...] = acc[...] * alpha + _dot(p.astype(bf16), v_ref[...], "nn")
        m_s[...] = m_new

        @pl.when(j == nj - 1)
        def _():
            l = l_s[...]
            o = acc[...] / l
            o_ref[...] = jnp.where(lo, o[:tq], o[tq:]).astype(bf16)
            lse = m_s[...] + jnp.log(l)
            lse_ref[...] = jnp.where(lo, lse[:tq], lse[tq:])

    kv = pl.BlockSpec((None, tk, LANES), lambda p, i, j: (p // 2, j, 0))
    qo = pl.BlockSpec((tq, LANES), lambda p, i, j: (i, p))
    return pl.pallas_call(
        body, name="flash_a_fwd", grid=(4, s // tq, nj), in_specs=[qo, kv, kv], out_specs=[qo, qo],
        out_shape=[SDS((s, 512), bf16), SDS((s, 512), f32)],
        scratch_shapes=[pltpu.VMEM((2 * tq, LANES), bf16), pltpu.VMEM((2 * tq, 1), f32), pltpu.VMEM((2 * tq, 1), f32),
                        pltpu.VMEM((2 * tq, LANES), f32)],
        compiler_params=_cparams())(qa, kd, vd)


def _flash_a_bwd(qa, kd, vd, oa, lse, doa, s):
    tq, tk = 256, 512
    ni = s // tq

    def body(q_ref, do_ref, o_ref, lse_ref, k_ref, v_ref, dq_ref, dk_ref, dv_ref, dk_acc, dv_acc):
        j, i = pl.program_id(1), pl.program_id(2)
        lo = _lo_mask((tq, LANES))
        rows = pl.ds(pl.multiple_of(i * tq, tq), tq)

        @pl.when(i == 0)
        def _():
            dk_acc[...] = jnp.zeros(dk_acc.shape, f32)
            dv_acc[...] = jnp.zeros(dv_acc.shape, f32)

        @pl.when(j == 0)
        def _():
            dq_ref[rows, :] = jnp.zeros((tq, 2 * LANES), f32)

        k, v = k_ref[...], v_ref[...]
        for pp in range(2):
            sl = slice(pp * LANES, (pp + 1) * LANES)
            do = do_ref[:, sl]
            qs = _stack_heads(q_ref[:, sl], lo, bf16)
            dos = _stack_heads(do, lo, bf16)
            delta = _rows_of(do * o_ref[:, sl].astype(f32), lo, "sum") * HEAD_DIM
            lses = _rows_of(lse_ref[:, sl], lo, "max")
            p = jnp.exp(_dot(qs, k, "nt") * SCALE - lses)
            dp = _dot(dos, v, "nt")
            ds = (p * (dp - delta) * SCALE).astype(bf16)
            dv_acc[...] += _dot(p.astype(bf16), dos, "tn")
            dk_acc[...] += _dot(ds, qs, "tn")
            dqs = _dot(ds, k, "nn")
            dq_ref[rows, sl] += jnp.where(lo, dqs[:tq], dqs[tq:])

        @pl.when(i == ni - 1)
        def _():
            a, b = dk_acc[...], dv_acc[...]
            dk_ref[...] = a + pltpu.roll(a, HEAD_DIM, 1)
            dv_ref[...] = b + pltpu.roll(b, HEAD_DIM, 1)

    grp = lambda: pl.BlockSpec((tq, 2 * LANES), lambda c, j, i: (i, c))
    kv = lambda: pl.BlockSpec((None, tk, LANES), lambda c, j, i: (c, j, 0))
    return pl.pallas_call(
        body, name="flash_a_bwd", grid=(2, s // tk, ni), in_specs=[grp(), grp(), grp(), grp(), kv(), kv()],
        out_specs=[pl.BlockSpec((s, 2 * LANES), lambda c, j, i: (0, c)), kv(), kv()],
        out_shape=[SDS((s, 512), f32), SDS((2, s, LANES), f32), SDS((2, s, LANES), f32)],
        scratch_shapes=[pltpu.VMEM((tk, LANES), f32), pltpu.VMEM((tk, LANES), f32)],
        compiler_params=_cparams())(qa, doa, oa, lse, kd, vd)


BQ = 128


def _band_valid(qpos, kpos, l_sub):
    return (jnp.abs(qpos - kpos) <= 64) & (kpos >= 0) & (kpos < l_sub) & (qpos >= 0) & (qpos < l_sub)


def _attn_b_fwd(qb, kb, vb, gi, s):
    d = B_GROUP_CFG[gi][1]
    l_sub = s // d
    nb = l_sub // BQ

    def body(q_ref, kp, kc, kn, vp, vc, vn, o_ref, lse_ref):
        n = pl.program_id(1)
        lo = _lo_mask((BQ, LANES))
        qs = _stack_heads(q_ref[...], lo, bf16)
        k = jnp.concatenate([kp[...], kc[...], kn[...]], axis=0)
        v = jnp.concatenate([vp[...], vc[...], vn[...]], axis=0)
        sc = _dot(qs, k, "nt") * SCALE
        qpos = n * BQ + (lax.broadcasted_iota(jnp.int32, sc.shape, 0) & (BQ - 1))
        kpos = (n - 1) * BQ + lax.broadcasted_iota(jnp.int32, sc.shape, 1)
        sc = jnp.where(_band_valid(qpos, kpos, l_sub), sc, NEG_INF)
        m = jnp.max(sc, axis=1, keepdims=True)
        p = jnp.exp(sc - m)
        den = jnp.sum(p, axis=1, keepdims=True)
        o = _dot(p.astype(bf16), v, "nn") / den
        o_ref[...] = jnp.where(lo, o[:BQ], o[BQ:])
        lse = m + jnp.log(den)
        lse_ref[...] = jnp.where(lo, lse[:BQ], lse[BQ:])

    col_in = lambda r, pp: r * 6 + gi * 2 + pp
    q_spec = pl.BlockSpec((BQ, LANES), lambda r, n, pp: (n, col_in(r, pp)))
    prev = pl.BlockSpec((BQ, LANES), lambda r, n, pp: (jnp.maximum(n - 1, 0), col_in(r, pp)))
    nxt = pl.BlockSpec((BQ, LANES), lambda r, n, pp: (jnp.minimum(n + 1, nb - 1), col_in(r, pp)))
    out = pl.BlockSpec((BQ, LANES), lambda r, n, pp: (n, r * 2 + pp))
    qv, kv, vv = (a.reshape(l_sub, d * 768) for a in (qb, kb, vb))
    o, lse = pl.pallas_call(
        body, name=f"attn_b_fwd{gi}", grid=(d, nb, 2),
        in_specs=[q_spec, prev, q_spec, nxt, prev, q_spec, nxt], out_specs=[out, out],
        out_shape=[SDS((l_sub, d * 256), f32), SDS((l_sub, d * 256), f32)],
        compiler_params=_cparams())(qv, kv, kv, kv, vv, vv, vv)
    return o.reshape(s, 256), lse.reshape(s, 256)


def _attn_b_dq(qb, kb, vb, dob, lse, delta, gi, s):
    d = B_GROUP_CFG[gi][1]
    l_sub = s // d
    nb = l_sub // BQ

    def body(q_ref, kp, kc, kn, vp, vc, vn, do_ref, lse_ref, dl_ref, dq_ref):
        n = pl.program_id(1)
        lo = _lo_mask((BQ, LANES))
        qs = _stack_heads(q_ref[...], lo, bf16)
        dos = _stack_heads(do_ref[...], lo, bf16)
        k = jnp.concatenate([kp[...], kc[...], kn[...]], axis=0)
        v = jnp.concatenate([vp[...], vc[...], vn[...]], axis=0)
        sc = _dot(qs, k, "nt") * SCALE
        qpos = n * BQ + (lax.broadcasted_iota(jnp.int32, sc.shape, 0) & (BQ - 1))
        kpos = (n - 1) * BQ + lax.broadcasted_iota(jnp.int32, sc.shape, 1)
        p = jnp.where(_band_valid(qpos, kpos, l_sub), jnp.exp(sc - _rows_of(lse_ref[...], lo, "max")), 0.0)
        dp = _dot(dos, v, "nt")
        ds = (p * (dp - _rows_of(dl_ref[...], lo, "sum")) * SCALE).astype(bf16)
        dqs = _dot(ds, k, "nn")
        dq_ref[...] = jnp.where(lo, dqs[:BQ], dqs[BQ:])

    col_in = lambda r, pp: r * 6 + gi * 2 + pp
    q_spec = pl.BlockSpec((BQ, LANES), lambda r, n, pp: (n, col_in(r, pp)))
    prev = pl.BlockSpec((BQ, LANES), lambda r, n, pp: (jnp.maximum(n - 1, 0), col_in(r, pp)))
    nxt = pl.BlockSpec((BQ, LANES), lambda r, n, pp: (jnp.minimum(n + 1, nb - 1), col_in(r, pp)))
    out = pl.BlockSpec((BQ, LANES), lambda r, n, pp: (n, r * 2 + pp))
    qv, kv, vv = (a.reshape(l_sub, d * 768) for a in (qb, kb, vb))
    g3 = lambda a: a.reshape(l_sub, d * 256)
    dq = pl.pallas_call(
        body, name=f"attn_b_dq{gi}", grid=(d, nb, 2),
        in_specs=[q_spec, prev, q_spec, nxt, prev, q_spec, nxt, out, out, out], out_specs=out,
        out_shape=SDS((l_sub, d * 256), f32),
        compiler_params=_cparams())(qv, kv, kv, kv, vv, vv, vv, g3(dob), g3(lse), g3(delta))
    return dq.reshape(s, 256)


def _attn_b_dkv(qb, kb, vb, dob, lse, delta, gi, s):
    d = B_GROUP_CFG[gi][1]
    l_sub = s // d
    nb = l_sub // BQ

    def body(k_ref, v_ref, qp, qc, qn, dop, doc, don, lp, lc, ln, dp_, dc_, dn_, dk_ref, dv_ref):
        m = pl.program_id(1)
        lo = _lo_mask((3 * BQ, LANES))
        cat = lambda a, b, c: jnp.concatenate([a[...], b[...], c[...]], axis=0)
        qs = _stack_heads(cat(qp, qc, qn), lo, bf16)
        dos = _stack_heads(cat(dop, doc, don), lo, bf16)
        lses = _rows_of(cat(lp, lc, ln), lo, "max")
        dls = _rows_of(cat(dp_, dc_, dn_), lo, "sum")
        k, v = k_ref[...], v_ref[...]
        sc = _dot(qs, k, "nt") * SCALE
        row = lax.broadcasted_iota(jnp.int32, sc.shape, 0)
        qpos = (m - 1) * BQ + jnp.where(row >= 3 * BQ, row - 3 * BQ, row)
        kpos = m * BQ + lax.broadcasted_iota(jnp.int32, sc.shape, 1)
        p = jnp.where(_band_valid(qpos, kpos, l_sub), jnp.exp(sc - lses), 0.0)
        dpv = _dot(dos, v, "nt")
        ds = (p * (dpv - dls) * SCALE).astype(bf16)
        dv_ref[...] = _dot(p.astype(bf16), dos, "tn")
        dk_ref[...] = _dot(ds, qs, "tn")

    col_in = lambda r, pp: r * 6 + gi * 2 + pp
    cur_in = pl.BlockSpec((BQ, LANES), lambda r, n, pp: (n, col_in(r, pp)))
    prev_in = pl.BlockSpec((BQ, LANES), lambda r, n, pp: (jnp.maximum(n - 1, 0), col_in(r, pp)))
    next_in = pl.BlockSpec((BQ, LANES), lambda r, n, pp: (jnp.minimum(n + 1, nb - 1), col_in(r, pp)))
    cur = pl.BlockSpec((BQ, LANES), lambda r, n, pp: (n, r * 2 + pp))
    prev = pl.BlockSpec((BQ, LANES), lambda r, n, pp: (jnp.maximum(n - 1, 0), r * 2 + pp))
    nxt = pl.BlockSpec((BQ, LANES), lambda r, n, pp: (jnp.minimum(n + 1, nb - 1), r * 2 + pp))
    qv, kv, vv = (a.reshape(l_sub, d * 768) for a in (qb, kb, vb))
    g3 = lambda a: a.reshape(l_sub, d * 256)
    dk, dv = pl.pallas_call(
        body, name=f"attn_b_dkv{gi}", grid=(d, nb, 2),
        in_specs=[cur_in, cur_in, prev_in, cur_in, next_in] + [prev, cur, nxt] * 3, out_specs=[cur, cur],
        out_shape=[SDS((l_sub, d * 256), f32), SDS((l_sub, d * 256), f32)],
        compiler_params=_cparams())(kv, vv, qv, qv, qv, *([g3(dob)] * 3), *([g3(lse)] * 3), *([g3(delta)] * 3))
    return dk.reshape(s, 256), dv.reshape(s, 256)


def _merge_b(o_list, lse_list, s):
    ts = 256

    def fn(o0, o1, o2, l0, l1, l2):
        m = jnp.maximum(jnp.maximum(l0, l1), l2)
        w0, w1, w2 = jnp.exp(l0 - m), jnp.exp(l1 - m), jnp.exp(l2 - m)
        den = w0 + w1 + w2
        return (w0 * o0 + w1 * o1 + w2 * o2) / den, m + jnp.log(den)

    row = pl.BlockSpec((ts, 256), lambda i: (i, 0))
    return _ew("merge_b", fn, (s // ts,), [*o_list, *lse_list], [row] * 6,
               [SDS((s, 256), bf16), SDS((s, 256), f32)], [row, row])


def _delta_b(dob, ob, s):
    ts = 256

    def fn(do, o):
        prod = do * o.astype(f32)
        lo = _lo_mask((ts, LANES))
        return jnp.concatenate([_head_stat(prod[:, :LANES], lo), _head_stat(prod[:, LANES:], lo)], axis=1)

    row = pl.BlockSpec((ts, 256), lambda i: (i, 0))
    return _ew("delta_b", fn, (s // ts,), [dob, ob], [row, row], [SDS((s, 256), f32)], [row])[0]


def _band(t0, u0, w, nt, nu, s, transposed):
    t = t0 + lax.broadcasted_iota(jnp.int32, (nt, nu), 0)
    u = u0 + lax.broadcasted_iota(jnp.int32, (nt, nu), 1)
    if transposed:
        lo, hi = jnp.clip(t - w // 2 + 1, 0, s), jnp.clip(t + w // 2 + 1, 0, s)
    else:
        lo, hi = jnp.clip(t - w // 2, 0, s), jnp.clip(t + w - w // 2, 0, s)
    return ((u >= lo) & (u < hi)).astype(f32)


def _pool_cnt(t0, w, nt, s):
    t = t0 + lax.broadcasted_iota(jnp.int32, (nt, 1), 0)
    return (jnp.clip(t + w - w // 2, 0, s) - jnp.clip(t - w // 2, 0, s)).astype(f32)


def _pool_fwd(z, lin, scale, s):
    tp = 256
    nt = s // tp

    def body(up, uc, un, lin_ref, sc_ref, pooled_o, pc_o):
        i = pl.program_id(0)
        ext = jnp.concatenate([up[...], uc[...], un[...]], axis=0)
        for g, w in enumerate(POOL_WINDOWS):
            sl = slice(g * LANES, (g + 1) * LANES)
            band = _band(i * tp, (i - 1) * tp, w, tp, 3 * tp, s, False)
            sm = jnp.dot(band, ext[:, sl], preferred_element_type=f32, precision=lax.Precision.HIGHEST)
            pooled = (sm / _pool_cnt(i * tp, w, tp, s) - uc[:, sl]).astype(bf16)
            pooled_o[:, sl] = pooled
            mixed = _dot(pooled, lin_ref[g].astype(bf16), "nn")
            pc_o[:, sl] = (mixed * sc_ref[:, sl]).astype(bf16)

    cb = C_UC // 512
    blk = lambda f: pl.BlockSpec((tp, 512), lambda i: (f(i), cb))
    row = pl.BlockSpec((tp, 512), lambda i: (i, 0))
    return pl.pallas_call(
        body, name="pool_fwd", grid=(nt,),
        in_specs=[blk(lambda i: jnp.maximum(i - 1, 0)), blk(lambda i: i), blk(lambda i: jnp.minimum(i + 1, nt - 1)),
                  pl.BlockSpec((4, LANES, LANES), lambda i: (0, 0, 0)), pl.BlockSpec((1, 512), lambda i: (0, 0))],
        out_specs=[row, row], out_shape=[SDS((s, 512), bf16), SDS((s, 512), bf16)],
        compiler_params=_cparams())(z, z, z, lin, scale)


def _pool_bwd1(dpc, pooled, lin, scale, s):
    tp = 256

    def body(dpc_ref, pooled_ref, lin_ref, sc_ref, dpn_o, dlin_o, dsc_o):
        i = pl.program_id(0)
        dsc = []
        for g, w in enumerate(POOL_WINDOWS):
            sl = slice(g * LANES, (g + 1) * LANES)
            pooled = pooled_ref[:, sl]
            linb = lin_ref[g].astype(bf16)
            mixed = _dot(pooled, linb, "nn")
            dpc_g = dpc_ref[:, sl]
            dsc.append(jnp.sum(dpc_g * mixed, axis=0, keepdims=True))
            dmixed = (dpc_g * sc_ref[:, sl]).astype(bf16)
            dpn_o[:, sl] = _dot(dmixed, linb, "nt") / _pool_cnt(i * tp, w, tp, s)
            dl = _dot(pooled, dmixed, "tn")

            @pl.when(i == 0)
            def _(g=g, dl=dl):
                dlin_o[g] = dl

            @pl.when(i > 0)
            def _(g=g, dl=dl):
                dlin_o[g] += dl

        dsc = jnp.concatenate(dsc, axis=1)

        @pl.when(i == 0)
        def _():
            dsc_o[...] = dsc

        @pl.when(i > 0)
        def _():
            dsc_o[...] += dsc

    row = pl.BlockSpec((tp, 512), lambda i: (i, 0))
    linspec = pl.BlockSpec((4, LANES, LANES), lambda i: (0, 0, 0))
    one = pl.BlockSpec((1, 512), lambda i: (0, 0))
    return pl.pallas_call(
        body, name="pool_bwd1", grid=(s // tp,), in_specs=[row, row, linspec, one], out_specs=[row, linspec, one],
        out_shape=[SDS((s, 512), f32), SDS((4, LANES, LANES), f32), SDS((1, 512), f32)],
        compiler_params=_cparams())(dpc, pooled, lin, scale)


def _pool_bwd2(dpn, dz, s):
    tp = 256
    nt = s // tp

    def body(dp, dc, dn, dz_in, dz_o):
        i = pl.program_id(0)
        ext = jnp.concatenate([dp[...], dc[...], dn[...]], axis=0)
        for g, w in enumerate(POOL_WINDOWS):
            sl = slice(g * LANES, (g + 1) * LANES)
            band = _band(i * tp, (i - 1) * tp, w, tp, 3 * tp, s, True)
            sm = jnp.dot(band, ext[:, sl], preferred_element_type=f32, precision=lax.Precision.HIGHEST)
            dz_o[:, sl] = (sm - dc[:, sl] * _pool_cnt(i * tp, w, tp, s)).astype(bf16)

    blk = lambda f: pl.BlockSpec((tp, 512), lambda i: (f(i), 0))
    return pl.pallas_call(
        body, name="pool_bwd2", grid=(nt,),
        in_specs=[blk(lambda i: jnp.maximum(i - 1, 0)), blk(lambda i: i), blk(lambda i: jnp.minimum(i + 1, nt - 1)),
                  pl.BlockSpec(memory_space=pl.ANY)],
        out_specs=pl.BlockSpec((tp, 512), lambda i: (i, C_UC // 512)), out_shape=SDS((s, IN_WIDTH), bf16),
        input_output_aliases={3: 0}, compiler_params=_cparams())(dpn, dpn, dpn, dz)


GW = 512


def _gate_fwd(y3, z, b_gate, s):
    ts = 256

    def fn(y0, y1, y2, g0, g1, g2, b0, b1, b2):
        return jax.nn.sigmoid(g0 + b0) * y0 + jax.nn.sigmoid(g1 + b1) * y1 + jax.nn.sigmoid(g2 + b2) * y2

    ysp = lambda b: pl.BlockSpec((None, ts, GW), lambda h, i: (b, i, h))
    gsp = lambda b: pl.BlockSpec((ts, GW), lambda h, i: (i, C_GZ // GW + 2 * b + h))
    bsp = lambda b: pl.BlockSpec((1, GW), lambda h, i: (0, 2 * b + h))
    return _ew("gate_fwd", fn, (2, s // ts), [y3, y3, y3, z, z, z, b_gate, b_gate, b_gate],
               [ysp(0), ysp(1), ysp(2), gsp(0), gsp(1), gsp(2), bsp(0), bsp(1), bsp(2)],
               [SDS((s, D_MODEL), bf16)], [pl.BlockSpec((ts, GW), lambda h, i: (i, h))])[0]


def _gate_bwd(dmerged, y3, z, b_gate, dz, s):
    ts = 256

    def fn(dm, y, gz, b):
        g = jax.nn.sigmoid(gz + b)
        dgz = dm * y * g * (1.0 - g)
        return dm * g, dgz, jnp.sum(dgz, axis=0, keepdims=True)

    return _ew("gate_bwd", fn, (3, 2, s // ts), [dmerged, y3, z, b_gate, dz],
               [pl.BlockSpec((ts, GW), lambda b, h, i: (i, h)), pl.BlockSpec((None, ts, GW), lambda b, h, i: (b, i, h)),
                pl.BlockSpec((ts, GW), lambda b, h, i: (i, C_GZ // GW + 2 * b + h)),
                pl.BlockSpec((1, GW), lambda b, h, i: (0, 2 * b + h)), pl.BlockSpec(memory_space=pl.ANY)],
               [SDS((3, s, D_MODEL), bf16), SDS((s, IN_WIDTH), bf16), SDS((1, 3 * D_MODEL), f32)],
               [pl.BlockSpec((None, ts, GW), lambda b, h, i: (b, i, h)),
                pl.BlockSpec((ts, GW), lambda b, h, i: (i, C_GZ // GW + 2 * b + h)),
                pl.BlockSpec((1, GW), lambda b, h, i: (0, 2 * b + h))], n_acc=1, aliases={4: 1})


def _loss_grad(y, tgt, s):
    ts = 256

    def fn(yt, tt):
        e = yt - tt
        return e * (1.0 / D_MODEL), jnp.sum(e * e, axis=0, keepdims=True) * (0.5 / D_MODEL)

    row = pl.BlockSpec((ts, D_MODEL), lambda i: (i, 0))
    one = pl.BlockSpec((1, D_MODEL), lambda i: (0, 0))
    return _ew("loss_grad", fn, (s // ts,), [y, tgt], [row, row], [SDS((s, D_MODEL), f32), SDS((1, D_MODEL), f32)],
               [row, one], n_acc=1)


TM = 512


def _layer_fwd(x, w, sm, tabs, s):
    nm = s // TM
    hb = _norm_fwd(x, sm["norm_mix"], s)
    z = _mm("mm_z", [hb, w["w_in"]],
            [pl.BlockSpec((TM, D_MODEL), lambda j, i, k: (i, 0)), pl.BlockSpec((None, D_MODEL, 1664), lambda j, i, k: (j, 0, 0))],
            SDS((s, IN_WIDTH), f32), pl.BlockSpec((TM, 1664), lambda j, i, k: (i, j)), (4, nm, 1), 1, "nn", None)
    qa, kd, vd, qb, kb, vb = _prep_fwd(z, tabs, sm["gains"], s)
    oa, lse_a = _flash_a_fwd(qa, kd, vd, s)
    ob_parts = [_attn_b_fwd(qb, kb, vb, gi, s) for gi in range(3)]
    ob, lse_b = _merge_b([p[0] for p in ob_parts], [p[1] for p in ob_parts], s)
    pooled, pc = _pool_fwd(z, sm["pool_lin"], sm["pool_scale"], s)

    y3 = _mm("mm_ya", [oa, w["w_branch_a"]],
             [pl.BlockSpec((TM, 512), lambda j, i, k: (i, 0)), pl.BlockSpec((None, 512, 256), lambda j, i, k: (j, 0, 0))],
             SDS((3, s, D_MODEL), f32), pl.BlockSpec((None, TM, 256), lambda j, i, k: (0, i, j)), (4, nm, 1), 1, "nn", None)
    y3 = _mm_alias("mm_yb", ob, w["w_branch_b"], y3, 256, 1, s)
    y3 = _mm_alias("mm_yc", pc, w["w_branch_c"], y3, 512, 2, s)
    merged = _gate_fwd(y3, z, sm["b_gate"], s)
    x1 = _mm("mm_out", [merged, w["w_out"], x],
             [pl.BlockSpec((TM, 256), lambda i, k: (i, k)), pl.BlockSpec((None, 256, D_MODEL), lambda i, k: (k, 0, 0)),
              pl.BlockSpec((TM, D_MODEL), lambda i, k: (i, 0))],
             SDS((s, D_MODEL), f32), pl.BlockSpec((TM, D_MODEL), lambda i, k: (i, 0)), (nm, 4), 4, "nn", (TM, D_MODEL),
             epilogue=lambda acc, xr: acc + xr, n_extra=1)
    h2 = _norm_fwd(x1, sm["norm_ffn"], s)
    g4, u4, a4 = _ffn_up(h2, w["w_ffn_gate"], w["w_ffn_up"], s)
    x2 = _mm("mm_down", [a4, w["w_ffn_down"], x1],
             [pl.BlockSpec((None, TM, FF_SHARD), lambda i, k: (k, i, 0)), pl.BlockSpec((None, FF_SHARD, D_MODEL), lambda i, k: (k, 0, 0)),
              pl.BlockSpec((TM, D_MODEL), lambda i, k: (i, 0))],
             SDS((s, D_MODEL), f32), pl.BlockSpec((TM, D_MODEL), lambda i, k: (i, 0)), (nm, 4), 4, "nn", (TM, D_MODEL),
             epilogue=lambda acc, xr: acc + xr, n_extra=1)
    saved = dict(x=x, hb=hb, z=z, qa=qa, kd=kd, vd=vd, qb=qb, kb=kb, vb=vb, oa=oa, lse_a=lse_a, ob=ob, lse_b=lse_b,
                 pooled=pooled, pc=pc, y3=y3, merged=merged, x1=x1, h2=h2, g4=g4, u4=u4, a4=a4)
    return x2, saved


def _mm_alias(name, a, wt, y3, kdim, b, s):
    nm = s // TM

    def body(a_ref, b_ref, y_in, o_ref):
        o_ref[...] = _dot(a_ref[...].astype(bf16), b_ref[...].astype(bf16), "nn")

    return pl.pallas_call(
        body, name=name, grid=(4, nm),
        in_specs=[pl.BlockSpec((TM, kdim), lambda j, i: (i, 0)), pl.BlockSpec((None, kdim, 256), lambda j, i: (j, 0, 0)),
                  pl.BlockSpec(memory_space=pl.ANY)],
        out_specs=pl.BlockSpec((None, TM, 256), lambda j, i: (b, i, j)), out_shape=SDS((3, s, D_MODEL), f32),
        input_output_aliases={2: 0}, compiler_params=_cparams())(a, wt, y3)


def _ffn_up(h2, wg, wu, s):
    nm = s // TM

    def body(h_ref, wg_ref, wu_ref, g_o, u_o, a_o):
        h = h_ref[...]
        g = _dot(h, wg_ref[...], "nn")
        u = _dot(h, wu_ref[...], "nn")
        g_o[...] = g
        u_o[...] = u
        a_o[...] = (g * jax.nn.sigmoid(g) * u).astype(bf16)

    wsp = pl.BlockSpec((None, D_MODEL, FF_SHARD), lambda j, i: (j, 0, 0))
    osp = pl.BlockSpec((None, TM, FF_SHARD), lambda j, i: (j, i, 0))
    return pl.pallas_call(
        body, name="ffn_up", grid=(4, nm), in_specs=[pl.BlockSpec((TM, D_MODEL), lambda j, i: (i, 0)), wsp, wsp],
        out_specs=[osp, osp, osp],
        out_shape=[SDS((4, s, FF_SHARD), f32), SDS((4, s, FF_SHARD), f32), SDS((4, s, FF_SHARD), bf16)],
        compiler_params=_cparams())(h2, wg, wu)


def _ffn_bwd_act(dx2, wd, g4, u4, s):
    nm = s // TM

    def body(dx_ref, wd_ref, g_ref, u_ref, dg_o, du_o):
        da = _dot(dx_ref[...].astype(bf16), wd_ref[...], "nt")
        g, u = g_ref[...], u_ref[...]
        sg = jax.nn.sigmoid(g)
        dg_o[...] = (da * u * sg * (1.0 + g * (1.0 - sg))).astype(bf16)
        du_o[...] = (da * g * sg).astype(bf16)

    osp = pl.BlockSpec((None, TM, FF_SHARD), lambda j, i: (j, i, 0))
    return pl.pallas_call(
        body, name="ffn_bwd_act", grid=(4, nm),
        in_specs=[pl.BlockSpec((TM, D_MODEL), lambda j, i: (i, 0)), pl.BlockSpec((None, FF_SHARD, D_MODEL), lambda j, i: (j, 0, 0)), osp, osp],
        out_specs=[osp, osp], out_shape=[SDS((4, s, FF_SHARD), bf16), SDS((4, s, FF_SHARD), bf16)],
        compiler_params=_cparams())(dx2, wd, g4, u4)


def _ffn_bwd_dh(dg4, du4, wg, wu, s):
    nm = s // TM

    def body(dg_ref, du_ref, wg_ref, wu_ref, o_ref, acc):
        k = pl.program_id(1)
        part = _dot(dg_ref[...], wg_ref[...], "nt") + _dot(du_ref[...], wu_ref[...], "nt")

        @pl.when(k == 0)
        def _():
            acc[...] = part

        @pl.when(k > 0)
        def _():
            acc[...] += part

        @pl.when(k == 3)
        def _():
            o_ref[...] = acc[...]

    asp = pl.BlockSpec((None, TM, FF_SHARD), lambda i, k: (k, i, 0))
    wsp = pl.BlockSpec((None, D_MODEL, FF_SHARD), lambda i, k: (k, 0, 0))
    return pl.pallas_call(
        body, name="ffn_bwd_dh", grid=(nm, 4), in_specs=[asp, asp, wsp, wsp],
        out_specs=pl.BlockSpec((TM, D_MODEL), lambda i, k: (i, 0)), out_shape=SDS((s, D_MODEL), f32),
        scratch_shapes=[pltpu.VMEM((TM, D_MODEL), f32)], compiler_params=_cparams())(dg4, du4, wg, wu)


def _wgrad(name, a, a_spec, b, b_spec, out_shape, out_block, s, tk=512):
    nk = s // tk
    return _mm(name, [a, b], [a_spec, b_spec], SDS(out_shape, f32),
               pl.BlockSpec((None,) + out_block, lambda j, k: (j, 0, 0)), (4, nk), nk, "tn", out_block)


def _layer_bwd(dx2, sv, w, sm, tabs, s):
    nm = s // TM
    tk = 512
    tok = lambda width: pl.BlockSpec((tk, width), lambda j, k: (k, 0))
    g_wd = _wgrad("wg_down", sv["a4"], pl.BlockSpec((None, tk, FF_SHARD), lambda j, k: (j, k, 0)), dx2, tok(D_MODEL),
                  (4, FF_SHARD, D_MODEL), (FF_SHARD, D_MODEL), s)
    dg4, du4 = _ffn_bwd_act(dx2, w["w_ffn_down"], sv["g4"], sv["u4"], s)
    sh704 = pl.BlockSpec((None, tk, FF_SHARD), lambda j, k: (j, k, 0))
    g_wg = _wgrad("wg_gate", sv["h2"], tok(D_MODEL), dg4, sh704, (4, D_MODEL, FF_SHARD), (D_MODEL, FF_SHARD), s)
    g_wu = _wgrad("wg_up", sv["h2"], tok(D_MODEL), du4, sh704, (4, D_MODEL, FF_SHARD), (D_MODEL, FF_SHARD), s)
    dh2 = _ffn_bwd_dh(dg4, du4, w["w_ffn_gate"], w["w_ffn_up"], s)
    dx1, g_norm_ffn = _norm_bwd(sv["x1"], sm["norm_ffn"], dh2, dx2, s)
    colblk = lambda width: pl.BlockSpec((tk, width), lambda j, k: (k, j))
    g_wo = _wgrad("wg_out", sv["merged"], colblk(256), dx1, tok(D_MODEL), (4, 256, D_MODEL), (256, D_MODEL), s)
    dmerged = _mm("mm_dmerged", [dx1, w["w_out"]],
                  [pl.BlockSpec((TM, D_MODEL), lambda j, i, k: (i, 0)), pl.BlockSpec((None, 256, D_MODEL), lambda j, i, k: (j, 0, 0))],
                  SDS((s, D_MODEL), f32), pl.BlockSpec((TM, 256), lambda j, i, k: (i, j)), (4, nm, 1), 1, "nt", None)
    dz = jnp.zeros((s, IN_WIDTH), bf16)
    dy3, dz, g_bgate = _gate_bwd(dmerged, sv["y3"], sv["z"], sm["b_gate"], dz, s)
    br_in, br_grads = [], []
    for b, (nm_, act, kdim) in enumerate((("a", sv["oa"], 512), ("b", sv["ob"], 256), ("c", sv["pc"], 512))):
        wt = w["w_branch_" + nm_]
        br_grads.append(_wgrad("wg_br" + nm_, act, tok(kdim), dy3,
                               pl.BlockSpec((None, tk, 256), lambda j, k, b=b: (b, k, j)), (4, kdim, 256), (kdim, 256), s))
        br_in.append(_mm("mm_dbr" + nm_, [dy3, wt],
                         [pl.BlockSpec((None, TM, 256), lambda i, k, b=b: (b, i, k)), pl.BlockSpec((None, kdim, 256), lambda i, k: (k, 0, 0))],
                         SDS((s, kdim), f32), pl.BlockSpec((TM, kdim), lambda i, k: (i, 0)), (nm, 4), 4, "nt", (TM, kdim)))
    doa, dob, dpc = br_in
    dpn, g_lin, g_scale = _pool_bwd1(dpc, sv["pooled"], sm["pool_lin"], sm["pool_scale"], s)
    dz = _pool_bwd2(dpn, dz, s)
    delta = _delta_b(dob, sv["ob"], s)
    dqb, dkb, dvb = [], [], []
    for gi in range(3):
        dqb.append(_attn_b_dq(sv["qb"], sv["kb"], sv["vb"], dob, sv["lse_b"], delta, gi, s))
        dk, dv = _attn_b_dkv(sv["qb"], sv["kb"], sv["vb"], dob, sv["lse_b"], delta, gi, s)
        dkb.append(dk)
        dvb.append(dv)
    dqa, dkd, dvd = _flash_a_bwd(sv["qa"], sv["kd"], sv["vd"], sv["oa"], sv["lse_a"], doa, s)
    dz, g_gains = _prep_bwd(sv["z"], tabs, sm["gains"], dqa, dkd, dvd, dqb, dkb, dvb, dz, s)
    g_win = _wgrad("wg_in", sv["hb"], tok(D_MODEL), dz, colblk(1664), (4, D_MODEL, 1664), (D_MODEL, 1664), s)
    dh = _mm("mm_dh", [dz, w["w_in"]],
             [pl.BlockSpec((TM, 1664), lambda i, k: (i, k)), pl.BlockSpec((None, D_MODEL, 1664), lambda i, k: (k, 0, 0))],
             SDS((s, D_MODEL), f32), pl.BlockSpec((TM, D_MODEL), lambda i, k: (i, 0)), (nm, 4), 4, "nt", (TM, D_MODEL))
    dx0, g_norm_mix = _norm_bwd(sv["x"], sm["norm_mix"], dh, dx1, s)
    big = dict(w_in=g_win, w_branch_a=br_grads[0], w_branch_b=br_grads[1], w_branch_c=br_grads[2], w_out=g_wo,
               w_ffn_gate=g_wg, w_ffn_up=g_wu, w_ffn_down=g_wd)
    gg = g_gains[0:4, :HEAD_DIM] + g_gains[0:4, HEAD_DIM:]
    small = jnp.concatenate([g_norm_mix.reshape(-1), g_bgate.reshape(-1), gg.reshape(-1), g_lin.reshape(-1),
                             g_scale.reshape(-1), g_norm_ffn.reshape(-1)]).reshape(SMALL_ROWS, LANES)
    return dx0, big, small


def _mesh_pos():
    return lax.axis_index("x"), lax.axis_index("y"), lax.axis_index("c")


def _chip_exchange(name, srcs, pick_src, n_layers_dst=None):
    n = len(srcs)
    any_spec = pl.BlockSpec(memory_space=pl.ANY)
    out_shapes = [SDS((N_CHIPS,) + tuple(a.shape[1:]), a.dtype) for a in srcs]

    def body(*refs):
        src, dst = refs[:n], refs[n:2 * n]
        send_sems, recv_sems, loc_sems = refs[2 * n:]
        x, y, c = _mesh_pos()
        me = 2 * x + y
        chips = [(1 - x, y), (x, 1 - y), (1 - x, 1 - y)]
        local = [pltpu.make_async_copy(pick_src(src[t], me, me), dst[t].at[me], loc_sems.at[t]) for t in range(n)]
        for cp in local:
            cp.start()
        remote = []
        for j, (px, py) in enumerate(chips):
            peer = 2 * px + py
            for t in range(n):
                remote.append(pltpu.make_async_remote_copy(
                    src_ref=pick_src(src[t], peer, me), dst_ref=dst[t].at[me],
                    send_sem=send_sems.at[j * n + t], recv_sem=recv_sems.at[j * n + t],
                    device_id=(px, py, c), device_id_type=MESH))
        for cp in remote:
            cp.start()
        for cp in remote:
            cp.wait_send()
        for j, (px, py) in enumerate(chips):
            peer = 2 * px + py
            for t in range(n):
                pltpu.make_async_remote_copy(
                    src_ref=pick_src(src[t], peer, me), dst_ref=dst[t].at[peer],
                    send_sem=send_sems.at[j * n + t], recv_sem=recv_sems.at[j * n + t],
                    device_id=(px, py, c), device_id_type=MESH).wait_recv()
        for cp in local:
            cp.wait()

    return pl.pallas_call(
        body, name=name, in_specs=[any_spec] * n, out_specs=[any_spec] * n, out_shape=out_shapes,
        scratch_shapes=[pltpu.SemaphoreType.DMA((3 * n,)), pltpu.SemaphoreType.DMA((3 * n,)), pltpu.SemaphoreType.DMA((n,))],
        compiler_params=pltpu.CompilerParams())(*srcs)


def _sibling_exchange(name, srcs):
    n = len(srcs)
    any_spec = pl.BlockSpec(memory_space=pl.ANY)

    def body(*refs):
        src, dst = refs[:n], refs[n:2 * n]
        send_sems, recv_sems = refs[2 * n:]
        x, y, c = _mesh_pos()
        cps = [pltpu.make_async_remote_copy(src_ref=src[t], dst_ref=dst[t], send_sem=send_sems.at[t], recv_sem=recv_sems.at[t],
                                            device_id=(x, y, 1 - c), device_id_type=MESH) for t in range(n)]
        for cp in cps:
            cp.start()
        for cp in cps:
            cp.wait()

    return pl.pallas_call(
        body, name=name, in_specs=[any_spec] * n, out_specs=[any_spec] * n,
        out_shape=[SDS(a.shape, a.dtype) for a in srcs],
        scratch_shapes=[pltpu.SemaphoreType.DMA((n,)), pltpu.SemaphoreType.DMA((n,))],
        compiler_params=pltpu.CompilerParams())(*srcs)


def _all_exchange(name, src):
    any_spec = pl.BlockSpec(memory_space=pl.ANY)

    def body(src_ref, dst_ref, send_sems, recv_sems, loc_sem):
        x, y, c = _mesh_pos()
        me = 4 * x + 2 * y + c
        peers = [(x ^ ((k >> 2) & 1), y ^ ((k >> 1) & 1), c ^ (k & 1)) for k in range(1, N_DEV)]
        local = pltpu.make_async_copy(src_ref, dst_ref.at[me], loc_sem)
        local.start()
        cps = [pltpu.make_async_remote_copy(src_ref=src_ref, dst_ref=dst_ref.at[me], send_sem=send_sems.at[k],
                                            recv_sem=recv_sems.at[k], device_id=p, device_id_type=MESH)
               for k, p in enumerate(peers)]
        for cp in cps:
            cp.start()
        for cp in cps:
            cp.wait_send()
        for k, (px, py, pc) in enumerate(peers):
            pltpu.make_async_remote_copy(src_ref=src_ref, dst_ref=dst_ref.at[4 * px + 2 * py + pc], send_sem=send_sems.at[k],
                                         recv_sem=recv_sems.at[k], device_id=(px, py, pc), device_id_type=MESH).wait_recv()
        local.wait()

    return pl.pallas_call(
        body, name=name, in_specs=[any_spec], out_specs=any_spec, out_shape=SDS((N_DEV,) + src.shape, src.dtype),
        scratch_shapes=[pltpu.SemaphoreType.DMA((N_DEV - 1,)), pltpu.SemaphoreType.DMA((N_DEV - 1,)), pltpu.SemaphoreType.DMA],
        compiler_params=pltpu.CompilerParams())(src)


def _cast_bf16(name, a):
    l, r, c = a.shape
    tr = _row_tile(r, c)
    spec = pl.BlockSpec((None, tr, c), lambda i, j: (i, j, 0))
    return _ew(name, lambda t: t, (l, r // tr), [a], [spec], [SDS(a.shape, bf16)], [spec])[0]


def _sum4(name, land):
    _, r, c = land.shape
    tr = _row_tile(r, c)
    return _ew(name, lambda a, b, cc, d: ((a + b) + cc) + d, (r // tr,), [land] * 4,
               [pl.BlockSpec((None, tr, c), lambda i, k=k: (k, i, 0)) for k in range(4)],
               [SDS((r, c), f32)], [pl.BlockSpec((tr, c), lambda i: (i, 0))])[0]


def _adam_math(g, w, m, v):
    m2 = ADAM_B1 * m + (1.0 - ADAM_B1) * g
    v2 = ADAM_B2 * v + (1.0 - ADAM_B2) * (g * g)
    m_hat = m2 / (1.0 - ADAM_B1 ** ADAM_STEP)
    v_hat = v2 / (1.0 - ADAM_B2 ** ADAM_STEP)
    delta = -ADAM_LR * (m_hat / (jnp.sqrt(v_hat) + ADAM_EPS) + ADAM_WD * w)
    return delta, m2, v2


def _adamw_big(name, p_own, p_sib, w, m, v):
    l, r, c = w.shape
    tr = _row_tile(r, c)

    def fn(a, b, wt, mt, vt):
        g = a + b
        return (g,) + _adam_math(g, wt, mt, vt)

    spec = pl.BlockSpec((None, tr, c), lambda i, j: (i, j, 0))
    return _ew(name, fn, (l, r // tr), [p_own, p_sib, w, m, v], [spec] * 5, [SDS(w.shape, f32)] * 4, [spec] * 4)


def _adamw_small(land, w, m, v):
    r = w.shape[0]
    tr = _row_tile(r, LANES)

    def fn(*t):
        g = t[0]
        for k in range(1, N_DEV):
            g = g + t[k]
        return (g,) + _adam_math(g, *t[N_DEV:])

    row = pl.BlockSpec((tr, LANES), lambda i: (i, 0))
    return _ew("adamw_small", fn, (r // tr,), [land] * N_DEV + [w, m, v],
               [pl.BlockSpec((None, tr, LANES), lambda i, k=k: (k, i, 0)) for k in range(N_DEV)] + [row] * 3,
               [SDS(w.shape, f32)] * 4, [row] * 4)


def _pack_small(d):
    return jnp.concatenate([d[k].reshape(DEPTH, -1) for k in SMALL], axis=1).reshape(DEPTH * SMALL_ROWS, LANES)


def _unpack_small(a, shapes):
    a = a.reshape(DEPTH, SMALL_ROWS * LANES)
    out, off = {}, 0
    for k, n in zip(SMALL, SMALL_SIZES):
        out[k] = a[:, off:off + n].reshape(shapes[k])
        off += n
    return out


def kernel(x, norm_mix, w_in, b_gate, qn_a, kn_a, qn_b, kn_b, pool_lin, pool_scale, w_branch_a, w_branch_b, w_branch_c, w_out, norm_ffn, w_ffn_gate, w_ffn_up, w_ffn_down, loss_target, m_norm_mix, m_w_in, m_b_gate, m_qn_a, m_kn_a, m_qn_b, m_kn_b, m_pool_lin, m_pool_scale, m_w_branch_a, m_w_branch_b, m_w_branch_c, m_w_out, m_norm_ffn, m_w_ffn_gate, m_w_ffn_up, m_w_ffn_down, v_norm_mix, v_w_in, v_b_gate, v_qn_a, v_kn_a, v_qn_b, v_kn_b, v_pool_lin, v_pool_scale, v_w_branch_a, v_w_branch_b, v_w_branch_c, v_w_out, v_norm_ffn, v_w_ffn_gate, v_w_ffn_up, v_w_ffn_down):
    args = dict(locals())
    s = x.shape[1]
    wts = {k: args[k] for k in WEIGHT_ORDER}
    mom = {k: args["m_" + k] for k in WEIGHT_ORDER}
    var = {k: args["v_" + k] for k in WEIGHT_ORDER}
    tabs = _rope_tables(s)

    shards16 = [_cast_bf16("cast_" + k, wts[k]) for k in BIG]
    full = []
    for l in range(DEPTH):
        got = _chip_exchange(f"gather{l}", shards16, lambda ref, peer, me, l=l: ref.at[l])
        full.append(dict(zip(BIG, got)))

    def small_of(l):
        tile2 = lambda a: jnp.concatenate([a, a])
        gains = jnp.stack([tile2(qn_a[l]), tile2(kn_a[l]), tile2(qn_b[l]), tile2(kn_b[l])])
        return dict(norm_mix=norm_mix[l][None], norm_ffn=norm_ffn[l][None], b_gate=b_gate[l][None], gains=gains,
                    pool_lin=pool_lin[l], pool_scale=pool_scale[l][None])

    xs = x[0]
    saved = []
    for l in range(DEPTH):
        xs, sv = _layer_fwd(xs, full[l], small_of(l), tabs, s)
        saved.append(sv)
    dy, loss_rows = _loss_grad(xs, loss_target[0], s)
    loss = lax.psum(jnp.sum(loss_rows), ("x", "y", "c"))

    part = {k: [None] * DEPTH for k in BIG}
    small_g = [None] * DEPTH
    for l in reversed(range(DEPTH)):
        dy, big, small_g[l] = _layer_bwd(dy, saved[l], full[l], small_of(l), tabs, s)
        land = _chip_exchange("grad_scatter", [big[k] for k in BIG], lambda ref, peer, me: ref.at[peer])
        for k, a in zip(BIG, land):
            part[k][l] = _sum4("sum4_" + k, a)

    p_own = [jnp.stack(part[k]) for k in BIG]
    p_sib = _sibling_exchange("grad_sibling", p_own)
    outs = {}
    for k, a, b in zip(BIG, p_own, p_sib):
        outs[k] = _adamw_big("adamw_" + k, a, b, wts[k], mom[k], var[k])

    land_s = _all_exchange("small_allgather", jnp.concatenate(small_g, axis=0))
    res_s = _adamw_small(land_s, _pack_small(wts), _pack_small(mom), _pack_small(var))
    shapes = {k: wts[k].shape for k in SMALL}
    small_out = [_unpack_small(r, shapes) for r in res_s]
    for k in SMALL:
        outs[k] = tuple(so[k] for so in small_out)

    flat = [loss, dy[None]]
    for idx in range(4):
        flat += [outs[k][idx] for k in WEIGHT_ORDER]
    return tuple(flat)
```

```python
import functools

import jax
import jax.numpy as jnp
from jax import lax
from jax.experimental import pallas as pl
from jax.experimental.pallas import tpu as pltpu

f32, bf16 = jnp.float32, jnp.bfloat16
SDS = jax.ShapeDtypeStruct

DEPTH = 4
D_MODEL = 1024
HEAD_DIM = 64
LANES = 128
GRID_W = 64
ROPE_THETA = 10000.0
EPS = 1e-6
NEG_INF = -1e30
SCALE = HEAD_DIM ** -0.5
B_GROUP_CFG = ((128, 1), (512, 4), (2048, 16))
POOL_WINDOWS = (2, 4, 8, 16)
N_CHIPS = 4
N_DEV = 8
C_QA, C_KA, C_VA, C_QB, C_KB, C_VB, C_UC, C_GZ = 0, 512, 640, 768, 1536, 2304, 3072, 3584
IN_WIDTH = 6656
FF_SHARD = 704
ADAM_LR, ADAM_B1, ADAM_B2, ADAM_EPS, ADAM_WD, ADAM_STEP = 0.001, 0.9, 0.999, 1e-08, 0.01, 10
VMEM_LIMIT = 56 * 1024 * 1024
MESH = pl.DeviceIdType.MESH

BIG = ("w_in", "w_branch_a", "w_branch_b", "w_branch_c", "w_out", "w_ffn_gate", "w_ffn_up", "w_ffn_down")
SMALL = ("norm_mix", "b_gate", "qn_a", "kn_a", "qn_b", "kn_b", "pool_lin", "pool_scale", "norm_ffn")
SMALL_SIZES = (1024, 3072, 64, 64, 64, 64, 65536, 512, 1024)
SMALL_ROWS = sum(SMALL_SIZES) // LANES
WEIGHT_ORDER = ("norm_mix", "w_in", "b_gate", "qn_a", "kn_a", "qn_b", "kn_b", "pool_lin", "pool_scale",
                "w_branch_a", "w_branch_b", "w_branch_c", "w_out", "norm_ffn", "w_ffn_gate", "w_ffn_up", "w_ffn_down")


def _cparams():
    return pltpu.CompilerParams(vmem_limit_bytes=VMEM_LIMIT)


def _lo_mask(shape):
    return lax.broadcasted_iota(jnp.int32, shape, len(shape) - 1) < HEAD_DIM


def _dot(a, b, dims):
    dn = {"nn": (((1,), (0,)), ((), ())), "nt": (((1,), (1,)), ((), ())), "tn": (((0,), (0,)), ((), ()))}[dims]
    return lax.dot_general(a, b, dn, preferred_element_type=f32)


def _mm(name, ins, in_specs, out_shape, out_spec, grid, nk, dims, acc_shape, epilogue=None, n_extra=0, aliases=None):
    multi = isinstance(out_shape, (list, tuple))
    n_out = len(out_shape) if multi else 1

    def body(*refs):
        a_ref, b_ref = refs[0], refs[1]
        extra = refs[2:2 + n_extra]
        outs = refs[2 + n_extra:2 + n_extra + n_out]
        k = pl.program_id(len(grid) - 1)
        part = _dot(a_ref[...].astype(bf16), b_ref[...].astype(bf16), dims)

        def finish(acc):
            res = epilogue(acc, *[e[...] for e in extra]) if epilogue is not None else acc
            res = res if isinstance(res, (list, tuple)) else (res,)
            for o, r in zip(outs, res):
                o[...] = r.astype(o.dtype)

        if nk == 1:
            finish(part)
        else:
            acc_ref = refs[-1]

            @pl.when(k == 0)
            def _():
                acc_ref[...] = part

            @pl.when(k > 0)
            def _():
                acc_ref[...] += part

            @pl.when(k == nk - 1)
            def _():
                finish(acc_ref[...])

    return pl.pallas_call(
        body, name=name, grid=grid, in_specs=in_specs,
        out_specs=list(out_spec) if multi else out_spec,
        out_shape=list(out_shape) if multi else out_shape,
        scratch_shapes=[] if nk == 1 else [pltpu.VMEM(acc_shape, f32)],
        input_output_aliases=aliases or {}, compiler_params=_cparams())(*ins)


def _ew(name, fn, grid, ins, in_specs, out_shapes, out_specs, n_acc=0, aliases=None):
    n_in = len(ins)
    n_out = len(out_shapes) - n_acc
    n_alias = len(aliases or {})

    def body(*refs):
        in_refs = refs[:n_in - n_alias] if n_alias else refs[:n_in]
        out_refs = refs[n_in:n_in + n_out]
        acc_refs = refs[n_in + n_out:]
        i = pl.program_id(len(grid) - 1)
        res = fn(*[r[...] for r in in_refs])
        res = res if isinstance(res, (list, tuple)) else (res,)
        for o, r in zip(out_refs, res[:n_out]):
            o[...] = r.astype(o.dtype)
        for a, r in zip(acc_refs, res[n_out:]):
            @pl.when(i == 0)
            def _(a=a, r=r):
                a[...] = r.astype(a.dtype)

            @pl.when(i > 0)
            def _(a=a, r=r):
                a[...] += r.astype(a.dtype)

    return pl.pallas_call(body, name=name, grid=grid, in_specs=in_specs, out_specs=out_specs, out_shape=out_shapes,
                          input_output_aliases=aliases or {}, compiler_params=_cparams())(*ins)


EW_BLOCK_ELEMS = 128 * 1024


def _row_tile(rows, cols):
    return max(t for t in range(8, rows + 1, 8) if rows % t == 0 and (t * cols <= EW_BLOCK_ELEMS or t == 8))


def _rope_tables(s):
    t = jnp.arange(s)
    inv_ax = ROPE_THETA ** (-jnp.arange(0, HEAD_DIM // 2, 2, dtype=f32) / (HEAD_DIM // 2))
    inv_sq = ROPE_THETA ** (-jnp.arange(0, HEAD_DIM, 2, dtype=f32) / HEAD_DIM)
    ang_row = (t // GRID_W).astype(f32)[:, None] * inv_ax[None, :]
    ang_col = (t % GRID_W).astype(f32)[:, None] * inv_ax[None, :]
    ang_seq = t.astype(f32)[:, None] * inv_sq[None, :]
    a_ax = jnp.concatenate([ang_row, ang_row, ang_col, ang_col], axis=1)
    sg_ax = jnp.concatenate([-jnp.ones(16), jnp.ones(16), -jnp.ones(16), jnp.ones(16)]).astype(f32)
    a_sq = jnp.concatenate([ang_seq, ang_seq], axis=1)
    sg_sq = jnp.concatenate([-jnp.ones(32), jnp.ones(32)]).astype(f32)
    two = lambda a: jnp.concatenate([a, a], axis=1)
    return (two(jnp.cos(a_ax)), two(jnp.sin(a_ax) * sg_ax), two(jnp.cos(a_sq)), two(jnp.sin(a_sq) * sg_sq))


def _head_stat(v, lo):
    s0 = jnp.sum(jnp.where(lo, v, 0.0), axis=1, keepdims=True)
    s1 = jnp.sum(jnp.where(lo, 0.0, v), axis=1, keepdims=True)
    return jnp.where(lo, s0, s1)


def _partner(y, off):
    lane = lax.broadcasted_iota(jnp.int32, y.shape, 1)
    first = (lane & (2 * off - 1)) < off
    return jnp.where(first, pltpu.roll(y, LANES - off, 1), pltpu.roll(y, off, 1))


def _normrope(xc, g, cos, sin, off):
    lo = _lo_mask(xc.shape)
    r = lax.rsqrt(_head_stat(xc * xc, lo) * (1.0 / HEAD_DIM) + EPS)
    y = xc * r * g
    return y * cos + _partner(y, off) * sin


def _normrope_bwd(xc, g, cos, sin, off, drot):
    lo = _lo_mask(xc.shape)
    r = lax.rsqrt(_head_stat(xc * xc, lo) * (1.0 / HEAD_DIM) + EPS)
    n = xc * r
    dy = drot * cos + _partner(drot * sin, off)
    dg = jnp.sum(dy * n, axis=0, keepdims=True)
    dn = dy * g
    dx = r * (dn - n * (_head_stat(dn * n, lo) * (1.0 / HEAD_DIM)))
    return dx, dg


def _norm_fwd(x, g, s):
    ts = 256

    def fn(xt, gt):
        r = lax.rsqrt(jnp.mean(xt * xt, axis=1, keepdims=True) + EPS)
        return xt * r * gt

    return _ew("norm_fwd", fn, (s // ts,), [x, g],
               [pl.BlockSpec((ts, D_MODEL), lambda i: (i, 0)), pl.BlockSpec((1, D_MODEL), lambda i: (0, 0))],
               [SDS((s, D_MODEL), bf16)], [pl.BlockSpec((ts, D_MODEL), lambda i: (i, 0))])[0]


def _norm_bwd(x, g, dh, dres, s):
    ts = 256

    def fn(xt, gt, dht, drt):
        r = lax.rsqrt(jnp.mean(xt * xt, axis=1, keepdims=True) + EPS)
        n = xt * r
        dn = dht * gt
        dx = drt + r * (dn - n * jnp.mean(dn * n, axis=1, keepdims=True))
        return dx, jnp.sum(dht * n, axis=0, keepdims=True)

    row = pl.BlockSpec((ts, D_MODEL), lambda i: (i, 0))
    one = pl.BlockSpec((1, D_MODEL), lambda i: (0, 0))
    return _ew("norm_bwd", fn, (s // ts,), [x, g, dh, dres], [row, one, row, row],
               [SDS((s, D_MODEL), f32), SDS((1, D_MODEL), f32)], [row, one], n_acc=1)


def _prep_fwd(z, tabs, gains, s):
    ts = 256
    cos_a, sin_a, cos_b, sin_b = tabs

    def body(za_ref, zq_ref, zk_ref, zv_ref, ca, sa, cb, sb, g_ref, qa_o, kd_o, vd_o, qb_o, kb_o, vb_o):
        lo = _lo_mask((ts, LANES))
        g = g_ref[...]
        ca_, sa_, cb_, sb_ = ca[...], sa[...], cb[...], sb[...]
        for c in range(4):
            qa_o[:, c * LANES:(c + 1) * LANES] = _normrope(za_ref[:, c * LANES:(c + 1) * LANES], g[0:1], ca_, sa_, 16).astype(bf16)
        k = _normrope(za_ref[:, C_KA:C_KA + LANES], g[1:2], ca_, sa_, 16)
        kr = pltpu.roll(k, HEAD_DIM, 1)
        kd_o[0] = k.astype(bf16)
        kd_o[1] = kr.astype(bf16)
        v = za_ref[:, C_VA:C_VA + LANES]
        ones_col = (lax.broadcasted_iota(jnp.int32, v.shape, 1) == HEAD_DIM).astype(f32)
        vd_o[0] = jnp.where(lo, v, ones_col).astype(bf16)
        vd_o[1] = jnp.where(lo, pltpu.roll(v, HEAD_DIM, 1), ones_col).astype(bf16)
        for c in range(6):
            sl = slice(c * LANES, (c + 1) * LANES)
            qb_o[:, sl] = _normrope(zq_ref[:, sl], g[2:3], cb_, sb_, 32).astype(bf16)
            kb_o[:, sl] = _normrope(zk_ref[:, sl], g[3:4], cb_, sb_, 32).astype(bf16)
        vb_o[...] = zv_ref[...].astype(bf16)

    w = 768
    zspec = lambda cb: pl.BlockSpec((ts, w), lambda i: (i, cb))
    tab = pl.BlockSpec((ts, LANES), lambda i: (i, 0))
    dup = pl.BlockSpec((2, ts, LANES), lambda i: (0, i, 0))
    return pl.pallas_call(
        body, name="prep_fwd", grid=(s // ts,),
        in_specs=[zspec(0), zspec(1), zspec(2), zspec(3), tab, tab, tab, tab, pl.BlockSpec((4, LANES), lambda i: (0, 0))],
        out_specs=[pl.BlockSpec((ts, 512), lambda i: (i, 0)), dup, dup, zspec(0), zspec(0), zspec(0)],
        out_shape=[SDS((s, 512), bf16), SDS((2, s, LANES), bf16), SDS((2, s, LANES), bf16),
                   SDS((s, w), bf16), SDS((s, w), bf16), SDS((s, w), bf16)],
        compiler_params=_cparams())(z, z, z, z, cos_a, sin_a, cos_b, sin_b, gains)


def _prep_bwd(z, tabs, gains, dqa, dkd, dvd, dqb, dkb, dvb, dz, s):
    ts = 256
    cos_a, sin_a, cos_b, sin_b = tabs

    def body(za_ref, zq_ref, zk_ref, ca, sa, cb, sb, g_ref, dqa_r, dkd_r, dvd_r,
             dq0, dq1, dq2, dk0, dk1, dk2, dv0, dv1, dv2, dz_in, dz_o, dg_o):
        i = pl.program_id(0)
        lo = _lo_mask((ts, LANES))
        g = g_ref[...]
        ca_, sa_, cb_, sb_ = ca[...], sa[...], cb[...], sb[...]
        dg = [jnp.zeros((1, LANES), f32) for _ in range(4)]
        for c in range(4):
            sl = slice(c * LANES, (c + 1) * LANES)
            dx, d = _normrope_bwd(za_ref[:, sl], g[0:1], ca_, sa_, 16, dqa_r[:, sl])
            dz_o[:, sl] = dx.astype(bf16)
            dg[0] += d
        dk = jnp.where(lo, dkd_r[0], pltpu.roll(dkd_r[1], HEAD_DIM, 1))
        dx, d = _normrope_bwd(za_ref[:, C_KA:C_KA + LANES], g[1:2], ca_, sa_, 16, dk)
        dz_o[:, C_KA:C_KA + LANES] = dx.astype(bf16)
        dg[1] += d
        dz_o[:, C_VA:C_VA + LANES] = jnp.where(lo, dvd_r[0], pltpu.roll(dvd_r[1], HEAD_DIM, 1)).astype(bf16)
        dqs, dks, dvs = (dq0, dq1, dq2), (dk0, dk1, dk2), (dv0, dv1, dv2)
        for c in range(6):
            sl = slice(c * LANES, (c + 1) * LANES)
            gsl = slice((c % 2) * LANES, (c % 2 + 1) * LANES)
            dx, d = _normrope_bwd(zq_ref[:, sl], g[2:3], cb_, sb_, 32, dqs[c // 2][:, gsl])
            dz_o[:, C_QB + c * LANES:C_QB + (c + 1) * LANES] = dx.astype(bf16)
            dg[2] += d
            dx, d = _normrope_bwd(zk_ref[:, sl], g[3:4], cb_, sb_, 32, dks[c // 2][:, gsl])
            dz_o[:, C_KB + c * LANES:C_KB + (c + 1) * LANES] = dx.astype(bf16)
            dg[3] += d
            dz_o[:, C_VB + c * LANES:C_VB + (c + 1) * LANES] = dvs[c // 2][:, gsl].astype(bf16)
        dgs = jnp.concatenate(dg + [jnp.zeros((4, LANES), f32)], axis=0)

        @pl.when(i == 0)
        def _():
            dg_o[...] = dgs

        @pl.when(i > 0)
        def _():
            dg_o[...] += dgs

    w = 768
    zspec = lambda cb: pl.BlockSpec((ts, w), lambda i: (i, cb))
    tab = pl.BlockSpec((ts, LANES), lambda i: (i, 0))
    dup = pl.BlockSpec((2, ts, LANES), lambda i: (0, i, 0))
    grp = pl.BlockSpec((ts, 256), lambda i: (i, 0))
    dzo, dgo = pl.pallas_call(
        body, name="prep_bwd", grid=(s // ts,),
        in_specs=[zspec(0), zspec(1), zspec(2), tab, tab, tab, tab, pl.BlockSpec((4, LANES), lambda i: (0, 0)),
                  pl.BlockSpec((ts, 512), lambda i: (i, 0)), dup, dup] + [grp] * 9 + [pl.BlockSpec(memory_space=pl.ANY)],
        out_specs=[pl.BlockSpec((ts, C_UC), lambda i: (i, 0)), pl.BlockSpec((8, LANES), lambda i: (0, 0))],
        out_shape=[SDS((s, IN_WIDTH), bf16), SDS((8, LANES), f32)],
        input_output_aliases={20: 0}, compiler_params=_cparams())(
            z, z, z, cos_a, sin_a, cos_b, sin_b, gains, dqa, dkd, dvd, *dqb, *dkb, *dvb, dz)
    return dzo, dgo


def _stack_heads(x, lo, dtype):
    z = jnp.zeros_like(x)
    return jnp.concatenate([jnp.where(lo, x, z), jnp.where(lo, z, x)], axis=0).astype(dtype)


def _rows_of(v, lo, kind):
    if kind == "max":
        a = jnp.max(jnp.where(lo, v, NEG_INF * 10), axis=1, keepdims=True)
        b = jnp.max(jnp.where(lo, NEG_INF * 10, v), axis=1, keepdims=True)
    else:
        a = jnp.sum(jnp.where(lo, v, 0.0), axis=1, keepdims=True) * (1.0 / HEAD_DIM)
        b = jnp.sum(jnp.where(lo, 0.0, v), axis=1, keepdims=True) * (1.0 / HEAD_DIM)
    return jnp.concatenate([a, b], axis=0)


def _stack_low(x, lo, dtype, scale=None):
    x = x.astype(f32)
    if scale is not None:
        x = x * scale
    z = jnp.zeros_like(x)
    return jnp.concatenate([jnp.where(lo, x, z), jnp.where(lo, pltpu.roll(x, HEAD_DIM, 1), z)], axis=0).astype(dtype)


def _unstack_low(xs, lo, rows):
    return jnp.where(lo, xs[:rows], pltpu.roll(xs[rows:], HEAD_DIM, 1))


FA_TQ, FA_TK = 256, 1024


def _host_exchange(body, n_in, n_out, grid, xchg):
    if xchg is None:
        return body, [], [], [], []
    srcs, pick_src = xchg
    n = len(srcs)
    any_spec = pl.BlockSpec(memory_space=pl.ANY)

    def hosted(*refs):
        ins, src = refs[:n_in], refs[n_in:n_in + n]
        outs, dst = refs[n_in + n:n_in + n + n_out], refs[n_in + n + n_out:n_in + 2 * n + n_out]
        scratch, sems = refs[n_in + 2 * n + n_out:-3], refs[-3:]
        ids = [pl.program_id(a) for a in range(len(grid))]
        first = functools.reduce(jnp.logical_and, [i == 0 for i in ids])
        last = functools.reduce(jnp.logical_and, [i == g - 1 for i, g in zip(ids, grid)])

        @pl.when(first)
        def _():
            _chip_start(_chip_copies(src, dst, pick_src, *sems, receiving=False))

        body(*ins, *outs, *scratch)

        @pl.when(last)
        def _():
            _chip_wait(_chip_copies(src, dst, pick_src, *sems))

    return hosted, [any_spec] * n, [any_spec] * n, _chip_landing(srcs), _chip_sems(n)


def _flash_a_fwd(qa, kd, vd, s, xchg=None):
    tq, tk = FA_TQ, 2 * FA_TK
    nk = s // tk

    def body(q_ref, k_ref, v_ref, o_ref, lse_ref, qs, m_s, acc):
        lo = _lo_mask((tq, LANES))
        qs[...] = _stack_low(q_ref[...], lo, bf16, SCALE)
        m_s[...] = jnp.full(m_s.shape, NEG_INF, f32)
        acc[...] = jnp.zeros(acc.shape, f32)

        def chunk(j, carry):
            rows = pl.ds(pl.multiple_of(j * tk, tk), tk)
            sc = _dot(qs[...], k_ref[rows, :], "nt")
            m_prev = m_s[...]
            m_new = jnp.maximum(m_prev, jnp.max(sc, axis=1, keepdims=True))
            p = jnp.exp(sc - m_new).astype(bf16)
            acc[...] = acc[...] * jnp.exp(m_prev - m_new) + _dot(p, v_ref[rows, :], "nn")
            m_s[...] = m_new
            return carry

        lax.fori_loop(0, nk, chunk, 0, unroll=2)
        a = acc[...]
        lane = lax.broadcasted_iota(jnp.int32, a.shape, 1)
        l = jnp.sum(jnp.where(lane == HEAD_DIM, a, 0.0), axis=1, keepdims=True)
        o_ref[...] = _unstack_low(a / l, lo, tq).astype(bf16)
        lse = m_s[...] + jnp.log(l)
        lse_ref[...] = jnp.where(lo, lse[:tq], lse[tq:])

    kv = pl.BlockSpec((None, s, LANES), lambda p, i: (p // 2, 0, 0))
    qo = pl.BlockSpec((tq, LANES), lambda p, i: (i, p))
    grid = (4, s // tq)
    hosted, xi, xo, xs, xsem = _host_exchange(body, 3, 2, grid, xchg)
    res = pl.pallas_call(
        hosted, name="flash_a_fwd" + ("_x" if xchg else ""), grid=grid, in_specs=[qo, kv, kv] + xi, out_specs=[qo, qo] + xo,
        out_shape=[SDS((s, 512), bf16), SDS((s, 512), f32)] + xs,
        scratch_shapes=[pltpu.VMEM((2 * tq, LANES), bf16), pltpu.VMEM((2 * tq, 1), f32),
                        pltpu.VMEM((2 * tq, LANES), f32)] + xsem,
        compiler_params=_cparams())(qa, kd, vd, *(xchg[0] if xchg else []))
    return res[:2], res[2:]


def _flash_a_bwd(qa, kd, vd, oa, lse, doa, s, xchg=None):
    tq, tk = FA_TQ, FA_TK
    nk = s // tk

    def body(q_ref, do_ref, o_ref, lse_ref, k_ref, v_ref, dq_ref, dk_ref, dv_ref, qs, dos, lse_s, dl_s, dq_s):
        i = pl.program_id(1)
        lo = _lo_mask((tq, LANES))

        @pl.when(i == 0)
        def _():
            dk_ref[...] = jnp.zeros(dk_ref.shape, f32)
            dv_ref[...] = jnp.zeros(dv_ref.shape, f32)

        for pp in range(2):
            sl = slice(pp * LANES, (pp + 1) * LANES)
            do = do_ref[:, sl]
            qs[...] = _stack_low(q_ref[:, sl], lo, bf16, SCALE)
            dos[...] = _stack_low(do, lo, bf16)
            dl_s[...] = _rows_of(do * o_ref[:, sl].astype(f32), lo, "sum") * HEAD_DIM
            lse_s[...] = _rows_of(lse_ref[:, sl], lo, "max")
            dq_s[...] = jnp.zeros(dq_s.shape, f32)

            def chunk(j, carry):
                rows = pl.ds(pl.multiple_of(j * tk, tk), tk)
                k, v = k_ref[rows, :], v_ref[rows, :]
                q_, do_ = qs[...], dos[...]
                p = jnp.exp(_dot(q_, k, "nt") - lse_s[...])
                ds = (p * (_dot(do_, v, "nt") - dl_s[...])).astype(bf16)
                dv_ref[rows, :] += _dot(p.astype(bf16), do_, "tn")
                dk_ref[rows, :] += _dot(ds, q_, "tn")
                dq_s[...] += _dot(ds, k, "nn")
                return carry

            lax.fori_loop(0, nk, chunk, 0)
            dq_ref[:, sl] = _unstack_low(dq_s[...], lo, tq) * SCALE

    grp = lambda: pl.BlockSpec((tq, 2 * LANES), lambda c, i: (i, c))
    kv = lambda: pl.BlockSpec((None, s, LANES), lambda c, i: (c, 0, 0))
    grid = (2, s // tq)
    hosted, xi, xo, xs, xsem = _host_exchange(body, 6, 3, grid, xchg)
    res = pl.pallas_call(
        hosted, name="flash_a_bwd" + ("_x" if xchg else ""), grid=grid,
        in_specs=[grp(), grp(), grp(), grp(), kv(), kv()] + xi, out_specs=[grp(), kv(), kv()] + xo,
        out_shape=[SDS((s, 512), f32), SDS((2, s, LANES), f32), SDS((2, s, LANES), f32)] + xs,
        scratch_shapes=[pltpu.VMEM((2 * tq, LANES), bf16), pltpu.VMEM((2 * tq, LANES), bf16), pltpu.VMEM((2 * tq, 1), f32),
                        pltpu.VMEM((2 * tq, 1), f32), pltpu.VMEM((2 * tq, LANES), f32)] + xsem,
        compiler_params=_cparams())(qa, doa, oa, lse, kd, vd, *(xchg[0] if xchg else []))
    return res[:3], res[3:]


BH = 128
BQ_MAX = 512


class _BandGeom:
    def __init__(self, gi, s):
        self.d = B_GROUP_CFG[gi][1]
        self.l_sub = s // self.d
        self.bq = min(BQ_MAX, self.l_sub)
        self.nb = self.l_sub // self.bq
        self.grid = (self.d, self.nb, 2)
        per, last = self.bq // BH, self.l_sub // BH - 1
        self.in_col = lambda r, pp: r * 6 + gi * 2 + pp
        self.out_col = lambda r, pp: r * 2 + pp
        self.cur = lambda col: pl.BlockSpec((self.bq, LANES), lambda r, n, pp: (n, col(r, pp)))
        self.prev = lambda col: pl.BlockSpec((BH, LANES), lambda r, n, pp: (jnp.maximum(n * per - 1, 0), col(r, pp)))
        self.next = lambda col: pl.BlockSpec((BH, LANES), lambda r, n, pp: (jnp.minimum((n + 1) * per, last), col(r, pp)))
        self.window = lambda col: [self.prev(col), self.cur(col), self.next(col)]
        self.view_in = lambda a: a.reshape(self.l_sub, self.d * 768)
        self.view_out = lambda a: a.reshape(self.l_sub, self.d * 256)

    def valid(self, qpos, kpos):
        return (jnp.abs(qpos - kpos) <= 64) & (kpos >= 0) & (kpos < self.l_sub) & (qpos >= 0) & (qpos < self.l_sub)


def _cat3(a, b, c):
    return jnp.concatenate([a[...], b[...], c[...]], axis=0)


def _attn_b_fwd(qb, kb, vb, gi, s):
    g = _BandGeom(gi, s)
    bq = g.bq

    def body(q_ref, kp, kc, kn, vp, vc, vn, o_ref, lse_ref):
        n = pl.program_id(1)
        lo = _lo_mask((bq, LANES))
        qs = _stack_heads(q_ref[...], lo, bf16)
        k, v = _cat3(kp, kc, kn), _cat3(vp, vc, vn)
        sc = _dot(qs, k, "nt") * SCALE
        qpos = n * bq + (lax.broadcasted_iota(jnp.int32, sc.shape, 0) & (bq - 1))
        kpos = n * bq - BH + lax.broadcasted_iota(jnp.int32, sc.shape, 1)
        sc = jnp.where(g.valid(qpos, kpos), sc, NEG_INF)
        m = jnp.max(sc, axis=1, keepdims=True)
        p = jnp.exp(sc - m)
        den = jnp.sum(p, axis=1, keepdims=True)
        o = _dot(p.astype(bf16), v, "nn") / den
        o_ref[...] = jnp.where(lo, o[:bq], o[bq:])
        lse = m + jnp.log(den)
        lse_ref[...] = jnp.where(lo, lse[:bq], lse[bq:])

    out = g.cur(g.out_col)
    qv, kv, vv = g.view_in(qb), g.view_in(kb), g.view_in(vb)
    o, lse = pl.pallas_call(
        body, name=f"attn_b_fwd{gi}", grid=g.grid,
        in_specs=[g.cur(g.in_col)] + g.window(g.in_col) * 2, out_specs=[out, out],
        out_shape=[SDS((g.l_sub, g.d * 256), f32), SDS((g.l_sub, g.d * 256), f32)],
        compiler_params=_cparams())(qv, kv, kv, kv, vv, vv, vv)
    return o.reshape(s, 256), lse.reshape(s, 256)


def _attn_b_dq(qb, kb, vb, dob, lse, delta, gi, s):
    g = _BandGeom(gi, s)
    bq = g.bq

    def body(q_ref, kp, kc, kn, vp, vc, vn, do_ref, lse_ref, dl_ref, dq_ref):
        n = pl.program_id(1)
        lo = _lo_mask((bq, LANES))
        qs = _stack_heads(q_ref[...], lo, bf16)
        dos = _stack_heads(do_ref[...], lo, bf16)
        k, v = _cat3(kp, kc, kn), _cat3(vp, vc, vn)
        sc = _dot(qs, k, "nt") * SCALE
        qpos = n * bq + (lax.broadcasted_iota(jnp.int32, sc.shape, 0) & (bq - 1))
        kpos = n * bq - BH + lax.broadcasted_iota(jnp.int32, sc.shape, 1)
        p = jnp.where(g.valid(qpos, kpos), jnp.exp(sc - _rows_of(lse_ref[...], lo, "max")), 0.0)
        dp = _dot(dos, v, "nt")
        ds = (p * (dp - _rows_of(dl_ref[...], lo, "sum")) * SCALE).astype(bf16)
        dqs = _dot(ds, k, "nn")
        dq_ref[...] = jnp.where(lo, dqs[:bq], dqs[bq:])

    out = g.cur(g.out_col)
    qv, kv, vv = g.view_in(qb), g.view_in(kb), g.view_in(vb)
    dq = pl.pallas_call(
        body, name=f"attn_b_dq{gi}", grid=g.grid,
        in_specs=[g.cur(g.in_col)] + g.window(g.in_col) * 2 + [out, out, out], out_specs=out,
        out_shape=SDS((g.l_sub, g.d * 256), f32),
        compiler_params=_cparams())(qv, kv, kv, kv, vv, vv, vv, g.view_out(dob), g.view_out(lse), g.view_out(delta))
    return dq.reshape(s, 256)


def _attn_b_dkv(qb, kb, vb, dob, lse, delta, gi, s):
    g = _BandGeom(gi, s)
    bq = g.bq
    nq = bq + 2 * BH

    def body(k_ref, v_ref, qp, qc, qn, dop, doc, don, lp, lc, ln, dp_, dc_, dn_, dk_ref, dv_ref):
        m = pl.program_id(1)
        lo = _lo_mask((nq, LANES))
        qs = _stack_heads(_cat3(qp, qc, qn), lo, bf16)
        dos = _stack_heads(_cat3(dop, doc, don), lo, bf16)
        lses = _rows_of(_cat3(lp, lc, ln), lo, "max")
        dls = _rows_of(_cat3(dp_, dc_, dn_), lo, "sum")
        k, v = k_ref[...], v_ref[...]
        sc = _dot(qs, k, "nt") * SCALE
        row = lax.broadcasted_iota(jnp.int32, sc.shape, 0)
        qpos = m * bq - BH + jnp.where(row >= nq, row - nq, row)
        kpos = m * bq + lax.broadcasted_iota(jnp.int32, sc.shape, 1)
        p = jnp.where(g.valid(qpos, kpos), jnp.exp(sc - lses), 0.0)
        dpv = _dot(dos, v, "nt")
        ds = (p * (dpv - dls) * SCALE).astype(bf16)
        dv_ref[...] = _dot(p.astype(bf16), dos, "tn")
        dk_ref[...] = _dot(ds, qs, "tn")

    cur = g.cur(g.out_col)
    qv, kv, vv = g.view_in(qb), g.view_in(kb), g.view_in(vb)
    dk, dv = pl.pallas_call(
        body, name=f"attn_b_dkv{gi}", grid=g.grid,
        in_specs=[g.cur(g.in_col)] * 2 + g.window(g.in_col) + g.window(g.out_col) * 3, out_specs=[cur, cur],
        out_shape=[SDS((g.l_sub, g.d * 256), f32), SDS((g.l_sub, g.d * 256), f32)],
        compiler_params=_cparams())(kv, vv, qv, qv, qv, *([g.view_out(dob)] * 3), *([g.view_out(lse)] * 3),
                                    *([g.view_out(delta)] * 3))
    return dk.reshape(s, 256), dv.reshape(s, 256)


def _merge_b(o_list, lse_list, s):
    ts = 256

    def fn(o0, o1, o2, l0, l1, l2):
        m = jnp.maximum(jnp.maximum(l0, l1), l2)
        w0, w1, w2 = jnp.exp(l0 - m), jnp.exp(l1 - m), jnp.exp(l2 - m)
        den = w0 + w1 + w2
        return (w0 * o0 + w1 * o1 + w2 * o2) / den, m + jnp.log(den)

    row = pl.BlockSpec((ts, 256), lambda i: (i, 0))
    return _ew("merge_b", fn, (s // ts,), [*o_list, *lse_list], [row] * 6,
               [SDS((s, 256), bf16), SDS((s, 256), f32)], [row, row])


def _delta_b(dob, ob, s):
    ts = 256

    def fn(do, o):
        prod = do * o.astype(f32)
        lo = _lo_mask((ts, LANES))
        return jnp.concatenate([_head_stat(prod[:, :LANES], lo), _head_stat(prod[:, LANES:], lo)], axis=1)

    row = pl.BlockSpec((ts, 256), lambda i: (i, 0))
    return _ew("delta_b", fn, (s // ts,), [dob, ob], [row, row], [SDS((s, 256), f32)], [row])[0]


def _band(t0, u0, w, nt, nu, s, transposed):
    t = t0 + lax.broadcasted_iota(jnp.int32, (nt, nu), 0)
    u = u0 + lax.broadcasted_iota(jnp.int32, (nt, nu), 1)
    if transposed:
        lo, hi = jnp.clip(t - w // 2 + 1, 0, s), jnp.clip(t + w // 2 + 1, 0, s)
    else:
        lo, hi = jnp.clip(t - w // 2, 0, s), jnp.clip(t + w - w // 2, 0, s)
    return ((u >= lo) & (u < hi)).astype(f32)


def _pool_cnt(t0, w, nt, s):
    t = t0 + lax.broadcasted_iota(jnp.int32, (nt, 1), 0)
    return (jnp.clip(t + w - w // 2, 0, s) - jnp.clip(t - w // 2, 0, s)).astype(f32)


def _pool_fwd(z, lin, scale, s):
    tp = 256
    nt = s // tp

    def body(up, uc, un, lin_ref, sc_ref, pooled_o, pc_o):
        i = pl.program_id(0)
        ext = jnp.concatenate([up[...], uc[...], un[...]], axis=0)
        for g, w in enumerate(POOL_WINDOWS):
            sl = slice(g * LANES, (g + 1) * LANES)
            band = _band(i * tp, (i - 1) * tp, w, tp, 3 * tp, s, False)
            sm = jnp.dot(band, ext[:, sl], preferred_element_type=f32, precision=lax.Precision.HIGHEST)
            pooled = (sm / _pool_cnt(i * tp, w, tp, s) - uc[:, sl]).astype(bf16)
            pooled_o[:, sl] = pooled
            mixed = _dot(pooled, lin_ref[g].astype(bf16), "nn")
            pc_o[:, sl] = (mixed * sc_ref[:, sl]).astype(bf16)

    cb = C_UC // 512
    blk = lambda f: pl.BlockSpec((tp, 512), lambda i: (f(i), cb))
    row = pl.BlockSpec((tp, 512), lambda i: (i, 0))
    return pl.pallas_call(
        body, name="pool_fwd", grid=(nt,),
        in_specs=[blk(lambda i: jnp.maximum(i - 1, 0)), blk(lambda i: i), blk(lambda i: jnp.minimum(i + 1, nt - 1)),
                  pl.BlockSpec((4, LANES, LANES), lambda i: (0, 0, 0)), pl.BlockSpec((1, 512), lambda i: (0, 0))],
        out_specs=[row, row], out_shape=[SDS((s, 512), bf16), SDS((s, 512), bf16)],
        compiler_params=_cparams())(z, z, z, lin, scale)


def _pool_bwd1(dpc, pooled, lin, scale, s):
    tp = 256

    def body(dpc_ref, pooled_ref, lin_ref, sc_ref, dpn_o, dlin_o, dsc_o):
        i = pl.program_id(0)
        dsc = []
        for g, w in enumerate(POOL_WINDOWS):
            sl = slice(g * LANES, (g + 1) * LANES)
            pooled = pooled_ref[:, sl]
            linb = lin_ref[g].astype(bf16)
            mixed = _dot(pooled, linb, "nn")
            dpc_g = dpc_ref[:, sl]
            dsc.append(jnp.sum(dpc_g * mixed, axis=0, keepdims=True))
            dmixed = (dpc_g * sc_ref[:, sl]).astype(bf16)
            dpn_o[:, sl] = _dot(dmixed, linb, "nt") / _pool_cnt(i * tp, w, tp, s)
            dl = _dot(pooled, dmixed, "tn")

            @pl.when(i == 0)
            def _(g=g, dl=dl):
                dlin_o[g] = dl

            @pl.when(i > 0)
            def _(g=g, dl=dl):
                dlin_o[g] += dl

        dsc = jnp.concatenate(dsc, axis=1)

        @pl.when(i == 0)
        def _():
            dsc_o[...] = dsc

        @pl.when(i > 0)
        def _():
            dsc_o[...] += dsc

    row = pl.BlockSpec((tp, 512), lambda i: (i, 0))
    linspec = pl.BlockSpec((4, LANES, LANES), lambda i: (0, 0, 0))
    one = pl.BlockSpec((1, 512), lambda i: (0, 0))
    return pl.pallas_call(
        body, name="pool_bwd1", grid=(s // tp,), in_specs=[row, row, linspec, one], out_specs=[row, linspec, one],
        out_shape=[SDS((s, 512), f32), SDS((4, LANES, LANES), f32), SDS((1, 512), f32)],
        compiler_params=_cparams())(dpc, pooled, lin, scale)


def _pool_bwd2(dpn, dz, s):
    tp = 256
    nt = s // tp

    def body(dp, dc, dn, dz_in, dz_o):
        i = pl.program_id(0)
        ext = jnp.concatenate([dp[...], dc[...], dn[...]], axis=0)
        for g, w in enumerate(POOL_WINDOWS):
            sl = slice(g * LANES, (g + 1) * LANES)
            band = _band(i * tp, (i - 1) * tp, w, tp, 3 * tp, s, True)
            sm = jnp.dot(band, ext[:, sl], preferred_element_type=f32, precision=lax.Precision.HIGHEST)
            dz_o[:, sl] = (sm - dc[:, sl] * _pool_cnt(i * tp, w, tp, s)).astype(bf16)

    blk = lambda f: pl.BlockSpec((tp, 512), lambda i: (f(i), 0))
    return pl.pallas_call(
        body, name="pool_bwd2", grid=(nt,),
        in_specs=[blk(lambda i: jnp.maximum(i - 1, 0)), blk(lambda i: i), blk(lambda i: jnp.minimum(i + 1, nt - 1)),
                  pl.BlockSpec(memory_space=pl.ANY)],
        out_specs=pl.BlockSpec((tp, 512), lambda i: (i, C_UC // 512)), out_shape=SDS((s, IN_WIDTH), bf16),
        input_output_aliases={3: 0}, compiler_params=_cparams())(dpn, dpn, dpn, dz)


GW = 512


def _gate_fwd(y3, z, b_gate, s):
    ts = 256

    def fn(y0, y1, y2, g0, g1, g2, b0, b1, b2):
        return jax.nn.sigmoid(g0 + b0) * y0 + jax.nn.sigmoid(g1 + b1) * y1 + jax.nn.sigmoid(g2 + b2) * y2

    ysp = lambda b: pl.BlockSpec((None, ts, GW), lambda h, i: (b, i, h))
    gsp = lambda b: pl.BlockSpec((ts, GW), lambda h, i: (i, C_GZ // GW + 2 * b + h))
    bsp = lambda b: pl.BlockSpec((1, GW), lambda h, i: (0, 2 * b + h))
    return _ew("gate_fwd", fn, (2, s // ts), [y3, y3, y3, z, z, z, b_gate, b_gate, b_gate],
               [ysp(0), ysp(1), ysp(2), gsp(0), gsp(1), gsp(2), bsp(0), bsp(1), bsp(2)],
               [SDS((s, D_MODEL), bf16)], [pl.BlockSpec((ts, GW), lambda h, i: (i, h))])[0]


def _gate_bwd(dmerged, y3, z, b_gate, dz, s):
    ts = 256

    def fn(dm, y, gz, b):
        g = jax.nn.sigmoid(gz + b)
        dgz = dm * y * g * (1.0 - g)
        return dm * g, dgz, jnp.sum(dgz, axis=0, keepdims=True)

    return _ew("gate_bwd", fn, (3, 2, s // ts), [dmerged, y3, z, b_gate, dz],
               [pl.BlockSpec((ts, GW), lambda b, h, i: (i, h)), pl.BlockSpec((None, ts, GW), lambda b, h, i: (b, i, h)),
                pl.BlockSpec((ts, GW), lambda b, h, i: (i, C_GZ // GW + 2 * b + h)),
                pl.BlockSpec((1, GW), lambda b, h, i: (0, 2 * b + h)), pl.BlockSpec(memory_space=pl.ANY)],
               [SDS((3, s, D_MODEL), bf16), SDS((s, IN_WIDTH), bf16), SDS((1, 3 * D_MODEL), f32)],
               [pl.BlockSpec((None, ts, GW), lambda b, h, i: (b, i, h)),
                pl.BlockSpec((ts, GW), lambda b, h, i: (i, C_GZ // GW + 2 * b + h)),
                pl.BlockSpec((1, GW), lambda b, h, i: (0, 2 * b + h))], n_acc=1, aliases={4: 1})


def _loss_grad(y, tgt, s):
    ts = 256

    def fn(yt, tt):
        e = yt - tt
        return e * (1.0 / D_MODEL), jnp.sum(e * e, axis=0, keepdims=True) * (0.5 / D_MODEL)

    row = pl.BlockSpec((ts, D_MODEL), lambda i: (i, 0))
    one = pl.BlockSpec((1, D_MODEL), lambda i: (0, 0))
    return _ew("loss_grad", fn, (s // ts,), [y, tgt], [row, row], [SDS((s, D_MODEL), f32), SDS((1, D_MODEL), f32)],
               [row, one], n_acc=1)


TM = 512


def _layer_fwd(x, w, sm, tabs, s, xchg=None):
    nm = s // TM
    hb = _norm_fwd(x, sm["norm_mix"], s)
    z = _mm("mm_z", [hb, w["w_in"]],
            [pl.BlockSpec((TM, D_MODEL), lambda j, i, k: (i, 0)), pl.BlockSpec((None, D_MODEL, 1664), lambda j, i, k: (j, 0, 0))],
            SDS((s, IN_WIDTH), f32), pl.BlockSpec((TM, 1664), lambda j, i, k: (i, j)), (4, nm, 1), 1, "nn", None)
    qa, kd, vd, qb, kb, vb = _prep_fwd(z, tabs, sm["gains"], s)
    (oa, lse_a), landed = _flash_a_fwd(qa, kd, vd, s, xchg)
    ob_parts = [_attn_b_fwd(qb, kb, vb, gi, s) for gi in range(3)]
    ob, lse_b = _merge_b([p[0] for p in ob_parts], [p[1] for p in ob_parts], s)
    pooled, pc = _pool_fwd(z, sm["pool_lin"], sm["pool_scale"], s)

    y3 = _mm("mm_ya", [oa, w["w_branch_a"]],
             [pl.BlockSpec((TM, 512), lambda j, i, k: (i, 0)), pl.BlockSpec((None, 512, 256), lambda j, i, k: (j, 0, 0))],
             SDS((3, s, D_MODEL), f32), pl.BlockSpec((None, TM, 256), lambda j, i, k: (0, i, j)), (4, nm, 1), 1, "nn", None)
    y3 = _mm_alias("mm_yb", ob, w["w_branch_b"], y3, 256, 1, s)
    y3 = _mm_alias("mm_yc", pc, w["w_branch_c"], y3, 512, 2, s)
    merged = _gate_fwd(y3, z, sm["b_gate"], s)
    x1 = _mm("mm_out", [merged, w["w_out"], x],
             [pl.BlockSpec((TM, 256), lambda i, k: (i, k)), pl.BlockSpec((None, 256, D_MODEL), lambda i, k: (k, 0, 0)),
              pl.BlockSpec((TM, D_MODEL), lambda i, k: (i, 0))],
             SDS((s, D_MODEL), f32), pl.BlockSpec((TM, D_MODEL), lambda i, k: (i, 0)), (nm, 4), 4, "nn", (TM, D_MODEL),
             epilogue=lambda acc, xr: acc + xr, n_extra=1)
    h2 = _norm_fwd(x1, sm["norm_ffn"], s)
    g4, u4, a4 = _ffn_up(h2, w["w_ffn_gate"], w["w_ffn_up"], s)
    x2 = _mm("mm_down", [a4, w["w_ffn_down"], x1],
             [pl.BlockSpec((None, TM, FF_SHARD), lambda i, k: (k, i, 0)), pl.BlockSpec((None, FF_SHARD, D_MODEL), lambda i, k: (k, 0, 0)),
              pl.BlockSpec((TM, D_MODEL), lambda i, k: (i, 0))],
             SDS((s, D_MODEL), f32), pl.BlockSpec((TM, D_MODEL), lambda i, k: (i, 0)), (nm, 4), 4, "nn", (TM, D_MODEL),
             epilogue=lambda acc, xr: acc + xr, n_extra=1)
    saved = dict(x=x, hb=hb, z=z, qa=qa, kd=kd, vd=vd, qb=qb, kb=kb, vb=vb, oa=oa, lse_a=lse_a, ob=ob, lse_b=lse_b,
                 pooled=pooled, pc=pc, y3=y3, merged=merged, x1=x1, h2=h2, g4=g4, u4=u4, a4=a4)
    return x2, saved, landed


def _mm_alias(name, a, wt, y3, kdim, b, s):
    nm = s // TM

    def body(a_ref, b_ref, y_in, o_ref):
        o_ref[...] = _dot(a_ref[...].astype(bf16), b_ref[...].astype(bf16), "nn")

    return pl.pallas_call(
        body, name=name, grid=(4, nm),
        in_specs=[pl.BlockSpec((TM, kdim), lambda j, i: (i, 0)), pl.BlockSpec((None, kdim, 256), lambda j, i: (j, 0, 0)),
                  pl.BlockSpec(memory_space=pl.ANY)],
        out_specs=pl.BlockSpec((None, TM, 256), lambda j, i: (b, i, j)), out_shape=SDS((3, s, D_MODEL), f32),
        input_output_aliases={2: 0}, compiler_params=_cparams())(a, wt, y3)


def _ffn_up(h2, wg, wu, s):
    nm = s // TM

    def body(h_ref, wg_ref, wu_ref, g_o, u_o, a_o):
        h = h_ref[...]
        g = _dot(h, wg_ref[...], "nn")
        u = _dot(h, wu_ref[...], "nn")
        g_o[...] = g
        u_o[...] = u
        a_o[...] = (g * jax.nn.sigmoid(g) * u).astype(bf16)

    wsp = pl.BlockSpec((None, D_MODEL, FF_SHARD), lambda j, i: (j, 0, 0))
    osp = pl.BlockSpec((None, TM, FF_SHARD), lambda j, i: (j, i, 0))
    return pl.pallas_call(
        body, name="ffn_up", grid=(4, nm), in_specs=[pl.BlockSpec((TM, D_MODEL), lambda j, i: (i, 0)), wsp, wsp],
        out_specs=[osp, osp, osp],
        out_shape=[SDS((4, s, FF_SHARD), f32), SDS((4, s, FF_SHARD), f32), SDS((4, s, FF_SHARD), bf16)],
        compiler_params=_cparams())(h2, wg, wu)


def _ffn_bwd_act(dx2, wd, g4, u4, s):
    nm = s // TM

    def body(dx_ref, wd_ref, g_ref, u_ref, dg_o, du_o):
        da = _dot(dx_ref[...].astype(bf16), wd_ref[...], "nt")
        g, u = g_ref[...], u_ref[...]
        sg = jax.nn.sigmoid(g)
        dg_o[...] = (da * u * sg * (1.0 + g * (1.0 - sg))).astype(bf16)
        du_o[...] = (da * g * sg).astype(bf16)

    osp = pl.BlockSpec((None, TM, FF_SHARD), lambda j, i: (j, i, 0))
    return pl.pallas_call(
        body, name="ffn_bwd_act", grid=(4, nm),
        in_specs=[pl.BlockSpec((TM, D_MODEL), lambda j, i: (i, 0)), pl.BlockSpec((None, FF_SHARD, D_MODEL), lambda j, i: (j, 0, 0)), osp, osp],
        out_specs=[osp, osp], out_shape=[SDS((4, s, FF_SHARD), bf16), SDS((4, s, FF_SHARD), bf16)],
        compiler_params=_cparams())(dx2, wd, g4, u4)


def _ffn_bwd_dh(dg4, du4, wg, wu, s):
    nm = s // TM

    def body(dg_ref, du_ref, wg_ref, wu_ref, o_ref, acc):
        k = pl.program_id(1)
        part = _dot(dg_ref[...], wg_ref[...], "nt") + _dot(du_ref[...], wu_ref[...], "nt")

        @pl.when(k == 0)
        def _():
            acc[...] = part

        @pl.when(k > 0)
        def _():
            acc[...] += part

        @pl.when(k == 3)
        def _():
            o_ref[...] = acc[...]

    asp = pl.BlockSpec((None, TM, FF_SHARD), lambda i, k: (k, i, 0))
    wsp = pl.BlockSpec((None, D_MODEL, FF_SHARD), lambda i, k: (k, 0, 0))
    return pl.pallas_call(
        body, name="ffn_bwd_dh", grid=(nm, 4), in_specs=[asp, asp, wsp, wsp],
        out_specs=pl.BlockSpec((TM, D_MODEL), lambda i, k: (i, 0)), out_shape=SDS((s, D_MODEL), f32),
        scratch_shapes=[pltpu.VMEM((TM, D_MODEL), f32)], compiler_params=_cparams())(dg4, du4, wg, wu)


def _wgrad(name, a, a_spec, b, b_spec, out_shape, out_block, s, tk=512):
    nk = s // tk
    return _mm(name, [a, b], [a_spec, b_spec], SDS(out_shape, f32),
               pl.BlockSpec((None,) + out_block, lambda j, k: (j, 0, 0)), (4, nk), nk, "tn", out_block)


def _layer_bwd(dx2, sv, w, sm, tabs, s, xchg=None):
    nm = s // TM
    tk = 512
    tok = lambda width: pl.BlockSpec((tk, width), lambda j, k: (k, 0))
    g_wd = _wgrad("wg_down", sv["a4"], pl.BlockSpec((None, tk, FF_SHARD), lambda j, k: (j, k, 0)), dx2, tok(D_MODEL),
                  (4, FF_SHARD, D_MODEL), (FF_SHARD, D_MODEL), s)
    dg4, du4 = _ffn_bwd_act(dx2, w["w_ffn_down"], sv["g4"], sv["u4"], s)
    sh704 = pl.BlockSpec((None, tk, FF_SHARD), lambda j, k: (j, k, 0))
    g_wg = _wgrad("wg_gate", sv["h2"], tok(D_MODEL), dg4, sh704, (4, D_MODEL, FF_SHARD), (D_MODEL, FF_SHARD), s)
    g_wu = _wgrad("wg_up", sv["h2"], tok(D_MODEL), du4, sh704, (4, D_MODEL, FF_SHARD), (D_MODEL, FF_SHARD), s)
    dh2 = _ffn_bwd_dh(dg4, du4, w["w_ffn_gate"], w["w_ffn_up"], s)
    dx1, g_norm_ffn = _norm_bwd(sv["x1"], sm["norm_ffn"], dh2, dx2, s)
    colblk = lambda width: pl.BlockSpec((tk, width), lambda j, k: (k, j))
    g_wo = _wgrad("wg_out", sv["merged"], colblk(256), dx1, tok(D_MODEL), (4, 256, D_MODEL), (256, D_MODEL), s)
    dmerged = _mm("mm_dmerged", [dx1, w["w_out"]],
                  [pl.BlockSpec((TM, D_MODEL), lambda j, i, k: (i, 0)), pl.BlockSpec((None, 256, D_MODEL), lambda j, i, k: (j, 0, 0))],
                  SDS((s, D_MODEL), f32), pl.BlockSpec((TM, 256), lambda j, i, k: (i, j)), (4, nm, 1), 1, "nt", None)
    dz = jnp.zeros((s, IN_WIDTH), bf16)
    dy3, dz, g_bgate = _gate_bwd(dmerged, sv["y3"], sv["z"], sm["b_gate"], dz, s)
    br_in, br_grads = [], []
    for b, (nm_, act, kdim) in enumerate((("a", sv["oa"], 512), ("b", sv["ob"], 256), ("c", sv["pc"], 512))):
        wt = w["w_branch_" + nm_]
        br_grads.append(_wgrad("wg_br" + nm_, act, tok(kdim), dy3,
                               pl.BlockSpec((None, tk, 256), lambda j, k, b=b: (b, k, j)), (4, kdim, 256), (kdim, 256), s))
        br_in.append(_mm("mm_dbr" + nm_, [dy3, wt],
                         [pl.BlockSpec((None, TM, 256), lambda i, k, b=b: (b, i, k)), pl.BlockSpec((None, kdim, 256), lambda i, k: (k, 0, 0))],
                         SDS((s, kdim), f32), pl.BlockSpec((TM, kdim), lambda i, k: (i, 0)), (nm, 4), 4, "nt", (TM, kdim)))
    doa, dob, dpc = br_in
    dpn, g_lin, g_scale = _pool_bwd1(dpc, sv["pooled"], sm["pool_lin"], sm["pool_scale"], s)
    dz = _pool_bwd2(dpn, dz, s)
    delta = _delta_b(dob, sv["ob"], s)
    dqb, dkb, dvb = [], [], []
    for gi in range(3):
        dqb.append(_attn_b_dq(sv["qb"], sv["kb"], sv["vb"], dob, sv["lse_b"], delta, gi, s))
        dk, dv = _attn_b_dkv(sv["qb"], sv["kb"], sv["vb"], dob, sv["lse_b"], delta, gi, s)
        dkb.append(dk)
        dvb.append(dv)
    (dqa, dkd, dvd), landed = _flash_a_bwd(sv["qa"], sv["kd"], sv["vd"], sv["oa"], sv["lse_a"], doa, s, xchg)
    dz, g_gains = _prep_bwd(sv["z"], tabs, sm["gains"], dqa, dkd, dvd, dqb, dkb, dvb, dz, s)
    g_win = _wgrad("wg_in", sv["hb"], tok(D_MODEL), dz, colblk(1664), (4, D_MODEL, 1664), (D_MODEL, 1664), s)
    dh = _mm("mm_dh", [dz, w["w_in"]],
             [pl.BlockSpec((TM, 1664), lambda i, k: (i, k)), pl.BlockSpec((None, D_MODEL, 1664), lambda i, k: (k, 0, 0))],
             SDS((s, D_MODEL), f32), pl.BlockSpec((TM, D_MODEL), lambda i, k: (i, 0)), (nm, 4), 4, "nt", (TM, D_MODEL))
    dx0, g_norm_mix = _norm_bwd(sv["x"], sm["norm_mix"], dh, dx1, s)
    big = dict(w_in=g_win, w_branch_a=br_grads[0], w_branch_b=br_grads[1], w_branch_c=br_grads[2], w_out=g_wo,
               w_ffn_gate=g_wg, w_ffn_up=g_wu, w_ffn_down=g_wd)
    gg = g_gains[0:4, :HEAD_DIM] + g_gains[0:4, HEAD_DIM:]
    small = jnp.concatenate([g_norm_mix.reshape(-1), g_bgate.reshape(-1), gg.reshape(-1), g_lin.reshape(-1),
                             g_scale.reshape(-1), g_norm_ffn.reshape(-1)]).reshape(SMALL_ROWS, LANES)
    return dx0, big, small, landed


def _mesh_pos():
    return lax.axis_index("x"), lax.axis_index("y"), lax.axis_index("c")


def _chip_copies(src, dst, pick_src, send_sems, recv_sems, loc_sems, receiving=True):
    n = len(src)
    x, y, c = _mesh_pos()
    me = 2 * x + y
    local = [pltpu.make_async_copy(pick_src(src[t], me), dst[t].at[me], loc_sems.at[t]) for t in range(n)]
    push, recv = [], []
    for j, (px, py) in enumerate([(1 - x, y), (x, 1 - y), (1 - x, 1 - y)]):
        peer = 2 * px + py
        for t in range(n):
            sems = dict(send_sem=send_sems.at[j * n + t], recv_sem=recv_sems.at[j * n + t],
                        device_id=(px, py, c), device_id_type=MESH)
            push.append(pltpu.make_async_remote_copy(src_ref=pick_src(src[t], peer), dst_ref=dst[t].at[me], **sems))
            if receiving:
                recv.append(pltpu.make_async_remote_copy(src_ref=pick_src(src[t], peer), dst_ref=dst[t].at[peer], **sems))
    return local, push, recv


def _chip_start(copies):
    local, push, _ = copies
    for cp in local + push:
        cp.start()


def _chip_wait(copies):
    local, push, recv = copies
    for cp in push:
        cp.wait_send()
    for cp in recv:
        cp.wait_recv()
    for cp in local:
        cp.wait()


def _chip_sems(n):
    return [pltpu.SemaphoreType.DMA((3 * n,)), pltpu.SemaphoreType.DMA((3 * n,)), pltpu.SemaphoreType.DMA((n,))]


def _chip_landing(srcs):
    return [SDS((N_CHIPS,) + tuple(a.shape[1:]), a.dtype) for a in srcs]


def _chip_exchange(name, srcs, pick_src):
    n = len(srcs)
    any_spec = pl.BlockSpec(memory_space=pl.ANY)

    def body(*refs):
        copies = _chip_copies(refs[:n], refs[n:2 * n], pick_src, *refs[2 * n:])
        _chip_start(copies)
        _chip_wait(copies)

    return pl.pallas_call(
        body, name=name, in_specs=[any_spec] * n, out_specs=[any_spec] * n, out_shape=_chip_landing(srcs),
        scratch_shapes=_chip_sems(n), compiler_params=pltpu.CompilerParams())(*srcs)


def _sibling_exchange(name, srcs):
    n = len(srcs)
    any_spec = pl.BlockSpec(memory_space=pl.ANY)

    def body(*refs):
        src, dst = refs[:n], refs[n:2 * n]
        send_sems, recv_sems = refs[2 * n:]
        x, y, c = _mesh_pos()
        cps = [pltpu.make_async_remote_copy(src_ref=src[t], dst_ref=dst[t], send_sem=send_sems.at[t], recv_sem=recv_sems.at[t],
                                            device_id=(x, y, 1 - c), device_id_type=MESH) for t in range(n)]
        for cp in cps:
            cp.start()
        for cp in cps:
            cp.wait()

    return pl.pallas_call(
        body, name=name, in_specs=[any_spec] * n, out_specs=[any_spec] * n,
        out_shape=[SDS(a.shape, a.dtype) for a in srcs],
        scratch_shapes=[pltpu.SemaphoreType.DMA((n,)), pltpu.SemaphoreType.DMA((n,))],
        compiler_params=pltpu.CompilerParams())(*srcs)


def _all_exchange(name, src):
    any_spec = pl.BlockSpec(memory_space=pl.ANY)

    def body(src_ref, dst_ref, send_sems, recv_sems, loc_sem):
        x, y, c = _mesh_pos()
        me = 4 * x + 2 * y + c
        peers = [(x ^ ((k >> 2) & 1), y ^ ((k >> 1) & 1), c ^ (k & 1)) for k in range(1, N_DEV)]
        local = pltpu.make_async_copy(src_ref, dst_ref.at[me], loc_sem)
        local.start()
        cps = [pltpu.make_async_remote_copy(src_ref=src_ref, dst_ref=dst_ref.at[me], send_sem=send_sems.at[k],
                                            recv_sem=recv_sems.at[k], device_id=p, device_id_type=MESH)
               for k, p in enumerate(peers)]
        for cp in cps:
            cp.start()
        for cp in cps:
            cp.wait_send()
        for k, (px, py, pc) in enumerate(peers):
            pltpu.make_async_remote_copy(src_ref=src_ref, dst_ref=dst_ref.at[4 * px + 2 * py + pc], send_sem=send_sems.at[k],
                                         recv_sem=recv_sems.at[k], device_id=(px, py, pc), device_id_type=MESH).wait_recv()
        local.wait()

    return pl.pallas_call(
        body, name=name, in_specs=[any_spec], out_specs=any_spec, out_shape=SDS((N_DEV,) + src.shape, src.dtype),
        scratch_shapes=[pltpu.SemaphoreType.DMA((N_DEV - 1,)), pltpu.SemaphoreType.DMA((N_DEV - 1,)), pltpu.SemaphoreType.DMA],
        compiler_params=pltpu.CompilerParams())(src)


def _cast_bf16(name, a):
    l, r, c = a.shape
    tr = _row_tile(r, c)
    spec = pl.BlockSpec((None, tr, c), lambda i, j: (i, j, 0))
    return _ew(name, lambda t: t, (l, r // tr), [a], [spec], [SDS(a.shape, bf16)], [spec])[0]


def _sum4(name, land):
    _, r, c = land.shape
    tr = _row_tile(r, c)
    return _ew(name, lambda a, b, cc, d: ((a + b) + cc) + d, (r // tr,), [land] * 4,
               [pl.BlockSpec((None, tr, c), lambda i, k=k: (k, i, 0)) for k in range(4)],
               [SDS((r, c), f32)], [pl.BlockSpec((tr, c), lambda i: (i, 0))])[0]


def _adam_math(g, w, m, v):
    m2 = ADAM_B1 * m + (1.0 - ADAM_B1) * g
    v2 = ADAM_B2 * v + (1.0 - ADAM_B2) * (g * g)
    m_hat = m2 / (1.0 - ADAM_B1 ** ADAM_STEP)
    v_hat = v2 / (1.0 - ADAM_B2 ** ADAM_STEP)
    delta = -ADAM_LR * (m_hat / (jnp.sqrt(v_hat) + ADAM_EPS) + ADAM_WD * w)
    return delta, m2, v2


def _adamw_big(name, p_own, p_sib, w, m, v):
    l, r, c = w.shape
    tr = _row_tile(r, c)

    def fn(a, b, wt, mt, vt):
        g = a + b
        return (g,) + _adam_math(g, wt, mt, vt)

    spec = pl.BlockSpec((None, tr, c), lambda i, j: (i, j, 0))
    return _ew(name, fn, (l, r // tr), [p_own, p_sib, w, m, v], [spec] * 5, [SDS(w.shape, f32)] * 4, [spec] * 4)


def _adamw_small(land, w, m, v):
    r = w.shape[0]
    tr = _row_tile(r, LANES)

    def fn(*t):
        g = t[0]
        for k in range(1, N_DEV):
            g = g + t[k]
        return (g,) + _adam_math(g, *t[N_DEV:])

    row = pl.BlockSpec((tr, LANES), lambda i: (i, 0))
    return _ew("adamw_small", fn, (r // tr,), [land] * N_DEV + [w, m, v],
               [pl.BlockSpec((None, tr, LANES), lambda i, k=k: (k, i, 0)) for k in range(N_DEV)] + [row] * 3,
               [SDS(w.shape, f32)] * 4, [row] * 4)


def _pack_small(d):
    return jnp.concatenate([d[k].reshape(DEPTH, -1) for k in SMALL], axis=1).reshape(DEPTH * SMALL_ROWS, LANES)


def _unpack_small(a, shapes):
    a = a.reshape(DEPTH, SMALL_ROWS * LANES)
    out, off = {}, 0
    for k, n in zip(SMALL, SMALL_SIZES):
        out[k] = a[:, off:off + n].reshape(shapes[k])
        off += n
    return out


def kernel(x, norm_mix, w_in, b_gate, qn_a, kn_a, qn_b, kn_b, pool_lin, pool_scale, w_branch_a, w_branch_b, w_branch_c, w_out, norm_ffn, w_ffn_gate, w_ffn_up, w_ffn_down, loss_target, m_norm_mix, m_w_in, m_b_gate, m_qn_a, m_kn_a, m_qn_b, m_kn_b, m_pool_lin, m_pool_scale, m_w_branch_a, m_w_branch_b, m_w_branch_c, m_w_out, m_norm_ffn, m_w_ffn_gate, m_w_ffn_up, m_w_ffn_down, v_norm_mix, v_w_in, v_b_gate, v_qn_a, v_kn_a, v_qn_b, v_kn_b, v_pool_lin, v_pool_scale, v_w_branch_a, v_w_branch_b, v_w_branch_c, v_w_out, v_norm_ffn, v_w_ffn_gate, v_w_ffn_up, v_w_ffn_down):
    args = dict(locals())
    s = x.shape[1]
    wts = {k: args[k] for k in WEIGHT_ORDER}
    mom = {k: args["m_" + k] for k in WEIGHT_ORDER}
    var = {k: args["v_" + k] for k in WEIGHT_ORDER}
    tabs = _rope_tables(s)

    shards16 = [_cast_bf16("cast_" + k, wts[k]) for k in BIG]
    whole = lambda ref, peer: ref.at[0]
    full = [None] * DEPTH
    full[0] = dict(zip(BIG, _chip_exchange("gather", [a[0:1] for a in shards16], whole)))

    def small_of(l):
        tile2 = lambda a: jnp.concatenate([a, a])
        gains = jnp.stack([tile2(qn_a[l]), tile2(kn_a[l]), tile2(qn_b[l]), tile2(kn_b[l])])
        return dict(norm_mix=norm_mix[l][None], norm_ffn=norm_ffn[l][None], b_gate=b_gate[l][None], gains=gains,
                    pool_lin=pool_lin[l], pool_scale=pool_scale[l][None])

    xs = x[0]
    saved = []
    for l in range(DEPTH):
        xchg = ([a[l + 1:l + 2] for a in shards16], whole) if l + 1 < DEPTH else None
        xs, sv, landed = _layer_fwd(xs, full[l], small_of(l), tabs, s, xchg)
        saved.append(sv)
        if xchg:
            full[l + 1] = dict(zip(BIG, landed))
    dy, loss_rows = _loss_grad(xs, loss_target[0], s)
    loss = lax.psum(jnp.sum(loss_rows), ("x", "y", "c"))

    part = {k: [None] * DEPTH for k in BIG}
    small_g = [None] * DEPTH
    to_chip = lambda ref, peer: ref.at[peer]
    pending = None
    for l in reversed(range(DEPTH)):
        xchg = (pending, to_chip) if pending is not None else None
        dy, big, small_g[l], landed = _layer_bwd(dy, saved[l], full[l], small_of(l), tabs, s, xchg)
        for k, a in zip(BIG, landed):
            part[k][l + 1] = _sum4("sum4_" + k, a)
        pending = [big[k] for k in BIG]
    for k, a in zip(BIG, _chip_exchange("grad_scatter", pending, to_chip)):
        part[k][0] = _sum4("sum4_" + k, a)

    p_own = [jnp.stack(part[k]) for k in BIG]
    p_sib = _sibling_exchange("grad_sibling", p_own)
    outs = {}
    for k, a, b in zip(BIG, p_own, p_sib):
        outs[k] = _adamw_big("adamw_" + k, a, b, wts[k], mom[k], var[k])

    land_s = _all_exchange("small_allgather", jnp.concatenate(small_g, axis=0))
    res_s = _adamw_small(land_s, _pack_small(wts), _pack_small(mom), _pack_small(var))
    shapes = {k: wts[k].shape for k in SMALL}
    small_out = [_unpack_small(r, shapes) for r in res_s]
    for k in SMALL:
        outs[k] = tuple(so[k] for so in small_out)

    flat = [loss, dy[None]]
    for idx in range(4):
        flat += [outs[k][idx] for k in WEIGHT_ORDER]
    return tuple(flat)
```

```python
import functools

import jax
import jax.numpy as jnp
from jax import lax
from jax.experimental import pallas as pl
from jax.experimental.pallas import tpu as pltpu

f32, bf16 = jnp.float32, jnp.bfloat16
SDS = jax.ShapeDtypeStruct

DEPTH = 4
D_MODEL = 1024
HEAD_DIM = 64
LANES = 128
GRID_W = 64
ROPE_THETA = 10000.0
EPS = 1e-6
NEG_INF = -1e30
SCALE = HEAD_DIM ** -0.5
B_GROUP_CFG = ((128, 1), (512, 4), (2048, 16))
POOL_WINDOWS = (2, 4, 8, 16)
N_CHIPS = 4
N_DEV = 8
C_QA, C_KA, C_VA, C_QB, C_KB, C_VB, C_UC, C_GZ = 0, 512, 640, 768, 1536, 2304, 3072, 3584
IN_WIDTH = 6656
FF_SHARD = 704
ADAM_LR, ADAM_B1, ADAM_B2, ADAM_EPS, ADAM_WD, ADAM_STEP = 0.001, 0.9, 0.999, 1e-08, 0.01, 10
VMEM_LIMIT = 56 * 1024 * 1024
MESH = pl.DeviceIdType.MESH

BIG = ("w_in", "w_branch_a", "w_branch_b", "w_branch_c", "w_out", "w_ffn_gate", "w_ffn_up", "w_ffn_down")
SMALL = ("norm_mix", "b_gate", "qn_a", "kn_a", "qn_b", "kn_b", "pool_lin", "pool_scale", "norm_ffn")
SMALL_SIZES = (1024, 3072, 64, 64, 64, 64, 65536, 512, 1024)
SMALL_ROWS = sum(SMALL_SIZES) // LANES
WEIGHT_ORDER = ("norm_mix", "w_in", "b_gate", "qn_a", "kn_a", "qn_b", "kn_b", "pool_lin", "pool_scale",
                "w_branch_a", "w_branch_b", "w_branch_c", "w_out", "norm_ffn", "w_ffn_gate", "w_ffn_up", "w_ffn_down")


def _cparams():
    return pltpu.CompilerParams(vmem_limit_bytes=VMEM_LIMIT)


def _lo_mask(shape):
    return lax.broadcasted_iota(jnp.int32, shape, len(shape) - 1) < HEAD_DIM


def _dot(a, b, dims):
    dn = {"nn": (((1,), (0,)), ((), ())), "nt": (((1,), (1,)), ((), ())), "tn": (((0,), (0,)), ((), ()))}[dims]
    return lax.dot_general(a, b, dn, preferred_element_type=f32)


def _mm(name, ins, in_specs, out_shape, out_spec, grid, nk, dims, acc_shape, epilogue=None, n_extra=0, aliases=None):
    multi = isinstance(out_shape, (list, tuple))
    n_out = len(out_shape) if multi else 1

    def body(*refs):
        a_ref, b_ref = refs[0], refs[1]
        extra = refs[2:2 + n_extra]
        outs = refs[2 + n_extra:2 + n_extra + n_out]
        k = pl.program_id(len(grid) - 1)
        part = _dot(a_ref[...].astype(bf16), b_ref[...].astype(bf16), dims)

        def finish(acc):
            res = epilogue(acc, *[e[...] for e in extra]) if epilogue is not None else acc
            res = res if isinstance(res, (list, tuple)) else (res,)
            for o, r in zip(outs, res):
                o[...] = r.astype(o.dtype)

        if nk == 1:
            finish(part)
        else:
            acc_ref = refs[-1]

            @pl.when(k == 0)
            def _():
                acc_ref[...] = part

            @pl.when(k > 0)
            def _():
                acc_ref[...] += part

            @pl.when(k == nk - 1)
            def _():
                finish(acc_ref[...])

    return pl.pallas_call(
        body, name=name, grid=grid, in_specs=in_specs,
        out_specs=list(out_spec) if multi else out_spec,
        out_shape=list(out_shape) if multi else out_shape,
        scratch_shapes=[] if nk == 1 else [pltpu.VMEM(acc_shape, f32)],
        input_output_aliases=aliases or {}, compiler_params=_cparams())(*ins)


def _ew(name, fn, grid, ins, in_specs, out_shapes, out_specs, n_acc=0, aliases=None):
    n_in = len(ins)
    n_out = len(out_shapes) - n_acc
    n_alias = len(aliases or {})

    def body(*refs):
        in_refs = refs[:n_in - n_alias] if n_alias else refs[:n_in]
        out_refs = refs[n_in:n_in + n_out]
        acc_refs = refs[n_in + n_out:]
        i = pl.program_id(len(grid) - 1)
        res = fn(*[r[...] for r in in_refs])
        res = res if isinstance(res, (list, tuple)) else (res,)
        for o, r in zip(out_refs, res[:n_out]):
            o[...] = r.astype(o.dtype)
        for a, r in zip(acc_refs, res[n_out:]):
            @pl.when(i == 0)
            def _(a=a, r=r):
                a[...] = r.astype(a.dtype)

            @pl.when(i > 0)
            def _(a=a, r=r):
                a[...] += r.astype(a.dtype)

    return pl.pallas_call(body, name=name, grid=grid, in_specs=in_specs, out_specs=out_specs, out_shape=out_shapes,
                          input_output_aliases=aliases or {}, compiler_params=_cparams())(*ins)


EW_BLOCK_ELEMS = 128 * 1024


def _row_tile(rows, cols):
    step = 16 if rows % 16 == 0 else 8
    return max(t for t in range(step, rows + 1, step) if rows % t == 0 and (t * cols <= EW_BLOCK_ELEMS or t == step))


def _rope_tables(s):
    t = jnp.arange(s)
    inv_ax = ROPE_THETA ** (-jnp.arange(0, HEAD_DIM // 2, 2, dtype=f32) / (HEAD_DIM // 2))
    inv_sq = ROPE_THETA ** (-jnp.arange(0, HEAD_DIM, 2, dtype=f32) / HEAD_DIM)
    ang_row = (t // GRID_W).astype(f32)[:, None] * inv_ax[None, :]
    ang_col = (t % GRID_W).astype(f32)[:, None] * inv_ax[None, :]
    ang_seq = t.astype(f32)[:, None] * inv_sq[None, :]
    a_ax = jnp.concatenate([ang_row, ang_row, ang_col, ang_col], axis=1)
    sg_ax = jnp.concatenate([-jnp.ones(16), jnp.ones(16), -jnp.ones(16), jnp.ones(16)]).astype(f32)
    a_sq = jnp.concatenate([ang_seq, ang_seq], axis=1)
    sg_sq = jnp.concatenate([-jnp.ones(32), jnp.ones(32)]).astype(f32)
    two = lambda a: jnp.concatenate([a, a], axis=1)
    return (two(jnp.cos(a_ax)), two(jnp.sin(a_ax) * sg_ax), two(jnp.cos(a_sq)), two(jnp.sin(a_sq) * sg_sq))


def _head_stat(v, lo):
    s0 = jnp.sum(jnp.where(lo, v, 0.0), axis=1, keepdims=True)
    s1 = jnp.sum(jnp.where(lo, 0.0, v), axis=1, keepdims=True)
    return jnp.where(lo, s0, s1)


def _partner(y, off):
    lane = lax.broadcasted_iota(jnp.int32, y.shape, 1)
    first = (lane & (2 * off - 1)) < off
    return jnp.where(first, pltpu.roll(y, LANES - off, 1), pltpu.roll(y, off, 1))


def _normrope(xc, g, cos, sin, off):
    lo = _lo_mask(xc.shape)
    r = lax.rsqrt(_head_stat(xc * xc, lo) * (1.0 / HEAD_DIM) + EPS)
    y = xc * r * g
    return y * cos + _partner(y, off) * sin


def _normrope_bwd(xc, g, cos, sin, off, drot):
    lo = _lo_mask(xc.shape)
    r = lax.rsqrt(_head_stat(xc * xc, lo) * (1.0 / HEAD_DIM) + EPS)
    n = xc * r
    dy = drot * cos + _partner(drot * sin, off)
    dg = jnp.sum(dy * n, axis=0, keepdims=True)
    dn = dy * g
    dx = r * (dn - n * (_head_stat(dn * n, lo) * (1.0 / HEAD_DIM)))
    return dx, dg


def _norm_fwd(x, g, s):
    ts = 256

    def fn(xt, gt):
        r = lax.rsqrt(jnp.mean(xt * xt, axis=1, keepdims=True) + EPS)
        return xt * r * gt

    return _ew("norm_fwd", fn, (s // ts,), [x, g],
               [pl.BlockSpec((ts, D_MODEL), lambda i: (i, 0)), pl.BlockSpec((1, D_MODEL), lambda i: (0, 0))],
               [SDS((s, D_MODEL), bf16)], [pl.BlockSpec((ts, D_MODEL), lambda i: (i, 0))])[0]


def _norm_bwd(x, g, dh, dres, s):
    ts = 256

    def fn(xt, gt, dht, drt):
        r = lax.rsqrt(jnp.mean(xt * xt, axis=1, keepdims=True) + EPS)
        n = xt * r
        dn = dht * gt
        dx = drt + r * (dn - n * jnp.mean(dn * n, axis=1, keepdims=True))
        return dx, jnp.sum(dht * n, axis=0, keepdims=True)

    row = pl.BlockSpec((ts, D_MODEL), lambda i: (i, 0))
    one = pl.BlockSpec((1, D_MODEL), lambda i: (0, 0))
    return _ew("norm_bwd", fn, (s // ts,), [x, g, dh, dres], [row, one, row, row],
               [SDS((s, D_MODEL), f32), SDS((1, D_MODEL), f32)], [row, one], n_acc=1)


def _prep_fwd(z, tabs, gains, s):
    ts = 256
    cos_a, sin_a, cos_b, sin_b = tabs

    def body(za_ref, zq_ref, zk_ref, zv_ref, ca, sa, cb, sb, g_ref, qa_o, kd_o, vd_o, qb_o, kb_o, vb_o):
        lo = _lo_mask((ts, LANES))
        g = g_ref[...]
        ca_, sa_, cb_, sb_ = ca[...], sa[...], cb[...], sb[...]
        for c in range(4):
            qa_o[:, c * LANES:(c + 1) * LANES] = _normrope(za_ref[:, c * LANES:(c + 1) * LANES], g[0:1], ca_, sa_, 16).astype(bf16)
        k = _normrope(za_ref[:, C_KA:C_KA + LANES], g[1:2], ca_, sa_, 16)
        kr = pltpu.roll(k, HEAD_DIM, 1)
        kd_o[0] = k.astype(bf16)
        kd_o[1] = kr.astype(bf16)
        v = za_ref[:, C_VA:C_VA + LANES]
        ones_col = (lax.broadcasted_iota(jnp.int32, v.shape, 1) == HEAD_DIM).astype(f32)
        vd_o[0] = jnp.where(lo, v, ones_col).astype(bf16)
        vd_o[1] = jnp.where(lo, pltpu.roll(v, HEAD_DIM, 1), ones_col).astype(bf16)
        for c in range(6):
            sl = slice(c * LANES, (c + 1) * LANES)
            qb_o[:, sl] = _normrope(zq_ref[:, sl], g[2:3], cb_, sb_, 32).astype(bf16)
            kb_o[:, sl] = _normrope(zk_ref[:, sl], g[3:4], cb_, sb_, 32).astype(bf16)
        vb_o[...] = zv_ref[...].astype(bf16)

    w = 768
    zspec = lambda cb: pl.BlockSpec((ts, w), lambda i: (i, cb))
    tab = pl.BlockSpec((ts, LANES), lambda i: (i, 0))
    dup = pl.BlockSpec((2, ts, LANES), lambda i: (0, i, 0))
    return pl.pallas_call(
        body, name="prep_fwd", grid=(s // ts,),
        in_specs=[zspec(0), zspec(1), zspec(2), zspec(3), tab, tab, tab, tab, pl.BlockSpec((4, LANES), lambda i: (0, 0))],
        out_specs=[pl.BlockSpec((ts, 512), lambda i: (i, 0)), dup, dup, zspec(0), zspec(0), zspec(0)],
        out_shape=[SDS((s, 512), bf16), SDS((2, s, LANES), bf16), SDS((2, s, LANES), bf16),
                   SDS((s, w), bf16), SDS((s, w), bf16), SDS((s, w), bf16)],
        compiler_params=_cparams())(z, z, z, z, cos_a, sin_a, cos_b, sin_b, gains)


def _prep_bwd(z, tabs, gains, dqa, dkd, dvd, dqb, dkb, dvb, dz, s):
    ts = 256
    cos_a, sin_a, cos_b, sin_b = tabs

    def body(za_ref, zq_ref, zk_ref, ca, sa, cb, sb, g_ref, dqa_r, dkd_r, dvd_r,
             dq0, dq1, dq2, dk0, dk1, dk2, dv0, dv1, dv2, dz_in, dz_o, dg_o):
        i = pl.program_id(0)
        lo = _lo_mask((ts, LANES))
        g = g_ref[...]
        ca_, sa_, cb_, sb_ = ca[...], sa[...], cb[...], sb[...]
        dg = [jnp.zeros((1, LANES), f32) for _ in range(4)]
        for c in range(4):
            sl = slice(c * LANES, (c + 1) * LANES)
            dx, d = _normrope_bwd(za_ref[:, sl], g[0:1], ca_, sa_, 16, dqa_r[:, sl])
            dz_o[:, sl] = dx.astype(bf16)
            dg[0] += d
        dk = jnp.where(lo, dkd_r[0], pltpu.roll(dkd_r[1], HEAD_DIM, 1))
        dx, d = _normrope_bwd(za_ref[:, C_KA:C_KA + LANES], g[1:2], ca_, sa_, 16, dk)
        dz_o[:, C_KA:C_KA + LANES] = dx.astype(bf16)
        dg[1] += d
        dz_o[:, C_VA:C_VA + LANES] = jnp.where(lo, dvd_r[0], pltpu.roll(dvd_r[1], HEAD_DIM, 1)).astype(bf16)
        dqs, dks, dvs = (dq0, dq1, dq2), (dk0, dk1, dk2), (dv0, dv1, dv2)
        for c in range(6):
            sl = slice(c * LANES, (c + 1) * LANES)
            gsl = slice((c % 2) * LANES, (c % 2 + 1) * LANES)
            dx, d = _normrope_bwd(zq_ref[:, sl], g[2:3], cb_, sb_, 32, dqs[c // 2][:, gsl])
            dz_o[:, C_QB + c * LANES:C_QB + (c + 1) * LANES] = dx.astype(bf16)
            dg[2] += d
            dx, d = _normrope_bwd(zk_ref[:, sl], g[3:4], cb_, sb_, 32, dks[c // 2][:, gsl])
            dz_o[:, C_KB + c * LANES:C_KB + (c + 1) * LANES] = dx.astype(bf16)
            dg[3] += d
            dz_o[:, C_VB + c * LANES:C_VB + (c + 1) * LANES] = dvs[c // 2][:, gsl].astype(bf16)
        dgs = jnp.concatenate(dg + [jnp.zeros((4, LANES), f32)], axis=0)

        @pl.when(i == 0)
        def _():
            dg_o[...] = dgs

        @pl.when(i > 0)
        def _():
            dg_o[...] += dgs

    w = 768
    zspec = lambda cb: pl.BlockSpec((ts, w), lambda i: (i, cb))
    tab = pl.BlockSpec((ts, LANES), lambda i: (i, 0))
    dup = pl.BlockSpec((2, ts, LANES), lambda i: (0, i, 0))
    grp = pl.BlockSpec((ts, 256), lambda i: (i, 0))
    dzo, dgo = pl.pallas_call(
        body, name="prep_bwd", grid=(s // ts,),
        in_specs=[zspec(0), zspec(1), zspec(2), tab, tab, tab, tab, pl.BlockSpec((4, LANES), lambda i: (0, 0)),
                  pl.BlockSpec((ts, 512), lambda i: (i, 0)), dup, dup] + [grp] * 9 + [pl.BlockSpec(memory_space=pl.ANY)],
        out_specs=[pl.BlockSpec((ts, C_UC), lambda i: (i, 0)), pl.BlockSpec((8, LANES), lambda i: (0, 0))],
        out_shape=[SDS((s, IN_WIDTH), bf16), SDS((8, LANES), f32)],
        input_output_aliases={20: 0}, compiler_params=_cparams())(
            z, z, z, cos_a, sin_a, cos_b, sin_b, gains, dqa, dkd, dvd, *dqb, *dkb, *dvb, dz)
    return dzo, dgo


def _stack_heads(x, lo, dtype):
    z = jnp.zeros_like(x)
    return jnp.concatenate([jnp.where(lo, x, z), jnp.where(lo, z, x)], axis=0).astype(dtype)


def _rows_of(v, lo, kind):
    if kind == "max":
        a = jnp.max(jnp.where(lo, v, NEG_INF * 10), axis=1, keepdims=True)
        b = jnp.max(jnp.where(lo, NEG_INF * 10, v), axis=1, keepdims=True)
    else:
        a = jnp.sum(jnp.where(lo, v, 0.0), axis=1, keepdims=True) * (1.0 / HEAD_DIM)
        b = jnp.sum(jnp.where(lo, 0.0, v), axis=1, keepdims=True) * (1.0 / HEAD_DIM)
    return jnp.concatenate([a, b], axis=0)


def _stack_low(x, lo, dtype, scale=None):
    x = x.astype(f32)
    if scale is not None:
        x = x * scale
    z = jnp.zeros_like(x)
    return jnp.concatenate([jnp.where(lo, x, z), jnp.where(lo, pltpu.roll(x, HEAD_DIM, 1), z)], axis=0).astype(dtype)


def _unstack_low(xs, lo, rows):
    return jnp.where(lo, xs[:rows], pltpu.roll(xs[rows:], HEAD_DIM, 1))


FA_TQ, FA_TK = 256, 1024


def _host_exchange(body, n_in, n_out, grid, xchg):
    if xchg is None:
        return body, [], [], [], []
    srcs, pick_src = xchg
    n = len(srcs)
    any_spec = pl.BlockSpec(memory_space=pl.ANY)

    def hosted(*refs):
        ins, src = refs[:n_in], refs[n_in:n_in + n]
        outs, dst = refs[n_in + n:n_in + n + n_out], refs[n_in + n + n_out:n_in + 2 * n + n_out]
        scratch, sems = refs[n_in + 2 * n + n_out:-3], refs[-3:]
        ids = [pl.program_id(a) for a in range(len(grid))]
        first = functools.reduce(jnp.logical_and, [i == 0 for i in ids])
        last = functools.reduce(jnp.logical_and, [i == g - 1 for i, g in zip(ids, grid)])

        @pl.when(first)
        def _():
            _chip_start(_chip_copies(src, dst, pick_src, *sems, receiving=False))

        body(*ins, *outs, *scratch)

        @pl.when(last)
        def _():
            _chip_wait(_chip_copies(src, dst, pick_src, *sems))

    return hosted, [any_spec] * n, [any_spec] * n, _chip_landing(srcs), _chip_sems(n)


def _flash_a_fwd(qa, kd, vd, s, xchg=None):
    tq, tk = FA_TQ, 2 * FA_TK
    nk = s // tk

    def body(q_ref, k_ref, v_ref, o_ref, lse_ref, qs, m_s, acc):
        lo = _lo_mask((tq, LANES))
        qs[...] = _stack_low(q_ref[...], lo, bf16, SCALE)
        m_s[...] = jnp.full(m_s.shape, NEG_INF, f32)
        acc[...] = jnp.zeros(acc.shape, f32)

        def chunk(j, carry):
            rows = pl.ds(pl.multiple_of(j * tk, tk), tk)
            sc = _dot(qs[...], k_ref[rows, :], "nt")
            m_prev = m_s[...]
            m_new = jnp.maximum(m_prev, jnp.max(sc, axis=1, keepdims=True))
            p = jnp.exp(sc - m_new).astype(bf16)
            acc[...] = acc[...] * jnp.exp(m_prev - m_new) + _dot(p, v_ref[rows, :], "nn")
            m_s[...] = m_new
            return carry

        lax.fori_loop(0, nk, chunk, 0, unroll=2)
        a = acc[...]
        lane = lax.broadcasted_iota(jnp.int32, a.shape, 1)
        l = jnp.sum(jnp.where(lane == HEAD_DIM, a, 0.0), axis=1, keepdims=True)
        o_ref[...] = _unstack_low(a / l, lo, tq).astype(bf16)
        lse = m_s[...] + jnp.log(l)
        lse_ref[...] = jnp.where(lo, lse[:tq], lse[tq:])

    kv = pl.BlockSpec((None, s, LANES), lambda p, i: (p // 2, 0, 0))
    qo = pl.BlockSpec((tq, LANES), lambda p, i: (i, p))
    grid = (4, s // tq)
    hosted, xi, xo, xs, xsem = _host_exchange(body, 3, 2, grid, xchg)
    res = pl.pallas_call(
        hosted, name="flash_a_fwd" + ("_x" if xchg else ""), grid=grid, in_specs=[qo, kv, kv] + xi, out_specs=[qo, qo] + xo,
        out_shape=[SDS((s, 512), bf16), SDS((s, 512), f32)] + xs,
        scratch_shapes=[pltpu.VMEM((2 * tq, LANES), bf16), pltpu.VMEM((2 * tq, 1), f32),
                        pltpu.VMEM((2 * tq, LANES), f32)] + xsem,
        compiler_params=_cparams())(qa, kd, vd, *(xchg[0] if xchg else []))
    return res[:2], res[2:]


def _flash_a_bwd(qa, kd, vd, oa, lse, doa, s, xchg=None):
    tq, tk = FA_TQ, FA_TK
    nk = s // tk

    def body(q_ref, do_ref, o_ref, lse_ref, k_ref, v_ref, dq_ref, dk_ref, dv_ref, qs, dos, lse_s, dl_s, dq_s):
        i = pl.program_id(1)
        lo = _lo_mask((tq, LANES))

        @pl.when(i == 0)
        def _():
            dk_ref[...] = jnp.zeros(dk_ref.shape, f32)
            dv_ref[...] = jnp.zeros(dv_ref.shape, f32)

        for pp in range(2):
            sl = slice(pp * LANES, (pp + 1) * LANES)
            do = do_ref[:, sl]
            qs[...] = _stack_low(q_ref[:, sl], lo, bf16, SCALE)
            dos[...] = _stack_low(do, lo, bf16)
            dl_s[...] = _rows_of(do * o_ref[:, sl].astype(f32), lo, "sum") * HEAD_DIM
            lse_s[...] = _rows_of(lse_ref[:, sl], lo, "max")
            dq_s[...] = jnp.zeros(dq_s.shape, f32)

            def chunk(j, carry):
                rows = pl.ds(pl.multiple_of(j * tk, tk), tk)
                k, v = k_ref[rows, :], v_ref[rows, :]
                q_, do_ = qs[...], dos[...]
                p = jnp.exp(_dot(q_, k, "nt") - lse_s[...])
                ds = (p * (_dot(do_, v, "nt") - dl_s[...])).astype(bf16)
                dv_ref[rows, :] += _dot(p.astype(bf16), do_, "tn")
                dk_ref[rows, :] += _dot(ds, q_, "tn")
                dq_s[...] += _dot(ds, k, "nn")
                return carry

            lax.fori_loop(0, nk, chunk, 0)
            dq_ref[:, sl] = _unstack_low(dq_s[...], lo, tq) * SCALE

    grp = lambda: pl.BlockSpec((tq, 2 * LANES), lambda c, i: (i, c))
    kv = lambda: pl.BlockSpec((None, s, LANES), lambda c, i: (c, 0, 0))
    grid = (2, s // tq)
    hosted, xi, xo, xs, xsem = _host_exchange(body, 6, 3, grid, xchg)
    res = pl.pallas_call(
        hosted, name="flash_a_bwd" + ("_x" if xchg else ""), grid=grid,
        in_specs=[grp(), grp(), grp(), grp(), kv(), kv()] + xi, out_specs=[grp(), kv(), kv()] + xo,
        out_shape=[SDS((s, 512), f32), SDS((2, s, LANES), f32), SDS((2, s, LANES), f32)] + xs,
        scratch_shapes=[pltpu.VMEM((2 * tq, LANES), bf16), pltpu.VMEM((2 * tq, LANES), bf16), pltpu.VMEM((2 * tq, 1), f32),
                        pltpu.VMEM((2 * tq, 1), f32), pltpu.VMEM((2 * tq, LANES), f32)] + xsem,
        compiler_params=_cparams())(qa, doa, oa, lse, kd, vd, *(xchg[0] if xchg else []))
    return res[:3], res[3:]


BH = 128
BQ_MAX = 512


class _BandGeom:
    def __init__(self, gi, s):
        self.d = B_GROUP_CFG[gi][1]
        self.l_sub = s // self.d
        self.bq = min(BQ_MAX, self.l_sub)
        self.nb = self.l_sub // self.bq
        self.grid = (self.d, self.nb, 2)
        per, last = self.bq // BH, self.l_sub // BH - 1
        self.in_col = lambda r, pp: r * 6 + gi * 2 + pp
        self.out_col = lambda r, pp: r * 2 + pp
        self.cur = lambda col: pl.BlockSpec((self.bq, LANES), lambda r, n, pp: (n, col(r, pp)))
        self.prev = lambda col: pl.BlockSpec((BH, LANES), lambda r, n, pp: (jnp.maximum(n * per - 1, 0), col(r, pp)))
        self.next = lambda col: pl.BlockSpec((BH, LANES), lambda r, n, pp: (jnp.minimum((n + 1) * per, last), col(r, pp)))
        self.window = lambda col: [self.prev(col), self.cur(col), self.next(col)]
        self.view_in = lambda a: a.reshape(self.l_sub, self.d * 768)
        self.view_out = lambda a: a.reshape(self.l_sub, self.d * 256)

    def fill_band_bias(self, bias_s, q_rows, q_off, k_off):
        @pl.when((pl.program_id(0) == 0) & (pl.program_id(1) == 0) & (pl.program_id(2) == 0))
        def _():
            r = lax.broadcasted_iota(jnp.int32, bias_s.shape, 0)
            q = jnp.where(r >= q_rows, r - q_rows, r) + q_off
            k = lax.broadcasted_iota(jnp.int32, bias_s.shape, 1) + k_off
            bias_s[...] = jnp.where(jnp.abs(q - k) <= 64, 0.0, NEG_INF)

    def edge_bias(self, first_pos, n, stacked_rows=False):
        if stacked_rows:
            r = lax.broadcasted_iota(jnp.int32, (2 * n, 1), 0)
            idx = jnp.where(r >= n, r - n, r)
        else:
            idx = lax.broadcasted_iota(jnp.int32, (1, n), 1)
        pos = first_pos + idx
        return jnp.where((pos >= 0) & (pos < self.l_sub), 0.0, NEG_INF)


def _cat3(a, b, c):
    return jnp.concatenate([a[...], b[...], c[...]], axis=0)


def _attn_b_fwd(qb, kb, vb, gi, s):
    g = _BandGeom(gi, s)
    bq = g.bq

    def body(q_ref, kp, kc, kn, vp, vc, vn, o_ref, lse_ref, bias_s):
        n = pl.program_id(1)
        g.fill_band_bias(bias_s, bq, 0, -BH)
        lo = _lo_mask((bq, LANES))
        qs = _stack_heads(q_ref[...], lo, bf16)
        k, v = _cat3(kp, kc, kn), _cat3(vp, vc, vn)
        sc = _dot(qs, k, "nt") * SCALE + bias_s[...] + g.edge_bias(n * bq - BH, bq + 2 * BH)
        m = jnp.max(sc, axis=1, keepdims=True)
        p = jnp.exp(sc - m)
        den = jnp.sum(p, axis=1, keepdims=True)
        o = _dot(p.astype(bf16), v, "nn") / den
        o_ref[...] = jnp.where(lo, o[:bq], o[bq:])
        lse = m + jnp.log(den)
        lse_ref[...] = jnp.where(lo, lse[:bq], lse[bq:])

    out = g.cur(g.out_col)
    qv, kv, vv = g.view_in(qb), g.view_in(kb), g.view_in(vb)
    o, lse = pl.pallas_call(
        body, name=f"attn_b_fwd{gi}", grid=g.grid,
        in_specs=[g.cur(g.in_col)] + g.window(g.in_col) * 2, out_specs=[out, out],
        out_shape=[SDS((g.l_sub, g.d * 256), f32), SDS((g.l_sub, g.d * 256), f32)],
        scratch_shapes=[pltpu.VMEM((2 * bq, bq + 2 * BH), f32)],
        compiler_params=_cparams())(qv, kv, kv, kv, vv, vv, vv)
    return o.reshape(s, 256), lse.reshape(s, 256)


def _attn_b_dq(qb, kb, vb, dob, lse, delta, gi, s):
    g = _BandGeom(gi, s)
    bq = g.bq

    def body(q_ref, kp, kc, kn, vp, vc, vn, do_ref, lse_ref, dl_ref, dq_ref, bias_s):
        n = pl.program_id(1)
        g.fill_band_bias(bias_s, bq, 0, -BH)
        lo = _lo_mask((bq, LANES))
        qs = _stack_heads(q_ref[...], lo, bf16)
        dos = _stack_heads(do_ref[...], lo, bf16)
        k, v = _cat3(kp, kc, kn), _cat3(vp, vc, vn)
        sc = _dot(qs, k, "nt") * SCALE + bias_s[...] + g.edge_bias(n * bq - BH, bq + 2 * BH)
        p = jnp.exp(sc - _rows_of(lse_ref[...], lo, "max"))
        dp = _dot(dos, v, "nt")
        ds = (p * (dp - _rows_of(dl_ref[...], lo, "sum")) * SCALE).astype(bf16)
        dqs = _dot(ds, k, "nn")
        dq_ref[...] = jnp.where(lo, dqs[:bq], dqs[bq:])

    out = g.cur(g.out_col)
    qv, kv, vv = g.view_in(qb), g.view_in(kb), g.view_in(vb)
    dq = pl.pallas_call(
        body, name=f"attn_b_dq{gi}", grid=g.grid,
        in_specs=[g.cur(g.in_col)] + g.window(g.in_col) * 2 + [out, out, out], out_specs=out,
        out_shape=SDS((g.l_sub, g.d * 256), f32), scratch_shapes=[pltpu.VMEM((2 * bq, bq + 2 * BH), f32)],
        compiler_params=_cparams())(qv, kv, kv, kv, vv, vv, vv, g.view_out(dob), g.view_out(lse), g.view_out(delta))
    return dq.reshape(s, 256)


def _attn_b_dkv(qb, kb, vb, dob, lse, delta, gi, s):
    g = _BandGeom(gi, s)
    bq = g.bq
    nq = bq + 2 * BH

    def body(k_ref, v_ref, qp, qc, qn, dop, doc, don, lp, lc, ln, dp_, dc_, dn_, dk_ref, dv_ref, bias_s):
        m = pl.program_id(1)
        g.fill_band_bias(bias_s, nq, -BH, 0)
        lo = _lo_mask((nq, LANES))
        qs = _stack_heads(_cat3(qp, qc, qn), lo, bf16)
        dos = _stack_heads(_cat3(dop, doc, don), lo, bf16)
        lses = _rows_of(_cat3(lp, lc, ln), lo, "max")
        dls = _rows_of(_cat3(dp_, dc_, dn_), lo, "sum")
        k, v = k_ref[...], v_ref[...]
        sc = _dot(qs, k, "nt") * SCALE + bias_s[...] + g.edge_bias(m * bq - BH, nq, stacked_rows=True)
        p = jnp.exp(sc - lses)
        dpv = _dot(dos, v, "nt")
        ds = (p * (dpv - dls) * SCALE).astype(bf16)
        dv_ref[...] = _dot(p.astype(bf16), dos, "tn")
        dk_ref[...] = _dot(ds, qs, "tn")

    cur = g.cur(g.out_col)
    qv, kv, vv = g.view_in(qb), g.view_in(kb), g.view_in(vb)
    dk, dv = pl.pallas_call(
        body, name=f"attn_b_dkv{gi}", grid=g.grid,
        in_specs=[g.cur(g.in_col)] * 2 + g.window(g.in_col) + g.window(g.out_col) * 3, out_specs=[cur, cur],
        out_shape=[SDS((g.l_sub, g.d * 256), f32), SDS((g.l_sub, g.d * 256), f32)],
        scratch_shapes=[pltpu.VMEM((2 * nq, bq), f32)],
        compiler_params=_cparams())(kv, vv, qv, qv, qv, *([g.view_out(dob)] * 3), *([g.view_out(lse)] * 3),
                                    *([g.view_out(delta)] * 3))
    return dk.reshape(s, 256), dv.reshape(s, 256)


def _merge_b(o_list, lse_list, s):
    ts = 256

    def fn(o0, o1, o2, l0, l1, l2):
        m = jnp.maximum(jnp.maximum(l0, l1), l2)
        w0, w1, w2 = jnp.exp(l0 - m), jnp.exp(l1 - m), jnp.exp(l2 - m)
        den = w0 + w1 + w2
        return (w0 * o0 + w1 * o1 + w2 * o2) / den, m + jnp.log(den)

    row = pl.BlockSpec((ts, 256), lambda i: (i, 0))
    return _ew("merge_b", fn, (s // ts,), [*o_list, *lse_list], [row] * 6,
               [SDS((s, 256), bf16), SDS((s, 256), f32)], [row, row])


def _delta_b(dob, ob, s):
    ts = 256

    def fn(do, o):
        prod = do * o.astype(f32)
        lo = _lo_mask((ts, LANES))
        return jnp.concatenate([_head_stat(prod[:, :LANES], lo), _head_stat(prod[:, LANES:], lo)], axis=1)

    row = pl.BlockSpec((ts, 256), lambda i: (i, 0))
    return _ew("delta_b", fn, (s // ts,), [dob, ob], [row, row], [SDS((s, 256), f32)], [row])[0]


def _band(t0, u0, w, nt, nu, s, transposed):
    t = t0 + lax.broadcasted_iota(jnp.int32, (nt, nu), 0)
    u = u0 + lax.broadcasted_iota(jnp.int32, (nt, nu), 1)
    if transposed:
        lo, hi = jnp.clip(t - w // 2 + 1, 0, s), jnp.clip(t + w // 2 + 1, 0, s)
    else:
        lo, hi = jnp.clip(t - w // 2, 0, s), jnp.clip(t + w - w // 2, 0, s)
    return ((u >= lo) & (u < hi)).astype(f32)


def _pool_cnt(t0, w, nt, s):
    t = t0 + lax.broadcasted_iota(jnp.int32, (nt, 1), 0)
    return (jnp.clip(t + w - w // 2, 0, s) - jnp.clip(t - w // 2, 0, s)).astype(f32)


POOL_HALO = 8


def _halo_specs(tp, s, cb):
    per, last = tp // POOL_HALO, s // POOL_HALO - 1
    return [pl.BlockSpec((POOL_HALO, 512), lambda i: (jnp.maximum(i * per - 1, 0), cb)),
            pl.BlockSpec((tp, 512), lambda i: (i, cb)),
            pl.BlockSpec((POOL_HALO, 512), lambda i: (jnp.minimum((i + 1) * per, last), cb))]


def _pool_fwd(z, lin, scale, s):
    tp = 256
    nt = s // tp

    def body(up, uc, un, lin_ref, sc_ref, pooled_o, pc_o):
        i = pl.program_id(0)
        ext = jnp.concatenate([up[...], uc[...], un[...]], axis=0)
        for g, w in enumerate(POOL_WINDOWS):
            sl = slice(g * LANES, (g + 1) * LANES)
            band = _band(i * tp, i * tp - POOL_HALO, w, tp, tp + 2 * POOL_HALO, s, False)
            sm = jnp.dot(band, ext[:, sl], preferred_element_type=f32, precision=lax.Precision.HIGHEST)
            pooled = (sm / _pool_cnt(i * tp, w, tp, s) - uc[:, sl]).astype(bf16)
            pooled_o[:, sl] = pooled
            mixed = _dot(pooled, lin_ref[g].astype(bf16), "nn")
            pc_o[:, sl] = (mixed * sc_ref[:, sl]).astype(bf16)

    row = pl.BlockSpec((tp, 512), lambda i: (i, 0))
    return pl.pallas_call(
        body, name="pool_fwd", grid=(nt,),
        in_specs=_halo_specs(tp, s, C_UC // 512)
        + [pl.BlockSpec((4, LANES, LANES), lambda i: (0, 0, 0)), pl.BlockSpec((1, 512), lambda i: (0, 0))],
        out_specs=[row, row], out_shape=[SDS((s, 512), bf16), SDS((s, 512), bf16)],
        compiler_params=_cparams())(z, z, z, lin, scale)


def _pool_bwd1(dpc, pooled, lin, scale, s):
    tp = 256

    def body(dpc_ref, pooled_ref, lin_ref, sc_ref, dpn_o, dlin_o, dsc_o):
        i = pl.program_id(0)
        dsc = []
        for g, w in enumerate(POOL_WINDOWS):
            sl = slice(g * LANES, (g + 1) * LANES)
            pooled = pooled_ref[:, sl]
            linb = lin_ref[g].astype(bf16)
            mixed = _dot(pooled, linb, "nn")
            dpc_g = dpc_ref[:, sl]
            dsc.append(jnp.sum(dpc_g * mixed, axis=0, keepdims=True))
            dmixed = (dpc_g * sc_ref[:, sl]).astype(bf16)
            dpn_o[:, sl] = _dot(dmixed, linb, "nt") / _pool_cnt(i * tp, w, tp, s)
            dl = _dot(pooled, dmixed, "tn")

            @pl.when(i == 0)
            def _(g=g, dl=dl):
                dlin_o[g] = dl

            @pl.when(i > 0)
            def _(g=g, dl=dl):
                dlin_o[g] += dl

        dsc = jnp.concatenate(dsc, axis=1)

        @pl.when(i == 0)
        def _():
            dsc_o[...] = dsc

        @pl.when(i > 0)
        def _():
            dsc_o[...] += dsc

    row = pl.BlockSpec((tp, 512), lambda i: (i, 0))
    linspec = pl.BlockSpec((4, LANES, LANES), lambda i: (0, 0, 0))
    one = pl.BlockSpec((1, 512), lambda i: (0, 0))
    return pl.pallas_call(
        body, name="pool_bwd1", grid=(s // tp,), in_specs=[row, row, linspec, one], out_specs=[row, linspec, one],
        out_shape=[SDS((s, 512), f32), SDS((4, LANES, LANES), f32), SDS((1, 512), f32)],
        compiler_params=_cparams())(dpc, pooled, lin, scale)


def _pool_bwd2(dpn, dz, s):
    tp = 256
    nt = s // tp

    def body(dp, dc, dn, dz_in, dz_o):
        i = pl.program_id(0)
        ext = jnp.concatenate([dp[...], dc[...], dn[...]], axis=0)
        for g, w in enumerate(POOL_WINDOWS):
            sl = slice(g * LANES, (g + 1) * LANES)
            band = _band(i * tp, i * tp - POOL_HALO, w, tp, tp + 2 * POOL_HALO, s, True)
            sm = jnp.dot(band, ext[:, sl], preferred_element_type=f32, precision=lax.Precision.HIGHEST)
            dz_o[:, sl] = (sm - dc[:, sl] * _pool_cnt(i * tp, w, tp, s)).astype(bf16)

    return pl.pallas_call(
        body, name="pool_bwd2", grid=(nt,),
        in_specs=_halo_specs(tp, s, 0) + [pl.BlockSpec(memory_space=pl.ANY)],
        out_specs=pl.BlockSpec((tp, 512), lambda i: (i, C_UC // 512)), out_shape=SDS((s, IN_WIDTH), bf16),
        input_output_aliases={3: 0}, compiler_params=_cparams())(dpn, dpn, dpn, dz)


GW = 512


MIX_TM = 256
BR_WIDTHS = (512, 256, 512)
N_GATE_BLOCKS = 3 * D_MODEL // GW


def _gate_specs(ts):
    return [pl.BlockSpec((ts, GW), lambda i, q=q: (i, C_GZ // GW + q)) for q in range(N_GATE_BLOCKS)]


def _mix_out_fwd(acts, z, b_gate, x, w, s):
    ts = MIX_TM

    def body(oa, ob, pc, *rest):
        gz = rest[:N_GATE_BLOCKS]
        bias, x_ref, wa, wb, wc, wo, y3_o, m_o, x1_o = rest[N_GATE_BLOCKS:]
        for b, (act, wt) in enumerate(((oa, wa), (ob, wb), (pc, wc))):
            a = act[...]
            for j in range(N_CHIPS):
                y3_o[b, :, j * 256:(j + 1) * 256] = _dot(a, wt[j], "nn")
        for h in range(2):
            hs = slice(h * GW, (h + 1) * GW)
            tot = jnp.zeros((ts, GW), f32)
            for b in range(3):
                g = jax.nn.sigmoid(gz[2 * b + h][...] + bias[:, b * D_MODEL + h * GW:b * D_MODEL + (h + 1) * GW])
                tot += g * y3_o[b, :, hs]
            m_o[:, hs] = tot.astype(bf16)
        acc = x_ref[...]
        for j in range(N_CHIPS):
            acc += _dot(m_o[:, j * 256:(j + 1) * 256], wo[j], "nn")
        x1_o[...] = acc

    row = lambda width: pl.BlockSpec((ts, width), lambda i: (i, 0))
    res3 = lambda a: pl.BlockSpec(a.shape, lambda i: (0, 0, 0))
    wts = [w["w_branch_a"], w["w_branch_b"], w["w_branch_c"], w["w_out"]]
    return pl.pallas_call(
        body, name="mix_out_fwd", grid=(s // ts,),
        in_specs=[row(BR_WIDTHS[0]), row(BR_WIDTHS[1]), row(BR_WIDTHS[2])] + _gate_specs(ts)
        + [pl.BlockSpec((1, 3 * D_MODEL), lambda i: (0, 0)), row(D_MODEL)] + [res3(a) for a in wts],
        out_specs=[pl.BlockSpec((3, ts, D_MODEL), lambda i: (0, i, 0)), row(D_MODEL), row(D_MODEL)],
        out_shape=[SDS((3, s, D_MODEL), f32), SDS((s, D_MODEL), bf16), SDS((s, D_MODEL), f32)],
        compiler_params=_cparams())(*acts, *([z] * N_GATE_BLOCKS), b_gate, x, *wts)


def _mix_out_bwd(dx1, y3, z, b_gate, w, s):
    ts = MIX_TM

    def body(dx_ref, y3_ref, *rest):
        gz = rest[:N_GATE_BLOCKS]
        bias, wa, wb, wc, wo, dy_o, dz_o, db_o, doa_o, dob_o, dpc_o, dm_s = rest[N_GATE_BLOCKS:]
        i = pl.program_id(0)
        dx = dx_ref[...].astype(bf16)
        for j in range(N_CHIPS):
            dm_s[:, j * 256:(j + 1) * 256] = _dot(dx, wo[j], "nt")
        dz_o[:, 0:C_GZ] = jnp.zeros((ts, C_GZ), bf16)
        dbs = []
        for b in range(3):
            for h in range(2):
                hs = slice(h * GW, (h + 1) * GW)
                g = jax.nn.sigmoid(gz[2 * b + h][...] + bias[:, b * D_MODEL + h * GW:b * D_MODEL + (h + 1) * GW])
                dm = dm_s[:, hs]
                dy_o[b, :, hs] = (dm * g).astype(bf16)
                dgz = dm * y3_ref[b, :, hs] * g * (1.0 - g)
                c0 = C_GZ + b * D_MODEL + h * GW
                dz_o[:, c0:c0 + GW] = dgz.astype(bf16)
                dbs.append(jnp.sum(dgz, axis=0, keepdims=True))
        db = jnp.concatenate(dbs, axis=1)

        @pl.when(i == 0)
        def _():
            db_o[...] = db

        @pl.when(i > 0)
        def _():
            db_o[...] += db

        for b, (wt, out) in enumerate(((wa, doa_o), (wb, dob_o), (wc, dpc_o))):
            acc = _dot(dy_o[b, :, 0:256], wt[0], "nt")
            for j in range(1, N_CHIPS):
                acc += _dot(dy_o[b, :, j * 256:(j + 1) * 256], wt[j], "nt")
            out[...] = acc

    row = lambda width: pl.BlockSpec((ts, width), lambda i: (i, 0))
    res3 = lambda a: pl.BlockSpec(a.shape, lambda i: (0, 0, 0))
    blk3 = pl.BlockSpec((3, ts, D_MODEL), lambda i: (0, i, 0))
    one = pl.BlockSpec((1, 3 * D_MODEL), lambda i: (0, 0))
    wts = [w["w_branch_a"], w["w_branch_b"], w["w_branch_c"], w["w_out"]]
    return pl.pallas_call(
        body, name="mix_out_bwd", grid=(s // ts,),
        in_specs=[row(D_MODEL), blk3] + _gate_specs(ts) + [one] + [res3(a) for a in wts],
        out_specs=[blk3, row(IN_WIDTH), one, row(BR_WIDTHS[0]), row(BR_WIDTHS[1]), row(BR_WIDTHS[2])],
        out_shape=[SDS((3, s, D_MODEL), bf16), SDS((s, IN_WIDTH), bf16), SDS((1, 3 * D_MODEL), f32),
                   SDS((s, BR_WIDTHS[0]), f32), SDS((s, BR_WIDTHS[1]), f32), SDS((s, BR_WIDTHS[2]), f32)],
        scratch_shapes=[pltpu.VMEM((ts, D_MODEL), f32)],
        compiler_params=_cparams())(dx1, y3, *([z] * N_GATE_BLOCKS), b_gate, *wts)


def _loss_grad(y, tgt, s):
    ts = 256

    def fn(yt, tt):
        e = yt - tt
        return e * (1.0 / D_MODEL), jnp.sum(e * e, axis=0, keepdims=True) * (0.5 / D_MODEL)

    row = pl.BlockSpec((ts, D_MODEL), lambda i: (i, 0))
    one = pl.BlockSpec((1, D_MODEL), lambda i: (0, 0))
    return _ew("loss_grad", fn, (s // ts,), [y, tgt], [row, row], [SDS((s, D_MODEL), f32), SDS((1, D_MODEL), f32)],
               [row, one], n_acc=1)


TM = 512


def _layer_fwd(x, w, sm, tabs, s, xchg=None):
    nm = s // TM
    hb = _norm_fwd(x, sm["norm_mix"], s)
    z = _mm("mm_z", [hb, w["w_in"]],
            [pl.BlockSpec((TM, D_MODEL), lambda j, i, k: (i, 0)), pl.BlockSpec((None, D_MODEL, 1664), lambda j, i, k: (j, 0, 0))],
            SDS((s, IN_WIDTH), f32), pl.BlockSpec((TM, 1664), lambda j, i, k: (i, j)), (4, nm, 1), 1, "nn", None)
    qa, kd, vd, qb, kb, vb = _prep_fwd(z, tabs, sm["gains"], s)
    (oa, lse_a), landed = _flash_a_fwd(qa, kd, vd, s, xchg)
    ob_parts = [_attn_b_fwd(qb, kb, vb, gi, s) for gi in range(3)]
    ob, lse_b = _merge_b([p[0] for p in ob_parts], [p[1] for p in ob_parts], s)
    pooled, pc = _pool_fwd(z, sm["pool_lin"], sm["pool_scale"], s)

    y3, merged, x1 = _mix_out_fwd([oa, ob, pc], z, sm["b_gate"], x, w, s)
    h2 = _norm_fwd(x1, sm["norm_ffn"], s)
    g4, u4, a4 = _ffn_up(h2, w["w_ffn_gate"], w["w_ffn_up"], s)
    x2 = _mm("mm_down", [a4, w["w_ffn_down"], x1],
             [pl.BlockSpec((None, TM, FF_SHARD), lambda i, k: (k, i, 0)), pl.BlockSpec((None, FF_SHARD, D_MODEL), lambda i, k: (k, 0, 0)),
              pl.BlockSpec((TM, D_MODEL), lambda i, k: (i, 0))],
             SDS((s, D_MODEL), f32), pl.BlockSpec((TM, D_MODEL), lambda i, k: (i, 0)), (nm, 4), 4, "nn", (TM, D_MODEL),
             epilogue=lambda acc, xr: acc + xr, n_extra=1)
    saved = dict(x=x, hb=hb, z=z, qa=qa, kd=kd, vd=vd, qb=qb, kb=kb, vb=vb, oa=oa, lse_a=lse_a, ob=ob, lse_b=lse_b,
                 pooled=pooled, pc=pc, y3=y3, merged=merged, x1=x1, h2=h2, g4=g4, u4=u4, a4=a4)
    return x2, saved, landed


def _ffn_up(h2, wg, wu, s):
    nm = s // TM

    def body(h_ref, wg_ref, wu_ref, g_o, u_o, a_o):
        h = h_ref[...]
        g = _dot(h, wg_ref[...], "nn")
        u = _dot(h, wu_ref[...], "nn")
        g_o[...] = g
        u_o[...] = u
        a_o[...] = (g * jax.nn.sigmoid(g) * u).astype(bf16)

    wsp = pl.BlockSpec((None, D_MODEL, FF_SHARD), lambda j, i: (j, 0, 0))
    osp = pl.BlockSpec((None, TM, FF_SHARD), lambda j, i: (j, i, 0))
    return pl.pallas_call(
        body, name="ffn_up", grid=(4, nm), in_specs=[pl.BlockSpec((TM, D_MODEL), lambda j, i: (i, 0)), wsp, wsp],
        out_specs=[osp, osp, osp],
        out_shape=[SDS((4, s, FF_SHARD), f32), SDS((4, s, FF_SHARD), f32), SDS((4, s, FF_SHARD), bf16)],
        compiler_params=_cparams())(h2, wg, wu)


def _ffn_bwd_act(dx2, wd, g4, u4, s):
    nm = s // TM

    def body(dx_ref, wd_ref, g_ref, u_ref, dg_o, du_o):
        da = _dot(dx_ref[...].astype(bf16), wd_ref[...], "nt")
        g, u = g_ref[...], u_ref[...]
        sg = jax.nn.sigmoid(g)
        dg_o[...] = (da * u * sg * (1.0 + g * (1.0 - sg))).astype(bf16)
        du_o[...] = (da * g * sg).astype(bf16)

    osp = pl.BlockSpec((None, TM, FF_SHARD), lambda j, i: (j, i, 0))
    return pl.pallas_call(
        body, name="ffn_bwd_act", grid=(4, nm),
        in_specs=[pl.BlockSpec((TM, D_MODEL), lambda j, i: (i, 0)), pl.BlockSpec((None, FF_SHARD, D_MODEL), lambda j, i: (j, 0, 0)), osp, osp],
        out_specs=[osp, osp], out_shape=[SDS((4, s, FF_SHARD), bf16), SDS((4, s, FF_SHARD), bf16)],
        compiler_params=_cparams())(dx2, wd, g4, u4)


def _ffn_bwd_dh(dg4, du4, wg, wu, s):
    nm = s // TM

    def body(dg_ref, du_ref, wg_ref, wu_ref, o_ref, acc):
        k = pl.program_id(1)
        part = _dot(dg_ref[...], wg_ref[...], "nt") + _dot(du_ref[...], wu_ref[...], "nt")

        @pl.when(k == 0)
        def _():
            acc[...] = part

        @pl.when(k > 0)
        def _():
            acc[...] += part

        @pl.when(k == 3)
        def _():
            o_ref[...] = acc[...]

    asp = pl.BlockSpec((None, TM, FF_SHARD), lambda i, k: (k, i, 0))
    wsp = pl.BlockSpec((None, D_MODEL, FF_SHARD), lambda i, k: (k, 0, 0))
    return pl.pallas_call(
        body, name="ffn_bwd_dh", grid=(nm, 4), in_specs=[asp, asp, wsp, wsp],
        out_specs=pl.BlockSpec((TM, D_MODEL), lambda i, k: (i, 0)), out_shape=SDS((s, D_MODEL), f32),
        scratch_shapes=[pltpu.VMEM((TM, D_MODEL), f32)], compiler_params=_cparams())(dg4, du4, wg, wu)


GRAD_WIRE_DTYPE = bf16


def _wgrad(name, a, a_spec, b, b_spec, out_shape, out_block, s, tk=512):
    nk = s // tk
    return _mm(name, [a, b], [a_spec, b_spec], SDS(out_shape, GRAD_WIRE_DTYPE),
               pl.BlockSpec((None,) + out_block, lambda j, k: (j, 0, 0)), (4, nk), nk, "tn", out_block)


def _layer_bwd(dx2, sv, w, sm, tabs, s, xchg=None):
    nm = s // TM
    tk = 512
    tok = lambda width: pl.BlockSpec((tk, width), lambda j, k: (k, 0))
    g_wd = _wgrad("wg_down", sv["a4"], pl.BlockSpec((None, tk, FF_SHARD), lambda j, k: (j, k, 0)), dx2, tok(D_MODEL),
                  (4, FF_SHARD, D_MODEL), (FF_SHARD, D_MODEL), s)
    dg4, du4 = _ffn_bwd_act(dx2, w["w_ffn_down"], sv["g4"], sv["u4"], s)
    sh704 = pl.BlockSpec((None, tk, FF_SHARD), lambda j, k: (j, k, 0))
    g_wg = _wgrad("wg_gate", sv["h2"], tok(D_MODEL), dg4, sh704, (4, D_MODEL, FF_SHARD), (D_MODEL, FF_SHARD), s)
    g_wu = _wgrad("wg_up", sv["h2"], tok(D_MODEL), du4, sh704, (4, D_MODEL, FF_SHARD), (D_MODEL, FF_SHARD), s)
    dh2 = _ffn_bwd_dh(dg4, du4, w["w_ffn_gate"], w["w_ffn_up"], s)
    dx1, g_norm_ffn = _norm_bwd(sv["x1"], sm["norm_ffn"], dh2, dx2, s)
    colblk = lambda width: pl.BlockSpec((tk, width), lambda j, k: (k, j))
    g_wo = _wgrad("wg_out", sv["merged"], colblk(256), dx1, tok(D_MODEL), (4, 256, D_MODEL), (256, D_MODEL), s)
    dy3, dz, g_bgate, doa, dob, dpc = _mix_out_bwd(dx1, sv["y3"], sv["z"], sm["b_gate"], w, s)
    br_grads = []
    for b, (nm_, act, kdim) in enumerate((("a", sv["oa"], 512), ("b", sv["ob"], 256), ("c", sv["pc"], 512))):
        br_grads.append(_wgrad("wg_br" + nm_, act, tok(kdim), dy3,
                               pl.BlockSpec((None, tk, 256), lambda j, k, b=b: (b, k, j)), (4, kdim, 256), (kdim, 256), s))
    dpn, g_lin, g_scale = _pool_bwd1(dpc, sv["pooled"], sm["pool_lin"], sm["pool_scale"], s)
    dz = _pool_bwd2(dpn, dz, s)
    delta = _delta_b(dob, sv["ob"], s)
    dqb, dkb, dvb = [], [], []
    for gi in range(3):
        dqb.append(_attn_b_dq(sv["qb"], sv["kb"], sv["vb"], dob, sv["lse_b"], delta, gi, s))
        dk, dv = _attn_b_dkv(sv["qb"], sv["kb"], sv["vb"], dob, sv["lse_b"], delta, gi, s)
        dkb.append(dk)
        dvb.append(dv)
    (dqa, dkd, dvd), landed = _flash_a_bwd(sv["qa"], sv["kd"], sv["vd"], sv["oa"], sv["lse_a"], doa, s, xchg)
    dz, g_gains = _prep_bwd(sv["z"], tabs, sm["gains"], dqa, dkd, dvd, dqb, dkb, dvb, dz, s)
    g_win = _wgrad("wg_in", sv["hb"], tok(D_MODEL), dz, colblk(1664), (4, D_MODEL, 1664), (D_MODEL, 1664), s)
    dh = _mm("mm_dh", [dz, w["w_in"]],
             [pl.BlockSpec((TM, 1664), lambda i, k: (i, k)), pl.BlockSpec((None, D_MODEL, 1664), lambda i, k: (k, 0, 0))],
             SDS((s, D_MODEL), f32), pl.BlockSpec((TM, D_MODEL), lambda i, k: (i, 0)), (nm, 4), 4, "nt", (TM, D_MODEL))
    dx0, g_norm_mix = _norm_bwd(sv["x"], sm["norm_mix"], dh, dx1, s)
    big = dict(w_in=g_win, w_branch_a=br_grads[0], w_branch_b=br_grads[1], w_branch_c=br_grads[2], w_out=g_wo,
               w_ffn_gate=g_wg, w_ffn_up=g_wu, w_ffn_down=g_wd)
    gg = g_gains[0:4, :HEAD_DIM] + g_gains[0:4, HEAD_DIM:]
    small = jnp.concatenate([g_norm_mix.reshape(-1), g_bgate.reshape(-1), gg.reshape(-1), g_lin.reshape(-1),
                             g_scale.reshape(-1), g_norm_ffn.reshape(-1)]).reshape(SMALL_ROWS, LANES)
    return dx0, big, small, landed


def _mesh_pos():
    return lax.axis_index("x"), lax.axis_index("y"), lax.axis_index("c")


def _chip_copies(src, dst, pick_src, send_sems, recv_sems, loc_sems, receiving=True):
    n = len(src)
    x, y, c = _mesh_pos()
    me = 2 * x + y
    local = [pltpu.make_async_copy(pick_src(src[t], me), dst[t].at[me], loc_sems.at[t]) for t in range(n)]
    push, recv = [], []
    for j, (px, py) in enumerate([(1 - x, y), (x, 1 - y), (1 - x, 1 - y)]):
        peer = 2 * px + py
        for t in range(n):
            sems = dict(send_sem=send_sems.at[j * n + t], recv_sem=recv_sems.at[j * n + t],
                        device_id=(px, py, c), device_id_type=MESH)
            push.append(pltpu.make_async_remote_copy(src_ref=pick_src(src[t], peer), dst_ref=dst[t].at[me], **sems))
            if receiving:
                recv.append(pltpu.make_async_remote_copy(src_ref=pick_src(src[t], peer), dst_ref=dst[t].at[peer], **sems))
    return local, push, recv


def _chip_start(copies):
    local, push, _ = copies
    for cp in local + push:
        cp.start()


def _chip_wait(copies):
    local, push, recv = copies
    for cp in push:
        cp.wait_send()
    for cp in recv:
        cp.wait_recv()
    for cp in local:
        cp.wait()


def _chip_sems(n):
    return [pltpu.SemaphoreType.DMA((3 * n,)), pltpu.SemaphoreType.DMA((3 * n,)), pltpu.SemaphoreType.DMA((n,))]


def _chip_landing(srcs):
    return [SDS((N_CHIPS,) + tuple(a.shape[1:]), a.dtype) for a in srcs]


def _chip_exchange(name, srcs, pick_src):
    n = len(srcs)
    any_spec = pl.BlockSpec(memory_space=pl.ANY)

    def body(*refs):
        copies = _chip_copies(refs[:n], refs[n:2 * n], pick_src, *refs[2 * n:])
        _chip_start(copies)
        _chip_wait(copies)

    return pl.pallas_call(
        body, name=name, in_specs=[any_spec] * n, out_specs=[any_spec] * n, out_shape=_chip_landing(srcs),
        scratch_shapes=_chip_sems(n), compiler_params=pltpu.CompilerParams())(*srcs)


def _sibling_exchange(name, srcs):
    n = len(srcs)
    any_spec = pl.BlockSpec(memory_space=pl.ANY)

    def body(*refs):
        src, dst = refs[:n], refs[n:2 * n]
        send_sems, recv_sems = refs[2 * n:]
        x, y, c = _mesh_pos()
        cps = [pltpu.make_async_remote_copy(src_ref=src[t], dst_ref=dst[t], send_sem=send_sems.at[t], recv_sem=recv_sems.at[t],
                                            device_id=(x, y, 1 - c), device_id_type=MESH) for t in range(n)]
        for cp in cps:
            cp.start()
        for cp in cps:
            cp.wait()

    return pl.pallas_call(
        body, name=name, in_specs=[any_spec] * n, out_specs=[any_spec] * n,
        out_shape=[SDS(a.shape, a.dtype) for a in srcs],
        scratch_shapes=[pltpu.SemaphoreType.DMA((n,)), pltpu.SemaphoreType.DMA((n,))],
        compiler_params=pltpu.CompilerParams())(*srcs)


def _all_exchange(name, src):
    any_spec = pl.BlockSpec(memory_space=pl.ANY)

    def body(src_ref, dst_ref, send_sems, recv_sems, loc_sem):
        x, y, c = _mesh_pos()
        me = 4 * x + 2 * y + c
        peers = [(x ^ ((k >> 2) & 1), y ^ ((k >> 1) & 1), c ^ (k & 1)) for k in range(1, N_DEV)]
        local = pltpu.make_async_copy(src_ref, dst_ref.at[me], loc_sem)
        local.start()
        cps = [pltpu.make_async_remote_copy(src_ref=src_ref, dst_ref=dst_ref.at[me], send_sem=send_sems.at[k],
                                            recv_sem=recv_sems.at[k], device_id=p, device_id_type=MESH)
               for k, p in enumerate(peers)]
        for cp in cps:
            cp.start()
        for cp in cps:
            cp.wait_send()
        for k, (px, py, pc) in enumerate(peers):
            pltpu.make_async_remote_copy(src_ref=src_ref, dst_ref=dst_ref.at[4 * px + 2 * py + pc], send_sem=send_sems.at[k],
                                         recv_sem=recv_sems.at[k], device_id=(px, py, pc), device_id_type=MESH).wait_recv()
        local.wait()

    return pl.pallas_call(
        body, name=name, in_specs=[any_spec], out_specs=any_spec, out_shape=SDS((N_DEV,) + src.shape, src.dtype),
        scratch_shapes=[pltpu.SemaphoreType.DMA((N_DEV - 1,)), pltpu.SemaphoreType.DMA((N_DEV - 1,)), pltpu.SemaphoreType.DMA],
        compiler_params=pltpu.CompilerParams())(src)


def _cast_bf16(name, a):
    l, r, c = a.shape
    tr = _row_tile(r, c)
    spec = pl.BlockSpec((None, tr, c), lambda i, j: (i, j, 0))
    return _ew(name, lambda t: t, (l, r // tr), [a], [spec], [SDS(a.shape, bf16)], [spec])[0]


def _sum4(name, land):
    _, r, c = land.shape
    tr = _row_tile(r, c)
    up = lambda t: t.astype(f32)
    return _ew(name, lambda a, b, cc, d: ((up(a) + up(b)) + up(cc)) + up(d), (r // tr,), [land] * 4,
               [pl.BlockSpec((None, tr, c), lambda i, k=k: (k, i, 0)) for k in range(4)],
               [SDS((r, c), f32)], [pl.BlockSpec((tr, c), lambda i: (i, 0))])[0]


def _adam_math(g, w, m, v):
    m2 = ADAM_B1 * m + (1.0 - ADAM_B1) * g
    v2 = ADAM_B2 * v + (1.0 - ADAM_B2) * (g * g)
    m_hat = m2 / (1.0 - ADAM_B1 ** ADAM_STEP)
    v_hat = v2 / (1.0 - ADAM_B2 ** ADAM_STEP)
    delta = -ADAM_LR * (m_hat / (jnp.sqrt(v_hat) + ADAM_EPS) + ADAM_WD * w)
    return delta, m2, v2


def _adamw_big(name, p_own, p_sib, w, m, v):
    l, r, c = w.shape
    tr = _row_tile(r, c)

    def fn(a, b, wt, mt, vt):
        g = a + b
        return (g,) + _adam_math(g, wt, mt, vt)

    spec = pl.BlockSpec((None, tr, c), lambda i, j: (i, j, 0))
    return _ew(name, fn, (l, r // tr), [p_own, p_sib, w, m, v], [spec] * 5, [SDS(w.shape, f32)] * 4, [spec] * 4)


def _adamw_small(land, w, m, v):
    r = w.shape[0]
    tr = _row_tile(r, LANES)

    def fn(*t):
        g = t[0]
        for k in range(1, N_DEV):
            g = g + t[k]
        return (g,) + _adam_math(g, *t[N_DEV:])

    row = pl.BlockSpec((tr, LANES), lambda i: (i, 0))
    return _ew("adamw_small", fn, (r // tr,), [land] * N_DEV + [w, m, v],
               [pl.BlockSpec((None, tr, LANES), lambda i, k=k: (k, i, 0)) for k in range(N_DEV)] + [row] * 3,
               [SDS(w.shape, f32)] * 4, [row] * 4)


def _pack_small(d):
    return jnp.concatenate([d[k].reshape(DEPTH, -1) for k in SMALL], axis=1).reshape(DEPTH * SMALL_ROWS, LANES)


def _unpack_small(a, shapes):
    a = a.reshape(DEPTH, SMALL_ROWS * LANES)
    out, off = {}, 0
    for k, n in zip(SMALL, SMALL_SIZES):
        out[k] = a[:, off:off + n].reshape(shapes[k])
        off += n
    return out


def kernel(x, norm_mix, w_in, b_gate, qn_a, kn_a, qn_b, kn_b, pool_lin, pool_scale, w_branch_a, w_branch_b, w_branch_c, w_out, norm_ffn, w_ffn_gate, w_ffn_up, w_ffn_down, loss_target, m_norm_mix, m_w_in, m_b_gate, m_qn_a, m_kn_a, m_qn_b, m_kn_b, m_pool_lin, m_pool_scale, m_w_branch_a, m_w_branch_b, m_w_branch_c, m_w_out, m_norm_ffn, m_w_ffn_gate, m_w_ffn_up, m_w_ffn_down, v_norm_mix, v_w_in, v_b_gate, v_qn_a, v_kn_a, v_qn_b, v_kn_b, v_pool_lin, v_pool_scale, v_w_branch_a, v_w_branch_b, v_w_branch_c, v_w_out, v_norm_ffn, v_w_ffn_gate, v_w_ffn_up, v_w_ffn_down):
    args = dict(locals())
    s = x.shape[1]
    wts = {k: args[k] for k in WEIGHT_ORDER}
    mom = {k: args["m_" + k] for k in WEIGHT_ORDER}
    var = {k: args["v_" + k] for k in WEIGHT_ORDER}
    tabs = _rope_tables(s)

    shards16 = [_cast_bf16("cast_" + k, wts[k]) for k in BIG]
    whole = lambda ref, peer: ref.at[0]
    full = [None] * DEPTH
    full[0] = dict(zip(BIG, _chip_exchange("gather", [a[0:1] for a in shards16], whole)))

    def small_of(l):
        tile2 = lambda a: jnp.concatenate([a, a])
        gains = jnp.stack([tile2(qn_a[l]), tile2(kn_a[l]), tile2(qn_b[l]), tile2(kn_b[l])])
        return dict(norm_mix=norm_mix[l][None], norm_ffn=norm_ffn[l][None], b_gate=b_gate[l][None], gains=gains,
                    pool_lin=pool_lin[l], pool_scale=pool_scale[l][None])

    xs = x[0]
    saved = []
    for l in range(DEPTH):
        xchg = ([a[l + 1:l + 2] for a in shards16], whole) if l + 1 < DEPTH else None
        xs, sv, landed = _layer_fwd(xs, full[l], small_of(l), tabs, s, xchg)
        saved.append(sv)
        if xchg:
            full[l + 1] = dict(zip(BIG, landed))
    dy, loss_rows = _loss_grad(xs, loss_target[0], s)
    loss = lax.psum(jnp.sum(loss_rows), ("x", "y", "c"))

    part = {k: [None] * DEPTH for k in BIG}
    small_g = [None] * DEPTH
    to_chip = lambda ref, peer: ref.at[peer]
    pending = None
    for l in reversed(range(DEPTH)):
        xchg = (pending, to_chip) if pending is not None else None
        dy, big, small_g[l], landed = _layer_bwd(dy, saved[l], full[l], small_of(l), tabs, s, xchg)
        for k, a in zip(BIG, landed):
            part[k][l + 1] = _sum4("sum4_" + k, a)
        pending = [big[k] for k in BIG]
    for k, a in zip(BIG, _chip_exchange("grad_scatter", pending, to_chip)):
        part[k][0] = _sum4("sum4_" + k, a)

    p_own = [jnp.stack(part[k]) for k in BIG]
    p_sib = _sibling_exchange("grad_sibling", p_own)
    outs = {}
    for k, a, b in zip(BIG, p_own, p_sib):
        outs[k] = _adamw_big("adamw_" + k, a, b, wts[k], mom[k], var[k])

    land_s = _all_exchange("small_allgather", jnp.concatenate(small_g, axis=0))
    res_s = _adamw_small(land_s, _pack_small(wts), _pack_small(mom), _pack_small(var))
    shapes = {k: wts[k].shape for k in SMALL}
    small_out = [_unpack_small(r, shapes) for r in res_s]
    for k in SMALL:
        outs[k] = tuple(so[k] for so in small_out)

    flat = [loss, dy[None]]
    for idx in range(4):
        flat += [outs[k][idx] for k in WEIGHT_ORDER]
    return tuple(flat)
```

```python
import functools

import jax
import jax.numpy as jnp
from jax import lax
from jax.experimental import pallas as pl
from jax.experimental.pallas import tpu as pltpu

f32, bf16 = jnp.float32, jnp.bfloat16
SDS = jax.ShapeDtypeStruct

DEPTH = 4
D_MODEL = 1024
HEAD_DIM = 64
LANES = 128
GRID_W = 64
ROPE_THETA = 10000.0
EPS = 1e-6
NEG_INF = -1e30
SCALE = HEAD_DIM ** -0.5
B_GROUP_CFG = ((128, 1), (512, 4), (2048, 16))
POOL_WINDOWS = (2, 4, 8, 16)
N_CHIPS = 4
N_DEV = 8
C_QA, C_KA, C_VA, C_QB, C_KB, C_VB, C_UC, C_GZ = 0, 512, 640, 768, 1536, 2304, 3072, 3584
IN_WIDTH = 6656
FF_SHARD = 704
ADAM_LR, ADAM_B1, ADAM_B2, ADAM_EPS, ADAM_WD, ADAM_STEP = 0.001, 0.9, 0.999, 1e-08, 0.01, 10
VMEM_LIMIT = 56 * 1024 * 1024
MESH = pl.DeviceIdType.MESH

MIXER = ("w_in", "w_branch_a", "w_branch_b", "w_branch_c", "w_out")
FFN = ("w_ffn_gate", "w_ffn_up", "w_ffn_down")
BIG = MIXER + FFN
REST = BIG[1:]
SMALL = ("norm_mix", "b_gate", "qn_a", "kn_a", "qn_b", "kn_b", "pool_lin", "pool_scale", "norm_ffn")
SMALL_SIZES = (1024, 3072, 64, 64, 64, 64, 65536, 512, 1024)
SMALL_ROWS = sum(SMALL_SIZES) // LANES
WEIGHT_ORDER = ("norm_mix", "w_in", "b_gate", "qn_a", "kn_a", "qn_b", "kn_b", "pool_lin", "pool_scale",
                "w_branch_a", "w_branch_b", "w_branch_c", "w_out", "norm_ffn", "w_ffn_gate", "w_ffn_up", "w_ffn_down")


def _cparams():
    return pltpu.CompilerParams(vmem_limit_bytes=VMEM_LIMIT)


def _lo_mask(shape):
    return lax.broadcasted_iota(jnp.int32, shape, len(shape) - 1) < HEAD_DIM


def _dot(a, b, dims):
    dn = {"nn": (((1,), (0,)), ((), ())), "nt": (((1,), (1,)), ((), ())), "tn": (((0,), (0,)), ((), ()))}[dims]
    return lax.dot_general(a, b, dn, preferred_element_type=f32)


def _mm(name, ins, in_specs, out_shape, out_spec, grid, nk, dims, acc_shape, epilogue=None, n_extra=0, aliases=None):
    multi = isinstance(out_shape, (list, tuple))
    n_out = len(out_shape) if multi else 1

    def body(*refs):
        a_ref, b_ref = refs[0], refs[1]
        extra = refs[2:2 + n_extra]
        outs = refs[2 + n_extra:2 + n_extra + n_out]
        k = pl.program_id(len(grid) - 1)
        part = _dot(a_ref[...].astype(bf16), b_ref[...].astype(bf16), dims)

        def finish(acc):
            res = epilogue(acc, *[e[...] for e in extra]) if epilogue is not None else acc
            res = res if isinstance(res, (list, tuple)) else (res,)
            for o, r in zip(outs, res):
                o[...] = r.astype(o.dtype)

        if nk == 1:
            finish(part)
        else:
            acc_ref = refs[-1]

            @pl.when(k == 0)
            def _():
                acc_ref[...] = part

            @pl.when(k > 0)
            def _():
                acc_ref[...] += part

            @pl.when(k == nk - 1)
            def _():
                finish(acc_ref[...])

    return pl.pallas_call(
        body, name=name, grid=grid, in_specs=in_specs,
        out_specs=list(out_spec) if multi else out_spec,
        out_shape=list(out_shape) if multi else out_shape,
        scratch_shapes=[] if nk == 1 else [pltpu.VMEM(acc_shape, f32)],
        input_output_aliases=aliases or {}, compiler_params=_cparams())(*ins)


def _ew(name, fn, grid, ins, in_specs, out_shapes, out_specs, n_acc=0, aliases=None):
    n_in = len(ins)
    n_out = len(out_shapes) - n_acc
    n_alias = len(aliases or {})

    def body(*refs):
        in_refs = refs[:n_in - n_alias] if n_alias else refs[:n_in]
        out_refs = refs[n_in:n_in + n_out]
        acc_refs = refs[n_in + n_out:]
        i = pl.program_id(len(grid) - 1)
        res = fn(*[r[...] for r in in_refs])
        res = res if isinstance(res, (list, tuple)) else (res,)
        for o, r in zip(out_refs, res[:n_out]):
            o[...] = r.astype(o.dtype)
        for a, r in zip(acc_refs, res[n_out:]):
            @pl.when(i == 0)
            def _(a=a, r=r):
                a[...] = r.astype(a.dtype)

            @pl.when(i > 0)
            def _(a=a, r=r):
                a[...] += r.astype(a.dtype)

    return pl.pallas_call(body, name=name, grid=grid, in_specs=in_specs, out_specs=out_specs, out_shape=out_shapes,
                          input_output_aliases=aliases or {}, compiler_params=_cparams())(*ins)


EW_BLOCK_ELEMS = 128 * 1024


def _row_tile(rows, cols):
    step = 16 if rows % 16 == 0 else 8
    return max(t for t in range(step, rows + 1, step) if rows % t == 0 and (t * cols <= EW_BLOCK_ELEMS or t == step))


def _rope_tables(s):
    t = jnp.arange(s)
    inv_ax = ROPE_THETA ** (-jnp.arange(0, HEAD_DIM // 2, 2, dtype=f32) / (HEAD_DIM // 2))
    inv_sq = ROPE_THETA ** (-jnp.arange(0, HEAD_DIM, 2, dtype=f32) / HEAD_DIM)
    ang_row = (t // GRID_W).astype(f32)[:, None] * inv_ax[None, :]
    ang_col = (t % GRID_W).astype(f32)[:, None] * inv_ax[None, :]
    ang_seq = t.astype(f32)[:, None] * inv_sq[None, :]
    a_ax = jnp.concatenate([ang_row, ang_row, ang_col, ang_col], axis=1)
    sg_ax = jnp.concatenate([-jnp.ones(16), jnp.ones(16), -jnp.ones(16), jnp.ones(16)]).astype(f32)
    a_sq = jnp.concatenate([ang_seq, ang_seq], axis=1)
    sg_sq = jnp.concatenate([-jnp.ones(32), jnp.ones(32)]).astype(f32)
    two = lambda a: jnp.concatenate([a, a], axis=1)
    return (two(jnp.cos(a_ax)), two(jnp.sin(a_ax) * sg_ax), two(jnp.cos(a_sq)), two(jnp.sin(a_sq) * sg_sq))


def _head_stat(v, lo):
    s0 = jnp.sum(jnp.where(lo, v, 0.0), axis=1, keepdims=True)
    s1 = jnp.sum(jnp.where(lo, 0.0, v), axis=1, keepdims=True)
    return jnp.where(lo, s0, s1)


def _partner(y, off):
    lane = lax.broadcasted_iota(jnp.int32, y.shape, 1)
    first = (lane & (2 * off - 1)) < off
    return jnp.where(first, pltpu.roll(y, LANES - off, 1), pltpu.roll(y, off, 1))


def _normrope(xc, g, cos, sin, off):
    lo = _lo_mask(xc.shape)
    r = lax.rsqrt(_head_stat(xc * xc, lo) * (1.0 / HEAD_DIM) + EPS)
    y = xc * r * g
    return y * cos + _partner(y, off) * sin


def _normrope_bwd(xc, g, cos, sin, off, drot):
    lo = _lo_mask(xc.shape)
    r = lax.rsqrt(_head_stat(xc * xc, lo) * (1.0 / HEAD_DIM) + EPS)
    n = xc * r
    dy = drot * cos + _partner(drot * sin, off)
    dg = jnp.sum(dy * n, axis=0, keepdims=True)
    dn = dy * g
    dx = r * (dn - n * (_head_stat(dn * n, lo) * (1.0 / HEAD_DIM)))
    return dx, dg


def _norm_fwd(x, g, s):
    ts = 256

    def fn(xt, gt):
        r = lax.rsqrt(jnp.mean(xt * xt, axis=1, keepdims=True) + EPS)
        return xt * r * gt

    return _ew("norm_fwd", fn, (s // ts,), [x, g],
               [pl.BlockSpec((ts, D_MODEL), lambda i: (i, 0)), pl.BlockSpec((1, D_MODEL), lambda i: (0, 0))],
               [SDS((s, D_MODEL), bf16)], [pl.BlockSpec((ts, D_MODEL), lambda i: (i, 0))])[0]


def _norm_bwd(x, g, dh, dres, s):
    ts = 256

    def fn(xt, gt, dht, drt):
        r = lax.rsqrt(jnp.mean(xt * xt, axis=1, keepdims=True) + EPS)
        n = xt * r
        dn = dht * gt
        dx = drt + r * (dn - n * jnp.mean(dn * n, axis=1, keepdims=True))
        return dx, jnp.sum(dht * n, axis=0, keepdims=True)

    row = pl.BlockSpec((ts, D_MODEL), lambda i: (i, 0))
    one = pl.BlockSpec((1, D_MODEL), lambda i: (0, 0))
    return _ew("norm_bwd", fn, (s // ts,), [x, g, dh, dres], [row, one, row, row],
               [SDS((s, D_MODEL), f32), SDS((1, D_MODEL), f32)], [row, one], n_acc=1)


def _prep_fwd(z, tabs, gains, s):
    ts = 256
    cos_a, sin_a, cos_b, sin_b = tabs

    def body(za_ref, zq_ref, zk_ref, zv_ref, ca, sa, cb, sb, g_ref, qa_o, kd_o, vd_o, qb_o, kb_o, vb_o):
        lo = _lo_mask((ts, LANES))
        g = g_ref[...]
        ca_, sa_, cb_, sb_ = ca[...], sa[...], cb[...], sb[...]
        for c in range(4):
            qa_o[:, c * LANES:(c + 1) * LANES] = _normrope(za_ref[:, c * LANES:(c + 1) * LANES], g[0:1], ca_, sa_, 16).astype(bf16)
        k = _normrope(za_ref[:, C_KA:C_KA + LANES], g[1:2], ca_, sa_, 16)
        kr = pltpu.roll(k, HEAD_DIM, 1)
        kd_o[0] = k.astype(bf16)
        kd_o[1] = kr.astype(bf16)
        v = za_ref[:, C_VA:C_VA + LANES]
        ones_col = (lax.broadcasted_iota(jnp.int32, v.shape, 1) == HEAD_DIM).astype(f32)
        vd_o[0] = jnp.where(lo, v, ones_col).astype(bf16)
        vd_o[1] = jnp.where(lo, pltpu.roll(v, HEAD_DIM, 1), ones_col).astype(bf16)
        for c in range(6):
            sl = slice(c * LANES, (c + 1) * LANES)
            qb_o[:, sl] = _normrope(zq_ref[:, sl], g[2:3], cb_, sb_, 32).astype(bf16)
            kb_o[:, sl] = _normrope(zk_ref[:, sl], g[3:4], cb_, sb_, 32).astype(bf16)
        vb_o[...] = zv_ref[...].astype(bf16)

    w = 768
    zspec = lambda cb: pl.BlockSpec((ts, w), lambda i: (i, cb))
    tab = pl.BlockSpec((ts, LANES), lambda i: (i, 0))
    dup = pl.BlockSpec((2, ts, LANES), lambda i: (0, i, 0))
    return pl.pallas_call(
        body, name="prep_fwd", grid=(s // ts,),
        in_specs=[zspec(0), zspec(1), zspec(2), zspec(3), tab, tab, tab, tab, pl.BlockSpec((4, LANES), lambda i: (0, 0))],
        out_specs=[pl.BlockSpec((ts, 512), lambda i: (i, 0)), dup, dup, zspec(0), zspec(0), zspec(0)],
        out_shape=[SDS((s, 512), bf16), SDS((2, s, LANES), bf16), SDS((2, s, LANES), bf16),
                   SDS((s, w), bf16), SDS((s, w), bf16), SDS((s, w), bf16)],
        compiler_params=_cparams())(z, z, z, z, cos_a, sin_a, cos_b, sin_b, gains)


def _prep_bwd(z, tabs, gains, dqa, dkd, dvd, dqb, dkb, dvb, dz, s):
    ts = 256
    cos_a, sin_a, cos_b, sin_b = tabs

    def body(za_ref, zq_ref, zk_ref, ca, sa, cb, sb, g_ref, dqa_r, dkd_r, dvd_r,
             dq0, dq1, dq2, dk0, dk1, dk2, dv0, dv1, dv2, dz_in, dz_o, dg_o):
        i = pl.program_id(0)
        lo = _lo_mask((ts, LANES))
        g = g_ref[...]
        ca_, sa_, cb_, sb_ = ca[...], sa[...], cb[...], sb[...]
        dg = [jnp.zeros((1, LANES), f32) for _ in range(4)]
        for c in range(4):
            sl = slice(c * LANES, (c + 1) * LANES)
            dx, d = _normrope_bwd(za_ref[:, sl], g[0:1], ca_, sa_, 16, dqa_r[:, sl])
            dz_o[:, sl] = dx.astype(bf16)
            dg[0] += d
        dk = jnp.where(lo, dkd_r[0], pltpu.roll(dkd_r[1], HEAD_DIM, 1))
        dx, d = _normrope_bwd(za_ref[:, C_KA:C_KA + LANES], g[1:2], ca_, sa_, 16, dk)
        dz_o[:, C_KA:C_KA + LANES] = dx.astype(bf16)
        dg[1] += d
        dz_o[:, C_VA:C_VA + LANES] = jnp.where(lo, dvd_r[0], pltpu.roll(dvd_r[1], HEAD_DIM, 1)).astype(bf16)
        dqs, dks, dvs = (dq0, dq1, dq2), (dk0, dk1, dk2), (dv0, dv1, dv2)
        for c in range(6):
            sl = slice(c * LANES, (c + 1) * LANES)
            gsl = slice((c % 2) * LANES, (c % 2 + 1) * LANES)
            dx, d = _normrope_bwd(zq_ref[:, sl], g[2:3], cb_, sb_, 32, dqs[c // 2][:, gsl])
            dz_o[:, C_QB + c * LANES:C_QB + (c + 1) * LANES] = dx.astype(bf16)
            dg[2] += d
            dx, d = _normrope_bwd(zk_ref[:, sl], g[3:4], cb_, sb_, 32, dks[c // 2][:, gsl])
            dz_o[:, C_KB + c * LANES:C_KB + (c + 1) * LANES] = dx.astype(bf16)
            dg[3] += d
            dz_o[:, C_VB + c * LANES:C_VB + (c + 1) * LANES] = dvs[c // 2][:, gsl].astype(bf16)
        dgs = jnp.concatenate(dg + [jnp.zeros((4, LANES), f32)], axis=0)

        @pl.when(i == 0)
        def _():
            dg_o[...] = dgs

        @pl.when(i > 0)
        def _():
            dg_o[...] += dgs

    w = 768
    zspec = lambda cb: pl.BlockSpec((ts, w), lambda i: (i, cb))
    tab = pl.BlockSpec((ts, LANES), lambda i: (i, 0))
    dup = pl.BlockSpec((2, ts, LANES), lambda i: (0, i, 0))
    grp = pl.BlockSpec((ts, 256), lambda i: (i, 0))
    dzo, dgo = pl.pallas_call(
        body, name="prep_bwd", grid=(s // ts,),
        in_specs=[zspec(0), zspec(1), zspec(2), tab, tab, tab, tab, pl.BlockSpec((4, LANES), lambda i: (0, 0)),
                  pl.BlockSpec((ts, 512), lambda i: (i, 0)), dup, dup] + [grp] * 9 + [pl.BlockSpec(memory_space=pl.ANY)],
        out_specs=[pl.BlockSpec((ts, C_UC), lambda i: (i, 0)), pl.BlockSpec((8, LANES), lambda i: (0, 0))],
        out_shape=[SDS((s, IN_WIDTH), bf16), SDS((8, LANES), f32)],
        input_output_aliases={20: 0}, compiler_params=_cparams())(
            z, z, z, cos_a, sin_a, cos_b, sin_b, gains, dqa, dkd, dvd, *dqb, *dkb, *dvb, dz)
    return dzo, dgo


def _stack_heads(x, lo, dtype):
    z = jnp.zeros_like(x)
    return jnp.concatenate([jnp.where(lo, x, z), jnp.where(lo, z, x)], axis=0).astype(dtype)


def _rows_of(v, lo, kind):
    if kind == "max":
        a = jnp.max(jnp.where(lo, v, NEG_INF * 10), axis=1, keepdims=True)
        b = jnp.max(jnp.where(lo, NEG_INF * 10, v), axis=1, keepdims=True)
    else:
        a = jnp.sum(jnp.where(lo, v, 0.0), axis=1, keepdims=True) * (1.0 / HEAD_DIM)
        b = jnp.sum(jnp.where(lo, 0.0, v), axis=1, keepdims=True) * (1.0 / HEAD_DIM)
    return jnp.concatenate([a, b], axis=0)


def _stack_low(x, lo, dtype, scale=None):
    x = x.astype(f32)
    if scale is not None:
        x = x * scale
    z = jnp.zeros_like(x)
    return jnp.concatenate([jnp.where(lo, x, z), jnp.where(lo, pltpu.roll(x, HEAD_DIM, 1), z)], axis=0).astype(dtype)


def _unstack_low(xs, lo, rows):
    return jnp.where(lo, xs[:rows], pltpu.roll(xs[rows:], HEAD_DIM, 1))


FA_TQ, FA_TK = 256, 1024


def _host_exchange(body, n_in, n_out, grid, xchg):
    if xchg is None:
        return body, [], [], [], []
    srcs, pick_src = xchg
    n = len(srcs)
    any_spec = pl.BlockSpec(memory_space=pl.ANY)

    def hosted(*refs):
        ins, src = refs[:n_in], refs[n_in:n_in + n]
        outs, dst = refs[n_in + n:n_in + n + n_out], refs[n_in + n + n_out:n_in + 2 * n + n_out]
        scratch, sems = refs[n_in + 2 * n + n_out:-3], refs[-3:]
        ids = [pl.program_id(a) for a in range(len(grid))]
        first = functools.reduce(jnp.logical_and, [i == 0 for i in ids])
        last = functools.reduce(jnp.logical_and, [i == g - 1 for i, g in zip(ids, grid)])

        @pl.when(first)
        def _():
            _chip_start(_chip_copies(src, dst, pick_src, *sems, receiving=False))

        body(*ins, *outs, *scratch)

        @pl.when(last)
        def _():
            _chip_wait(_chip_copies(src, dst, pick_src, *sems))

    return hosted, [any_spec] * n, [any_spec] * n, _chip_landing(srcs), _chip_sems(n)


def _flash_a_fwd(qa, kd, vd, s, xchg=None):
    tq, tk = FA_TQ, 2 * FA_TK
    nk = s // tk

    def body(q_ref, k_ref, v_ref, o_ref, lse_ref, qs, m_s, acc):
        lo = _lo_mask((tq, LANES))
        qs[...] = _stack_low(q_ref[...], lo, bf16, SCALE)
        m_s[...] = jnp.full(m_s.shape, NEG_INF, f32)
        acc[...] = jnp.zeros(acc.shape, f32)

        def chunk(j, carry):
            rows = pl.ds(pl.multiple_of(j * tk, tk), tk)
            sc = _dot(qs[...], k_ref[rows, :], "nt")
            m_prev = m_s[...]
            m_new = jnp.maximum(m_prev, jnp.max(sc, axis=1, keepdims=True))
            p = jnp.exp(sc - m_new).astype(bf16)
            acc[...] = acc[...] * jnp.exp(m_prev - m_new) + _dot(p, v_ref[rows, :], "nn")
            m_s[...] = m_new
            return carry

        lax.fori_loop(0, nk, chunk, 0, unroll=2)
        a = acc[...]
        lane = lax.broadcasted_iota(jnp.int32, a.shape, 1)
        l = jnp.sum(jnp.where(lane == HEAD_DIM, a, 0.0), axis=1, keepdims=True)
        o_ref[...] = _unstack_low(a / l, lo, tq).astype(bf16)
        lse = m_s[...] + jnp.log(l)
        lse_ref[...] = jnp.where(lo, lse[:tq], lse[tq:])

    kv = pl.BlockSpec((None, s, LANES), lambda p, i: (p // 2, 0, 0))
    qo = pl.BlockSpec((tq, LANES), lambda p, i: (i, p))
    grid = (4, s // tq)
    hosted, xi, xo, xs, xsem = _host_exchange(body, 3, 2, grid, xchg)
    res = pl.pallas_call(
        hosted, name="flash_a_fwd" + ("_x" if xchg else ""), grid=grid, in_specs=[qo, kv, kv] + xi, out_specs=[qo, qo] + xo,
        out_shape=[SDS((s, 512), bf16), SDS((s, 512), f32)] + xs,
        scratch_shapes=[pltpu.VMEM((2 * tq, LANES), bf16), pltpu.VMEM((2 * tq, 1), f32),
                        pltpu.VMEM((2 * tq, LANES), f32)] + xsem,
        compiler_params=_cparams())(qa, kd, vd, *(xchg[0] if xchg else []))
    return res[:2], res[2:]


def _flash_a_bwd(qa, kd, vd, oa, lse, doa, s, xchg=None):
    tq, tk = FA_TQ, FA_TK
    nk = s // tk

    def body(q_ref, do_ref, o_ref, lse_ref, k_ref, v_ref, dq_ref, dk_ref, dv_ref, qs, dos, lse_s, dl_s, dq_s):
        i = pl.program_id(1)
        lo = _lo_mask((tq, LANES))

        @pl.when(i == 0)
        def _():
            dk_ref[...] = jnp.zeros(dk_ref.shape, f32)
            dv_ref[...] = jnp.zeros(dv_ref.shape, f32)

        for pp in range(2):
            sl = slice(pp * LANES, (pp + 1) * LANES)
            do = do_ref[:, sl]
            qs[...] = _stack_low(q_ref[:, sl], lo, bf16, SCALE)
            dos[...] = _stack_low(do, lo, bf16)
            dl_s[...] = _rows_of(do * o_ref[:, sl].astype(f32), lo, "sum") * HEAD_DIM
            lse_s[...] = _rows_of(lse_ref[:, sl], lo, "max")
            dq_s[...] = jnp.zeros(dq_s.shape, f32)

            def chunk(j, carry):
                rows = pl.ds(pl.multiple_of(j * tk, tk), tk)
                k, v = k_ref[rows, :], v_ref[rows, :]
                q_, do_ = qs[...], dos[...]
                p = jnp.exp(_dot(q_, k, "nt") - lse_s[...])
                ds = (p * (_dot(do_, v, "nt") - dl_s[...])).astype(bf16)
                dv_ref[rows, :] += _dot(p.astype(bf16), do_, "tn")
                dk_ref[rows, :] += _dot(ds, q_, "tn")
                dq_s[...] += _dot(ds, k, "nn")
                return carry

            lax.fori_loop(0, nk, chunk, 0)
            dq_ref[:, sl] = _unstack_low(dq_s[...], lo, tq) * SCALE

    grp = lambda: pl.BlockSpec((tq, 2 * LANES), lambda c, i: (i, c))
    kv = lambda: pl.BlockSpec((None, s, LANES), lambda c, i: (c, 0, 0))
    grid = (2, s // tq)
    hosted, xi, xo, xs, xsem = _host_exchange(body, 6, 3, grid, xchg)
    res = pl.pallas_call(
        hosted, name="flash_a_bwd" + ("_x" if xchg else ""), grid=grid,
        in_specs=[grp(), grp(), grp(), grp(), kv(), kv()] + xi, out_specs=[grp(), kv(), kv()] + xo,
        out_shape=[SDS((s, 512), f32), SDS((2, s, LANES), f32), SDS((2, s, LANES), f32)] + xs,
        scratch_shapes=[pltpu.VMEM((2 * tq, LANES), bf16), pltpu.VMEM((2 * tq, LANES), bf16), pltpu.VMEM((2 * tq, 1), f32),
                        pltpu.VMEM((2 * tq, 1), f32), pltpu.VMEM((2 * tq, LANES), f32)] + xsem,
        compiler_params=_cparams())(qa, doa, oa, lse, kd, vd, *(xchg[0] if xchg else []))
    return res[:3], res[3:]


BH = 128
BQ_MAX = 512


class _BandGeom:
    def __init__(self, gi, s):
        self.d = B_GROUP_CFG[gi][1]
        self.l_sub = s // self.d
        self.bq = min(BQ_MAX, self.l_sub)
        self.nb = self.l_sub // self.bq
        self.grid = (self.d, self.nb, 2)
        per, last = self.bq // BH, self.l_sub // BH - 1
        self.in_col = lambda r, pp: r * 6 + gi * 2 + pp
        self.out_col = lambda r, pp: r * 2 + pp
        self.cur = lambda col: pl.BlockSpec((self.bq, LANES), lambda r, n, pp: (n, col(r, pp)))
        self.prev = lambda col: pl.BlockSpec((BH, LANES), lambda r, n, pp: (jnp.maximum(n * per - 1, 0), col(r, pp)))
        self.next = lambda col: pl.BlockSpec((BH, LANES), lambda r, n, pp: (jnp.minimum((n + 1) * per, last), col(r, pp)))
        self.window = lambda col: [self.prev(col), self.cur(col), self.next(col)]
        self.view_in = lambda a: a.reshape(self.l_sub, self.d * 768)
        self.view_out = lambda a: a.reshape(self.l_sub, self.d * 256)

    def fill_band_bias(self, bias_s, q_rows, q_off, k_off):
        @pl.when((pl.program_id(0) == 0) & (pl.program_id(1) == 0) & (pl.program_id(2) == 0))
        def _():
            r = lax.broadcasted_iota(jnp.int32, bias_s.shape, 0)
            q = jnp.where(r >= q_rows, r - q_rows, r) + q_off
            k = lax.broadcasted_iota(jnp.int32, bias_s.shape, 1) + k_off
            bias_s[...] = jnp.where(jnp.abs(q - k) <= 64, 0.0, NEG_INF)

    def edge_bias(self, first_pos, n, stacked_rows=False):
        if stacked_rows:
            r = lax.broadcasted_iota(jnp.int32, (2 * n, 1), 0)
            idx = jnp.where(r >= n, r - n, r)
        else:
            idx = lax.broadcasted_iota(jnp.int32, (1, n), 1)
        pos = first_pos + idx
        return jnp.where((pos >= 0) & (pos < self.l_sub), 0.0, NEG_INF)


def _cat3(a, b, c):
    return jnp.concatenate([a[...], b[...], c[...]], axis=0)


def _attn_b_fwd(qb, kb, vb, gi, s):
    g = _BandGeom(gi, s)
    bq = g.bq

    def body(q_ref, kp, kc, kn, vp, vc, vn, o_ref, lse_ref, bias_s):
        n = pl.program_id(1)
        g.fill_band_bias(bias_s, bq, 0, -BH)
        lo = _lo_mask((bq, LANES))
        qs = _stack_heads(q_ref[...], lo, bf16)
        k, v = _cat3(kp, kc, kn), _cat3(vp, vc, vn)
        sc = _dot(qs, k, "nt") * SCALE + bias_s[...] + g.edge_bias(n * bq - BH, bq + 2 * BH)
        m = jnp.max(sc, axis=1, keepdims=True)
        p = jnp.exp(sc - m)
        den = jnp.sum(p, axis=1, keepdims=True)
        o = _dot(p.astype(bf16), v, "nn") / den
        o_ref[...] = jnp.where(lo, o[:bq], o[bq:])
        lse = m + jnp.log(den)
        lse_ref[...] = jnp.where(lo, lse[:bq], lse[bq:])

    out = g.cur(g.out_col)
    qv, kv, vv = g.view_in(qb), g.view_in(kb), g.view_in(vb)
    o, lse = pl.pallas_call(
        body, name=f"attn_b_fwd{gi}", grid=g.grid,
        in_specs=[g.cur(g.in_col)] + g.window(g.in_col) * 2, out_specs=[out, out],
        out_shape=[SDS((g.l_sub, g.d * 256), f32), SDS((g.l_sub, g.d * 256), f32)],
        scratch_shapes=[pltpu.VMEM((2 * bq, bq + 2 * BH), f32)],
        compiler_params=_cparams())(qv, kv, kv, kv, vv, vv, vv)
    return o.reshape(s, 256), lse.reshape(s, 256)


def _attn_b_dq(qb, kb, vb, dob, lse, delta, gi, s):
    g = _BandGeom(gi, s)
    bq = g.bq

    def body(q_ref, kp, kc, kn, vp, vc, vn, do_ref, lse_ref, dl_ref, dq_ref, bias_s):
        n = pl.program_id(1)
        g.fill_band_bias(bias_s, bq, 0, -BH)
        lo = _lo_mask((bq, LANES))
        qs = _stack_heads(q_ref[...], lo, bf16)
        dos = _stack_heads(do_ref[...], lo, bf16)
        k, v = _cat3(kp, kc, kn), _cat3(vp, vc, vn)
        sc = _dot(qs, k, "nt") * SCALE + bias_s[...] + g.edge_bias(n * bq - BH, bq + 2 * BH)
        p = jnp.exp(sc - _rows_of(lse_ref[...], lo, "max"))
        dp = _dot(dos, v, "nt")
        ds = (p * (dp - _rows_of(dl_ref[...], lo, "sum")) * SCALE).astype(bf16)
        dqs = _dot(ds, k, "nn")
        dq_ref[...] = jnp.where(lo, dqs[:bq], dqs[bq:])

    out = g.cur(g.out_col)
    qv, kv, vv = g.view_in(qb), g.view_in(kb), g.view_in(vb)
    dq = pl.pallas_call(
        body, name=f"attn_b_dq{gi}", grid=g.grid,
        in_specs=[g.cur(g.in_col)] + g.window(g.in_col) * 2 + [out, out, out], out_specs=out,
        out_shape=SDS((g.l_sub, g.d * 256), f32), scratch_shapes=[pltpu.VMEM((2 * bq, bq + 2 * BH), f32)],
        compiler_params=_cparams())(qv, kv, kv, kv, vv, vv, vv, g.view_out(dob), g.view_out(lse), g.view_out(delta))
    return dq.reshape(s, 256)


def _attn_b_dkv(qb, kb, vb, dob, lse, delta, gi, s):
    g = _BandGeom(gi, s)
    bq = g.bq
    nq = bq + 2 * BH

    def body(k_ref, v_ref, qp, qc, qn, dop, doc, don, lp, lc, ln, dp_, dc_, dn_, dk_ref, dv_ref, bias_s):
        m = pl.program_id(1)
        g.fill_band_bias(bias_s, nq, -BH, 0)
        lo = _lo_mask((nq, LANES))
        qs = _stack_heads(_cat3(qp, qc, qn), lo, bf16)
        dos = _stack_heads(_cat3(dop, doc, don), lo, bf16)
        lses = _rows_of(_cat3(lp, lc, ln), lo, "max")
        dls = _rows_of(_cat3(dp_, dc_, dn_), lo, "sum")
        k, v = k_ref[...], v_ref[...]
        sc = _dot(qs, k, "nt") * SCALE + bias_s[...] + g.edge_bias(m * bq - BH, nq, stacked_rows=True)
        p = jnp.exp(sc - lses)
        dpv = _dot(dos, v, "nt")
        ds = (p * (dpv - dls) * SCALE).astype(bf16)
        dv_ref[...] = _dot(p.astype(bf16), dos, "tn")
        dk_ref[...] = _dot(ds, qs, "tn")

    cur = g.cur(g.out_col)
    qv, kv, vv = g.view_in(qb), g.view_in(kb), g.view_in(vb)
    dk, dv = pl.pallas_call(
        body, name=f"attn_b_dkv{gi}", grid=g.grid,
        in_specs=[g.cur(g.in_col)] * 2 + g.window(g.in_col) + g.window(g.out_col) * 3, out_specs=[cur, cur],
        out_shape=[SDS((g.l_sub, g.d * 256), f32), SDS((g.l_sub, g.d * 256), f32)],
        scratch_shapes=[pltpu.VMEM((2 * nq, bq), f32)],
        compiler_params=_cparams())(kv, vv, qv, qv, qv, *([g.view_out(dob)] * 3), *([g.view_out(lse)] * 3),
                                    *([g.view_out(delta)] * 3))
    return dk.reshape(s, 256), dv.reshape(s, 256)


def _merge_b(o_list, lse_list, s):
    ts = 256

    def fn(o0, o1, o2, l0, l1, l2):
        m = jnp.maximum(jnp.maximum(l0, l1), l2)
        w0, w1, w2 = jnp.exp(l0 - m), jnp.exp(l1 - m), jnp.exp(l2 - m)
        den = w0 + w1 + w2
        return (w0 * o0 + w1 * o1 + w2 * o2) / den, m + jnp.log(den)

    row = pl.BlockSpec((ts, 256), lambda i: (i, 0))
    return _ew("merge_b", fn, (s // ts,), [*o_list, *lse_list], [row] * 6,
               [SDS((s, 256), bf16), SDS((s, 256), f32)], [row, row])


def _delta_b(dob, ob, s):
    ts = 256

    def fn(do, o):
        prod = do * o.astype(f32)
        lo = _lo_mask((ts, LANES))
        return jnp.concatenate([_head_stat(prod[:, :LANES], lo), _head_stat(prod[:, LANES:], lo)], axis=1)

    row = pl.BlockSpec((ts, 256), lambda i: (i, 0))
    return _ew("delta_b", fn, (s // ts,), [dob, ob], [row, row], [SDS((s, 256), f32)], [row])[0]


def _band(t0, u0, w, nt, nu, s, transposed):
    t = t0 + lax.broadcasted_iota(jnp.int32, (nt, nu), 0)
    u = u0 + lax.broadcasted_iota(jnp.int32, (nt, nu), 1)
    if transposed:
        lo, hi = jnp.clip(t - w // 2 + 1, 0, s), jnp.clip(t + w // 2 + 1, 0, s)
    else:
        lo, hi = jnp.clip(t - w // 2, 0, s), jnp.clip(t + w - w // 2, 0, s)
    return ((u >= lo) & (u < hi)).astype(f32)


def _pool_cnt(t0, w, nt, s):
    t = t0 + lax.broadcasted_iota(jnp.int32, (nt, 1), 0)
    return (jnp.clip(t + w - w // 2, 0, s) - jnp.clip(t - w // 2, 0, s)).astype(f32)


POOL_HALO = 8


def _halo_specs(tp, s, cb):
    per, last = tp // POOL_HALO, s // POOL_HALO - 1
    return [pl.BlockSpec((POOL_HALO, 512), lambda i: (jnp.maximum(i * per - 1, 0), cb)),
            pl.BlockSpec((tp, 512), lambda i: (i, cb)),
            pl.BlockSpec((POOL_HALO, 512), lambda i: (jnp.minimum((i + 1) * per, last), cb))]


def _pool_fwd(z, lin, scale, s):
    tp = 256
    nt = s // tp

    def body(up, uc, un, lin_ref, sc_ref, pooled_o, pc_o):
        i = pl.program_id(0)
        ext = jnp.concatenate([up[...], uc[...], un[...]], axis=0)
        for g, w in enumerate(POOL_WINDOWS):
            sl = slice(g * LANES, (g + 1) * LANES)
            band = _band(i * tp, i * tp - POOL_HALO, w, tp, tp + 2 * POOL_HALO, s, False)
            sm = jnp.dot(band, ext[:, sl], preferred_element_type=f32, precision=lax.Precision.HIGHEST)
            pooled = (sm / _pool_cnt(i * tp, w, tp, s) - uc[:, sl]).astype(bf16)
            pooled_o[:, sl] = pooled
            mixed = _dot(pooled, lin_ref[g].astype(bf16), "nn")
            pc_o[:, sl] = (mixed * sc_ref[:, sl]).astype(bf16)

    row = pl.BlockSpec((tp, 512), lambda i: (i, 0))
    return pl.pallas_call(
        body, name="pool_fwd", grid=(nt,),
        in_specs=_halo_specs(tp, s, C_UC // 512)
        + [pl.BlockSpec((4, LANES, LANES), lambda i: (0, 0, 0)), pl.BlockSpec((1, 512), lambda i: (0, 0))],
        out_specs=[row, row], out_shape=[SDS((s, 512), bf16), SDS((s, 512), bf16)],
        compiler_params=_cparams())(z, z, z, lin, scale)


def _pool_bwd1(dpc, pooled, lin, scale, s):
    tp = 256

    def body(dpc_ref, pooled_ref, lin_ref, sc_ref, dpn_o, dlin_o, dsc_o):
        i = pl.program_id(0)
        dsc = []
        for g, w in enumerate(POOL_WINDOWS):
            sl = slice(g * LANES, (g + 1) * LANES)
            pooled = pooled_ref[:, sl]
            linb = lin_ref[g].astype(bf16)
            mixed = _dot(pooled, linb, "nn")
            dpc_g = dpc_ref[:, sl]
            dsc.append(jnp.sum(dpc_g * mixed, axis=0, keepdims=True))
            dmixed = (dpc_g * sc_ref[:, sl]).astype(bf16)
            dpn_o[:, sl] = _dot(dmixed, linb, "nt") / _pool_cnt(i * tp, w, tp, s)
            dl = _dot(pooled, dmixed, "tn")

            @pl.when(i == 0)
            def _(g=g, dl=dl):
                dlin_o[g] = dl

            @pl.when(i > 0)
            def _(g=g, dl=dl):
                dlin_o[g] += dl

        dsc = jnp.concatenate(dsc, axis=1)

        @pl.when(i == 0)
        def _():
            dsc_o[...] = dsc

        @pl.when(i > 0)
        def _():
            dsc_o[...] += dsc

    row = pl.BlockSpec((tp, 512), lambda i: (i, 0))
    linspec = pl.BlockSpec((4, LANES, LANES), lambda i: (0, 0, 0))
    one = pl.BlockSpec((1, 512), lambda i: (0, 0))
    return pl.pallas_call(
        body, name="pool_bwd1", grid=(s // tp,), in_specs=[row, row, linspec, one], out_specs=[row, linspec, one],
        out_shape=[SDS((s, 512), f32), SDS((4, LANES, LANES), f32), SDS((1, 512), f32)],
        compiler_params=_cparams())(dpc, pooled, lin, scale)


def _pool_bwd2(dpn, dz, s):
    tp = 256
    nt = s // tp

    def body(dp, dc, dn, dz_in, dz_o):
        i = pl.program_id(0)
        ext = jnp.concatenate([dp[...], dc[...], dn[...]], axis=0)
        for g, w in enumerate(POOL_WINDOWS):
            sl = slice(g * LANES, (g + 1) * LANES)
            band = _band(i * tp, i * tp - POOL_HALO, w, tp, tp + 2 * POOL_HALO, s, True)
            sm = jnp.dot(band, ext[:, sl], preferred_element_type=f32, precision=lax.Precision.HIGHEST)
            dz_o[:, sl] = (sm - dc[:, sl] * _pool_cnt(i * tp, w, tp, s)).astype(bf16)

    return pl.pallas_call(
        body, name="pool_bwd2", grid=(nt,),
        in_specs=_halo_specs(tp, s, 0) + [pl.BlockSpec(memory_space=pl.ANY)],
        out_specs=pl.BlockSpec((tp, 512), lambda i: (i, C_UC // 512)), out_shape=SDS((s, IN_WIDTH), bf16),
        input_output_aliases={3: 0}, compiler_params=_cparams())(dpn, dpn, dpn, dz)


GW = 512


MIX_TM = 256
BR_WIDTHS = (512, 256, 512)
N_GATE_BLOCKS = 3 * D_MODEL // GW


def _gate_specs(ts):
    return [pl.BlockSpec((ts, GW), lambda i, q=q: (i, C_GZ // GW + q)) for q in range(N_GATE_BLOCKS)]


def _mix_out_fwd(acts, z, b_gate, x, w, s):
    ts = MIX_TM

    def body(oa, ob, pc, *rest):
        gz = rest[:N_GATE_BLOCKS]
        bias, x_ref, wa, wb, wc, wo, y3_o, m_o, x1_o = rest[N_GATE_BLOCKS:]
        for b, (act, wt) in enumerate(((oa, wa), (ob, wb), (pc, wc))):
            a = act[...]
            for j in range(N_CHIPS):
                y3_o[b, :, j * 256:(j + 1) * 256] = _dot(a, wt[j], "nn")
        for h in range(2):
            hs = slice(h * GW, (h + 1) * GW)
            tot = jnp.zeros((ts, GW), f32)
            for b in range(3):
                g = jax.nn.sigmoid(gz[2 * b + h][...] + bias[:, b * D_MODEL + h * GW:b * D_MODEL + (h + 1) * GW])
                tot += g * y3_o[b, :, hs]
            m_o[:, hs] = tot.astype(bf16)
        acc = x_ref[...]
        for j in range(N_CHIPS):
            acc += _dot(m_o[:, j * 256:(j + 1) * 256], wo[j], "nn")
        x1_o[...] = acc

    row = lambda width: pl.BlockSpec((ts, width), lambda i: (i, 0))
    res3 = lambda a: pl.BlockSpec(a.shape, lambda i: (0, 0, 0))
    wts = [w["w_branch_a"], w["w_branch_b"], w["w_branch_c"], w["w_out"]]
    return pl.pallas_call(
        body, name="mix_out_fwd", grid=(s // ts,),
        in_specs=[row(BR_WIDTHS[0]), row(BR_WIDTHS[1]), row(BR_WIDTHS[2])] + _gate_specs(ts)
        + [pl.BlockSpec((1, 3 * D_MODEL), lambda i: (0, 0)), row(D_MODEL)] + [res3(a) for a in wts],
        out_specs=[pl.BlockSpec((3, ts, D_MODEL), lambda i: (0, i, 0)), row(D_MODEL), row(D_MODEL)],
        out_shape=[SDS((3, s, D_MODEL), f32), SDS((s, D_MODEL), bf16), SDS((s, D_MODEL), f32)],
        compiler_params=_cparams())(*acts, *([z] * N_GATE_BLOCKS), b_gate, x, *wts)


def _mix_out_bwd(dx1, y3, z, b_gate, w, s):
    ts = MIX_TM

    def body(dx_ref, y3_ref, *rest):
        gz = rest[:N_GATE_BLOCKS]
        bias, wa, wb, wc, wo, dy_o, dz_o, db_o, doa_o, dob_o, dpc_o, dm_s = rest[N_GATE_BLOCKS:]
        i = pl.program_id(0)
        dx = dx_ref[...].astype(bf16)
        for j in range(N_CHIPS):
            dm_s[:, j * 256:(j + 1) * 256] = _dot(dx, wo[j], "nt")
        dz_o[:, 0:C_GZ] = jnp.zeros((ts, C_GZ), bf16)
        dbs = []
        for b in range(3):
            for h in range(2):
                hs = slice(h * GW, (h + 1) * GW)
                g = jax.nn.sigmoid(gz[2 * b + h][...] + bias[:, b * D_MODEL + h * GW:b * D_MODEL + (h + 1) * GW])
                dm = dm_s[:, hs]
                dy_o[b, :, hs] = (dm * g).astype(bf16)
                dgz = dm * y3_ref[b, :, hs] * g * (1.0 - g)
                c0 = C_GZ + b * D_MODEL + h * GW
                dz_o[:, c0:c0 + GW] = dgz.astype(bf16)
                dbs.append(jnp.sum(dgz, axis=0, keepdims=True))
        db = jnp.concatenate(dbs, axis=1)

        @pl.when(i == 0)
        def _():
            db_o[...] = db

        @pl.when(i > 0)
        def _():
            db_o[...] += db

        for b, (wt, out) in enumerate(((wa, doa_o), (wb, dob_o), (wc, dpc_o))):
            acc = _dot(dy_o[b, :, 0:256], wt[0], "nt")
            for j in range(1, N_CHIPS):
                acc += _dot(dy_o[b, :, j * 256:(j + 1) * 256], wt[j], "nt")
            out[...] = acc

    row = lambda width: pl.BlockSpec((ts, width), lambda i: (i, 0))
    res3 = lambda a: pl.BlockSpec(a.shape, lambda i: (0, 0, 0))
    blk3 = pl.BlockSpec((3, ts, D_MODEL), lambda i: (0, i, 0))
    one = pl.BlockSpec((1, 3 * D_MODEL), lambda i: (0, 0))
    wts = [w["w_branch_a"], w["w_branch_b"], w["w_branch_c"], w["w_out"]]
    return pl.pallas_call(
        body, name="mix_out_bwd", grid=(s // ts,),
        in_specs=[row(D_MODEL), blk3] + _gate_specs(ts) + [one] + [res3(a) for a in wts],
        out_specs=[blk3, row(IN_WIDTH), one, row(BR_WIDTHS[0]), row(BR_WIDTHS[1]), row(BR_WIDTHS[2])],
        out_shape=[SDS((3, s, D_MODEL), bf16), SDS((s, IN_WIDTH), bf16), SDS((1, 3 * D_MODEL), f32),
                   SDS((s, BR_WIDTHS[0]), f32), SDS((s, BR_WIDTHS[1]), f32), SDS((s, BR_WIDTHS[2]), f32)],
        scratch_shapes=[pltpu.VMEM((ts, D_MODEL), f32)],
        compiler_params=_cparams())(dx1, y3, *([z] * N_GATE_BLOCKS), b_gate, *wts)


def _loss_grad(y, tgt, s):
    ts = 256

    def fn(yt, tt):
        e = yt - tt
        return e * (1.0 / D_MODEL), jnp.sum(e * e, axis=0, keepdims=True) * (0.5 / D_MODEL)

    row = pl.BlockSpec((ts, D_MODEL), lambda i: (i, 0))
    one = pl.BlockSpec((1, D_MODEL), lambda i: (0, 0))
    return _ew("loss_grad", fn, (s // ts,), [y, tgt], [row, row], [SDS((s, D_MODEL), f32), SDS((1, D_MODEL), f32)],
               [row, one], n_acc=1)


TM = 512


def _layer_fwd(x, w, sm, tabs, s, xchg=None):
    nm = s // TM
    hb = _norm_fwd(x, sm["norm_mix"], s)
    z = _mm("mm_z", [hb, w["w_in"]],
            [pl.BlockSpec((TM, D_MODEL), lambda j, i, k: (i, 0)), pl.BlockSpec((None, D_MODEL, 1664), lambda j, i, k: (j, 0, 0))],
            SDS((s, IN_WIDTH), f32), pl.BlockSpec((TM, 1664), lambda j, i, k: (i, j)), (4, nm, 1), 1, "nn", None)
    qa, kd, vd, qb, kb, vb = _prep_fwd(z, tabs, sm["gains"], s)
    (oa, lse_a), landed = _flash_a_fwd(qa, kd, vd, s, xchg)
    if xchg is not None:
        w = dict(w, **dict(zip(REST + ("next_w_in",), landed)))
    ob_parts = [_attn_b_fwd(qb, kb, vb, gi, s) for gi in range(3)]
    ob, lse_b = _merge_b([p[0] for p in ob_parts], [p[1] for p in ob_parts], s)
    pooled, pc = _pool_fwd(z, sm["pool_lin"], sm["pool_scale"], s)

    y3, merged, x1 = _mix_out_fwd([oa, ob, pc], z, sm["b_gate"], x, w, s)
    h2 = _norm_fwd(x1, sm["norm_ffn"], s)
    g4, u4, a4 = _ffn_up(h2, w["w_ffn_gate"], w["w_ffn_up"], s)
    x2 = _mm_shard_sum("mm_down", [(a4, w["w_ffn_down"])], "nn", D_MODEL, s, TM, add=x1)
    saved = dict(x=x, hb=hb, z=z, qa=qa, kd=kd, vd=vd, qb=qb, kb=kb, vb=vb, oa=oa, lse_a=lse_a, ob=ob, lse_b=lse_b,
                 pooled=pooled, pc=pc, y3=y3, merged=merged, x1=x1, h2=h2, g4=g4, u4=u4, a4=a4)
    return x2, saved, w


def _ffn_up(h2, wg, wu, s):
    nm = s // TM

    def body(h_ref, wg_ref, wu_ref, g_o, u_o, a_o):
        h = h_ref[...]
        g = _dot(h, wg_ref[...], "nn")
        u = _dot(h, wu_ref[...], "nn")
        g_o[...] = g
        u_o[...] = u
        a_o[...] = (g * jax.nn.sigmoid(g) * u).astype(bf16)

    wsp = pl.BlockSpec((None, D_MODEL, FF_SHARD), lambda j, i: (j, 0, 0))
    osp = pl.BlockSpec((None, TM, FF_SHARD), lambda j, i: (j, i, 0))
    return pl.pallas_call(
        body, name="ffn_up", grid=(4, nm), in_specs=[pl.BlockSpec((TM, D_MODEL), lambda j, i: (i, 0)), wsp, wsp],
        out_specs=[osp, osp, osp],
        out_shape=[SDS((4, s, FF_SHARD), f32), SDS((4, s, FF_SHARD), f32), SDS((4, s, FF_SHARD), bf16)],
        compiler_params=_cparams())(h2, wg, wu)


def _ffn_bwd_act(dx2, wd, g4, u4, s):
    nm = s // TM

    def body(dx_ref, wd_ref, g_ref, u_ref, dg_o, du_o):
        da = _dot(dx_ref[...].astype(bf16), wd_ref[...], "nt")
        g, u = g_ref[...], u_ref[...]
        sg = jax.nn.sigmoid(g)
        dg_o[...] = (da * u * sg * (1.0 + g * (1.0 - sg))).astype(bf16)
        du_o[...] = (da * g * sg).astype(bf16)

    osp = pl.BlockSpec((None, TM, FF_SHARD), lambda j, i: (j, i, 0))
    return pl.pallas_call(
        body, name="ffn_bwd_act", grid=(4, nm),
        in_specs=[pl.BlockSpec((TM, D_MODEL), lambda j, i: (i, 0)), pl.BlockSpec((None, FF_SHARD, D_MODEL), lambda j, i: (j, 0, 0)), osp, osp],
        out_specs=[osp, osp], out_shape=[SDS((4, s, FF_SHARD), bf16), SDS((4, s, FF_SHARD), bf16)],
        compiler_params=_cparams())(dx2, wd, g4, u4)


def _mm_shard_sum(name, pairs, dims, out_width, s, tm, add=None):
    n = len(pairs)

    def body(*refs):
        a_refs, w_refs = refs[:n], refs[n:2 * n]
        o_ref = refs[-1]
        acc = refs[2 * n][...] if add is not None else None
        for a_ref, w_ref in zip(a_refs, w_refs):
            for j in range(N_CHIPS):
                if len(a_ref.shape) == 3:
                    a = a_ref[j]
                else:
                    c = a_ref.shape[1] // N_CHIPS
                    a = a_ref[:, j * c:(j + 1) * c]
                part = _dot(a.astype(bf16), w_ref[j], dims)
                acc = part if acc is None else acc + part
        o_ref[...] = acc

    a_specs = [pl.BlockSpec((N_CHIPS, tm, a.shape[2]), lambda i: (0, i, 0)) if a.ndim == 3
               else pl.BlockSpec((tm, a.shape[1]), lambda i: (i, 0)) for a, _ in pairs]
    w_specs = [pl.BlockSpec(wt.shape, lambda i: (0, 0, 0)) for _, wt in pairs]
    row = pl.BlockSpec((tm, out_width), lambda i: (i, 0))
    extra, extra_specs = ([add], [row]) if add is not None else ([], [])
    return pl.pallas_call(
        body, name=name, grid=(s // tm,), in_specs=a_specs + w_specs + extra_specs, out_specs=row,
        out_shape=SDS((s, out_width), f32), compiler_params=_cparams())(
            *[a for a, _ in pairs], *[wt for _, wt in pairs], *extra)


GRAD_WIRE_DTYPE = bf16


WG_TK = 2048


def _wgrad(name, a, a_spec, b, b_spec, out_shape, out_block, s):
    nk = s // min(WG_TK, s)
    return _mm(name, [a, b], [a_spec, b_spec], SDS(out_shape, GRAD_WIRE_DTYPE),
               pl.BlockSpec((None,) + out_block, lambda j, k: (j, 0, 0)), (4, nk), nk, "tn", out_block)


def _layer_bwd(dx2, sv, w, sm, tabs, s, pending_mixer=None):
    tk = min(WG_TK, s)
    tok = lambda width: pl.BlockSpec((tk, width), lambda j, k: (k, 0))
    g_wd = _wgrad("wg_down", sv["a4"], pl.BlockSpec((None, tk, FF_SHARD), lambda j, k: (j, k, 0)), dx2, tok(D_MODEL),
                  (4, FF_SHARD, D_MODEL), (FF_SHARD, D_MODEL), s)
    dg4, du4 = _ffn_bwd_act(dx2, w["w_ffn_down"], sv["g4"], sv["u4"], s)
    sh704 = pl.BlockSpec((None, tk, FF_SHARD), lambda j, k: (j, k, 0))
    g_wg = _wgrad("wg_gate", sv["h2"], tok(D_MODEL), dg4, sh704, (4, D_MODEL, FF_SHARD), (D_MODEL, FF_SHARD), s)
    g_wu = _wgrad("wg_up", sv["h2"], tok(D_MODEL), du4, sh704, (4, D_MODEL, FF_SHARD), (D_MODEL, FF_SHARD), s)
    dh2 = _mm_shard_sum("ffn_bwd_dh", [(dg4, w["w_ffn_gate"]), (du4, w["w_ffn_up"])], "nt", D_MODEL, s, TM)
    dx1, g_norm_ffn = _norm_bwd(sv["x1"], sm["norm_ffn"], dh2, dx2, s)
    colblk = lambda width: pl.BlockSpec((tk, width), lambda j, k: (k, j))
    g_wo = _wgrad("wg_out", sv["merged"], colblk(256), dx1, tok(D_MODEL), (4, 256, D_MODEL), (256, D_MODEL), s)
    dy3, dz, g_bgate, doa, dob, dpc = _mix_out_bwd(dx1, sv["y3"], sv["z"], sm["b_gate"], w, s)
    br_grads = []
    for b, (nm_, act, kdim) in enumerate((("a", sv["oa"], 512), ("b", sv["ob"], 256), ("c", sv["pc"], 512))):
        br_grads.append(_wgrad("wg_br" + nm_, act, tok(kdim), dy3,
                               pl.BlockSpec((None, tk, 256), lambda j, k, b=b: (b, k, j)), (4, kdim, 256), (kdim, 256), s))
    dpn, g_lin, g_scale = _pool_bwd1(dpc, sv["pooled"], sm["pool_lin"], sm["pool_scale"], s)
    dz = _pool_bwd2(dpn, dz, s)
    delta = _delta_b(dob, sv["ob"], s)
    dqb, dkb, dvb = [], [], []
    for gi in range(3):
        dqb.append(_attn_b_dq(sv["qb"], sv["kb"], sv["vb"], dob, sv["lse_b"], delta, gi, s))
        dk, dv = _attn_b_dkv(sv["qb"], sv["kb"], sv["vb"], dob, sv["lse_b"], delta, gi, s)
        dkb.append(dk)
        dvb.append(dv)
    if pending_mixer is not None:
        xchg = (list(pending_mixer) + [g_wg, g_wu, g_wd], lambda ref, peer: ref.at[peer])
    else:
        xchg = None
    (dqa, dkd, dvd), landed = _flash_a_bwd(sv["qa"], sv["kd"], sv["vd"], sv["oa"], sv["lse_a"], doa, s, xchg)
    dz, g_gains = _prep_bwd(sv["z"], tabs, sm["gains"], dqa, dkd, dvd, dqb, dkb, dvb, dz, s)
    g_win = _wgrad("wg_in", sv["hb"], tok(D_MODEL), dz, colblk(1664), (4, D_MODEL, 1664), (D_MODEL, 1664), s)
    dh = _mm_shard_sum("mm_dh", [(dz, w["w_in"])], "nt", D_MODEL, s, 256)
    dx0, g_norm_mix = _norm_bwd(sv["x"], sm["norm_mix"], dh, dx1, s)
    big = dict(w_in=g_win, w_branch_a=br_grads[0], w_branch_b=br_grads[1], w_branch_c=br_grads[2], w_out=g_wo,
               w_ffn_gate=g_wg, w_ffn_up=g_wu, w_ffn_down=g_wd)
    gg = g_gains[0:4, :HEAD_DIM] + g_gains[0:4, HEAD_DIM:]
    small = jnp.concatenate([g_norm_mix.reshape(-1), g_bgate.reshape(-1), gg.reshape(-1), g_lin.reshape(-1),
                             g_scale.reshape(-1), g_norm_ffn.reshape(-1)]).reshape(SMALL_ROWS, LANES)
    return dx0, big, small, landed


def _mesh_pos():
    return lax.axis_index("x"), lax.axis_index("y"), lax.axis_index("c")


def _chip_copies(src, dst, pick_src, send_sems, recv_sems, loc_sems, receiving=True):
    n = len(src)
    x, y, c = _mesh_pos()
    me = 2 * x + y
    local = [pltpu.make_async_copy(pick_src(src[t], me), dst[t].at[me], loc_sems.at[t]) for t in range(n)]
    push, recv = [], []
    for j, (px, py) in enumerate([(1 - x, y), (x, 1 - y), (1 - x, 1 - y)]):
        peer = 2 * px + py
        for t in range(n):
            sems = dict(send_sem=send_sems.at[j * n + t], recv_sem=recv_sems.at[j * n + t],
                        device_id=(px, py, c), device_id_type=MESH)
            push.append(pltpu.make_async_remote_copy(src_ref=pick_src(src[t], peer), dst_ref=dst[t].at[me], **sems))
            if receiving:
                recv.append(pltpu.make_async_remote_copy(src_ref=pick_src(src[t], peer), dst_ref=dst[t].at[peer], **sems))
    return local, push, recv


def _chip_start(copies):
    local, push, _ = copies
    for cp in local + push:
        cp.start()


def _chip_wait(copies):
    local, push, recv = copies
    for cp in push:
        cp.wait_send()
    for cp in recv:
        cp.wait_recv()
    for cp in local:
        cp.wait()


def _chip_sems(n):
    return [pltpu.SemaphoreType.DMA((3 * n,)), pltpu.SemaphoreType.DMA((3 * n,)), pltpu.SemaphoreType.DMA((n,))]


def _chip_landing(srcs):
    return [SDS((N_CHIPS,) + tuple(a.shape[1:]), a.dtype) for a in srcs]


def _chip_exchange(name, srcs, pick_src):
    n = len(srcs)
    any_spec = pl.BlockSpec(memory_space=pl.ANY)

    def body(*refs):
        copies = _chip_copies(refs[:n], refs[n:2 * n], pick_src, *refs[2 * n:])
        _chip_start(copies)
        _chip_wait(copies)

    return pl.pallas_call(
        body, name=name, in_specs=[any_spec] * n, out_specs=[any_spec] * n, out_shape=_chip_landing(srcs),
        scratch_shapes=_chip_sems(n), compiler_params=pltpu.CompilerParams())(*srcs)


def _sibling_exchange(name, srcs):
    n = len(srcs)
    any_spec = pl.BlockSpec(memory_space=pl.ANY)

    def body(*refs):
        src, dst = refs[:n], refs[n:2 * n]
        send_sems, recv_sems = refs[2 * n:]
        x, y, c = _mesh_pos()
        cps = [pltpu.make_async_remote_copy(src_ref=src[t], dst_ref=dst[t], send_sem=send_sems.at[t], recv_sem=recv_sems.at[t],
                                            device_id=(x, y, 1 - c), device_id_type=MESH) for t in range(n)]
        for cp in cps:
            cp.start()
        for cp in cps:
            cp.wait()

    return pl.pallas_call(
        body, name=name, in_specs=[any_spec] * n, out_specs=[any_spec] * n,
        out_shape=[SDS(a.shape, a.dtype) for a in srcs],
        scratch_shapes=[pltpu.SemaphoreType.DMA((n,)), pltpu.SemaphoreType.DMA((n,))],
        compiler_params=pltpu.CompilerParams())(*srcs)


def _all_exchange(name, src):
    any_spec = pl.BlockSpec(memory_space=pl.ANY)

    def body(src_ref, dst_ref, send_sems, recv_sems, loc_sem):
        x, y, c = _mesh_pos()
        me = 4 * x + 2 * y + c
        peers = [(x ^ ((k >> 2) & 1), y ^ ((k >> 1) & 1), c ^ (k & 1)) for k in range(1, N_DEV)]
        local = pltpu.make_async_copy(src_ref, dst_ref.at[me], loc_sem)
        local.start()
        cps = [pltpu.make_async_remote_copy(src_ref=src_ref, dst_ref=dst_ref.at[me], send_sem=send_sems.at[k],
                                            recv_sem=recv_sems.at[k], device_id=p, device_id_type=MESH)
               for k, p in enumerate(peers)]
        for cp in cps:
            cp.start()
        for cp in cps:
            cp.wait_send()
        for k, (px, py, pc) in enumerate(peers):
            pltpu.make_async_remote_copy(src_ref=src_ref, dst_ref=dst_ref.at[4 * px + 2 * py + pc], send_sem=send_sems.at[k],
                                         recv_sem=recv_sems.at[k], device_id=(px, py, pc), device_id_type=MESH).wait_recv()
        local.wait()

    return pl.pallas_call(
        body, name=name, in_specs=[any_spec], out_specs=any_spec, out_shape=SDS((N_DEV,) + src.shape, src.dtype),
        scratch_shapes=[pltpu.SemaphoreType.DMA((N_DEV - 1,)), pltpu.SemaphoreType.DMA((N_DEV - 1,)), pltpu.SemaphoreType.DMA],
        compiler_params=pltpu.CompilerParams())(src)


def _cast_bf16(name, a):
    l, r, c = a.shape
    tr = _row_tile(r, c)
    spec = pl.BlockSpec((None, tr, c), lambda i, j: (i, j, 0))
    return _ew(name, lambda t: t, (l, r // tr), [a], [spec], [SDS(a.shape, bf16)], [spec])[0]


def _sum4(name, land):
    _, r, c = land.shape
    tr = _row_tile(r, c)
    up = lambda t: t.astype(f32)
    return _ew(name, lambda a, b, cc, d: ((up(a) + up(b)) + up(cc)) + up(d), (r // tr,), [land] * 4,
               [pl.BlockSpec((None, tr, c), lambda i, k=k: (k, i, 0)) for k in range(4)],
               [SDS((r, c), f32)], [pl.BlockSpec((tr, c), lambda i: (i, 0))])[0]


def _adam_math(g, w, m, v):
    m2 = ADAM_B1 * m + (1.0 - ADAM_B1) * g
    v2 = ADAM_B2 * v + (1.0 - ADAM_B2) * (g * g)
    m_hat = m2 / (1.0 - ADAM_B1 ** ADAM_STEP)
    v_hat = v2 / (1.0 - ADAM_B2 ** ADAM_STEP)
    delta = -ADAM_LR * (m_hat / (jnp.sqrt(v_hat) + ADAM_EPS) + ADAM_WD * w)
    return delta, m2, v2


def _adamw_big(name, p_own, p_sib, w, m, v):
    l, r, c = w.shape
    tr = _row_tile(r, c)

    def fn(a, b, wt, mt, vt):
        g = a + b
        return (g,) + _adam_math(g, wt, mt, vt)

    spec = pl.BlockSpec((None, tr, c), lambda i, j: (i, j, 0))
    return _ew(name, fn, (l, r // tr), [p_own, p_sib, w, m, v], [spec] * 5, [SDS(w.shape, f32)] * 4, [spec] * 4)


def _adamw_small(land, w, m, v):
    r = w.shape[0]
    tr = _row_tile(r, LANES)

    def fn(*t):
        g = t[0]
        for k in range(1, N_DEV):
            g = g + t[k]
        return (g,) + _adam_math(g, *t[N_DEV:])

    row = pl.BlockSpec((tr, LANES), lambda i: (i, 0))
    return _ew("adamw_small", fn, (r // tr,), [land] * N_DEV + [w, m, v],
               [pl.BlockSpec((None, tr, LANES), lambda i, k=k: (k, i, 0)) for k in range(N_DEV)] + [row] * 3,
               [SDS(w.shape, f32)] * 4, [row] * 4)


def _pack_small(d):
    return jnp.concatenate([d[k].reshape(DEPTH, -1) for k in SMALL], axis=1).reshape(DEPTH * SMALL_ROWS, LANES)


def _unpack_small(a, shapes):
    a = a.reshape(DEPTH, SMALL_ROWS * LANES)
    out, off = {}, 0
    for k, n in zip(SMALL, SMALL_SIZES):
        out[k] = a[:, off:off + n].reshape(shapes[k])
        off += n
    return out


def kernel(x, norm_mix, w_in, b_gate, qn_a, kn_a, qn_b, kn_b, pool_lin, pool_scale, w_branch_a, w_branch_b, w_branch_c, w_out, norm_ffn, w_ffn_gate, w_ffn_up, w_ffn_down, loss_target, m_norm_mix, m_w_in, m_b_gate, m_qn_a, m_kn_a, m_qn_b, m_kn_b, m_pool_lin, m_pool_scale, m_w_branch_a, m_w_branch_b, m_w_branch_c, m_w_out, m_norm_ffn, m_w_ffn_gate, m_w_ffn_up, m_w_ffn_down, v_norm_mix, v_w_in, v_b_gate, v_qn_a, v_kn_a, v_qn_b, v_kn_b, v_pool_lin, v_pool_scale, v_w_branch_a, v_w_branch_b, v_w_branch_c, v_w_out, v_norm_ffn, v_w_ffn_gate, v_w_ffn_up, v_w_ffn_down):
    args = dict(locals())
    s = x.shape[1]
    wts = {k: args[k] for k in WEIGHT_ORDER}
    mom = {k: args["m_" + k] for k in WEIGHT_ORDER}
    var = {k: args["v_" + k] for k in WEIGHT_ORDER}
    tabs = _rope_tables(s)

    shards16 = {k: _cast_bf16("cast_" + k, wts[k]) for k in BIG}
    whole = lambda ref, peer: ref.at[0]
    w_in_l = _chip_exchange("gather_w_in", [shards16["w_in"][0:1]], whole)[0]
    full = [None] * DEPTH

    def small_of(l):
        tile2 = lambda a: jnp.concatenate([a, a])
        gains = jnp.stack([tile2(qn_a[l]), tile2(kn_a[l]), tile2(qn_b[l]), tile2(kn_b[l])])
        return dict(norm_mix=norm_mix[l][None], norm_ffn=norm_ffn[l][None], b_gate=b_gate[l][None], gains=gains,
                    pool_lin=pool_lin[l], pool_scale=pool_scale[l][None])

    xs = x[0]
    saved = []
    for l in range(DEPTH):
        nxt = (l + 1) % DEPTH
        srcs = [shards16[k][l:l + 1] for k in REST] + [shards16["w_in"][nxt:nxt + 1]]
        xs, sv, full[l] = _layer_fwd(xs, dict(w_in=w_in_l), small_of(l), tabs, s, (srcs, whole))
        saved.append(sv)
        w_in_l = full[l]["next_w_in"]
    dy, loss_rows = _loss_grad(xs, loss_target[0], s)
    loss = lax.psum(jnp.sum(loss_rows), ("x", "y", "c"))

    part = {k: [None] * DEPTH for k in BIG}
    small_g = [None] * DEPTH
    to_chip = lambda ref, peer: ref.at[peer]
    assert GRAD_WIRE_DTYPE == bf16
    pending = [full[DEPTH - 1][k] for k in MIXER]
    for l in reversed(range(DEPTH)):
        dy, big, small_g[l], landed = _layer_bwd(dy, saved[l], full[l], small_of(l), tabs, s, pending)
        if l + 1 < DEPTH:
            for k, a in zip(MIXER, landed[:len(MIXER)]):
                part[k][l + 1] = _sum4("sum4_" + k, a)
        for k, a in zip(FFN, landed[len(MIXER):]):
            part[k][l] = _sum4("sum4_" + k, a)
        pending = [big[k] for k in MIXER]
    for k, a in zip(MIXER, _chip_exchange("grad_scatter", pending, to_chip)):
        part[k][0] = _sum4("sum4_" + k, a)

    p_own = [jnp.stack(part[k]) for k in BIG]
    p_sib = _sibling_exchange("grad_sibling", p_own)
    outs = {}
    for k, a, b in zip(BIG, p_own, p_sib):
        outs[k] = _adamw_big("adamw_" + k, a, b, wts[k], mom[k], var[k])

    land_s = _all_exchange("small_allgather", jnp.concatenate(small_g, axis=0))
    res_s = _adamw_small(land_s, _pack_small(wts), _pack_small(mom), _pack_small(var))
    shapes = {k: wts[k].shape for k in SMALL}
    small_out = [_unpack_small(r, shapes) for r in res_s]
    for k in SMALL:
        outs[k] = tuple(so[k] for so in small_out)

    flat = [loss, dy[None]]
    for idx in range(4):
        flat += [outs[k][idx] for k in WEIGHT_ORDER]
    return tuple(flat)
```

```python
import functools

import jax
import jax.numpy as jnp
from jax import lax
from jax.experimental import pallas as pl
from jax.experimental.pallas import tpu as pltpu

f32, bf16 = jnp.float32, jnp.bfloat16
SDS = jax.ShapeDtypeStruct

DEPTH = 4
D_MODEL = 1024
HEAD_DIM = 64
LANES = 128
GRID_W = 64
ROPE_THETA = 10000.0
EPS = 1e-6
NEG_INF = -1e30
SCALE = HEAD_DIM ** -0.5
B_GROUP_CFG = ((128, 1), (512, 4), (2048, 16))
POOL_WINDOWS = (2, 4, 8, 16)
N_CHIPS = 4
N_DEV = 8
C_QA, C_KA, C_VA, C_QB, C_KB, C_VB, C_UC, C_GZ = 0, 512, 640, 768, 1536, 2304, 3072, 3584
IN_WIDTH = 6656
FF_SHARD = 704
ADAM_LR, ADAM_B1, ADAM_B2, ADAM_EPS, ADAM_WD, ADAM_STEP = 0.001, 0.9, 0.999, 1e-08, 0.01, 10
VMEM_LIMIT = 56 * 1024 * 1024
MESH = pl.DeviceIdType.MESH

MIXER = ("w_in", "w_branch_a", "w_branch_b", "w_branch_c", "w_out")
FFN = ("w_ffn_gate", "w_ffn_up", "w_ffn_down")
BIG = MIXER + FFN
REST = BIG[1:]
SMALL = ("norm_mix", "b_gate", "qn_a", "kn_a", "qn_b", "kn_b", "pool_lin", "pool_scale", "norm_ffn")
SMALL_SIZES = (1024, 3072, 64, 64, 64, 64, 65536, 512, 1024)
SMALL_ROWS = sum(SMALL_SIZES) // LANES
WEIGHT_ORDER = ("norm_mix", "w_in", "b_gate", "qn_a", "kn_a", "qn_b", "kn_b", "pool_lin", "pool_scale",
                "w_branch_a", "w_branch_b", "w_branch_c", "w_out", "norm_ffn", "w_ffn_gate", "w_ffn_up", "w_ffn_down")


def _cparams():
    return pltpu.CompilerParams(vmem_limit_bytes=VMEM_LIMIT)


def _lo_mask(shape):
    return lax.broadcasted_iota(jnp.int32, shape, len(shape) - 1) < HEAD_DIM


def _dot(a, b, dims):
    dn = {"nn": (((1,), (0,)), ((), ())), "nt": (((1,), (1,)), ((), ())), "tn": (((0,), (0,)), ((), ()))}[dims]
    return lax.dot_general(a, b, dn, preferred_element_type=f32)


def _mm(name, ins, in_specs, out_shape, out_spec, grid, nk, dims, acc_shape, epilogue=None, n_extra=0, aliases=None):
    multi = isinstance(out_shape, (list, tuple))
    n_out = len(out_shape) if multi else 1

    def body(*refs):
        a_ref, b_ref = refs[0], refs[1]
        extra = refs[2:2 + n_extra]
        outs = refs[2 + n_extra:2 + n_extra + n_out]
        k = pl.program_id(len(grid) - 1)
        part = _dot(a_ref[...].astype(bf16), b_ref[...].astype(bf16), dims)

        def finish(acc):
            res = epilogue(acc, *[e[...] for e in extra]) if epilogue is not None else acc
            res = res if isinstance(res, (list, tuple)) else (res,)
            for o, r in zip(outs, res):
                o[...] = r.astype(o.dtype)

        if nk == 1:
            finish(part)
        else:
            acc_ref = refs[-1]

            @pl.when(k == 0)
            def _():
                acc_ref[...] = part

            @pl.when(k > 0)
            def _():
                acc_ref[...] += part

            @pl.when(k == nk - 1)
            def _():
                finish(acc_ref[...])

    return pl.pallas_call(
        body, name=name, grid=grid, in_specs=in_specs,
        out_specs=list(out_spec) if multi else out_spec,
        out_shape=list(out_shape) if multi else out_shape,
        scratch_shapes=[] if nk == 1 else [pltpu.VMEM(acc_shape, f32)],
        input_output_aliases=aliases or {}, compiler_params=_cparams())(*ins)


def _ew(name, fn, grid, ins, in_specs, out_shapes, out_specs, n_acc=0, aliases=None):
    n_in = len(ins)
    n_out = len(out_shapes) - n_acc
    n_alias = len(aliases or {})

    def body(*refs):
        in_refs = refs[:n_in - n_alias] if n_alias else refs[:n_in]
        out_refs = refs[n_in:n_in + n_out]
        acc_refs = refs[n_in + n_out:]
        i = pl.program_id(len(grid) - 1)
        res = fn(*[r[...] for r in in_refs])
        res = res if isinstance(res, (list, tuple)) else (res,)
        for o, r in zip(out_refs, res[:n_out]):
            o[...] = r.astype(o.dtype)
        for a, r in zip(acc_refs, res[n_out:]):
            @pl.when(i == 0)
            def _(a=a, r=r):
                a[...] = r.astype(a.dtype)

            @pl.when(i > 0)
            def _(a=a, r=r):
                a[...] += r.astype(a.dtype)

    return pl.pallas_call(body, name=name, grid=grid, in_specs=in_specs, out_specs=out_specs, out_shape=out_shapes,
                          input_output_aliases=aliases or {}, compiler_params=_cparams())(*ins)


EW_BLOCK_ELEMS = 128 * 1024


def _row_tile(rows, cols):
    step = 16 if rows % 16 == 0 else 8
    return max(t for t in range(step, rows + 1, step) if rows % t == 0 and (t * cols <= EW_BLOCK_ELEMS or t == step))


def _rope_tables(s):
    t = jnp.arange(s)
    inv_ax = ROPE_THETA ** (-jnp.arange(0, HEAD_DIM // 2, 2, dtype=f32) / (HEAD_DIM // 2))
    inv_sq = ROPE_THETA ** (-jnp.arange(0, HEAD_DIM, 2, dtype=f32) / HEAD_DIM)
    ang_row = (t // GRID_W).astype(f32)[:, None] * inv_ax[None, :]
    ang_col = (t % GRID_W).astype(f32)[:, None] * inv_ax[None, :]
    ang_seq = t.astype(f32)[:, None] * inv_sq[None, :]
    a_ax = jnp.concatenate([ang_row, ang_row, ang_col, ang_col], axis=1)
    sg_ax = jnp.concatenate([-jnp.ones(16), jnp.ones(16), -jnp.ones(16), jnp.ones(16)]).astype(f32)
    a_sq = jnp.concatenate([ang_seq, ang_seq], axis=1)
    sg_sq = jnp.concatenate([-jnp.ones(32), jnp.ones(32)]).astype(f32)
    two = lambda a: jnp.concatenate([a, a], axis=1)
    return (two(jnp.cos(a_ax)), two(jnp.sin(a_ax) * sg_ax), two(jnp.cos(a_sq)), two(jnp.sin(a_sq) * sg_sq))


def _head_stat(v, lo):
    r = lax.broadcasted_iota(jnp.int32, (LANES, LANES), 0) < HEAD_DIM
    c = lax.broadcasted_iota(jnp.int32, (LANES, LANES), 1) < HEAD_DIM
    same_head = (r == c).astype(f32)
    return jnp.dot(v, same_head, preferred_element_type=f32, precision=lax.Precision.HIGHEST)


def _partner(y, off):
    lane = lax.broadcasted_iota(jnp.int32, y.shape, 1)
    first = (lane & (2 * off - 1)) < off
    return jnp.where(first, pltpu.roll(y, LANES - off, 1), pltpu.roll(y, off, 1))


def _normrope(xc, g, cos, sin, off):
    lo = _lo_mask(xc.shape)
    r = lax.rsqrt(_head_stat(xc * xc, lo) * (1.0 / HEAD_DIM) + EPS)
    y = xc * r * g
    return y * cos + _partner(y, off) * sin


def _normrope_bwd(xc, g, cos, sin, off, drot):
    lo = _lo_mask(xc.shape)
    r = lax.rsqrt(_head_stat(xc * xc, lo) * (1.0 / HEAD_DIM) + EPS)
    n = xc * r
    dy = drot * cos + _partner(drot * sin, off)
    dg = jnp.sum(dy * n, axis=0, keepdims=True)
    dn = dy * g
    dx = r * (dn - n * (_head_stat(dn * n, lo) * (1.0 / HEAD_DIM)))
    return dx, dg


def _norm_fwd(x, g, s):
    ts = 256

    def fn(xt, gt):
        r = lax.rsqrt(jnp.mean(xt * xt, axis=1, keepdims=True) + EPS)
        return xt * r * gt

    return _ew("norm_fwd", fn, (s // ts,), [x, g],
               [pl.BlockSpec((ts, D_MODEL), lambda i: (i, 0)), pl.BlockSpec((1, D_MODEL), lambda i: (0, 0))],
               [SDS((s, D_MODEL), bf16)], [pl.BlockSpec((ts, D_MODEL), lambda i: (i, 0))])[0]


def _norm_bwd(x, g, dh, dres, s):
    ts = 256

    def fn(xt, gt, dht, drt):
        r = lax.rsqrt(jnp.mean(xt * xt, axis=1, keepdims=True) + EPS)
        n = xt * r
        dn = dht * gt
        dx = drt + r * (dn - n * jnp.mean(dn * n, axis=1, keepdims=True))
        return dx, jnp.sum(dht * n, axis=0, keepdims=True)

    row = pl.BlockSpec((ts, D_MODEL), lambda i: (i, 0))
    one = pl.BlockSpec((1, D_MODEL), lambda i: (0, 0))
    return _ew("norm_bwd", fn, (s // ts,), [x, g, dh, dres], [row, one, row, row],
               [SDS((s, D_MODEL), f32), SDS((1, D_MODEL), f32)], [row, one], n_acc=1)


def _prep_fwd(z, tabs, gains, s):
    ts = 256
    cos_a, sin_a, cos_b, sin_b = tabs

    def body(za_ref, zq_ref, zk_ref, zv_ref, ca, sa, cb, sb, g_ref, qa_o, kd_o, vd_o, qb_o, kb_o, vb_o):
        lo = _lo_mask((ts, LANES))
        g = g_ref[...]
        ca_, sa_, cb_, sb_ = ca[...], sa[...], cb[...], sb[...]
        for c in range(4):
            qa_o[:, c * LANES:(c + 1) * LANES] = _normrope(za_ref[:, c * LANES:(c + 1) * LANES], g[0:1], ca_, sa_, 16).astype(bf16)
        k = _normrope(za_ref[:, C_KA:C_KA + LANES], g[1:2], ca_, sa_, 16)
        kr = pltpu.roll(k, HEAD_DIM, 1)
        kd_o[0] = k.astype(bf16)
        kd_o[1] = kr.astype(bf16)
        v = za_ref[:, C_VA:C_VA + LANES]
        ones_col = (lax.broadcasted_iota(jnp.int32, v.shape, 1) == HEAD_DIM).astype(f32)
        vd_o[0] = jnp.where(lo, v, ones_col).astype(bf16)
        vd_o[1] = jnp.where(lo, pltpu.roll(v, HEAD_DIM, 1), ones_col).astype(bf16)
        for c in range(6):
            sl = slice(c * LANES, (c + 1) * LANES)
            qb_o[:, sl] = _normrope(zq_ref[:, sl], g[2:3], cb_, sb_, 32).astype(bf16)
            kb_o[:, sl] = _normrope(zk_ref[:, sl], g[3:4], cb_, sb_, 32).astype(bf16)
        vb_o[...] = zv_ref[...].astype(bf16)

    w = 768
    zspec = lambda cb: pl.BlockSpec((ts, w), lambda i: (i, cb))
    tab = pl.BlockSpec((ts, LANES), lambda i: (i, 0))
    dup = pl.BlockSpec((2, ts, LANES), lambda i: (0, i, 0))
    return pl.pallas_call(
        body, name="prep_fwd", grid=(s // ts,),
        in_specs=[zspec(0), zspec(1), zspec(2), zspec(3), tab, tab, tab, tab, pl.BlockSpec((4, LANES), lambda i: (0, 0))],
        out_specs=[pl.BlockSpec((ts, 512), lambda i: (i, 0)), dup, dup, zspec(0), zspec(0), zspec(0)],
        out_shape=[SDS((s, 512), bf16), SDS((2, s, LANES), bf16), SDS((2, s, LANES), bf16),
                   SDS((s, w), bf16), SDS((s, w), bf16), SDS((s, w), bf16)],
        compiler_params=_cparams())(z, z, z, z, cos_a, sin_a, cos_b, sin_b, gains)


def _prep_bwd(z, tabs, gains, dqa, dkd, dvd, dqb, dkb, dvb, dz, s):
    ts = 256
    cos_a, sin_a, cos_b, sin_b = tabs

    def body(za_ref, zq_ref, zk_ref, ca, sa, cb, sb, g_ref, dqa_r, dkd_r, dvd_r,
             dq0, dq1, dq2, dk0, dk1, dk2, dv0, dv1, dv2, dz_in, dz_o, dg_o):
        i = pl.program_id(0)
        lo = _lo_mask((ts, LANES))
        g = g_ref[...]
        ca_, sa_, cb_, sb_ = ca[...], sa[...], cb[...], sb[...]
        dg = [jnp.zeros((1, LANES), f32) for _ in range(4)]
        for c in range(4):
            sl = slice(c * LANES, (c + 1) * LANES)
            dx, d = _normrope_bwd(za_ref[:, sl], g[0:1], ca_, sa_, 16, dqa_r[:, sl])
            dz_o[:, sl] = dx.astype(bf16)
            dg[0] += d
        dk = jnp.where(lo, dkd_r[0], pltpu.roll(dkd_r[1], HEAD_DIM, 1))
        dx, d = _normrope_bwd(za_ref[:, C_KA:C_KA + LANES], g[1:2], ca_, sa_, 16, dk)
        dz_o[:, C_KA:C_KA + LANES] = dx.astype(bf16)
        dg[1] += d
        dz_o[:, C_VA:C_VA + LANES] = jnp.where(lo, dvd_r[0], pltpu.roll(dvd_r[1], HEAD_DIM, 1)).astype(bf16)
        dqs, dks, dvs = (dq0, dq1, dq2), (dk0, dk1, dk2), (dv0, dv1, dv2)
        for c in range(6):
            sl = slice(c * LANES, (c + 1) * LANES)
            gsl = slice((c % 2) * LANES, (c % 2 + 1) * LANES)
            dx, d = _normrope_bwd(zq_ref[:, sl], g[2:3], cb_, sb_, 32, dqs[c // 2][:, gsl])
            dz_o[:, C_QB + c * LANES:C_QB + (c + 1) * LANES] = dx.astype(bf16)
            dg[2] += d
            dx, d = _normrope_bwd(zk_ref[:, sl], g[3:4], cb_, sb_, 32, dks[c // 2][:, gsl])
            dz_o[:, C_KB + c * LANES:C_KB + (c + 1) * LANES] = dx.astype(bf16)
            dg[3] += d
            dz_o[:, C_VB + c * LANES:C_VB + (c + 1) * LANES] = dvs[c // 2][:, gsl].astype(bf16)
        dgs = jnp.concatenate(dg + [jnp.zeros((4, LANES), f32)], axis=0)

        @pl.when(i == 0)
        def _():
            dg_o[...] = dgs

        @pl.when(i > 0)
        def _():
            dg_o[...] += dgs

    w = 768
    zspec = lambda cb: pl.BlockSpec((ts, w), lambda i: (i, cb))
    tab = pl.BlockSpec((ts, LANES), lambda i: (i, 0))
    dup = pl.BlockSpec((2, ts, LANES), lambda i: (0, i, 0))
    grp = pl.BlockSpec((ts, 256), lambda i: (i, 0))
    dzo, dgo = pl.pallas_call(
        body, name="prep_bwd", grid=(s // ts,),
        in_specs=[zspec(0), zspec(1), zspec(2), tab, tab, tab, tab, pl.BlockSpec((4, LANES), lambda i: (0, 0)),
                  pl.BlockSpec((ts, 512), lambda i: (i, 0)), dup, dup] + [grp] * 9 + [pl.BlockSpec(memory_space=pl.ANY)],
        out_specs=[pl.BlockSpec((ts, C_UC), lambda i: (i, 0)), pl.BlockSpec((8, LANES), lambda i: (0, 0))],
        out_shape=[SDS((s, IN_WIDTH), bf16), SDS((8, LANES), f32)],
        input_output_aliases={20: 0}, compiler_params=_cparams())(
            z, z, z, cos_a, sin_a, cos_b, sin_b, gains, dqa, dkd, dvd, *dqb, *dkb, *dvb, dz)
    return dzo, dgo


def _stack_heads(x, lo, dtype):
    z = jnp.zeros_like(x)
    return jnp.concatenate([jnp.where(lo, x, z), jnp.where(lo, z, x)], axis=0).astype(dtype)


def _rows_of(v, lo, kind):
    if kind == "max":
        a = jnp.max(jnp.where(lo, v, NEG_INF * 10), axis=1, keepdims=True)
        b = jnp.max(jnp.where(lo, NEG_INF * 10, v), axis=1, keepdims=True)
    else:
        a = jnp.sum(jnp.where(lo, v, 0.0), axis=1, keepdims=True) * (1.0 / HEAD_DIM)
        b = jnp.sum(jnp.where(lo, 0.0, v), axis=1, keepdims=True) * (1.0 / HEAD_DIM)
    return jnp.concatenate([a, b], axis=0)


def _stack_low(x, lo, dtype, scale=None):
    x = x.astype(f32)
    if scale is not None:
        x = x * scale
    z = jnp.zeros_like(x)
    return jnp.concatenate([jnp.where(lo, x, z), jnp.where(lo, pltpu.roll(x, HEAD_DIM, 1), z)], axis=0).astype(dtype)


def _unstack_low(xs, lo, rows):
    return jnp.where(lo, xs[:rows], pltpu.roll(xs[rows:], HEAD_DIM, 1))


FA_TQ, FA_TK = 256, 1024


def _host_exchange(body, n_in, n_out, grid, xchg):
    if xchg is None:
        return body, [], [], [], []
    srcs, pick_src = xchg
    n = len(srcs)
    any_spec = pl.BlockSpec(memory_space=pl.ANY)

    def hosted(*refs):
        ins, src = refs[:n_in], refs[n_in:n_in + n]
        outs, dst = refs[n_in + n:n_in + n + n_out], refs[n_in + n + n_out:n_in + 2 * n + n_out]
        scratch, sems = refs[n_in + 2 * n + n_out:-3], refs[-3:]
        ids = [pl.program_id(a) for a in range(len(grid))]
        first = functools.reduce(jnp.logical_and, [i == 0 for i in ids])
        last = functools.reduce(jnp.logical_and, [i == g - 1 for i, g in zip(ids, grid)])

        @pl.when(first)
        def _():
            _chip_start(_chip_copies(src, dst, pick_src, *sems, receiving=False))

        body(*ins, *outs, *scratch)

        @pl.when(last)
        def _():
            _chip_wait(_chip_copies(src, dst, pick_src, *sems))

    return hosted, [any_spec] * n, [any_spec] * n, _chip_landing(srcs), _chip_sems(n)


def _flash_a_fwd(qa, kd, vd, s, xchg=None):
    tq, tk = FA_TQ, min(4 * FA_TK, s)
    nk = s // tk

    def body(q_ref, k_ref, v_ref, o_ref, lse_ref, qs, m_s, acc):
        lo = _lo_mask((tq, LANES))
        for pp in range(2):
            qs[pp] = _stack_low(q_ref[:, pp * LANES:(pp + 1) * LANES], lo, bf16, SCALE)
        m_s[...] = jnp.full(m_s.shape, NEG_INF, f32)
        acc[...] = jnp.zeros(acc.shape, f32)

        def chunk(j, carry):
            rows = pl.ds(pl.multiple_of(j * tk, tk), tk)
            k, v = k_ref[rows, :], v_ref[rows, :]
            for pp in range(2):
                sc = _dot(qs[pp], k, "nt")
                m_prev = m_s[pp]
                m_new = jnp.maximum(m_prev, jnp.max(sc, axis=1, keepdims=True))
                p = jnp.exp(sc - m_new).astype(bf16)
                acc[pp] = acc[pp] * jnp.exp(m_prev - m_new) + _dot(p, v, "nn")
                m_s[pp] = m_new
            return carry

        lax.fori_loop(0, nk, chunk, 0)
        for pp in range(2):
            sl = slice(pp * LANES, (pp + 1) * LANES)
            a = acc[pp]
            lane = lax.broadcasted_iota(jnp.int32, a.shape, 1)
            l = jnp.sum(jnp.where(lane == HEAD_DIM, a, 0.0), axis=1, keepdims=True)
            o_ref[:, sl] = _unstack_low(a / l, lo, tq).astype(bf16)
            lse = m_s[pp] + jnp.log(l)
            lse_ref[:, sl] = jnp.where(lo, lse[:tq], lse[tq:])

    kv = pl.BlockSpec((None, s, LANES), lambda c, i: (c, 0, 0))
    qo = pl.BlockSpec((tq, 2 * LANES), lambda c, i: (i, c))
    grid = (2, s // tq)
    hosted, xi, xo, xs, xsem = _host_exchange(body, 3, 2, grid, xchg)
    res = pl.pallas_call(
        hosted, name="flash_a_fwd" + ("_x" if xchg else ""), grid=grid, in_specs=[qo, kv, kv] + xi, out_specs=[qo, qo] + xo,
        out_shape=[SDS((s, 512), bf16), SDS((s, 512), f32)] + xs,
        scratch_shapes=[pltpu.VMEM((2, 2 * tq, LANES), bf16), pltpu.VMEM((2, 2 * tq, 1), f32),
                        pltpu.VMEM((2, 2 * tq, LANES), f32)] + xsem,
        compiler_params=_cparams())(qa, kd, vd, *(xchg[0] if xchg else []))
    return res[:2], res[2:]


def _flash_a_bwd(qa, kd, vd, oa, lse, doa, s, xchg=None):
    tq, tk = FA_TQ, FA_TK
    nk = s // tk

    def body(q_ref, do_ref, o_ref, lse_ref, k_ref, v_ref, dq_ref, dk_ref, dv_ref, qs, dos, lse_s, dl_s, dq_s):
        i = pl.program_id(1)
        lo = _lo_mask((tq, LANES))

        @pl.when(i == 0)
        def _():
            dk_ref[...] = jnp.zeros(dk_ref.shape, f32)
            dv_ref[...] = jnp.zeros(dv_ref.shape, f32)

        for pp in range(2):
            sl = slice(pp * LANES, (pp + 1) * LANES)
            do = do_ref[:, sl]
            qs[...] = _stack_low(q_ref[:, sl], lo, bf16, SCALE)
            dos[...] = _stack_low(do, lo, bf16)
            dl_s[...] = _rows_of(do * o_ref[:, sl].astype(f32), lo, "sum") * HEAD_DIM
            lse_s[...] = _rows_of(lse_ref[:, sl], lo, "max")
            dq_s[...] = jnp.zeros(dq_s.shape, f32)

            def chunk(j, carry):
                rows = pl.ds(pl.multiple_of(j * tk, tk), tk)
                k, v = k_ref[rows, :], v_ref[rows, :]
                q_, do_ = qs[...], dos[...]
                p = jnp.exp(_dot(q_, k, "nt") - lse_s[...])
                ds = (p * (_dot(do_, v, "nt") - dl_s[...])).astype(bf16)
                dv_ref[rows, :] += _dot(p.astype(bf16), do_, "tn")
                dk_ref[rows, :] += _dot(ds, q_, "tn")
                dq_s[...] += _dot(ds, k, "nn")
                return carry

            lax.fori_loop(0, nk, chunk, 0)
            dq_ref[:, sl] = _unstack_low(dq_s[...], lo, tq) * SCALE

    grp = lambda: pl.BlockSpec((tq, 2 * LANES), lambda c, i: (i, c))
    kv = lambda: pl.BlockSpec((None, s, LANES), lambda c, i: (c, 0, 0))
    grid = (2, s // tq)
    hosted, xi, xo, xs, xsem = _host_exchange(body, 6, 3, grid, xchg)
    res = pl.pallas_call(
        hosted, name="flash_a_bwd" + ("_x" if xchg else ""), grid=grid,
        in_specs=[grp(), grp(), grp(), grp(), kv(), kv()] + xi, out_specs=[grp(), kv(), kv()] + xo,
        out_shape=[SDS((s, 512), f32), SDS((2, s, LANES), f32), SDS((2, s, LANES), f32)] + xs,
        scratch_shapes=[pltpu.VMEM((2 * tq, LANES), bf16), pltpu.VMEM((2 * tq, LANES), bf16), pltpu.VMEM((2 * tq, 1), f32),
                        pltpu.VMEM((2 * tq, 1), f32), pltpu.VMEM((2 * tq, LANES), f32)] + xsem,
        compiler_params=_cparams())(qa, doa, oa, lse, kd, vd, *(xchg[0] if xchg else []))
    return res[:3], res[3:]


BH = 128
BQ_MAX = 512


class _BandGeom:
    def __init__(self, gi, s):
        self.d = B_GROUP_CFG[gi][1]
        self.l_sub = s // self.d
        self.bq = min(BQ_MAX, self.l_sub)
        self.nb = self.l_sub // self.bq
        self.grid = (self.d, self.nb, 2)
        per, last = self.bq // BH, self.l_sub // BH - 1
        self.in_col = lambda r, pp: r * 6 + gi * 2 + pp
        self.out_col = lambda r, pp: r * 2 + pp
        self.cur = lambda col: pl.BlockSpec((self.bq, LANES), lambda r, n, pp: (n, col(r, pp)))
        self.prev = lambda col: pl.BlockSpec((BH, LANES), lambda r, n, pp: (jnp.maximum(n * per - 1, 0), col(r, pp)))
        self.next = lambda col: pl.BlockSpec((BH, LANES), lambda r, n, pp: (jnp.minimum((n + 1) * per, last), col(r, pp)))
        self.window = lambda col: [self.prev(col), self.cur(col), self.next(col)]
        self.view_in = lambda a: a.reshape(self.l_sub, self.d * 768)
        self.view_out = lambda a: a.reshape(self.l_sub, self.d * 256)

    def fill_band_bias(self, bias_s, q_rows, q_off, k_off):
        @pl.when((pl.program_id(0) == 0) & (pl.program_id(1) == 0) & (pl.program_id(2) == 0))
        def _():
            r = lax.broadcasted_iota(jnp.int32, bias_s.shape, 0)
            q = jnp.where(r >= q_rows, r - q_rows, r) + q_off
            k = lax.broadcasted_iota(jnp.int32, bias_s.shape, 1) + k_off
            bias_s[...] = jnp.where(jnp.abs(q - k) <= 64, 0.0, NEG_INF)

    def edge_bias(self, first_pos, n, stacked_rows=False):
        if stacked_rows:
            r = lax.broadcasted_iota(jnp.int32, (2 * n, 1), 0)
            idx = jnp.where(r >= n, r - n, r)
        else:
            idx = lax.broadcasted_iota(jnp.int32, (1, n), 1)
        pos = first_pos + idx
        return jnp.where((pos >= 0) & (pos < self.l_sub), 0.0, NEG_INF)


def _cat3(a, b, c):
    return jnp.concatenate([a[...], b[...], c[...]], axis=0)


def _attn_b_fwd(qb, kb, vb, gi, s):
    g = _BandGeom(gi, s)
    bq = g.bq

    def body(q_ref, kp, kc, kn, vp, vc, vn, o_ref, lse_ref, bias_s):
        n = pl.program_id(1)
        g.fill_band_bias(bias_s, bq, 0, -BH)
        lo = _lo_mask((bq, LANES))
        qs = _stack_heads(q_ref[...], lo, bf16)
        k, v = _cat3(kp, kc, kn), _cat3(vp, vc, vn)
        sc = _dot(qs, k, "nt") * SCALE + bias_s[...] + g.edge_bias(n * bq - BH, bq + 2 * BH)
        m = jnp.max(sc, axis=1, keepdims=True)
        p = jnp.exp(sc - m)
        den = jnp.sum(p, axis=1, keepdims=True)
        o = _dot(p.astype(bf16), v, "nn") / den
        o_ref[...] = jnp.where(lo, o[:bq], o[bq:])
        lse = m + jnp.log(den)
        lse_ref[...] = jnp.where(lo, lse[:bq], lse[bq:])

    out = g.cur(g.out_col)
    qv, kv, vv = g.view_in(qb), g.view_in(kb), g.view_in(vb)
    o, lse = pl.pallas_call(
        body, name=f"attn_b_fwd{gi}", grid=g.grid,
        in_specs=[g.cur(g.in_col)] + g.window(g.in_col) * 2, out_specs=[out, out],
        out_shape=[SDS((g.l_sub, g.d * 256), f32), SDS((g.l_sub, g.d * 256), f32)],
        scratch_shapes=[pltpu.VMEM((2 * bq, bq + 2 * BH), f32)],
        compiler_params=_cparams())(qv, kv, kv, kv, vv, vv, vv)
    return o.reshape(s, 256), lse.reshape(s, 256)


def _attn_b_dq(qb, kb, vb, dob, lse, delta, gi, s):
    g = _BandGeom(gi, s)
    bq = g.bq

    def body(q_ref, kp, kc, kn, vp, vc, vn, do_ref, lse_ref, dl_ref, dq_ref, bias_s):
        n = pl.program_id(1)
        g.fill_band_bias(bias_s, bq, 0, -BH)
        lo = _lo_mask((bq, LANES))
        qs = _stack_heads(q_ref[...], lo, bf16)
        dos = _stack_heads(do_ref[...], lo, bf16)
        k, v = _cat3(kp, kc, kn), _cat3(vp, vc, vn)
        sc = _dot(qs, k, "nt") * SCALE + bias_s[...] + g.edge_bias(n * bq - BH, bq + 2 * BH)
        p = jnp.exp(sc - _rows_of(lse_ref[...], lo, "max"))
        dp = _dot(dos, v, "nt")
        ds = (p * (dp - _rows_of(dl_ref[...], lo, "sum")) * SCALE).astype(bf16)
        dqs = _dot(ds, k, "nn")
        dq_ref[...] = jnp.where(lo, dqs[:bq], dqs[bq:])

    out = g.cur(g.out_col)
    qv, kv, vv = g.view_in(qb), g.view_in(kb), g.view_in(vb)
    dq = pl.pallas_call(
        body, name=f"attn_b_dq{gi}", grid=g.grid,
        in_specs=[g.cur(g.in_col)] + g.window(g.in_col) * 2 + [out, out, out], out_specs=out,
        out_shape=SDS((g.l_sub, g.d * 256), f32), scratch_shapes=[pltpu.VMEM((2 * bq, bq + 2 * BH), f32)],
        compiler_params=_cparams())(qv, kv, kv, kv, vv, vv, vv, g.view_out(dob), g.view_out(lse), g.view_out(delta))
    return dq.reshape(s, 256)


def _attn_b_dkv(qb, kb, vb, dob, lse, delta, gi, s):
    g = _BandGeom(gi, s)
    bq = g.bq
    nq = bq + 2 * BH

    def body(k_ref, v_ref, qp, qc, qn, dop, doc, don, lp, lc, ln, dp_, dc_, dn_, dk_ref, dv_ref, bias_s):
        m = pl.program_id(1)
        g.fill_band_bias(bias_s, nq, -BH, 0)
        lo = _lo_mask((nq, LANES))
        qs = _stack_heads(_cat3(qp, qc, qn), lo, bf16)
        dos = _stack_heads(_cat3(dop, doc, don), lo, bf16)
        lses = _rows_of(_cat3(lp, lc, ln), lo, "max")
        dls = _rows_of(_cat3(dp_, dc_, dn_), lo, "sum")
        k, v = k_ref[...], v_ref[...]
        sc = _dot(qs, k, "nt") * SCALE + bias_s[...] + g.edge_bias(m * bq - BH, nq, stacked_rows=True)
        p = jnp.exp(sc - lses)
        dpv = _dot(dos, v, "nt")
        ds = (p * (dpv - dls) * SCALE).astype(bf16)
        dv_ref[...] = _dot(p.astype(bf16), dos, "tn")
        dk_ref[...] = _dot(ds, qs, "tn")

    cur = g.cur(g.out_col)
    qv, kv, vv = g.view_in(qb), g.view_in(kb), g.view_in(vb)
    dk, dv = pl.pallas_call(
        body, name=f"attn_b_dkv{gi}", grid=g.grid,
        in_specs=[g.cur(g.in_col)] * 2 + g.window(g.in_col) + g.window(g.out_col) * 3, out_specs=[cur, cur],
        out_shape=[SDS((g.l_sub, g.d * 256), f32), SDS((g.l_sub, g.d * 256), f32)],
        scratch_shapes=[pltpu.VMEM((2 * nq, bq), f32)],
        compiler_params=_cparams())(kv, vv, qv, qv, qv, *([g.view_out(dob)] * 3), *([g.view_out(lse)] * 3),
                                    *([g.view_out(delta)] * 3))
    return dk.reshape(s, 256), dv.reshape(s, 256)


def _merge_b(o_list, lse_list, s):
    ts = 256

    def fn(o0, o1, o2, l0, l1, l2):
        m = jnp.maximum(jnp.maximum(l0, l1), l2)
        w0, w1, w2 = jnp.exp(l0 - m), jnp.exp(l1 - m), jnp.exp(l2 - m)
        den = w0 + w1 + w2
        return (w0 * o0 + w1 * o1 + w2 * o2) / den, m + jnp.log(den)

    row = pl.BlockSpec((ts, 256), lambda i: (i, 0))
    return _ew("merge_b", fn, (s // ts,), [*o_list, *lse_list], [row] * 6,
               [SDS((s, 256), bf16), SDS((s, 256), f32)], [row, row])


def _delta_b(dob, ob, s):
    ts = 256

    def fn(do, o):
        prod = do * o.astype(f32)
        lo = _lo_mask((ts, LANES))
        return jnp.concatenate([_head_stat(prod[:, :LANES], lo), _head_stat(prod[:, LANES:], lo)], axis=1)

    row = pl.BlockSpec((ts, 256), lambda i: (i, 0))
    return _ew("delta_b", fn, (s // ts,), [dob, ob], [row, row], [SDS((s, 256), f32)], [row])[0]


def _band(t0, u0, w, nt, nu, s, transposed):
    t = t0 + lax.broadcasted_iota(jnp.int32, (nt, nu), 0)
    u = u0 + lax.broadcasted_iota(jnp.int32, (nt, nu), 1)
    if transposed:
        lo, hi = jnp.clip(t - w // 2 + 1, 0, s), jnp.clip(t + w // 2 + 1, 0, s)
    else:
        lo, hi = jnp.clip(t - w // 2, 0, s), jnp.clip(t + w - w // 2, 0, s)
    return ((u >= lo) & (u < hi)).astype(f32)


def _pool_cnt(t0, w, nt, s):
    t = t0 + lax.broadcasted_iota(jnp.int32, (nt, 1), 0)
    return (jnp.clip(t + w - w // 2, 0, s) - jnp.clip(t - w // 2, 0, s)).astype(f32)


POOL_HALO = 8


def _halo_specs(tp, s, cb):
    per, last = tp // POOL_HALO, s // POOL_HALO - 1
    return [pl.BlockSpec((POOL_HALO, 512), lambda i: (jnp.maximum(i * per - 1, 0), cb)),
            pl.BlockSpec((tp, 512), lambda i: (i, cb)),
            pl.BlockSpec((POOL_HALO, 512), lambda i: (jnp.minimum((i + 1) * per, last), cb))]


def _pool_fwd(z, lin, scale, s):
    tp = 256
    nt = s // tp

    def body(up, uc, un, lin_ref, sc_ref, pooled_o, pc_o):
        i = pl.program_id(0)
        ext = jnp.concatenate([up[...], uc[...], un[...]], axis=0)
        for g, w in enumerate(POOL_WINDOWS):
            sl = slice(g * LANES, (g + 1) * LANES)
            band = _band(i * tp, i * tp - POOL_HALO, w, tp, tp + 2 * POOL_HALO, s, False)
            sm = jnp.dot(band, ext[:, sl], preferred_element_type=f32, precision=lax.Precision.HIGHEST)
            pooled = (sm / _pool_cnt(i * tp, w, tp, s) - uc[:, sl]).astype(bf16)
            pooled_o[:, sl] = pooled
            mixed = _dot(pooled, lin_ref[g].astype(bf16), "nn")
            pc_o[:, sl] = (mixed * sc_ref[:, sl]).astype(bf16)

    row = pl.BlockSpec((tp, 512), lambda i: (i, 0))
    return pl.pallas_call(
        body, name="pool_fwd", grid=(nt,),
        in_specs=_halo_specs(tp, s, C_UC // 512)
        + [pl.BlockSpec((4, LANES, LANES), lambda i: (0, 0, 0)), pl.BlockSpec((1, 512), lambda i: (0, 0))],
        out_specs=[row, row], out_shape=[SDS((s, 512), bf16), SDS((s, 512), bf16)],
        compiler_params=_cparams())(z, z, z, lin, scale)


def _pool_bwd1(dpc, pooled, lin, scale, s):
    tp = 256

    def body(dpc_ref, pooled_ref, lin_ref, sc_ref, dpn_o, dlin_o, dsc_o):
        i = pl.program_id(0)
        dsc = []
        for g, w in enumerate(POOL_WINDOWS):
            sl = slice(g * LANES, (g + 1) * LANES)
            pooled = pooled_ref[:, sl]
            linb = lin_ref[g].astype(bf16)
            mixed = _dot(pooled, linb, "nn")
            dpc_g = dpc_ref[:, sl]
            dsc.append(jnp.sum(dpc_g * mixed, axis=0, keepdims=True))
            dmixed = (dpc_g * sc_ref[:, sl]).astype(bf16)
            dpn_o[:, sl] = _dot(dmixed, linb, "nt") / _pool_cnt(i * tp, w, tp, s)
            dl = _dot(pooled, dmixed, "tn")

            @pl.when(i == 0)
            def _(g=g, dl=dl):
                dlin_o[g] = dl

            @pl.when(i > 0)
            def _(g=g, dl=dl):
                dlin_o[g] += dl

        dsc = jnp.concatenate(dsc, axis=1)

        @pl.when(i == 0)
        def _():
            dsc_o[...] = dsc

        @pl.when(i > 0)
        def _():
            dsc_o[...] += dsc

    row = pl.BlockSpec((tp, 512), lambda i: (i, 0))
    linspec = pl.BlockSpec((4, LANES, LANES), lambda i: (0, 0, 0))
    one = pl.BlockSpec((1, 512), lambda i: (0, 0))
    return pl.pallas_call(
        body, name="pool_bwd1", grid=(s // tp,), in_specs=[row, row, linspec, one], out_specs=[row, linspec, one],
        out_shape=[SDS((s, 512), f32), SDS((4, LANES, LANES), f32), SDS((1, 512), f32)],
        compiler_params=_cparams())(dpc, pooled, lin, scale)


def _pool_bwd2(dpn, dz, s):
    tp = 256
    nt = s // tp

    def body(dp, dc, dn, dz_in, dz_o):
        i = pl.program_id(0)
        ext = jnp.concatenate([dp[...], dc[...], dn[...]], axis=0)
        for g, w in enumerate(POOL_WINDOWS):
            sl = slice(g * LANES, (g + 1) * LANES)
            band = _band(i * tp, i * tp - POOL_HALO, w, tp, tp + 2 * POOL_HALO, s, True)
            sm = jnp.dot(band, ext[:, sl], preferred_element_type=f32, precision=lax.Precision.HIGHEST)
            dz_o[:, sl] = (sm - dc[:, sl] * _pool_cnt(i * tp, w, tp, s)).astype(bf16)

    return pl.pallas_call(
        body, name="pool_bwd2", grid=(nt,),
        in_specs=_halo_specs(tp, s, 0) + [pl.BlockSpec(memory_space=pl.ANY)],
        out_specs=pl.BlockSpec((tp, 512), lambda i: (i, C_UC // 512)), out_shape=SDS((s, IN_WIDTH), bf16),
        input_output_aliases={3: 0}, compiler_params=_cparams())(dpn, dpn, dpn, dz)


GW = 512


MIX_TM = 256
BR_WIDTHS = (512, 256, 512)
N_GATE_BLOCKS = 3 * D_MODEL // GW


def _gate_specs(ts):
    return [pl.BlockSpec((ts, GW), lambda i, q=q: (i, C_GZ // GW + q)) for q in range(N_GATE_BLOCKS)]


def _mix_out_fwd(acts, z, b_gate, x, w, s):
    ts = MIX_TM

    def body(oa, ob, pc, *rest):
        gz = rest[:N_GATE_BLOCKS]
        bias, x_ref, wa, wb, wc, wo, y3_o, m_o, x1_o = rest[N_GATE_BLOCKS:]
        for b, (act, wt) in enumerate(((oa, wa), (ob, wb), (pc, wc))):
            a = act[...]
            for j in range(N_CHIPS):
                y3_o[b, :, j * 256:(j + 1) * 256] = _dot(a, wt[j], "nn")
        for h in range(2):
            hs = slice(h * GW, (h + 1) * GW)
            tot = jnp.zeros((ts, GW), f32)
            for b in range(3):
                g = jax.nn.sigmoid(gz[2 * b + h][...] + bias[:, b * D_MODEL + h * GW:b * D_MODEL + (h + 1) * GW])
                tot += g * y3_o[b, :, hs]
            m_o[:, hs] = tot.astype(bf16)
        acc = x_ref[...]
        for j in range(N_CHIPS):
            acc += _dot(m_o[:, j * 256:(j + 1) * 256], wo[j], "nn")
        x1_o[...] = acc

    row = lambda width: pl.BlockSpec((ts, width), lambda i: (i, 0))
    res3 = lambda a: pl.BlockSpec(a.shape, lambda i: (0, 0, 0))
    wts = [w["w_branch_a"], w["w_branch_b"], w["w_branch_c"], w["w_out"]]
    return pl.pallas_call(
        body, name="mix_out_fwd", grid=(s // ts,),
        in_specs=[row(BR_WIDTHS[0]), row(BR_WIDTHS[1]), row(BR_WIDTHS[2])] + _gate_specs(ts)
        + [pl.BlockSpec((1, 3 * D_MODEL), lambda i: (0, 0)), row(D_MODEL)] + [res3(a) for a in wts],
        out_specs=[pl.BlockSpec((3, ts, D_MODEL), lambda i: (0, i, 0)), row(D_MODEL), row(D_MODEL)],
        out_shape=[SDS((3, s, D_MODEL), f32), SDS((s, D_MODEL), bf16), SDS((s, D_MODEL), f32)],
        compiler_params=_cparams())(*acts, *([z] * N_GATE_BLOCKS), b_gate, x, *wts)


def _mix_out_bwd(dx1, y3, z, b_gate, w, s):
    ts = MIX_TM

    def body(dx_ref, y3_ref, *rest):
        gz = rest[:N_GATE_BLOCKS]
        bias, wa, wb, wc, wo, dy_o, dz_o, db_o, doa_o, dob_o, dpc_o, dm_s = rest[N_GATE_BLOCKS:]
        i = pl.program_id(0)
        dx = dx_ref[...].astype(bf16)
        for j in range(N_CHIPS):
            dm_s[:, j * 256:(j + 1) * 256] = _dot(dx, wo[j], "nt")
        dz_o[:, 0:C_GZ] = jnp.zeros((ts, C_GZ), bf16)
        dbs = []
        for b in range(3):
            for h in range(2):
                hs = slice(h * GW, (h + 1) * GW)
                g = jax.nn.sigmoid(gz[2 * b + h][...] + bias[:, b * D_MODEL + h * GW:b * D_MODEL + (h + 1) * GW])
                dm = dm_s[:, hs]
                dy_o[b, :, hs] = (dm * g).astype(bf16)
                dgz = dm * y3_ref[b, :, hs] * g * (1.0 - g)
                c0 = C_GZ + b * D_MODEL + h * GW
                dz_o[:, c0:c0 + GW] = dgz.astype(bf16)
                dbs.append(jnp.sum(dgz, axis=0, keepdims=True))
        db = jnp.concatenate(dbs, axis=1)

        @pl.when(i == 0)
        def _():
            db_o[...] = db

        @pl.when(i > 0)
        def _():
            db_o[...] += db

        for b, (wt, out) in enumerate(((wa, doa_o), (wb, dob_o), (wc, dpc_o))):
            acc = _dot(dy_o[b, :, 0:256], wt[0], "nt")
            for j in range(1, N_CHIPS):
                acc += _dot(dy_o[b, :, j * 256:(j + 1) * 256], wt[j], "nt")
            out[...] = acc

    row = lambda width: pl.BlockSpec((ts, width), lambda i: (i, 0))
    res3 = lambda a: pl.BlockSpec(a.shape, lambda i: (0, 0, 0))
    blk3 = pl.BlockSpec((3, ts, D_MODEL), lambda i: (0, i, 0))
    one = pl.BlockSpec((1, 3 * D_MODEL), lambda i: (0, 0))
    wts = [w["w_branch_a"], w["w_branch_b"], w["w_branch_c"], w["w_out"]]
    return pl.pallas_call(
        body, name="mix_out_bwd", grid=(s // ts,),
        in_specs=[row(D_MODEL), blk3] + _gate_specs(ts) + [one] + [res3(a) for a in wts],
        out_specs=[blk3, row(IN_WIDTH), one, row(BR_WIDTHS[0]), row(BR_WIDTHS[1]), row(BR_WIDTHS[2])],
        out_shape=[SDS((3, s, D_MODEL), bf16), SDS((s, IN_WIDTH), bf16), SDS((1, 3 * D_MODEL), f32),
                   SDS((s, BR_WIDTHS[0]), f32), SDS((s, BR_WIDTHS[1]), f32), SDS((s, BR_WIDTHS[2]), f32)],
        scratch_shapes=[pltpu.VMEM((ts, D_MODEL), f32)],
        compiler_params=_cparams())(dx1, y3, *([z] * N_GATE_BLOCKS), b_gate, *wts)


def _loss_grad(y, tgt, s):
    ts = 256

    def fn(yt, tt):
        e = yt - tt
        return e * (1.0 / D_MODEL), jnp.sum(e * e, axis=0, keepdims=True) * (0.5 / D_MODEL)

    row = pl.BlockSpec((ts, D_MODEL), lambda i: (i, 0))
    one = pl.BlockSpec((1, D_MODEL), lambda i: (0, 0))
    return _ew("loss_grad", fn, (s // ts,), [y, tgt], [row, row], [SDS((s, D_MODEL), f32), SDS((1, D_MODEL), f32)],
               [row, one], n_acc=1)


TM = 512


def _layer_fwd(x, w, sm, tabs, s, xchg=None):
    nm = s // TM
    hb = _norm_fwd(x, sm["norm_mix"], s)
    z = _mm("mm_z", [hb, w["w_in"]],
            [pl.BlockSpec((TM, D_MODEL), lambda j, i, k: (i, 0)), pl.BlockSpec((None, D_MODEL, 1664), lambda j, i, k: (j, 0, 0))],
            SDS((s, IN_WIDTH), f32), pl.BlockSpec((TM, 1664), lambda j, i, k: (i, j)), (4, nm, 1), 1, "nn", None)
    qa, kd, vd, qb, kb, vb = _prep_fwd(z, tabs, sm["gains"], s)
    (oa, lse_a), landed = _flash_a_fwd(qa, kd, vd, s, xchg)
    if xchg is not None:
        w = dict(w, **dict(zip(REST + ("next_w_in",), landed)))
    ob_parts = [_attn_b_fwd(qb, kb, vb, gi, s) for gi in range(3)]
    ob, lse_b = _merge_b([p[0] for p in ob_parts], [p[1] for p in ob_parts], s)
    pooled, pc = _pool_fwd(z, sm["pool_lin"], sm["pool_scale"], s)

    y3, merged, x1 = _mix_out_fwd([oa, ob, pc], z, sm["b_gate"], x, w, s)
    h2 = _norm_fwd(x1, sm["norm_ffn"], s)
    g4, u4, a4 = _ffn_up(h2, w["w_ffn_gate"], w["w_ffn_up"], s)
    x2 = _mm_shard_sum("mm_down", [(a4, w["w_ffn_down"])], "nn", D_MODEL, s, TM, add=x1)
    saved = dict(x=x, hb=hb, z=z, qa=qa, kd=kd, vd=vd, qb=qb, kb=kb, vb=vb, oa=oa, lse_a=lse_a, ob=ob, lse_b=lse_b,
                 pooled=pooled, pc=pc, y3=y3, merged=merged, x1=x1, h2=h2, g4=g4, u4=u4, a4=a4)
    return x2, saved, w


def _ffn_up(h2, wg, wu, s):
    nm = s // TM

    def body(h_ref, wg_ref, wu_ref, g_o, u_o, a_o):
        h = h_ref[...]
        g = _dot(h, wg_ref[...], "nn")
        u = _dot(h, wu_ref[...], "nn")
        g_o[...] = g.astype(bf16)
        u_o[...] = u.astype(bf16)
        a_o[...] = (g * jax.nn.sigmoid(g) * u).astype(bf16)

    wsp = pl.BlockSpec((None, D_MODEL, FF_SHARD), lambda j, i: (j, 0, 0))
    osp = pl.BlockSpec((None, TM, FF_SHARD), lambda j, i: (j, i, 0))
    return pl.pallas_call(
        body, name="ffn_up", grid=(4, nm), in_specs=[pl.BlockSpec((TM, D_MODEL), lambda j, i: (i, 0)), wsp, wsp],
        out_specs=[osp, osp, osp],
        out_shape=[SDS((4, s, FF_SHARD), bf16), SDS((4, s, FF_SHARD), bf16), SDS((4, s, FF_SHARD), bf16)],
        compiler_params=_cparams())(h2, wg, wu)


def _ffn_bwd_act(dx2, wd, g4, u4, s):
    def body(dx_ref, wd_ref, g_ref, u_ref, dg_o, du_o):
        dx = dx_ref[...].astype(bf16)
        for j in range(N_CHIPS):
            da = _dot(dx, wd_ref[j], "nt")
            g, u = g_ref[j].astype(f32), u_ref[j].astype(f32)
            sg = jax.nn.sigmoid(g)
            dg_o[j] = (da * u * sg * (1.0 + g * (1.0 - sg))).astype(bf16)
            du_o[j] = (da * g * sg).astype(bf16)

    osp = pl.BlockSpec((N_CHIPS, TM, FF_SHARD), lambda i: (0, i, 0))
    return pl.pallas_call(
        body, name="ffn_bwd_act", grid=(s // TM,),
        in_specs=[pl.BlockSpec((TM, D_MODEL), lambda i: (i, 0)), pl.BlockSpec((N_CHIPS, FF_SHARD, D_MODEL), lambda i: (0, 0, 0)),
                  osp, osp],
        out_specs=[osp, osp], out_shape=[SDS((4, s, FF_SHARD), bf16), SDS((4, s, FF_SHARD), bf16)],
        compiler_params=_cparams())(dx2, wd, g4, u4)


def _mm_shard_sum(name, pairs, dims, out_width, s, tm, add=None):
    n = len(pairs)

    def body(*refs):
        a_refs, w_refs = refs[:n], refs[n:2 * n]
        o_ref = refs[-1]
        acc = refs[2 * n][...] if add is not None else None
        for a_ref, w_ref in zip(a_refs, w_refs):
            for j in range(N_CHIPS):
                if len(a_ref.shape) == 3:
                    a = a_ref[j]
                else:
                    c = a_ref.shape[1] // N_CHIPS
                    a = a_ref[:, j * c:(j + 1) * c]
                part = _dot(a.astype(bf16), w_ref[j], dims)
                acc = part if acc is None else acc + part
        o_ref[...] = acc

    a_specs = [pl.BlockSpec((N_CHIPS, tm, a.shape[2]), lambda i: (0, i, 0)) if a.ndim == 3
               else pl.BlockSpec((tm, a.shape[1]), lambda i: (i, 0)) for a, _ in pairs]
    w_specs = [pl.BlockSpec(wt.shape, lambda i: (0, 0, 0)) for _, wt in pairs]
    row = pl.BlockSpec((tm, out_width), lambda i: (i, 0))
    extra, extra_specs = ([add], [row]) if add is not None else ([], [])
    return pl.pallas_call(
        body, name=name, grid=(s // tm,), in_specs=a_specs + w_specs + extra_specs, out_specs=row,
        out_shape=SDS((s, out_width), f32), compiler_params=_cparams())(
            *[a for a, _ in pairs], *[wt for _, wt in pairs], *extra)


GRAD_WIRE_DTYPE = bf16


WG_TK = 2048


def _wgrad(name, a, a_spec, b, b_spec, out_shape, out_block, s):
    nk = s // min(WG_TK, s)
    return _mm(name, [a, b], [a_spec, b_spec], SDS(out_shape, GRAD_WIRE_DTYPE),
               pl.BlockSpec((None,) + out_block, lambda j, k: (j, 0, 0)), (4, nk), nk, "tn", out_block)


def _layer_bwd(dx2, sv, w, sm, tabs, s, pending_mixer=None):
    tk = min(WG_TK, s)
    tok = lambda width: pl.BlockSpec((tk, width), lambda j, k: (k, 0))
    g_wd = _wgrad("wg_down", sv["a4"], pl.BlockSpec((None, tk, FF_SHARD), lambda j, k: (j, k, 0)), dx2, tok(D_MODEL),
                  (4, FF_SHARD, D_MODEL), (FF_SHARD, D_MODEL), s)
    dg4, du4 = _ffn_bwd_act(dx2, w["w_ffn_down"], sv["g4"], sv["u4"], s)
    sh704 = pl.BlockSpec((None, tk, FF_SHARD), lambda j, k: (j, k, 0))
    g_wg = _wgrad("wg_gate", sv["h2"], tok(D_MODEL), dg4, sh704, (4, D_MODEL, FF_SHARD), (D_MODEL, FF_SHARD), s)
    g_wu = _wgrad("wg_up", sv["h2"], tok(D_MODEL), du4, sh704, (4, D_MODEL, FF_SHARD), (D_MODEL, FF_SHARD), s)
    dh2 = _mm_shard_sum("ffn_bwd_dh", [(dg4, w["w_ffn_gate"]), (du4, w["w_ffn_up"])], "nt", D_MODEL, s, TM)
    dx1, g_norm_ffn = _norm_bwd(sv["x1"], sm["norm_ffn"], dh2, dx2, s)
    colblk = lambda width: pl.BlockSpec((tk, width), lambda j, k: (k, j))
    g_wo = _wgrad("wg_out", sv["merged"], colblk(256), dx1, tok(D_MODEL), (4, 256, D_MODEL), (256, D_MODEL), s)
    dy3, dz, g_bgate, doa, dob, dpc = _mix_out_bwd(dx1, sv["y3"], sv["z"], sm["b_gate"], w, s)
    br_grads = []
    for b, (nm_, act, kdim) in enumerate((("a", sv["oa"], 512), ("b", sv["ob"], 256), ("c", sv["pc"], 512))):
        br_grads.append(_wgrad("wg_br" + nm_, act, tok(kdim), dy3,
                               pl.BlockSpec((None, tk, 256), lambda j, k, b=b: (b, k, j)), (4, kdim, 256), (kdim, 256), s))
    dpn, g_lin, g_scale = _pool_bwd1(dpc, sv["pooled"], sm["pool_lin"], sm["pool_scale"], s)
    dz = _pool_bwd2(dpn, dz, s)
    delta = _delta_b(dob, sv["ob"], s)
    dqb, dkb, dvb = [], [], []
    for gi in range(3):
        dqb.append(_attn_b_dq(sv["qb"], sv["kb"], sv["vb"], dob, sv["lse_b"], delta, gi, s))
        dk, dv = _attn_b_dkv(sv["qb"], sv["kb"], sv["vb"], dob, sv["lse_b"], delta, gi, s)
        dkb.append(dk)
        dvb.append(dv)
    if pending_mixer is not None:
        xchg = (list(pending_mixer) + [g_wg, g_wu, g_wd], lambda ref, peer: ref.at[peer])
    else:
        xchg = None
    (dqa, dkd, dvd), landed = _flash_a_bwd(sv["qa"], sv["kd"], sv["vd"], sv["oa"], sv["lse_a"], doa, s, xchg)
    dz, g_gains = _prep_bwd(sv["z"], tabs, sm["gains"], dqa, dkd, dvd, dqb, dkb, dvb, dz, s)
    g_win = _wgrad("wg_in", sv["hb"], tok(D_MODEL), dz, colblk(1664), (4, D_MODEL, 1664), (D_MODEL, 1664), s)
    dh = _mm_shard_sum("mm_dh", [(dz, w["w_in"])], "nt", D_MODEL, s, 256)
    dx0, g_norm_mix = _norm_bwd(sv["x"], sm["norm_mix"], dh, dx1, s)
    big = dict(w_in=g_win, w_branch_a=br_grads[0], w_branch_b=br_grads[1], w_branch_c=br_grads[2], w_out=g_wo,
               w_ffn_gate=g_wg, w_ffn_up=g_wu, w_ffn_down=g_wd)
    gg = g_gains[0:4, :HEAD_DIM] + g_gains[0:4, HEAD_DIM:]
    small = jnp.concatenate([g_norm_mix.reshape(-1), g_bgate.reshape(-1), gg.reshape(-1), g_lin.reshape(-1),
                             g_scale.reshape(-1), g_norm_ffn.reshape(-1)]).reshape(SMALL_ROWS, LANES)
    return dx0, big, small, landed


def _mesh_pos():
    return lax.axis_index("x"), lax.axis_index("y"), lax.axis_index("c")


def _chip_copies(src, dst, pick_src, send_sems, recv_sems, loc_sems, receiving=True):
    n = len(src)
    x, y, c = _mesh_pos()
    me = 2 * x + y
    local = [pltpu.make_async_copy(pick_src(src[t], me), dst[t].at[me], loc_sems.at[t]) for t in range(n)]
    push, recv = [], []
    for j, (px, py) in enumerate([(1 - x, y), (x, 1 - y), (1 - x, 1 - y)]):
        peer = 2 * px + py
        for t in range(n):
            sems = dict(send_sem=send_sems.at[j * n + t], recv_sem=recv_sems.at[j * n + t],
                        device_id=(px, py, c), device_id_type=MESH)
            push.append(pltpu.make_async_remote_copy(src_ref=pick_src(src[t], peer), dst_ref=dst[t].at[me], **sems))
            if receiving:
                recv.append(pltpu.make_async_remote_copy(src_ref=pick_src(src[t], peer), dst_ref=dst[t].at[peer], **sems))
    return local, push, recv


def _chip_start(copies):
    local, push, _ = copies
    for cp in local + push:
        cp.start()


def _chip_wait(copies):
    local, push, recv = copies
    for cp in push:
        cp.wait_send()
    for cp in recv:
        cp.wait_recv()
    for cp in local:
        cp.wait()


def _chip_sems(n):
    return [pltpu.SemaphoreType.DMA((3 * n,)), pltpu.SemaphoreType.DMA((3 * n,)), pltpu.SemaphoreType.DMA((n,))]


def _chip_landing(srcs):
    return [SDS((N_CHIPS,) + tuple(a.shape[1:]), a.dtype) for a in srcs]


def _chip_exchange(name, srcs, pick_src):
    n = len(srcs)
    any_spec = pl.BlockSpec(memory_space=pl.ANY)

    def body(*refs):
        copies = _chip_copies(refs[:n], refs[n:2 * n], pick_src, *refs[2 * n:])
        _chip_start(copies)
        _chip_wait(copies)

    return pl.pallas_call(
        body, name=name, in_specs=[any_spec] * n, out_specs=[any_spec] * n, out_shape=_chip_landing(srcs),
        scratch_shapes=_chip_sems(n), compiler_params=pltpu.CompilerParams())(*srcs)


def _sibling_exchange(name, srcs):
    n = len(srcs)
    any_spec = pl.BlockSpec(memory_space=pl.ANY)

    def body(*refs):
        src, dst = refs[:n], refs[n:2 * n]
        send_sems, recv_sems = refs[2 * n:]
        x, y, c = _mesh_pos()
        cps = [pltpu.make_async_remote_copy(src_ref=src[t], dst_ref=dst[t], send_sem=send_sems.at[t], recv_sem=recv_sems.at[t],
                                            device_id=(x, y, 1 - c), device_id_type=MESH) for t in range(n)]
        for cp in cps:
            cp.start()
        for cp in cps:
            cp.wait()

    return pl.pallas_call(
        body, name=name, in_specs=[any_spec] * n, out_specs=[any_spec] * n,
        out_shape=[SDS(a.shape, a.dtype) for a in srcs],
        scratch_shapes=[pltpu.SemaphoreType.DMA((n,)), pltpu.SemaphoreType.DMA((n,))],
        compiler_params=pltpu.CompilerParams())(*srcs)


def _all_exchange(name, src):
    any_spec = pl.BlockSpec(memory_space=pl.ANY)

    def body(src_ref, dst_ref, send_sems, recv_sems, loc_sem):
        x, y, c = _mesh_pos()
        me = 4 * x + 2 * y + c
        peers = [(x ^ ((k >> 2) & 1), y ^ ((k >> 1) & 1), c ^ (k & 1)) for k in range(1, N_DEV)]
        local = pltpu.make_async_copy(src_ref, dst_ref.at[me], loc_sem)
        local.start()
        cps = [pltpu.make_async_remote_copy(src_ref=src_ref, dst_ref=dst_ref.at[me], send_sem=send_sems.at[k],
                                            recv_sem=recv_sems.at[k], device_id=p, device_id_type=MESH)
               for k, p in enumerate(peers)]
        for cp in cps:
            cp.start()
        for cp in cps:
            cp.wait_send()
        for k, (px, py, pc) in enumerate(peers):
            pltpu.make_async_remote_copy(src_ref=src_ref, dst_ref=dst_ref.at[4 * px + 2 * py + pc], send_sem=send_sems.at[k],
                                         recv_sem=recv_sems.at[k], device_id=(px, py, pc), device_id_type=MESH).wait_recv()
        local.wait()

    return pl.pallas_call(
        body, name=name, in_specs=[any_spec], out_specs=any_spec, out_shape=SDS((N_DEV,) + src.shape, src.dtype),
        scratch_shapes=[pltpu.SemaphoreType.DMA((N_DEV - 1,)), pltpu.SemaphoreType.DMA((N_DEV - 1,)), pltpu.SemaphoreType.DMA],
        compiler_params=pltpu.CompilerParams())(src)


def _cast_bf16(name, a):
    l, r, c = a.shape
    tr = _row_tile(r, c)
    spec = pl.BlockSpec((None, tr, c), lambda i, j: (i, j, 0))
    return _ew(name, lambda t: t, (l, r // tr), [a], [spec], [SDS(a.shape, bf16)], [spec])[0]


def _sum4(name, land):
    _, r, c = land.shape
    tr = _row_tile(r, c)
    up = lambda t: t.astype(f32)
    return _ew(name, lambda a, b, cc, d: ((up(a) + up(b)) + up(cc)) + up(d), (r // tr,), [land] * 4,
               [pl.BlockSpec((None, tr, c), lambda i, k=k: (k, i, 0)) for k in range(4)],
               [SDS((r, c), f32)], [pl.BlockSpec((tr, c), lambda i: (i, 0))])[0]


def _adam_math(g, w, m, v):
    m2 = ADAM_B1 * m + (1.0 - ADAM_B1) * g
    v2 = ADAM_B2 * v + (1.0 - ADAM_B2) * (g * g)
    m_hat = m2 / (1.0 - ADAM_B1 ** ADAM_STEP)
    v_hat = v2 / (1.0 - ADAM_B2 ** ADAM_STEP)
    delta = -ADAM_LR * (m_hat / (jnp.sqrt(v_hat) + ADAM_EPS) + ADAM_WD * w)
    return delta, m2, v2


def _adamw_big(name, p_own, p_sib, w, m, v):
    l, r, c = w.shape
    tr = _row_tile(r, c)

    def fn(a, b, wt, mt, vt):
        g = a + b
        return (g,) + _adam_math(g, wt, mt, vt)

    spec = pl.BlockSpec((None, tr, c), lambda i, j: (i, j, 0))
    return _ew(name, fn, (l, r // tr), [p_own, p_sib, w, m, v], [spec] * 5, [SDS(w.shape, f32)] * 4, [spec] * 4)


def _adamw_small(land, w, m, v):
    r = w.shape[0]
    tr = _row_tile(r, LANES)

    def fn(*t):
        g = t[0]
        for k in range(1, N_DEV):
            g = g + t[k]
        return (g,) + _adam_math(g, *t[N_DEV:])

    row = pl.BlockSpec((tr, LANES), lambda i: (i, 0))
    return _ew("adamw_small", fn, (r // tr,), [land] * N_DEV + [w, m, v],
               [pl.BlockSpec((None, tr, LANES), lambda i, k=k: (k, i, 0)) for k in range(N_DEV)] + [row] * 3,
               [SDS(w.shape, f32)] * 4, [row] * 4)


def _pack_small(d):
    return jnp.concatenate([d[k].reshape(DEPTH, -1) for k in SMALL], axis=1).reshape(DEPTH * SMALL_ROWS, LANES)


def _unpack_small(a, shapes):
    a = a.reshape(DEPTH, SMALL_ROWS * LANES)
    out, off = {}, 0
    for k, n in zip(SMALL, SMALL_SIZES):
        out[k] = a[:, off:off + n].reshape(shapes[k])
        off += n
    return out


def kernel(x, norm_mix, w_in, b_gate, qn_a, kn_a, qn_b, kn_b, pool_lin, pool_scale, w_branch_a, w_branch_b, w_branch_c, w_out, norm_ffn, w_ffn_gate, w_ffn_up, w_ffn_down, loss_target, m_norm_mix, m_w_in, m_b_gate, m_qn_a, m_kn_a, m_qn_b, m_kn_b, m_pool_lin, m_pool_scale, m_w_branch_a, m_w_branch_b, m_w_branch_c, m_w_out, m_norm_ffn, m_w_ffn_gate, m_w_ffn_up, m_w_ffn_down, v_norm_mix, v_w_in, v_b_gate, v_qn_a, v_kn_a, v_qn_b, v_kn_b, v_pool_lin, v_pool_scale, v_w_branch_a, v_w_branch_b, v_w_branch_c, v_w_out, v_norm_ffn, v_w_ffn_gate, v_w_ffn_up, v_w_ffn_down):
    args = dict(locals())
    s = x.shape[1]
    wts = {k: args[k] for k in WEIGHT_ORDER}
    mom = {k: args["m_" + k] for k in WEIGHT_ORDER}
    var = {k: args["v_" + k] for k in WEIGHT_ORDER}
    tabs = _rope_tables(s)

    shards16 = {k: _cast_bf16("cast_" + k, wts[k]) for k in BIG}
    whole = lambda ref, peer: ref.at[0]
    w_in_l = _chip_exchange("gather_w_in", [shards16["w_in"][0:1]], whole)[0]
    full = [None] * DEPTH

    def small_of(l):
        tile2 = lambda a: jnp.concatenate([a, a])
        gains = jnp.stack([tile2(qn_a[l]), tile2(kn_a[l]), tile2(qn_b[l]), tile2(kn_b[l])])
        return dict(norm_mix=norm_mix[l][None], norm_ffn=norm_ffn[l][None], b_gate=b_gate[l][None], gains=gains,
                    pool_lin=pool_lin[l], pool_scale=pool_scale[l][None])

    xs = x[0]
    saved = []
    for l in range(DEPTH):
        nxt = (l + 1) % DEPTH
        srcs = [shards16[k][l:l + 1] for k in REST] + [shards16["w_in"][nxt:nxt + 1]]
        xs, sv, full[l] = _layer_fwd(xs, dict(w_in=w_in_l), small_of(l), tabs, s, (srcs, whole))
        saved.append(sv)
        w_in_l = full[l]["next_w_in"]
    dy, loss_rows = _loss_grad(xs, loss_target[0], s)
    loss = lax.psum(jnp.sum(loss_rows), ("x", "y", "c"))

    part = {k: [None] * DEPTH for k in BIG}
    small_g = [None] * DEPTH
    to_chip = lambda ref, peer: ref.at[peer]
    assert GRAD_WIRE_DTYPE == bf16
    pending = [full[DEPTH - 1][k] for k in MIXER]
    for l in reversed(range(DEPTH)):
        dy, big, small_g[l], landed = _layer_bwd(dy, saved[l], full[l], small_of(l), tabs, s, pending)
        if l + 1 < DEPTH:
            for k, a in zip(MIXER, landed[:len(MIXER)]):
                part[k][l + 1] = _sum4("sum4_" + k, a)
        for k, a in zip(FFN, landed[len(MIXER):]):
            part[k][l] = _sum4("sum4_" + k, a)
        pending = [big[k] for k in MIXER]
    for k, a in zip(MIXER, _chip_exchange("grad_scatter", pending, to_chip)):
        part[k][0] = _sum4("sum4_" + k, a)

    p_own = [jnp.stack(part[k]) for k in BIG]
    p_sib = _sibling_exchange("grad_sibling", p_own)
    outs = {}
    for k, a, b in zip(BIG, p_own, p_sib):
        outs[k] = _adamw_big("adamw_" + k, a, b, wts[k], mom[k], var[k])

    land_s = _all_exchange("small_allgather", jnp.concatenate(small_g, axis=0))
    res_s = _adamw_small(land_s, _pack_small(wts), _pack_small(mom), _pack_small(var))
    shapes = {k: wts[k].shape for k in SMALL}
    small_out = [_unpack_small(r, shapes) for r in res_s]
    for k in SMALL:
        outs[k] = tuple(so[k] for so in small_out)

    flat = [loss, dy[None]]
    for idx in range(4):
        flat += [outs[k][idx] for k in WEIGHT_ORDER]
    return tuple(flat)
```

```python
import functools

import jax
import jax.numpy as jnp
from jax import lax
from jax.experimental import pallas as pl
from jax.experimental.pallas import tpu as pltpu

f32, bf16 = jnp.float32, jnp.bfloat16
SDS = jax.ShapeDtypeStruct

DEPTH = 4
D_MODEL = 1024
HEAD_DIM = 64
LANES = 128
GRID_W = 64
ROPE_THETA = 10000.0
EPS = 1e-6
NEG_INF = -1e30
SCALE = HEAD_DIM ** -0.5
B_GROUP_CFG = ((128, 1), (512, 4), (2048, 16))
POOL_WINDOWS = (2, 4, 8, 16)
N_CHIPS = 4
N_DEV = 8
C_QA, C_KA, C_VA, C_QB, C_KB, C_VB, C_UC, C_GZ = 0, 512, 640, 768, 1536, 2304, 3072, 3584
IN_WIDTH = 6656
FF_SHARD = 704
ADAM_LR, ADAM_B1, ADAM_B2, ADAM_EPS, ADAM_WD, ADAM_STEP = 0.001, 0.9, 0.999, 1e-08, 0.01, 10
VMEM_LIMIT = 56 * 1024 * 1024
MESH = pl.DeviceIdType.MESH

MIXER = ("w_in", "w_branch_a", "w_branch_b", "w_branch_c", "w_out")
FFN = ("w_ffn_gate", "w_ffn_up", "w_ffn_down")
BIG = MIXER + FFN
REST = BIG[1:]
SMALL = ("norm_mix", "b_gate", "qn_a", "kn_a", "qn_b", "kn_b", "pool_lin", "pool_scale", "norm_ffn")
SMALL_SIZES = (1024, 3072, 64, 64, 64, 64, 65536, 512, 1024)
SMALL_ROWS = sum(SMALL_SIZES) // LANES
WEIGHT_ORDER = ("norm_mix", "w_in", "b_gate", "qn_a", "kn_a", "qn_b", "kn_b", "pool_lin", "pool_scale",
                "w_branch_a", "w_branch_b", "w_branch_c", "w_out", "norm_ffn", "w_ffn_gate", "w_ffn_up", "w_ffn_down")


def _cparams():
    return pltpu.CompilerParams(vmem_limit_bytes=VMEM_LIMIT)


def _lo_mask(shape):
    return lax.broadcasted_iota(jnp.int32, shape, len(shape) - 1) < HEAD_DIM


def _dot(a, b, dims):
    dn = {"nn": (((1,), (0,)), ((), ())), "nt": (((1,), (1,)), ((), ())), "tn": (((0,), (0,)), ((), ()))}[dims]
    return lax.dot_general(a, b, dn, preferred_element_type=f32)


def _mm(name, ins, in_specs, out_shape, out_spec, grid, nk, dims, acc_shape, epilogue=None, n_extra=0, aliases=None):
    multi = isinstance(out_shape, (list, tuple))
    n_out = len(out_shape) if multi else 1

    def body(*refs):
        a_ref, b_ref = refs[0], refs[1]
        extra = refs[2:2 + n_extra]
        outs = refs[2 + n_extra:2 + n_extra + n_out]
        k = pl.program_id(len(grid) - 1)
        part = _dot(a_ref[...].astype(bf16), b_ref[...].astype(bf16), dims)

        def finish(acc):
            res = epilogue(acc, *[e[...] for e in extra]) if epilogue is not None else acc
            res = res if isinstance(res, (list, tuple)) else (res,)
            for o, r in zip(outs, res):
                o[...] = r.astype(o.dtype)

        if nk == 1:
            finish(part)
        else:
            acc_ref = refs[-1]

            @pl.when(k == 0)
            def _():
                acc_ref[...] = part

            @pl.when(k > 0)
            def _():
                acc_ref[...] += part

            @pl.when(k == nk - 1)
            def _():
                finish(acc_ref[...])

    return pl.pallas_call(
        body, name=name, grid=grid, in_specs=in_specs,
        out_specs=list(out_spec) if multi else out_spec,
        out_shape=list(out_shape) if multi else out_shape,
        scratch_shapes=[] if nk == 1 else [pltpu.VMEM(acc_shape, f32)],
        input_output_aliases=aliases or {}, compiler_params=_cparams())(*ins)


def _ew(name, fn, grid, ins, in_specs, out_shapes, out_specs, n_acc=0, aliases=None):
    n_in = len(ins)
    n_out = len(out_shapes) - n_acc
    n_alias = len(aliases or {})

    def body(*refs):
        in_refs = refs[:n_in - n_alias] if n_alias else refs[:n_in]
        out_refs = refs[n_in:n_in + n_out]
        acc_refs = refs[n_in + n_out:]
        i = pl.program_id(len(grid) - 1)
        res = fn(*[r[...] for r in in_refs])
        res = res if isinstance(res, (list, tuple)) else (res,)
        for o, r in zip(out_refs, res[:n_out]):
            o[...] = r.astype(o.dtype)
        for a, r in zip(acc_refs, res[n_out:]):
            @pl.when(i == 0)
            def _(a=a, r=r):
                a[...] = r.astype(a.dtype)

            @pl.when(i > 0)
            def _(a=a, r=r):
                a[...] += r.astype(a.dtype)

    return pl.pallas_call(body, name=name, grid=grid, in_specs=in_specs, out_specs=out_specs, out_shape=out_shapes,
                          input_output_aliases=aliases or {}, compiler_params=_cparams())(*ins)


EW_BLOCK_ELEMS = 128 * 1024


def _row_tile(rows, cols):
    step = 16 if rows % 16 == 0 else 8
    return max(t for t in range(step, rows + 1, step) if rows % t == 0 and (t * cols <= EW_BLOCK_ELEMS or t == step))


def _rope_tables(s):
    t = jnp.arange(s)
    inv_ax = ROPE_THETA ** (-jnp.arange(0, HEAD_DIM // 2, 2, dtype=f32) / (HEAD_DIM // 2))
    inv_sq = ROPE_THETA ** (-jnp.arange(0, HEAD_DIM, 2, dtype=f32) / HEAD_DIM)
    ang_row = (t // GRID_W).astype(f32)[:, None] * inv_ax[None, :]
    ang_col = (t % GRID_W).astype(f32)[:, None] * inv_ax[None, :]
    ang_seq = t.astype(f32)[:, None] * inv_sq[None, :]
    a_ax = jnp.concatenate([ang_row, ang_row, ang_col, ang_col], axis=1)
    sg_ax = jnp.concatenate([-jnp.ones(16), jnp.ones(16), -jnp.ones(16), jnp.ones(16)]).astype(f32)
    a_sq = jnp.concatenate([ang_seq, ang_seq], axis=1)
    sg_sq = jnp.concatenate([-jnp.ones(32), jnp.ones(32)]).astype(f32)
    two = lambda a: jnp.concatenate([a, a], axis=1)
    return (two(jnp.cos(a_ax)), two(jnp.sin(a_ax) * sg_ax), two(jnp.cos(a_sq)), two(jnp.sin(a_sq) * sg_sq))


def _head_stat(v, lo):
    r = lax.broadcasted_iota(jnp.int32, (LANES, LANES), 0) < HEAD_DIM
    c = lax.broadcasted_iota(jnp.int32, (LANES, LANES), 1) < HEAD_DIM
    same_head = (r == c).astype(f32)
    return jnp.dot(v, same_head, preferred_element_type=f32, precision=lax.Precision.HIGHEST)


def _partner(y, off):
    lane = lax.broadcasted_iota(jnp.int32, y.shape, 1)
    first = (lane & (2 * off - 1)) < off
    return jnp.where(first, pltpu.roll(y, LANES - off, 1), pltpu.roll(y, off, 1))


def _normrope(xc, g, cos, sin, off):
    lo = _lo_mask(xc.shape)
    r = lax.rsqrt(_head_stat(xc * xc, lo) * (1.0 / HEAD_DIM) + EPS)
    y = xc * r * g
    return y * cos + _partner(y, off) * sin


def _normrope_bwd(xc, g, cos, sin, off, drot):
    lo = _lo_mask(xc.shape)
    r = lax.rsqrt(_head_stat(xc * xc, lo) * (1.0 / HEAD_DIM) + EPS)
    n = xc * r
    dy = drot * cos + _partner(drot * sin, off)
    dg = jnp.sum(dy * n, axis=0, keepdims=True)
    dn = dy * g
    dx = r * (dn - n * (_head_stat(dn * n, lo) * (1.0 / HEAD_DIM)))
    return dx, dg


def _norm_fwd(x, g, s):
    ts = 256

    def fn(xt, gt):
        r = lax.rsqrt(jnp.mean(xt * xt, axis=1, keepdims=True) + EPS)
        return xt * r * gt

    return _ew("norm_fwd", fn, (s // ts,), [x, g],
               [pl.BlockSpec((ts, D_MODEL), lambda i: (i, 0)), pl.BlockSpec((1, D_MODEL), lambda i: (0, 0))],
               [SDS((s, D_MODEL), bf16)], [pl.BlockSpec((ts, D_MODEL), lambda i: (i, 0))])[0]


def _norm_bwd(x, g, dh, dres, s):
    ts = 256

    def fn(xt, gt, dht, drt):
        r = lax.rsqrt(jnp.mean(xt * xt, axis=1, keepdims=True) + EPS)
        n = xt * r
        dn = dht * gt
        dx = drt + r * (dn - n * jnp.mean(dn * n, axis=1, keepdims=True))
        return dx, jnp.sum(dht * n, axis=0, keepdims=True)

    row = pl.BlockSpec((ts, D_MODEL), lambda i: (i, 0))
    one = pl.BlockSpec((1, D_MODEL), lambda i: (0, 0))
    return _ew("norm_bwd", fn, (s // ts,), [x, g, dh, dres], [row, one, row, row],
               [SDS((s, D_MODEL), f32), SDS((1, D_MODEL), f32)], [row, one], n_acc=1)


def _prep_fwd(z, tabs, gains, s):
    ts = 256
    cos_a, sin_a, cos_b, sin_b = tabs

    def body(za_ref, zq_ref, zk_ref, zv_ref, ca, sa, cb, sb, g_ref, qa_o, kd_o, vd_o, qb_o, kb_o, vb_o):
        lo = _lo_mask((ts, LANES))
        g = g_ref[...]
        ca_, sa_, cb_, sb_ = ca[...], sa[...], cb[...], sb[...]
        for c in range(4):
            qa_o[:, c * LANES:(c + 1) * LANES] = _normrope(za_ref[:, c * LANES:(c + 1) * LANES], g[0:1], ca_, sa_, 16).astype(bf16)
        k = _normrope(za_ref[:, C_KA:C_KA + LANES], g[1:2], ca_, sa_, 16)
        kr = pltpu.roll(k, HEAD_DIM, 1)
        kd_o[0] = k.astype(bf16)
        kd_o[1] = kr.astype(bf16)
        v = za_ref[:, C_VA:C_VA + LANES]
        ones_col = (lax.broadcasted_iota(jnp.int32, v.shape, 1) == HEAD_DIM).astype(f32)
        vd_o[0] = jnp.where(lo, v, ones_col).astype(bf16)
        vd_o[1] = jnp.where(lo, pltpu.roll(v, HEAD_DIM, 1), ones_col).astype(bf16)
        for c in range(6):
            sl = slice(c * LANES, (c + 1) * LANES)
            qb_o[:, sl] = _normrope(zq_ref[:, sl], g[2:3], cb_, sb_, 32).astype(bf16)
            kb_o[:, sl] = _normrope(zk_ref[:, sl], g[3:4], cb_, sb_, 32).astype(bf16)
        vb_o[...] = zv_ref[...].astype(bf16)

    w = 768
    zspec = lambda cb: pl.BlockSpec((ts, w), lambda i: (i, cb))
    tab = pl.BlockSpec((ts, LANES), lambda i: (i, 0))
    dup = pl.BlockSpec((2, ts, LANES), lambda i: (0, i, 0))
    return pl.pallas_call(
        body, name="prep_fwd", grid=(s // ts,),
        in_specs=[zspec(0), zspec(1), zspec(2), zspec(3), tab, tab, tab, tab, pl.BlockSpec((4, LANES), lambda i: (0, 0))],
        out_specs=[pl.BlockSpec((ts, 512), lambda i: (i, 0)), dup, dup, zspec(0), zspec(0), zspec(0)],
        out_shape=[SDS((s, 512), bf16), SDS((2, s, LANES), bf16), SDS((2, s, LANES), bf16),
                   SDS((s, w), bf16), SDS((s, w), bf16), SDS((s, w), bf16)],
        compiler_params=_cparams())(z, z, z, z, cos_a, sin_a, cos_b, sin_b, gains)


def _prep_bwd(z, tabs, gains, dqa, dkd, dvd, dqb, dkb, dvb, dz, s):
    ts = 256
    cos_a, sin_a, cos_b, sin_b = tabs

    def body(za_ref, zq_ref, zk_ref, ca, sa, cb, sb, g_ref, dqa_r, dkd_r, dvd_r,
             dq0, dq1, dq2, dk0, dk1, dk2, dv0, dv1, dv2, dz_in, dz_o, dg_o):
        i = pl.program_id(0)
        lo = _lo_mask((ts, LANES))
        g = g_ref[...]
        ca_, sa_, cb_, sb_ = ca[...], sa[...], cb[...], sb[...]
        dg = [jnp.zeros((1, LANES), f32) for _ in range(4)]
        for c in range(4):
            sl = slice(c * LANES, (c + 1) * LANES)
            dx, d = _normrope_bwd(za_ref[:, sl], g[0:1], ca_, sa_, 16, dqa_r[:, sl])
            dz_o[:, sl] = dx.astype(bf16)
            dg[0] += d
        dk = jnp.where(lo, dkd_r[0], pltpu.roll(dkd_r[1], HEAD_DIM, 1))
        dx, d = _normrope_bwd(za_ref[:, C_KA:C_KA + LANES], g[1:2], ca_, sa_, 16, dk)
        dz_o[:, C_KA:C_KA + LANES] = dx.astype(bf16)
        dg[1] += d
        dz_o[:, C_VA:C_VA + LANES] = jnp.where(lo, dvd_r[0], pltpu.roll(dvd_r[1], HEAD_DIM, 1)).astype(bf16)
        dqs, dks, dvs = (dq0, dq1, dq2), (dk0, dk1, dk2), (dv0, dv1, dv2)
        for c in range(6):
            sl = slice(c * LANES, (c + 1) * LANES)
            gsl = slice((c % 2) * LANES, (c % 2 + 1) * LANES)
            dx, d = _normrope_bwd(zq_ref[:, sl], g[2:3], cb_, sb_, 32, dqs[c // 2][:, gsl])
            dz_o[:, C_QB + c * LANES:C_QB + (c + 1) * LANES] = dx.astype(bf16)
            dg[2] += d
            dx, d = _normrope_bwd(zk_ref[:, sl], g[3:4], cb_, sb_, 32, dks[c // 2][:, gsl])
            dz_o[:, C_KB + c * LANES:C_KB + (c + 1) * LANES] = dx.astype(bf16)
            dg[3] += d
            dz_o[:, C_VB + c * LANES:C_VB + (c + 1) * LANES] = dvs[c // 2][:, gsl].astype(bf16)
        dgs = jnp.concatenate(dg + [jnp.zeros((4, LANES), f32)], axis=0)

        @pl.when(i == 0)
        def _():
            dg_o[...] = dgs

        @pl.when(i > 0)
        def _():
            dg_o[...] += dgs

    w = 768
    zspec = lambda cb: pl.BlockSpec((ts, w), lambda i: (i, cb))
    tab = pl.BlockSpec((ts, LANES), lambda i: (i, 0))
    dup = pl.BlockSpec((2, ts, LANES), lambda i: (0, i, 0))
    grp = pl.BlockSpec((ts, 256), lambda i: (i, 0))
    dzo, dgo = pl.pallas_call(
        body, name="prep_bwd", grid=(s // ts,),
        in_specs=[zspec(0), zspec(1), zspec(2), tab, tab, tab, tab, pl.BlockSpec((4, LANES), lambda i: (0, 0)),
                  pl.BlockSpec((ts, 512), lambda i: (i, 0)), dup, dup] + [grp] * 9 + [pl.BlockSpec(memory_space=pl.ANY)],
        out_specs=[pl.BlockSpec((ts, C_UC), lambda i: (i, 0)), pl.BlockSpec((8, LANES), lambda i: (0, 0))],
        out_shape=[SDS((s, IN_WIDTH), bf16), SDS((8, LANES), f32)],
        input_output_aliases={20: 0}, compiler_params=_cparams())(
            z, z, z, cos_a, sin_a, cos_b, sin_b, gains, dqa, dkd, dvd, *dqb, *dkb, *dvb, dz)
    return dzo, dgo


def _stack_heads(x, lo, dtype):
    z = jnp.zeros_like(x)
    return jnp.concatenate([jnp.where(lo, x, z), jnp.where(lo, z, x)], axis=0).astype(dtype)


def _rows_of(v, lo, kind):
    if kind == "max":
        a = jnp.max(jnp.where(lo, v, NEG_INF * 10), axis=1, keepdims=True)
        b = jnp.max(jnp.where(lo, NEG_INF * 10, v), axis=1, keepdims=True)
    else:
        a = jnp.sum(jnp.where(lo, v, 0.0), axis=1, keepdims=True) * (1.0 / HEAD_DIM)
        b = jnp.sum(jnp.where(lo, 0.0, v), axis=1, keepdims=True) * (1.0 / HEAD_DIM)
    return jnp.concatenate([a, b], axis=0)


def _stack_low(x, lo, dtype, scale=None):
    x = x.astype(f32)
    if scale is not None:
        x = x * scale
    z = jnp.zeros_like(x)
    return jnp.concatenate([jnp.where(lo, x, z), jnp.where(lo, pltpu.roll(x, HEAD_DIM, 1), z)], axis=0).astype(dtype)


def _unstack_low(xs, lo, rows):
    return jnp.where(lo, xs[:rows], pltpu.roll(xs[rows:], HEAD_DIM, 1))


FA_TQ, FA_TK = 256, 1024


def _host_exchange(body, n_in, n_out, grid, xchg):
    if xchg is None:
        return body, [], [], [], []
    srcs, pick_src = xchg
    n = len(srcs)
    any_spec = pl.BlockSpec(memory_space=pl.ANY)

    def hosted(*refs):
        ins, src = refs[:n_in], refs[n_in:n_in + n]
        outs, dst = refs[n_in + n:n_in + n + n_out], refs[n_in + n + n_out:n_in + 2 * n + n_out]
        scratch, sems = refs[n_in + 2 * n + n_out:-3], refs[-3:]
        ids = [pl.program_id(a) for a in range(len(grid))]
        first = functools.reduce(jnp.logical_and, [i == 0 for i in ids])
        last = functools.reduce(jnp.logical_and, [i == g - 1 for i, g in zip(ids, grid)])

        @pl.when(first)
        def _():
            _chip_start(_chip_copies(src, dst, pick_src, *sems, receiving=False))

        body(*ins, *outs, *scratch)

        @pl.when(last)
        def _():
            _chip_wait(_chip_copies(src, dst, pick_src, *sems))

    return hosted, [any_spec] * n, [any_spec] * n, _chip_landing(srcs), _chip_sems(n)


def _flash_a_fwd(qa, kd, vd, s, xchg=None):
    tq, tk = FA_TQ, min(4 * FA_TK, s)
    nk = s // tk

    def body(q_ref, k_ref, v_ref, o_ref, lse_ref, qs, m_s, acc):
        lo = _lo_mask((tq, LANES))
        for pp in range(2):
            qs[pp] = _stack_low(q_ref[:, pp * LANES:(pp + 1) * LANES], lo, bf16, SCALE)
        m_s[...] = jnp.full(m_s.shape, NEG_INF, f32)
        acc[...] = jnp.zeros(acc.shape, f32)

        def chunk(j, carry):
            rows = pl.ds(pl.multiple_of(j * tk, tk), tk)
            k, v = k_ref[rows, :], v_ref[rows, :]
            for pp in range(2):
                sc = _dot(qs[pp], k, "nt")
                m_prev = m_s[pp]
                m_new = jnp.maximum(m_prev, jnp.max(sc, axis=1, keepdims=True))
                p = jnp.exp(sc - m_new).astype(bf16)
                acc[pp] = acc[pp] * jnp.exp(m_prev - m_new) + _dot(p, v, "nn")
                m_s[pp] = m_new
            return carry

        lax.fori_loop(0, nk, chunk, 0)
        for pp in range(2):
            sl = slice(pp * LANES, (pp + 1) * LANES)
            a = acc[pp]
            lane = lax.broadcasted_iota(jnp.int32, a.shape, 1)
            l = jnp.sum(jnp.where(lane == HEAD_DIM, a, 0.0), axis=1, keepdims=True)
            o_ref[:, sl] = _unstack_low(a / l, lo, tq).astype(bf16)
            lse = m_s[pp] + jnp.log(l)
            lse_ref[:, sl] = jnp.where(lo, lse[:tq], lse[tq:])

    kv = pl.BlockSpec((None, s, LANES), lambda c, i: (c, 0, 0))
    qo = pl.BlockSpec((tq, 2 * LANES), lambda c, i: (i, c))
    grid = (2, s // tq)
    hosted, xi, xo, xs, xsem = _host_exchange(body, 3, 2, grid, xchg)
    res = pl.pallas_call(
        hosted, name="flash_a_fwd" + ("_x" if xchg else ""), grid=grid, in_specs=[qo, kv, kv] + xi, out_specs=[qo, qo] + xo,
        out_shape=[SDS((s, 512), bf16), SDS((s, 512), f32)] + xs,
        scratch_shapes=[pltpu.VMEM((2, 2 * tq, LANES), bf16), pltpu.VMEM((2, 2 * tq, 1), f32),
                        pltpu.VMEM((2, 2 * tq, LANES), f32)] + xsem,
        compiler_params=_cparams())(qa, kd, vd, *(xchg[0] if xchg else []))
    return res[:2], res[2:]


def _flash_a_bwd(qa, kd, vd, oa, lse, doa, s, xchg=None):
    tq, tk = FA_TQ, 2 * FA_TK
    nk = s // tk

    def body(q_ref, do_ref, o_ref, lse_ref, k_ref, v_ref, dq_ref, dk_ref, dv_ref, qs, dos, lse_s, dl_s, dq_s):
        i = pl.program_id(1)
        lo = _lo_mask((tq, LANES))

        @pl.when(i == 0)
        def _():
            dk_ref[...] = jnp.zeros(dk_ref.shape, f32)
            dv_ref[...] = jnp.zeros(dv_ref.shape, f32)

        for pp in range(2):
            sl = slice(pp * LANES, (pp + 1) * LANES)
            do = do_ref[:, sl]
            qs[...] = _stack_low(q_ref[:, sl], lo, bf16, SCALE)
            dos[...] = _stack_low(do, lo, bf16)
            dl_s[...] = _rows_of(do * o_ref[:, sl].astype(f32), lo, "sum") * HEAD_DIM
            lse_s[...] = _rows_of(lse_ref[:, sl], lo, "max")
            dq_s[...] = jnp.zeros(dq_s.shape, f32)

            def chunk(j, carry):
                rows = pl.ds(pl.multiple_of(j * tk, tk), tk)
                k, v = k_ref[rows, :], v_ref[rows, :]
                q_, do_ = qs[...], dos[...]
                p = jnp.exp(_dot(q_, k, "nt") - lse_s[...])
                ds = (p * (_dot(do_, v, "nt") - dl_s[...])).astype(bf16)
                dv_ref[rows, :] += _dot(p.astype(bf16), do_, "tn")
                dk_ref[rows, :] += _dot(ds, q_, "tn")
                dq_s[...] += _dot(ds, k, "nn")
                return carry

            lax.fori_loop(0, nk, chunk, 0)
            dq_ref[:, sl] = _unstack_low(dq_s[...], lo, tq) * SCALE

    grp = lambda: pl.BlockSpec((tq, 2 * LANES), lambda c, i: (i, c))
    kv = lambda: pl.BlockSpec((None, s, LANES), lambda c, i: (c, 0, 0))
    grid = (2, s // tq)
    hosted, xi, xo, xs, xsem = _host_exchange(body, 6, 3, grid, xchg)
    res = pl.pallas_call(
        hosted, name="flash_a_bwd" + ("_x" if xchg else ""), grid=grid,
        in_specs=[grp(), grp(), grp(), grp(), kv(), kv()] + xi, out_specs=[grp(), kv(), kv()] + xo,
        out_shape=[SDS((s, 512), f32), SDS((2, s, LANES), f32), SDS((2, s, LANES), f32)] + xs,
        scratch_shapes=[pltpu.VMEM((2 * tq, LANES), bf16), pltpu.VMEM((2 * tq, LANES), bf16), pltpu.VMEM((2 * tq, 1), f32),
                        pltpu.VMEM((2 * tq, 1), f32), pltpu.VMEM((2 * tq, LANES), f32)] + xsem,
        compiler_params=_cparams())(qa, doa, oa, lse, kd, vd, *(xchg[0] if xchg else []))
    return res[:3], res[3:]


BH = 128
BQ_MAX = 512


class _BandGeom:
    def __init__(self, gi, s):
        self.d = B_GROUP_CFG[gi][1]
        self.l_sub = s // self.d
        self.bq = min(BQ_MAX, self.l_sub)
        self.nb = self.l_sub // self.bq
        self.grid = (self.d, self.nb, 2)
        per, last = self.bq // BH, self.l_sub // BH - 1
        self.in_col = lambda r, pp: r * 6 + gi * 2 + pp
        self.out_col = lambda r, pp: r * 2 + pp
        self.cur = lambda col: pl.BlockSpec((self.bq, LANES), lambda r, n, pp: (n, col(r, pp)))
        self.prev = lambda col: pl.BlockSpec((BH, LANES), lambda r, n, pp: (jnp.maximum(n * per - 1, 0), col(r, pp)))
        self.next = lambda col: pl.BlockSpec((BH, LANES), lambda r, n, pp: (jnp.minimum((n + 1) * per, last), col(r, pp)))
        self.window = lambda col: [self.prev(col), self.cur(col), self.next(col)]
        self.view_in = lambda a: a.reshape(self.l_sub, self.d * 768)
        self.view_out = lambda a: a.reshape(self.l_sub, self.d * 256)

    def fill_band_bias(self, bias_s, q_rows, q_off, k_off):
        @pl.when((pl.program_id(0) == 0) & (pl.program_id(1) == 0) & (pl.program_id(2) == 0))
        def _():
            r = lax.broadcasted_iota(jnp.int32, bias_s.shape, 0)
            q = jnp.where(r >= q_rows, r - q_rows, r) + q_off
            k = lax.broadcasted_iota(jnp.int32, bias_s.shape, 1) + k_off
            bias_s[...] = jnp.where(jnp.abs(q - k) <= 64, 0.0, NEG_INF)

    def edge_bias(self, first_pos, n, stacked_rows=False):
        if stacked_rows:
            r = lax.broadcasted_iota(jnp.int32, (2 * n, 1), 0)
            idx = jnp.where(r >= n, r - n, r)
        else:
            idx = lax.broadcasted_iota(jnp.int32, (1, n), 1)
        pos = first_pos + idx
        return jnp.where((pos >= 0) & (pos < self.l_sub), 0.0, NEG_INF)


def _cat3(a, b, c):
    return jnp.concatenate([a[...], b[...], c[...]], axis=0)


def _attn_b_fwd(qb, kb, vb, gi, s):
    g = _BandGeom(gi, s)
    bq = g.bq

    def body(q_ref, kp, kc, kn, vp, vc, vn, o_ref, lse_ref, bias_s):
        n = pl.program_id(1)
        g.fill_band_bias(bias_s, bq, 0, -BH)
        lo = _lo_mask((bq, LANES))
        qs = _stack_heads(q_ref[...], lo, bf16)
        k, v = _cat3(kp, kc, kn), _cat3(vp, vc, vn)
        sc = _dot(qs, k, "nt") * SCALE + bias_s[...] + g.edge_bias(n * bq - BH, bq + 2 * BH)
        m = jnp.max(sc, axis=1, keepdims=True)
        p = jnp.exp(sc - m)
        den = jnp.sum(p, axis=1, keepdims=True)
        o = _dot(p.astype(bf16), v, "nn") / den
        o_ref[...] = jnp.where(lo, o[:bq], o[bq:])
        lse = m + jnp.log(den)
        lse_ref[...] = jnp.where(lo, lse[:bq], lse[bq:])

    out = g.cur(g.out_col)
    qv, kv, vv = g.view_in(qb), g.view_in(kb), g.view_in(vb)
    o, lse = pl.pallas_call(
        body, name=f"attn_b_fwd{gi}", grid=g.grid,
        in_specs=[g.cur(g.in_col)] + g.window(g.in_col) * 2, out_specs=[out, out],
        out_shape=[SDS((g.l_sub, g.d * 256), f32), SDS((g.l_sub, g.d * 256), f32)],
        scratch_shapes=[pltpu.VMEM((2 * bq, bq + 2 * BH), f32)],
        compiler_params=_cparams())(qv, kv, kv, kv, vv, vv, vv)
    return o.reshape(s, 256), lse.reshape(s, 256)


def _attn_b_dq(qb, kb, vb, dob, lse, delta, gi, s):
    g = _BandGeom(gi, s)
    bq = g.bq

    def body(q_ref, kp, kc, kn, vp, vc, vn, do_ref, lse_ref, dl_ref, dq_ref, bias_s):
        n = pl.program_id(1)
        g.fill_band_bias(bias_s, bq, 0, -BH)
        lo = _lo_mask((bq, LANES))
        qs = _stack_heads(q_ref[...], lo, bf16)
        dos = _stack_heads(do_ref[...], lo, bf16)
        k, v = _cat3(kp, kc, kn), _cat3(vp, vc, vn)
        sc = _dot(qs, k, "nt") * SCALE + bias_s[...] + g.edge_bias(n * bq - BH, bq + 2 * BH)
        p = jnp.exp(sc - _rows_of(lse_ref[...], lo, "max"))
        dp = _dot(dos, v, "nt")
        ds = (p * (dp - _rows_of(dl_ref[...], lo, "sum")) * SCALE).astype(bf16)
        dqs = _dot(ds, k, "nn")
        dq_ref[...] = jnp.where(lo, dqs[:bq], dqs[bq:])

    out = g.cur(g.out_col)
    qv, kv, vv = g.view_in(qb), g.view_in(kb), g.view_in(vb)
    dq = pl.pallas_call(
        body, name=f"attn_b_dq{gi}", grid=g.grid,
        in_specs=[g.cur(g.in_col)] + g.window(g.in_col) * 2 + [out, out, out], out_specs=out,
        out_shape=SDS((g.l_sub, g.d * 256), f32), scratch_shapes=[pltpu.VMEM((2 * bq, bq + 2 * BH), f32)],
        compiler_params=_cparams())(qv, kv, kv, kv, vv, vv, vv, g.view_out(dob), g.view_out(lse), g.view_out(delta))
    return dq.reshape(s, 256)


def _attn_b_dkv(qb, kb, vb, dob, lse, delta, gi, s):
    g = _BandGeom(gi, s)
    bq = g.bq
    nq = bq + 2 * BH

    def body(k_ref, v_ref, qp, qc, qn, dop, doc, don, lp, lc, ln, dp_, dc_, dn_, dk_ref, dv_ref, bias_s):
        m = pl.program_id(1)
        g.fill_band_bias(bias_s, nq, -BH, 0)
        lo = _lo_mask((nq, LANES))
        qs = _stack_heads(_cat3(qp, qc, qn), lo, bf16)
        dos = _stack_heads(_cat3(dop, doc, don), lo, bf16)
        lses = _rows_of(_cat3(lp, lc, ln), lo, "max")
        dls = _rows_of(_cat3(dp_, dc_, dn_), lo, "sum")
        k, v = k_ref[...], v_ref[...]
        sc = _dot(qs, k, "nt") * SCALE + bias_s[...] + g.edge_bias(m * bq - BH, nq, stacked_rows=True)
        p = jnp.exp(sc - lses)
        dpv = _dot(dos, v, "nt")
        ds = (p * (dpv - dls) * SCALE).astype(bf16)
        dv_ref[...] = _dot(p.astype(bf16), dos, "tn")
        dk_ref[...] = _dot(ds, qs, "tn")

    cur = g.cur(g.out_col)
    qv, kv, vv = g.view_in(qb), g.view_in(kb), g.view_in(vb)
    dk, dv = pl.pallas_call(
        body, name=f"attn_b_dkv{gi}", grid=g.grid,
        in_specs=[g.cur(g.in_col)] * 2 + g.window(g.in_col) + g.window(g.out_col) * 3, out_specs=[cur, cur],
        out_shape=[SDS((g.l_sub, g.d * 256), f32), SDS((g.l_sub, g.d * 256), f32)],
        scratch_shapes=[pltpu.VMEM((2 * nq, bq), f32)],
        compiler_params=_cparams())(kv, vv, qv, qv, qv, *([g.view_out(dob)] * 3), *([g.view_out(lse)] * 3),
                                    *([g.view_out(delta)] * 3))
    return dk.reshape(s, 256), dv.reshape(s, 256)


def _merge_b(o_list, lse_list, s):
    ts = 256

    def fn(o0, o1, o2, l0, l1, l2):
        m = jnp.maximum(jnp.maximum(l0, l1), l2)
        w0, w1, w2 = jnp.exp(l0 - m), jnp.exp(l1 - m), jnp.exp(l2 - m)
        den = w0 + w1 + w2
        return (w0 * o0 + w1 * o1 + w2 * o2) / den, m + jnp.log(den)

    row = pl.BlockSpec((ts, 256), lambda i: (i, 0))
    return _ew("merge_b", fn, (s // ts,), [*o_list, *lse_list], [row] * 6,
               [SDS((s, 256), bf16), SDS((s, 256), f32)], [row, row])


def _delta_b(dob, ob, s):
    ts = 256

    def fn(do, o):
        prod = do * o.astype(f32)
        lo = _lo_mask((ts, LANES))
        return jnp.concatenate([_head_stat(prod[:, :LANES], lo), _head_stat(prod[:, LANES:], lo)], axis=1)

    row = pl.BlockSpec((ts, 256), lambda i: (i, 0))
    return _ew("delta_b", fn, (s // ts,), [dob, ob], [row, row], [SDS((s, 256), f32)], [row])[0]


def _band(t0, u0, w, nt, nu, s, transposed):
    t = t0 + lax.broadcasted_iota(jnp.int32, (nt, nu), 0)
    u = u0 + lax.broadcasted_iota(jnp.int32, (nt, nu), 1)
    if transposed:
        lo, hi = jnp.clip(t - w // 2 + 1, 0, s), jnp.clip(t + w // 2 + 1, 0, s)
    else:
        lo, hi = jnp.clip(t - w // 2, 0, s), jnp.clip(t + w - w // 2, 0, s)
    return ((u >= lo) & (u < hi)).astype(f32)


def _pool_cnt(t0, w, nt, s):
    t = t0 + lax.broadcasted_iota(jnp.int32, (nt, 1), 0)
    return (jnp.clip(t + w - w // 2, 0, s) - jnp.clip(t - w // 2, 0, s)).astype(f32)


POOL_HALO = 8


def _halo_specs(tp, s, cb):
    per, last = tp // POOL_HALO, s // POOL_HALO - 1
    return [pl.BlockSpec((POOL_HALO, 512), lambda i: (jnp.maximum(i * per - 1, 0), cb)),
            pl.BlockSpec((tp, 512), lambda i: (i, cb)),
            pl.BlockSpec((POOL_HALO, 512), lambda i: (jnp.minimum((i + 1) * per, last), cb))]


def _pool_fwd(z, lin, scale, s):
    tp = 256
    nt = s // tp

    def body(up, uc, un, lin_ref, sc_ref, pooled_o, pc_o):
        i = pl.program_id(0)
        ext = jnp.concatenate([up[...], uc[...], un[...]], axis=0)
        for g, w in enumerate(POOL_WINDOWS):
            sl = slice(g * LANES, (g + 1) * LANES)
            band = _band(i * tp, i * tp - POOL_HALO, w, tp, tp + 2 * POOL_HALO, s, False)
            sm = jnp.dot(band, ext[:, sl], preferred_element_type=f32, precision=lax.Precision.HIGHEST)
            pooled = (sm / _pool_cnt(i * tp, w, tp, s) - uc[:, sl]).astype(bf16)
            pooled_o[:, sl] = pooled
            mixed = _dot(pooled, lin_ref[g].astype(bf16), "nn")
            pc_o[:, sl] = (mixed * sc_ref[:, sl]).astype(bf16)

    row = pl.BlockSpec((tp, 512), lambda i: (i, 0))
    return pl.pallas_call(
        body, name="pool_fwd", grid=(nt,),
        in_specs=_halo_specs(tp, s, C_UC // 512)
        + [pl.BlockSpec((4, LANES, LANES), lambda i: (0, 0, 0)), pl.BlockSpec((1, 512), lambda i: (0, 0))],
        out_specs=[row, row], out_shape=[SDS((s, 512), bf16), SDS((s, 512), bf16)],
        compiler_params=_cparams())(z, z, z, lin, scale)


def _pool_bwd1(dpc, pooled, lin, scale, s):
    tp = 256

    def body(dpc_ref, pooled_ref, lin_ref, sc_ref, dpn_o, dlin_o, dsc_o):
        i = pl.program_id(0)
        dsc = []
        for g, w in enumerate(POOL_WINDOWS):
            sl = slice(g * LANES, (g + 1) * LANES)
            pooled = pooled_ref[:, sl]
            linb = lin_ref[g].astype(bf16)
            mixed = _dot(pooled, linb, "nn")
            dpc_g = dpc_ref[:, sl]
            dsc.append(jnp.sum(dpc_g * mixed, axis=0, keepdims=True))
            dmixed = (dpc_g * sc_ref[:, sl]).astype(bf16)
            dpn_o[:, sl] = _dot(dmixed, linb, "nt") / _pool_cnt(i * tp, w, tp, s)
            dl = _dot(pooled, dmixed, "tn")

            @pl.when(i == 0)
            def _(g=g, dl=dl):
                dlin_o[g] = dl

            @pl.when(i > 0)
            def _(g=g, dl=dl):
                dlin_o[g] += dl

        dsc = jnp.concatenate(dsc, axis=1)

        @pl.when(i == 0)
        def _():
            dsc_o[...] = dsc

        @pl.when(i > 0)
        def _():
            dsc_o[...] += dsc

    row = pl.BlockSpec((tp, 512), lambda i: (i, 0))
    linspec = pl.BlockSpec((4, LANES, LANES), lambda i: (0, 0, 0))
    one = pl.BlockSpec((1, 512), lambda i: (0, 0))
    return pl.pallas_call(
        body, name="pool_bwd1", grid=(s // tp,), in_specs=[row, row, linspec, one], out_specs=[row, linspec, one],
        out_shape=[SDS((s, 512), f32), SDS((4, LANES, LANES), f32), SDS((1, 512), f32)],
        compiler_params=_cparams())(dpc, pooled, lin, scale)


def _pool_bwd2(dpn, dz, s):
    tp = 256
    nt = s // tp

    def body(dp, dc, dn, dz_in, dz_o):
        i = pl.program_id(0)
        ext = jnp.concatenate([dp[...], dc[...], dn[...]], axis=0)
        for g, w in enumerate(POOL_WINDOWS):
            sl = slice(g * LANES, (g + 1) * LANES)
            band = _band(i * tp, i * tp - POOL_HALO, w, tp, tp + 2 * POOL_HALO, s, True)
            sm = jnp.dot(band, ext[:, sl], preferred_element_type=f32, precision=lax.Precision.HIGHEST)
            dz_o[:, sl] = (sm - dc[:, sl] * _pool_cnt(i * tp, w, tp, s)).astype(bf16)

    return pl.pallas_call(
        body, name="pool_bwd2", grid=(nt,),
        in_specs=_halo_specs(tp, s, 0) + [pl.BlockSpec(memory_space=pl.ANY)],
        out_specs=pl.BlockSpec((tp, 512), lambda i: (i, C_UC // 512)), out_shape=SDS((s, IN_WIDTH), bf16),
        input_output_aliases={3: 0}, compiler_params=_cparams())(dpn, dpn, dpn, dz)


GW = 512


MIX_TM = 256
BR_WIDTHS = (512, 256, 512)
N_GATE_BLOCKS = 3 * D_MODEL // GW


def _gate_specs(ts):
    return [pl.BlockSpec((ts, GW), lambda i, q=q: (i, C_GZ // GW + q)) for q in range(N_GATE_BLOCKS)]


def _rms(xt, gt):
    return xt * lax.rsqrt(jnp.mean(xt * xt, axis=1, keepdims=True) + EPS) * gt


def _mix_out_fwd(acts, z, b_gate, x, norm_gain, w, s):
    ts = MIX_TM

    def body(oa, ob, pc, *rest):
        gz = rest[:N_GATE_BLOCKS]
        bias, x_ref, gain, wa, wb, wc, wo, y3_o, m_o, x1_o, h2_o, y_s = rest[N_GATE_BLOCKS:]
        for b, (act, wt) in enumerate(((oa, wa), (ob, wb), (pc, wc))):
            a = act[...]
            for j in range(N_CHIPS):
                y_s[b, :, j * 256:(j + 1) * 256] = _dot(a, wt[j], "nn")
        y3_o[...] = y_s[...].astype(bf16)
        for h in range(2):
            hs = slice(h * GW, (h + 1) * GW)
            tot = jnp.zeros((ts, GW), f32)
            for b in range(3):
                g = jax.nn.sigmoid(gz[2 * b + h][...] + bias[:, b * D_MODEL + h * GW:b * D_MODEL + (h + 1) * GW])
                tot += g * y_s[b, :, hs]
            m_o[:, hs] = tot.astype(bf16)
        acc = x_ref[...]
        for j in range(N_CHIPS):
            acc += _dot(m_o[:, j * 256:(j + 1) * 256], wo[j], "nn")
        x1_o[...] = acc
        h2_o[...] = _rms(acc, gain[...]).astype(bf16)

    row = lambda width: pl.BlockSpec((ts, width), lambda i: (i, 0))
    res3 = lambda a: pl.BlockSpec(a.shape, lambda i: (0, 0, 0))
    wts = [w["w_branch_a"], w["w_branch_b"], w["w_branch_c"], w["w_out"]]
    return pl.pallas_call(
        body, name="mix_out_fwd", grid=(s // ts,),
        in_specs=[row(BR_WIDTHS[0]), row(BR_WIDTHS[1]), row(BR_WIDTHS[2])] + _gate_specs(ts)
        + [pl.BlockSpec((1, 3 * D_MODEL), lambda i: (0, 0)), row(D_MODEL), pl.BlockSpec((1, D_MODEL), lambda i: (0, 0))]
        + [res3(a) for a in wts],
        out_specs=[pl.BlockSpec((3, ts, D_MODEL), lambda i: (0, i, 0)), row(D_MODEL), row(D_MODEL), row(D_MODEL)],
        out_shape=[SDS((3, s, D_MODEL), bf16), SDS((s, D_MODEL), bf16), SDS((s, D_MODEL), f32), SDS((s, D_MODEL), bf16)],
        scratch_shapes=[pltpu.VMEM((3, ts, D_MODEL), f32)],
        compiler_params=_cparams())(*acts, *([z] * N_GATE_BLOCKS), b_gate, x, norm_gain, *wts)


def _mix_out_bwd(dx1, y3, z, b_gate, w, s):
    ts = MIX_TM

    def body(dx_ref, y3_ref, *rest):
        gz = rest[:N_GATE_BLOCKS]
        bias, wa, wb, wc, wo, dy_o, dz_o, db_o, doa_o, dob_o, dpc_o, dm_s = rest[N_GATE_BLOCKS:]
        i = pl.program_id(0)
        dx = dx_ref[...].astype(bf16)
        for j in range(N_CHIPS):
            dm_s[:, j * 256:(j + 1) * 256] = _dot(dx, wo[j], "nt")
        dz_o[:, 0:C_GZ] = jnp.zeros((ts, C_GZ), bf16)
        dbs = []
        for b in range(3):
            for h in range(2):
                hs = slice(h * GW, (h + 1) * GW)
                g = jax.nn.sigmoid(gz[2 * b + h][...] + bias[:, b * D_MODEL + h * GW:b * D_MODEL + (h + 1) * GW])
                dm = dm_s[:, hs]
                dy_o[b, :, hs] = (dm * g).astype(bf16)
                dgz = dm * y3_ref[b, :, hs].astype(f32) * g * (1.0 - g)
                c0 = C_GZ + b * D_MODEL + h * GW
                dz_o[:, c0:c0 + GW] = dgz.astype(bf16)
                dbs.append(jnp.sum(dgz, axis=0, keepdims=True))
        db = jnp.concatenate(dbs, axis=1)

        @pl.when(i == 0)
        def _():
            db_o[...] = db

        @pl.when(i > 0)
        def _():
            db_o[...] += db

        for b, (wt, out) in enumerate(((wa, doa_o), (wb, dob_o), (wc, dpc_o))):
            acc = _dot(dy_o[b, :, 0:256], wt[0], "nt")
            for j in range(1, N_CHIPS):
                acc += _dot(dy_o[b, :, j * 256:(j + 1) * 256], wt[j], "nt")
            out[...] = acc

    row = lambda width: pl.BlockSpec((ts, width), lambda i: (i, 0))
    res3 = lambda a: pl.BlockSpec(a.shape, lambda i: (0, 0, 0))
    blk3 = pl.BlockSpec((3, ts, D_MODEL), lambda i: (0, i, 0))
    one = pl.BlockSpec((1, 3 * D_MODEL), lambda i: (0, 0))
    wts = [w["w_branch_a"], w["w_branch_b"], w["w_branch_c"], w["w_out"]]
    return pl.pallas_call(
        body, name="mix_out_bwd", grid=(s // ts,),
        in_specs=[row(D_MODEL), blk3] + _gate_specs(ts) + [one] + [res3(a) for a in wts],
        out_specs=[blk3, row(IN_WIDTH), one, row(BR_WIDTHS[0]), row(BR_WIDTHS[1]), row(BR_WIDTHS[2])],
        out_shape=[SDS((3, s, D_MODEL), bf16), SDS((s, IN_WIDTH), bf16), SDS((1, 3 * D_MODEL), f32),
                   SDS((s, BR_WIDTHS[0]), f32), SDS((s, BR_WIDTHS[1]), f32), SDS((s, BR_WIDTHS[2]), f32)],
        scratch_shapes=[pltpu.VMEM((ts, D_MODEL), f32)],
        compiler_params=_cparams())(dx1, y3, *([z] * N_GATE_BLOCKS), b_gate, *wts)


def _loss_grad(y, tgt, s):
    ts = 256

    def fn(yt, tt):
        e = yt - tt
        return e * (1.0 / D_MODEL), jnp.sum(e * e, axis=0, keepdims=True) * (0.5 / D_MODEL)

    row = pl.BlockSpec((ts, D_MODEL), lambda i: (i, 0))
    one = pl.BlockSpec((1, D_MODEL), lambda i: (0, 0))
    return _ew("loss_grad", fn, (s // ts,), [y, tgt], [row, row], [SDS((s, D_MODEL), f32), SDS((1, D_MODEL), f32)],
               [row, one], n_acc=1)


TM = 512


def _layer_fwd(x, w, sm, tabs, s, xchg=None, hb=None, next_gain=None):
    nm = s // TM
    if hb is None:
        hb = _norm_fwd(x, sm["norm_mix"], s)
    z = _mm("mm_z", [hb, w["w_in"]],
            [pl.BlockSpec((TM, D_MODEL), lambda j, i, k: (i, 0)), pl.BlockSpec((None, D_MODEL, 1664), lambda j, i, k: (j, 0, 0))],
            SDS((s, IN_WIDTH), f32), pl.BlockSpec((TM, 1664), lambda j, i, k: (i, j)), (4, nm, 1), 1, "nn", None)
    qa, kd, vd, qb, kb, vb = _prep_fwd(z, tabs, sm["gains"], s)
    (oa, lse_a), landed = _flash_a_fwd(qa, kd, vd, s, xchg)
    if xchg is not None:
        w = dict(w, **dict(zip(REST + ("next_w_in",), landed)))
    ob_parts = [_attn_b_fwd(qb, kb, vb, gi, s) for gi in range(3)]
    ob, lse_b = _merge_b([p[0] for p in ob_parts], [p[1] for p in ob_parts], s)
    pooled, pc = _pool_fwd(z, sm["pool_lin"], sm["pool_scale"], s)

    y3, merged, x1, h2 = _mix_out_fwd([oa, ob, pc], z, sm["b_gate"], x, sm["norm_ffn"], w, s)
    g4, u4, a4 = _ffn_up(h2, w["w_ffn_gate"], w["w_ffn_up"], s)
    x2 = _mm_shard_sum("mm_down", [(a4, w["w_ffn_down"])], "nn", D_MODEL, s, TM, add=x1, norm_gain=next_gain)
    x2, hb_next = x2 if next_gain is not None else (x2, None)
    saved = dict(x=x, hb=hb, z=z, qa=qa, kd=kd, vd=vd, qb=qb, kb=kb, vb=vb, oa=oa, lse_a=lse_a, ob=ob, lse_b=lse_b,
                 pooled=pooled, pc=pc, y3=y3, merged=merged, x1=x1, h2=h2, g4=g4, u4=u4, a4=a4)
    return x2, saved, w, hb_next


def _ffn_up(h2, wg, wu, s):
    nm = s // TM

    def body(h_ref, wg_ref, wu_ref, g_o, u_o, a_o):
        h = h_ref[...]
        g = _dot(h, wg_ref[...], "nn")
        u = _dot(h, wu_ref[...], "nn")
        g_o[...] = g.astype(bf16)
        u_o[...] = u.astype(bf16)
        a_o[...] = (g * jax.nn.sigmoid(g) * u).astype(bf16)

    wsp = pl.BlockSpec((None, D_MODEL, FF_SHARD), lambda j, i: (j, 0, 0))
    osp = pl.BlockSpec((None, TM, FF_SHARD), lambda j, i: (j, i, 0))
    return pl.pallas_call(
        body, name="ffn_up", grid=(4, nm), in_specs=[pl.BlockSpec((TM, D_MODEL), lambda j, i: (i, 0)), wsp, wsp],
        out_specs=[osp, osp, osp],
        out_shape=[SDS((4, s, FF_SHARD), bf16), SDS((4, s, FF_SHARD), bf16), SDS((4, s, FF_SHARD), bf16)],
        compiler_params=_cparams())(h2, wg, wu)


def _ffn_bwd_act(dx2, wd, g4, u4, s):
    def body(dx_ref, wd_ref, g_ref, u_ref, dg_o, du_o):
        dx = dx_ref[...].astype(bf16)
        for j in range(N_CHIPS):
            da = _dot(dx, wd_ref[j], "nt")
            g, u = g_ref[j].astype(f32), u_ref[j].astype(f32)
            sg = jax.nn.sigmoid(g)
            dg_o[j] = (da * u * sg * (1.0 + g * (1.0 - sg))).astype(bf16)
            du_o[j] = (da * g * sg).astype(bf16)

    osp = pl.BlockSpec((N_CHIPS, TM, FF_SHARD), lambda i: (0, i, 0))
    return pl.pallas_call(
        body, name="ffn_bwd_act", grid=(s // TM,),
        in_specs=[pl.BlockSpec((TM, D_MODEL), lambda i: (i, 0)), pl.BlockSpec((N_CHIPS, FF_SHARD, D_MODEL), lambda i: (0, 0, 0)),
                  osp, osp],
        out_specs=[osp, osp], out_shape=[SDS((4, s, FF_SHARD), bf16), SDS((4, s, FF_SHARD), bf16)],
        compiler_params=_cparams())(dx2, wd, g4, u4)


def _mm_shard_sum(name, pairs, dims, out_width, s, tm, add=None, norm_gain=None):
    n = len(pairs)

    def body(*refs):
        a_refs, w_refs = refs[:n], refs[n:2 * n]
        o_ref = refs[-2] if norm_gain is not None else refs[-1]
        acc = refs[2 * n][...] if add is not None else None
        for a_ref, w_ref in zip(a_refs, w_refs):
            for j in range(N_CHIPS):
                if len(a_ref.shape) == 3:
                    a = a_ref[j]
                else:
                    c = a_ref.shape[1] // N_CHIPS
                    a = a_ref[:, j * c:(j + 1) * c]
                part = _dot(a.astype(bf16), w_ref[j], dims)
                acc = part if acc is None else acc + part
        o_ref[...] = acc
        if norm_gain is not None:
            refs[-1][...] = _rms(acc, refs[2 * n + (add is not None)][...]).astype(bf16)

    a_specs = [pl.BlockSpec((N_CHIPS, tm, a.shape[2]), lambda i: (0, i, 0)) if a.ndim == 3
               else pl.BlockSpec((tm, a.shape[1]), lambda i: (i, 0)) for a, _ in pairs]
    w_specs = [pl.BlockSpec(wt.shape, lambda i: (0, 0, 0)) for _, wt in pairs]
    row = pl.BlockSpec((tm, out_width), lambda i: (i, 0))
    extra, extra_specs = ([add], [row]) if add is not None else ([], [])
    if norm_gain is not None:
        extra, extra_specs = extra + [norm_gain], extra_specs + [pl.BlockSpec((1, out_width), lambda i: (0, 0))]
    normed = norm_gain is not None
    return pl.pallas_call(
        body, name=name + ("_norm" if normed else ""), grid=(s // tm,), in_specs=a_specs + w_specs + extra_specs,
        out_specs=[row, row] if normed else row,
        out_shape=[SDS((s, out_width), f32), SDS((s, out_width), bf16)] if normed else SDS((s, out_width), f32),
        compiler_params=_cparams())(*[a for a, _ in pairs], *[wt for _, wt in pairs], *extra)


GRAD_WIRE_DTYPE = bf16


WG_TK = 2048


def _wgrad(name, a, a_spec, b, b_spec, out_shape, out_block, s):
    nk = s // min(WG_TK, s)
    return _mm(name, [a, b], [a_spec, b_spec], SDS(out_shape, GRAD_WIRE_DTYPE),
               pl.BlockSpec((None,) + out_block, lambda j, k: (j, 0, 0)), (4, nk), nk, "tn", out_block)


def _layer_bwd(dx2, sv, w, sm, tabs, s, pending_mixer=None):
    tk = min(WG_TK, s)
    tok = lambda width: pl.BlockSpec((tk, width), lambda j, k: (k, 0))
    g_wd = _wgrad("wg_down", sv["a4"], pl.BlockSpec((None, tk, FF_SHARD), lambda j, k: (j, k, 0)), dx2, tok(D_MODEL),
                  (4, FF_SHARD, D_MODEL), (FF_SHARD, D_MODEL), s)
    dg4, du4 = _ffn_bwd_act(dx2, w["w_ffn_down"], sv["g4"], sv["u4"], s)
    sh704 = pl.BlockSpec((None, tk, FF_SHARD), lambda j, k: (j, k, 0))
    g_wg = _wgrad("wg_gate", sv["h2"], tok(D_MODEL), dg4, sh704, (4, D_MODEL, FF_SHARD), (D_MODEL, FF_SHARD), s)
    g_wu = _wgrad("wg_up", sv["h2"], tok(D_MODEL), du4, sh704, (4, D_MODEL, FF_SHARD), (D_MODEL, FF_SHARD), s)
    dh2 = _mm_shard_sum("ffn_bwd_dh", [(dg4, w["w_ffn_gate"]), (du4, w["w_ffn_up"])], "nt", D_MODEL, s, TM)
    dx1, g_norm_ffn = _norm_bwd(sv["x1"], sm["norm_ffn"], dh2, dx2, s)
    colblk = lambda width: pl.BlockSpec((tk, width), lambda j, k: (k, j))
    g_wo = _wgrad("wg_out", sv["merged"], colblk(256), dx1, tok(D_MODEL), (4, 256, D_MODEL), (256, D_MODEL), s)
    dy3, dz, g_bgate, doa, dob, dpc = _mix_out_bwd(dx1, sv["y3"], sv["z"], sm["b_gate"], w, s)
    br_grads = []
    for b, (nm_, act, kdim) in enumerate((("a", sv["oa"], 512), ("b", sv["ob"], 256), ("c", sv["pc"], 512))):
        br_grads.append(_wgrad("wg_br" + nm_, act, tok(kdim), dy3,
                               pl.BlockSpec((None, tk, 256), lambda j, k, b=b: (b, k, j)), (4, kdim, 256), (kdim, 256), s))
    dpn, g_lin, g_scale = _pool_bwd1(dpc, sv["pooled"], sm["pool_lin"], sm["pool_scale"], s)
    dz = _pool_bwd2(dpn, dz, s)
    delta = _delta_b(dob, sv["ob"], s)
    dqb, dkb, dvb = [], [], []
    for gi in range(3):
        dqb.append(_attn_b_dq(sv["qb"], sv["kb"], sv["vb"], dob, sv["lse_b"], delta, gi, s))
        dk, dv = _attn_b_dkv(sv["qb"], sv["kb"], sv["vb"], dob, sv["lse_b"], delta, gi, s)
        dkb.append(dk)
        dvb.append(dv)
    if pending_mixer is not None:
        xchg = (list(pending_mixer) + [g_wg, g_wu, g_wd], lambda ref, peer: ref.at[peer])
    else:
        xchg = None
    (dqa, dkd, dvd), landed = _flash_a_bwd(sv["qa"], sv["kd"], sv["vd"], sv["oa"], sv["lse_a"], doa, s, xchg)
    dz, g_gains = _prep_bwd(sv["z"], tabs, sm["gains"], dqa, dkd, dvd, dqb, dkb, dvb, dz, s)
    g_win = _wgrad("wg_in", sv["hb"], tok(D_MODEL), dz, colblk(1664), (4, D_MODEL, 1664), (D_MODEL, 1664), s)
    dh = _mm_shard_sum("mm_dh", [(dz, w["w_in"])], "nt", D_MODEL, s, 256)
    dx0, g_norm_mix = _norm_bwd(sv["x"], sm["norm_mix"], dh, dx1, s)
    big = dict(w_in=g_win, w_branch_a=br_grads[0], w_branch_b=br_grads[1], w_branch_c=br_grads[2], w_out=g_wo,
               w_ffn_gate=g_wg, w_ffn_up=g_wu, w_ffn_down=g_wd)
    gg = g_gains[0:4, :HEAD_DIM] + g_gains[0:4, HEAD_DIM:]
    small = jnp.concatenate([g_norm_mix.reshape(-1), g_bgate.reshape(-1), gg.reshape(-1), g_lin.reshape(-1),
                             g_scale.reshape(-1), g_norm_ffn.reshape(-1)]).reshape(SMALL_ROWS, LANES)
    return dx0, big, small, landed


def _mesh_pos():
    return lax.axis_index("x"), lax.axis_index("y"), lax.axis_index("c")


def _chip_copies(src, dst, pick_src, send_sems, recv_sems, loc_sems, receiving=True):
    n = len(src)
    x, y, c = _mesh_pos()
    me = 2 * x + y
    local = [pltpu.make_async_copy(pick_src(src[t], me), dst[t].at[me], loc_sems.at[t]) for t in range(n)]
    push, recv = [], []
    for j, (px, py) in enumerate([(1 - x, y), (x, 1 - y), (1 - x, 1 - y)]):
        peer = 2 * px + py
        for t in range(n):
            sems = dict(send_sem=send_sems.at[j * n + t], recv_sem=recv_sems.at[j * n + t],
                        device_id=(px, py, c), device_id_type=MESH)
            push.append(pltpu.make_async_remote_copy(src_ref=pick_src(src[t], peer), dst_ref=dst[t].at[me], **sems))
            if receiving:
                recv.append(pltpu.make_async_remote_copy(src_ref=pick_src(src[t], peer), dst_ref=dst[t].at[peer], **sems))
    return local, push, recv


def _chip_start(copies):
    local, push, _ = copies
    for cp in local + push:
        cp.start()


def _chip_wait(copies):
    local, push, recv = copies
    for cp in push:
        cp.wait_send()
    for cp in recv:
        cp.wait_recv()
    for cp in local:
        cp.wait()


def _chip_sems(n):
    return [pltpu.SemaphoreType.DMA((3 * n,)), pltpu.SemaphoreType.DMA((3 * n,)), pltpu.SemaphoreType.DMA((n,))]


def _chip_landing(srcs):
    return [SDS((N_CHIPS,) + tuple(a.shape[1:]), a.dtype) for a in srcs]


def _chip_exchange(name, srcs, pick_src):
    n = len(srcs)
    any_spec = pl.BlockSpec(memory_space=pl.ANY)

    def body(*refs):
        copies = _chip_copies(refs[:n], refs[n:2 * n], pick_src, *refs[2 * n:])
        _chip_start(copies)
        _chip_wait(copies)

    return pl.pallas_call(
        body, name=name, in_specs=[any_spec] * n, out_specs=[any_spec] * n, out_shape=_chip_landing(srcs),
        scratch_shapes=_chip_sems(n), compiler_params=pltpu.CompilerParams())(*srcs)


def _sibling_exchange(name, srcs):
    n = len(srcs)
    any_spec = pl.BlockSpec(memory_space=pl.ANY)

    def body(*refs):
        src, dst = refs[:n], refs[n:2 * n]
        send_sems, recv_sems = refs[2 * n:]
        x, y, c = _mesh_pos()
        cps = [pltpu.make_async_remote_copy(src_ref=src[t], dst_ref=dst[t], send_sem=send_sems.at[t], recv_sem=recv_sems.at[t],
                                            device_id=(x, y, 1 - c), device_id_type=MESH) for t in range(n)]
        for cp in cps:
            cp.start()
        for cp in cps:
            cp.wait()

    return pl.pallas_call(
        body, name=name, in_specs=[any_spec] * n, out_specs=[any_spec] * n,
        out_shape=[SDS(a.shape, a.dtype) for a in srcs],
        scratch_shapes=[pltpu.SemaphoreType.DMA((n,)), pltpu.SemaphoreType.DMA((n,))],
        compiler_params=pltpu.CompilerParams())(*srcs)


def _all_exchange(name, src):
    any_spec = pl.BlockSpec(memory_space=pl.ANY)

    def body(src_ref, dst_ref, send_sems, recv_sems, loc_sem):
        x, y, c = _mesh_pos()
        me = 4 * x + 2 * y + c
        peers = [(x ^ ((k >> 2) & 1), y ^ ((k >> 1) & 1), c ^ (k & 1)) for k in range(1, N_DEV)]
        local = pltpu.make_async_copy(src_ref, dst_ref.at[me], loc_sem)
        local.start()
        cps = [pltpu.make_async_remote_copy(src_ref=src_ref, dst_ref=dst_ref.at[me], send_sem=send_sems.at[k],
                                            recv_sem=recv_sems.at[k], device_id=p, device_id_type=MESH)
               for k, p in enumerate(peers)]
        for cp in cps:
            cp.start()
        for cp in cps:
            cp.wait_send()
        for k, (px, py, pc) in enumerate(peers):
            pltpu.make_async_remote_copy(src_ref=src_ref, dst_ref=dst_ref.at[4 * px + 2 * py + pc], send_sem=send_sems.at[k],
                                         recv_sem=recv_sems.at[k], device_id=(px, py, pc), device_id_type=MESH).wait_recv()
        local.wait()

    return pl.pallas_call(
        body, name=name, in_specs=[any_spec], out_specs=any_spec, out_shape=SDS((N_DEV,) + src.shape, src.dtype),
        scratch_shapes=[pltpu.SemaphoreType.DMA((N_DEV - 1,)), pltpu.SemaphoreType.DMA((N_DEV - 1,)), pltpu.SemaphoreType.DMA],
        compiler_params=pltpu.CompilerParams())(src)


def _cast_bf16(name, a):
    l, r, c = a.shape
    tr = _row_tile(r, c)
    spec = pl.BlockSpec((None, tr, c), lambda i, j: (i, j, 0))
    return _ew(name, lambda t: t, (l, r // tr), [a], [spec], [SDS(a.shape, bf16)], [spec])[0]


def _sum4(name, land):
    _, r, c = land.shape
    tr = _row_tile(r, c)
    up = lambda t: t.astype(f32)
    return _ew(name, lambda a, b, cc, d: ((up(a) + up(b)) + up(cc)) + up(d), (r // tr,), [land] * 4,
               [pl.BlockSpec((None, tr, c), lambda i, k=k: (k, i, 0)) for k in range(4)],
               [SDS((r, c), f32)], [pl.BlockSpec((tr, c), lambda i: (i, 0))])[0]


def _adam_math(g, w, m, v):
    m2 = ADAM_B1 * m + (1.0 - ADAM_B1) * g
    v2 = ADAM_B2 * v + (1.0 - ADAM_B2) * (g * g)
    m_hat = m2 / (1.0 - ADAM_B1 ** ADAM_STEP)
    v_hat = v2 / (1.0 - ADAM_B2 ** ADAM_STEP)
    delta = -ADAM_LR * (m_hat / (jnp.sqrt(v_hat) + ADAM_EPS) + ADAM_WD * w)
    return delta, m2, v2


def _adamw_big(name, p_own, p_sib, w, m, v):
    l, r, c = w.shape
    tr = _row_tile(r, c)

    def fn(a, b, wt, mt, vt):
        g = a + b
        return (g,) + _adam_math(g, wt, mt, vt)

    spec = pl.BlockSpec((None, tr, c), lambda i, j: (i, j, 0))
    return _ew(name, fn, (l, r // tr), [p_own, p_sib, w, m, v], [spec] * 5, [SDS(w.shape, f32)] * 4, [spec] * 4)


def _adamw_small(land, w, m, v):
    r = w.shape[0]
    tr = _row_tile(r, LANES)

    def fn(*t):
        g = t[0]
        for k in range(1, N_DEV):
            g = g + t[k]
        return (g,) + _adam_math(g, *t[N_DEV:])

    row = pl.BlockSpec((tr, LANES), lambda i: (i, 0))
    return _ew("adamw_small", fn, (r // tr,), [land] * N_DEV + [w, m, v],
               [pl.BlockSpec((None, tr, LANES), lambda i, k=k: (k, i, 0)) for k in range(N_DEV)] + [row] * 3,
               [SDS(w.shape, f32)] * 4, [row] * 4)


def _pack_small(d):
    return jnp.concatenate([d[k].reshape(DEPTH, -1) for k in SMALL], axis=1).reshape(DEPTH * SMALL_ROWS, LANES)


def _unpack_small(a, shapes):
    a = a.reshape(DEPTH, SMALL_ROWS * LANES)
    out, off = {}, 0
    for k, n in zip(SMALL, SMALL_SIZES):
        out[k] = a[:, off:off + n].reshape(shapes[k])
        off += n
    return out


def kernel(x, norm_mix, w_in, b_gate, qn_a, kn_a, qn_b, kn_b, pool_lin, pool_scale, w_branch_a, w_branch_b, w_branch_c, w_out, norm_ffn, w_ffn_gate, w_ffn_up, w_ffn_down, loss_target, m_norm_mix, m_w_in, m_b_gate, m_qn_a, m_kn_a, m_qn_b, m_kn_b, m_pool_lin, m_pool_scale, m_w_branch_a, m_w_branch_b, m_w_branch_c, m_w_out, m_norm_ffn, m_w_ffn_gate, m_w_ffn_up, m_w_ffn_down, v_norm_mix, v_w_in, v_b_gate, v_qn_a, v_kn_a, v_qn_b, v_kn_b, v_pool_lin, v_pool_scale, v_w_branch_a, v_w_branch_b, v_w_branch_c, v_w_out, v_norm_ffn, v_w_ffn_gate, v_w_ffn_up, v_w_ffn_down):
    args = dict(locals())
    s = x.shape[1]
    wts = {k: args[k] for k in WEIGHT_ORDER}
    mom = {k: args["m_" + k] for k in WEIGHT_ORDER}
    var = {k: args["v_" + k] for k in WEIGHT_ORDER}
    tabs = _rope_tables(s)

    shards16 = {k: _cast_bf16("cast_" + k, wts[k]) for k in BIG}
    whole = lambda ref, peer: ref.at[0]
    w_in_l = _chip_exchange("gather_w_in", [shards16["w_in"][0:1]], whole)[0]
    full = [None] * DEPTH

    def small_of(l):
        tile2 = lambda a: jnp.concatenate([a, a])
        gains = jnp.stack([tile2(qn_a[l]), tile2(kn_a[l]), tile2(qn_b[l]), tile2(kn_b[l])])
        return dict(norm_mix=norm_mix[l][None], norm_ffn=norm_ffn[l][None], b_gate=b_gate[l][None], gains=gains,
                    pool_lin=pool_lin[l], pool_scale=pool_scale[l][None])

    xs = x[0]
    saved = []
    hb = None
    for l in range(DEPTH):
        nxt = (l + 1) % DEPTH
        srcs = [shards16[k][l:l + 1] for k in REST] + [shards16["w_in"][nxt:nxt + 1]]
        next_gain = norm_mix[l + 1][None] if l + 1 < DEPTH else None
        xs, sv, full[l], hb = _layer_fwd(xs, dict(w_in=w_in_l), small_of(l), tabs, s, (srcs, whole), hb, next_gain)
        saved.append(sv)
        w_in_l = full[l]["next_w_in"]
    dy, loss_rows = _loss_grad(xs, loss_target[0], s)
    loss = lax.psum(jnp.sum(loss_rows), ("x", "y", "c"))

    part = {k: [None] * DEPTH for k in BIG}
    small_g = [None] * DEPTH
    to_chip = lambda ref, peer: ref.at[peer]
    assert GRAD_WIRE_DTYPE == bf16
    pending = [full[DEPTH - 1][k] for k in MIXER]
    for l in reversed(range(DEPTH)):
        dy, big, small_g[l], landed = _layer_bwd(dy, saved[l], full[l], small_of(l), tabs, s, pending)
        if l + 1 < DEPTH:
            for k, a in zip(MIXER, landed[:len(MIXER)]):
                part[k][l + 1] = _sum4("sum4_" + k, a)
        for k, a in zip(FFN, landed[len(MIXER):]):
            part[k][l] = _sum4("sum4_" + k, a)
        pending = [big[k] for k in MIXER]
    for k, a in zip(MIXER, _chip_exchange("grad_scatter", pending, to_chip)):
        part[k][0] = _sum4("sum4_" + k, a)

    p_own = [jnp.stack(part[k]) for k in BIG]
    p_sib = _sibling_exchange("grad_sibling", p_own)
    outs = {}
    for k, a, b in zip(BIG, p_own, p_sib):
        outs[k] = _adamw_big("adamw_" + k, a, b, wts[k], mom[k], var[k])

    land_s = _all_exchange("small_allgather", jnp.concatenate(small_g, axis=0))
    res_s = _adamw_small(land_s, _pack_small(wts), _pack_small(mom), _pack_small(var))
    shapes = {k: wts[k].shape for k in SMALL}
    small_out = [_unpack_small(r, shapes) for r in res_s]
    for k in SMALL:
        outs[k] = tuple(so[k] for so in small_out)

    flat = [loss, dy[None]]
    for idx in range(4):
        flat += [outs[k][idx] for k in WEIGHT_ORDER]
    return tuple(flat)
```

```python
import functools

import jax
import jax.numpy as jnp
from jax import lax
from jax.experimental import pallas as pl
from jax.experimental.pallas import tpu as pltpu

f32, bf16 = jnp.float32, jnp.bfloat16
SDS = jax.ShapeDtypeStruct

DEPTH = 4
D_MODEL = 1024
HEAD_DIM = 64
LANES = 128
GRID_W = 64
ROPE_THETA = 10000.0
EPS = 1e-6
NEG_INF = -1e30
SCALE = HEAD_DIM ** -0.5
B_GROUP_CFG = ((128, 1), (512, 4), (2048, 16))
POOL_WINDOWS = (2, 4, 8, 16)
N_CHIPS = 4
N_DEV = 8
C_QA, C_KA, C_VA, C_QB, C_KB, C_VB, C_UC, C_GZ = 0, 512, 640, 768, 1536, 2304, 3072, 3584
IN_WIDTH = 6656
FF_SHARD = 704
ADAM_LR, ADAM_B1, ADAM_B2, ADAM_EPS, ADAM_WD, ADAM_STEP = 0.001, 0.9, 0.999, 1e-08, 0.01, 10
VMEM_LIMIT = 56 * 1024 * 1024
MESH = pl.DeviceIdType.MESH

MIXER = ("w_in", "w_branch_a", "w_branch_b", "w_branch_c", "w_out")
FFN = ("w_ffn_gate", "w_ffn_up", "w_ffn_down")
BIG = MIXER + FFN
REST = BIG[1:]
SMALL = ("norm_mix", "b_gate", "qn_a", "kn_a", "qn_b", "kn_b", "pool_lin", "pool_scale", "norm_ffn")
SMALL_SIZES = (1024, 3072, 64, 64, 64, 64, 65536, 512, 1024)
SMALL_ROWS = sum(SMALL_SIZES) // LANES
WEIGHT_ORDER = ("norm_mix", "w_in", "b_gate", "qn_a", "kn_a", "qn_b", "kn_b", "pool_lin", "pool_scale",
                "w_branch_a", "w_branch_b", "w_branch_c", "w_out", "norm_ffn", "w_ffn_gate", "w_ffn_up", "w_ffn_down")


def _cparams():
    return pltpu.CompilerParams(vmem_limit_bytes=VMEM_LIMIT)


def _lo_mask(shape):
    return lax.broadcasted_iota(jnp.int32, shape, len(shape) - 1) < HEAD_DIM


def _dot(a, b, dims):
    dn = {"nn": (((1,), (0,)), ((), ())), "nt": (((1,), (1,)), ((), ())), "tn": (((0,), (0,)), ((), ()))}[dims]
    return lax.dot_general(a, b, dn, preferred_element_type=f32)


def _mm(name, ins, in_specs, out_shape, out_spec, grid, nk, dims, acc_shape, epilogue=None, n_extra=0, aliases=None):
    multi = isinstance(out_shape, (list, tuple))
    n_out = len(out_shape) if multi else 1

    def body(*refs):
        a_ref, b_ref = refs[0], refs[1]
        extra = refs[2:2 + n_extra]
        outs = refs[2 + n_extra:2 + n_extra + n_out]
        k = pl.program_id(len(grid) - 1)
        part = _dot(a_ref[...].astype(bf16), b_ref[...].astype(bf16), dims)

        def finish(acc):
            res = epilogue(acc, *[e[...] for e in extra]) if epilogue is not None else acc
            res = res if isinstance(res, (list, tuple)) else (res,)
            for o, r in zip(outs, res):
                o[...] = r.astype(o.dtype)

        if nk == 1:
            finish(part)
        else:
            acc_ref = refs[-1]

            @pl.when(k == 0)
            def _():
                acc_ref[...] = part

            @pl.when(k > 0)
            def _():
                acc_ref[...] += part

            @pl.when(k == nk - 1)
            def _():
                finish(acc_ref[...])

    return pl.pallas_call(
        body, name=name, grid=grid, in_specs=in_specs,
        out_specs=list(out_spec) if multi else out_spec,
        out_shape=list(out_shape) if multi else out_shape,
        scratch_shapes=[] if nk == 1 else [pltpu.VMEM(acc_shape, f32)],
        input_output_aliases=aliases or {}, compiler_params=_cparams())(*ins)


def _ew(name, fn, grid, ins, in_specs, out_shapes, out_specs, n_acc=0, aliases=None):
    n_in = len(ins)
    n_out = len(out_shapes) - n_acc
    n_alias = len(aliases or {})

    def body(*refs):
        in_refs = refs[:n_in - n_alias] if n_alias else refs[:n_in]
        out_refs = refs[n_in:n_in + n_out]
        acc_refs = refs[n_in + n_out:]
        i = pl.program_id(len(grid) - 1)
        res = fn(*[r[...] for r in in_refs])
        res = res if isinstance(res, (list, tuple)) else (res,)
        for o, r in zip(out_refs, res[:n_out]):
            o[...] = r.astype(o.dtype)
        for a, r in zip(acc_refs, res[n_out:]):
            @pl.when(i == 0)
            def _(a=a, r=r):
                a[...] = r.astype(a.dtype)

            @pl.when(i > 0)
            def _(a=a, r=r):
                a[...] += r.astype(a.dtype)

    return pl.pallas_call(body, name=name, grid=grid, in_specs=in_specs, out_specs=out_specs, out_shape=out_shapes,
                          input_output_aliases=aliases or {}, compiler_params=_cparams())(*ins)


EW_BLOCK_ELEMS = 128 * 1024


def _row_tile(rows, cols):
    step = 16 if rows % 16 == 0 else 8
    return max(t for t in range(step, rows + 1, step) if rows % t == 0 and (t * cols <= EW_BLOCK_ELEMS or t == step))


def _rope_tables(s):
    t = jnp.arange(s)
    inv_ax = ROPE_THETA ** (-jnp.arange(0, HEAD_DIM // 2, 2, dtype=f32) / (HEAD_DIM // 2))
    inv_sq = ROPE_THETA ** (-jnp.arange(0, HEAD_DIM, 2, dtype=f32) / HEAD_DIM)
    ang_row = (t // GRID_W).astype(f32)[:, None] * inv_ax[None, :]
    ang_col = (t % GRID_W).astype(f32)[:, None] * inv_ax[None, :]
    ang_seq = t.astype(f32)[:, None] * inv_sq[None, :]
    a_ax = jnp.concatenate([ang_row, ang_row, ang_col, ang_col], axis=1)
    sg_ax = jnp.concatenate([-jnp.ones(16), jnp.ones(16), -jnp.ones(16), jnp.ones(16)]).astype(f32)
    a_sq = jnp.concatenate([ang_seq, ang_seq], axis=1)
    sg_sq = jnp.concatenate([-jnp.ones(32), jnp.ones(32)]).astype(f32)
    two = lambda a: jnp.concatenate([a, a], axis=1)
    return (two(jnp.cos(a_ax)), two(jnp.sin(a_ax) * sg_ax), two(jnp.cos(a_sq)), two(jnp.sin(a_sq) * sg_sq))


def _head_stat(v, lo):
    r = lax.broadcasted_iota(jnp.int32, (LANES, LANES), 0) < HEAD_DIM
    c = lax.broadcasted_iota(jnp.int32, (LANES, LANES), 1) < HEAD_DIM
    same_head = (r == c).astype(f32)
    return jnp.dot(v, same_head, preferred_element_type=f32, precision=lax.Precision.HIGHEST)


def _partner(y, off):
    lane = lax.broadcasted_iota(jnp.int32, y.shape, 1)
    first = (lane & (2 * off - 1)) < off
    return jnp.where(first, pltpu.roll(y, LANES - off, 1), pltpu.roll(y, off, 1))


def _normrope(xc, g, cos, sin, off):
    lo = _lo_mask(xc.shape)
    r = lax.rsqrt(_head_stat(xc * xc, lo) * (1.0 / HEAD_DIM) + EPS)
    y = xc * r * g
    return y * cos + _partner(y, off) * sin


def _normrope_bwd(xc, g, cos, sin, off, drot):
    lo = _lo_mask(xc.shape)
    r = lax.rsqrt(_head_stat(xc * xc, lo) * (1.0 / HEAD_DIM) + EPS)
    n = xc * r
    dy = drot * cos + _partner(drot * sin, off)
    dg = jnp.sum(dy * n, axis=0, keepdims=True)
    dn = dy * g
    dx = r * (dn - n * (_head_stat(dn * n, lo) * (1.0 / HEAD_DIM)))
    return dx, dg


def _norm_fwd(x, g, s):
    ts = 256

    def fn(xt, gt):
        r = lax.rsqrt(jnp.mean(xt * xt, axis=1, keepdims=True) + EPS)
        return xt * r * gt

    return _ew("norm_fwd", fn, (s // ts,), [x, g],
               [pl.BlockSpec((ts, D_MODEL), lambda i: (i, 0)), pl.BlockSpec((1, D_MODEL), lambda i: (0, 0))],
               [SDS((s, D_MODEL), bf16)], [pl.BlockSpec((ts, D_MODEL), lambda i: (i, 0))])[0]


def _norm_bwd(x, g, dh, dres, s):
    ts = 256

    def fn(xt, gt, dht, drt):
        r = lax.rsqrt(jnp.mean(xt * xt, axis=1, keepdims=True) + EPS)
        n = xt * r
        dn = dht * gt
        dx = drt + r * (dn - n * jnp.mean(dn * n, axis=1, keepdims=True))
        return dx, jnp.sum(dht * n, axis=0, keepdims=True)

    row = pl.BlockSpec((ts, D_MODEL), lambda i: (i, 0))
    one = pl.BlockSpec((1, D_MODEL), lambda i: (0, 0))
    return _ew("norm_bwd", fn, (s // ts,), [x, g, dh, dres], [row, one, row, row],
               [SDS((s, D_MODEL), f32), SDS((1, D_MODEL), f32)], [row, one], n_acc=1)


def _prep_fwd(z, tabs, gains, s):
    ts = 256
    cos_a, sin_a, cos_b, sin_b = tabs

    def body(za_ref, zq_ref, zk_ref, zv_ref, ca, sa, cb, sb, g_ref, qa_o, kd_o, vd_o, *grp_o):
        lo = _lo_mask((ts, LANES))
        g = g_ref[...]
        ca_, sa_, cb_, sb_ = ca[...], sa[...], cb[...], sb[...]
        for c in range(4):
            qa_o[:, c * LANES:(c + 1) * LANES] = _normrope(za_ref[:, c * LANES:(c + 1) * LANES], g[0:1], ca_, sa_, 16).astype(bf16)
        k = _normrope(za_ref[:, C_KA:C_KA + LANES], g[1:2], ca_, sa_, 16)
        kr = pltpu.roll(k, HEAD_DIM, 1)
        kd_o[0] = k.astype(bf16)
        kd_o[1] = kr.astype(bf16)
        v = za_ref[:, C_VA:C_VA + LANES]
        ones_col = (lax.broadcasted_iota(jnp.int32, v.shape, 1) == HEAD_DIM).astype(f32)
        vd_o[0] = jnp.where(lo, v, ones_col).astype(bf16)
        vd_o[1] = jnp.where(lo, pltpu.roll(v, HEAD_DIM, 1), ones_col).astype(bf16)
        for c in range(6):
            sl = slice(c * LANES, (c + 1) * LANES)
            gi, gsl = c // 2, slice((c % 2) * LANES, (c % 2 + 1) * LANES)
            grp_o[gi][:, gsl] = _normrope(zq_ref[:, sl], g[2:3], cb_, sb_, 32).astype(bf16)
            grp_o[3 + gi][:, gsl] = _normrope(zk_ref[:, sl], g[3:4], cb_, sb_, 32).astype(bf16)
            grp_o[6 + gi][:, gsl] = zv_ref[:, sl].astype(bf16)

    w = 768
    grp = pl.BlockSpec((ts, 256), lambda i: (i, 0))
    zspec = lambda cb: pl.BlockSpec((ts, w), lambda i: (i, cb))
    tab = pl.BlockSpec((ts, LANES), lambda i: (i, 0))
    dup = pl.BlockSpec((2, ts, LANES), lambda i: (0, i, 0))
    res = pl.pallas_call(
        body, name="prep_fwd", grid=(s // ts,),
        in_specs=[zspec(0), zspec(1), zspec(2), zspec(3), tab, tab, tab, tab, pl.BlockSpec((4, LANES), lambda i: (0, 0))],
        out_specs=[pl.BlockSpec((ts, 512), lambda i: (i, 0)), dup, dup] + [grp] * 9,
        out_shape=[SDS((s, 512), bf16), SDS((2, s, LANES), bf16), SDS((2, s, LANES), bf16)] + [SDS((s, 256), bf16)] * 9,
        compiler_params=_cparams())(z, z, z, z, cos_a, sin_a, cos_b, sin_b, gains)
    return res[0], res[1], res[2], res[3:6], res[6:9], res[9:12]


def _prep_bwd(z, tabs, gains, dqa, dkd, dvd, dqb, dkb, dvb, dz, s):
    ts = 256
    cos_a, sin_a, cos_b, sin_b = tabs

    def body(za_ref, zq_ref, zk_ref, ca, sa, cb, sb, g_ref, dqa_r, dkd_r, dvd_r,
             dq0, dq1, dq2, dk0, dk1, dk2, dv0, dv1, dv2, dz_in, dz_o, dg_o):
        i = pl.program_id(0)
        lo = _lo_mask((ts, LANES))
        g = g_ref[...]
        ca_, sa_, cb_, sb_ = ca[...], sa[...], cb[...], sb[...]
        dg = [jnp.zeros((1, LANES), f32) for _ in range(4)]
        for c in range(4):
            sl = slice(c * LANES, (c + 1) * LANES)
            dx, d = _normrope_bwd(za_ref[:, sl], g[0:1], ca_, sa_, 16, dqa_r[:, sl])
            dz_o[:, sl] = dx.astype(bf16)
            dg[0] += d
        dk = jnp.where(lo, dkd_r[0], pltpu.roll(dkd_r[1], HEAD_DIM, 1))
        dx, d = _normrope_bwd(za_ref[:, C_KA:C_KA + LANES], g[1:2], ca_, sa_, 16, dk)
        dz_o[:, C_KA:C_KA + LANES] = dx.astype(bf16)
        dg[1] += d
        dz_o[:, C_VA:C_VA + LANES] = jnp.where(lo, dvd_r[0], pltpu.roll(dvd_r[1], HEAD_DIM, 1)).astype(bf16)
        dqs, dks, dvs = (dq0, dq1, dq2), (dk0, dk1, dk2), (dv0, dv1, dv2)
        for c in range(6):
            sl = slice(c * LANES, (c + 1) * LANES)
            gsl = slice((c % 2) * LANES, (c % 2 + 1) * LANES)
            dx, d = _normrope_bwd(zq_ref[:, sl], g[2:3], cb_, sb_, 32, dqs[c // 2][:, gsl])
            dz_o[:, C_QB + c * LANES:C_QB + (c + 1) * LANES] = dx.astype(bf16)
            dg[2] += d
            dx, d = _normrope_bwd(zk_ref[:, sl], g[3:4], cb_, sb_, 32, dks[c // 2][:, gsl])
            dz_o[:, C_KB + c * LANES:C_KB + (c + 1) * LANES] = dx.astype(bf16)
            dg[3] += d
            dz_o[:, C_VB + c * LANES:C_VB + (c + 1) * LANES] = dvs[c // 2][:, gsl].astype(bf16)
        dgs = jnp.concatenate(dg + [jnp.zeros((4, LANES), f32)], axis=0)

        @pl.when(i == 0)
        def _():
            dg_o[...] = dgs

        @pl.when(i > 0)
        def _():
            dg_o[...] += dgs

    w = 768
    zspec = lambda cb: pl.BlockSpec((ts, w), lambda i: (i, cb))
    tab = pl.BlockSpec((ts, LANES), lambda i: (i, 0))
    dup = pl.BlockSpec((2, ts, LANES), lambda i: (0, i, 0))
    grp = pl.BlockSpec((ts, 256), lambda i: (i, 0))
    dzo, dgo = pl.pallas_call(
        body, name="prep_bwd", grid=(s // ts,),
        in_specs=[zspec(0), zspec(1), zspec(2), tab, tab, tab, tab, pl.BlockSpec((4, LANES), lambda i: (0, 0)),
                  pl.BlockSpec((ts, 512), lambda i: (i, 0)), dup, dup] + [grp] * 9 + [pl.BlockSpec(memory_space=pl.ANY)],
        out_specs=[pl.BlockSpec((ts, C_UC), lambda i: (i, 0)), pl.BlockSpec((8, LANES), lambda i: (0, 0))],
        out_shape=[SDS((s, IN_WIDTH), bf16), SDS((8, LANES), f32)],
        input_output_aliases={20: 0}, compiler_params=_cparams())(
            z, z, z, cos_a, sin_a, cos_b, sin_b, gains, dqa, dkd, dvd, *dqb, *dkb, *dvb, dz)
    return dzo, dgo


def _stack_heads(x, lo, dtype):
    z = jnp.zeros_like(x)
    return jnp.concatenate([jnp.where(lo, x, z), jnp.where(lo, z, x)], axis=0).astype(dtype)


def _rows_of(v, lo, kind):
    if kind == "max":
        a = jnp.max(jnp.where(lo, v, NEG_INF * 10), axis=1, keepdims=True)
        b = jnp.max(jnp.where(lo, NEG_INF * 10, v), axis=1, keepdims=True)
    else:
        a = jnp.sum(jnp.where(lo, v, 0.0), axis=1, keepdims=True) * (1.0 / HEAD_DIM)
        b = jnp.sum(jnp.where(lo, 0.0, v), axis=1, keepdims=True) * (1.0 / HEAD_DIM)
    return jnp.concatenate([a, b], axis=0)


def _stack_low(x, lo, dtype, scale=None):
    x = x.astype(f32)
    if scale is not None:
        x = x * scale
    z = jnp.zeros_like(x)
    return jnp.concatenate([jnp.where(lo, x, z), jnp.where(lo, pltpu.roll(x, HEAD_DIM, 1), z)], axis=0).astype(dtype)


def _unstack_low(xs, lo, rows):
    return jnp.where(lo, xs[:rows], pltpu.roll(xs[rows:], HEAD_DIM, 1))


FA_TQ, FA_TK = 256, 1024


def _host_exchange(body, n_in, n_out, grid, xchg):
    if xchg is None:
        return body, [], [], [], []
    srcs, pick_src = xchg
    n = len(srcs)
    any_spec = pl.BlockSpec(memory_space=pl.ANY)

    def hosted(*refs):
        ins, src = refs[:n_in], refs[n_in:n_in + n]
        outs, dst = refs[n_in + n:n_in + n + n_out], refs[n_in + n + n_out:n_in + 2 * n + n_out]
        scratch, sems = refs[n_in + 2 * n + n_out:-3], refs[-3:]
        ids = [pl.program_id(a) for a in range(len(grid))]
        first = functools.reduce(jnp.logical_and, [i == 0 for i in ids])
        last = functools.reduce(jnp.logical_and, [i == g - 1 for i, g in zip(ids, grid)])

        @pl.when(first)
        def _():
            _chip_start(_chip_copies(src, dst, pick_src, *sems, receiving=False))

        body(*ins, *outs, *scratch)

        @pl.when(last)
        def _():
            _chip_wait(_chip_copies(src, dst, pick_src, *sems))

    return hosted, [any_spec] * n, [any_spec] * n, _chip_landing(srcs), _chip_sems(n)


def _flash_a_fwd(qa, kd, vd, s, xchg=None):
    tq, tk = FA_TQ, min(4 * FA_TK, s)
    nk = s // tk

    def body(q_ref, k_ref, v_ref, o_ref, lse_ref, qs, m_s, acc):
        lo = _lo_mask((tq, LANES))
        for pp in range(2):
            qs[pp] = _stack_low(q_ref[:, pp * LANES:(pp + 1) * LANES], lo, bf16, SCALE)
        m_s[...] = jnp.full(m_s.shape, NEG_INF, f32)
        acc[...] = jnp.zeros(acc.shape, f32)

        def chunk(j, carry):
            rows = pl.ds(pl.multiple_of(j * tk, tk), tk)
            k, v = k_ref[rows, :], v_ref[rows, :]
            for pp in range(2):
                sc = _dot(qs[pp], k, "nt")
                m_prev = m_s[pp]
                m_new = jnp.maximum(m_prev, jnp.max(sc, axis=1, keepdims=True))
                p = jnp.exp(sc - m_new).astype(bf16)
                acc[pp] = acc[pp] * jnp.exp(m_prev - m_new) + _dot(p, v, "nn")
                m_s[pp] = m_new
            return carry

        lax.fori_loop(0, nk, chunk, 0)
        for pp in range(2):
            sl = slice(pp * LANES, (pp + 1) * LANES)
            a = acc[pp]
            lane = lax.broadcasted_iota(jnp.int32, a.shape, 1)
            l = jnp.sum(jnp.where(lane == HEAD_DIM, a, 0.0), axis=1, keepdims=True)
            o_ref[:, sl] = _unstack_low(a / l, lo, tq).astype(bf16)
            lse = m_s[pp] + jnp.log(l)
            lse_ref[:, sl] = jnp.where(lo, lse[:tq], lse[tq:])

    kv = pl.BlockSpec((None, s, LANES), lambda c, i: (c, 0, 0))
    qo = pl.BlockSpec((tq, 2 * LANES), lambda c, i: (i, c))
    grid = (2, s // tq)
    hosted, xi, xo, xs, xsem = _host_exchange(body, 3, 2, grid, xchg)
    res = pl.pallas_call(
        hosted, name="flash_a_fwd" + ("_x" if xchg else ""), grid=grid, in_specs=[qo, kv, kv] + xi, out_specs=[qo, qo] + xo,
        out_shape=[SDS((s, 512), bf16), SDS((s, 512), f32)] + xs,
        scratch_shapes=[pltpu.VMEM((2, 2 * tq, LANES), bf16), pltpu.VMEM((2, 2 * tq, 1), f32),
                        pltpu.VMEM((2, 2 * tq, LANES), f32)] + xsem,
        compiler_params=_cparams())(qa, kd, vd, *(xchg[0] if xchg else []))
    return res[:2], res[2:]


def _flash_a_bwd(qa, kd, vd, oa, lse, doa, s, xchg=None):
    tq, tk = FA_TQ, 2 * FA_TK
    nk = s // tk

    def body(q_ref, do_ref, o_ref, lse_ref, k_ref, v_ref, dq_ref, dk_ref, dv_ref, qs, dos, lse_s, dl_s, dq_s):
        i = pl.program_id(1)
        lo = _lo_mask((tq, LANES))

        @pl.when(i == 0)
        def _():
            dk_ref[...] = jnp.zeros(dk_ref.shape, f32)
            dv_ref[...] = jnp.zeros(dv_ref.shape, f32)

        for pp in range(2):
            sl = slice(pp * LANES, (pp + 1) * LANES)
            do = do_ref[:, sl]
            qs[...] = _stack_low(q_ref[:, sl], lo, bf16, SCALE)
            dos[...] = _stack_low(do, lo, bf16)
            dl_s[...] = _rows_of(do * o_ref[:, sl].astype(f32), lo, "sum") * HEAD_DIM
            lse_s[...] = _rows_of(lse_ref[:, sl], lo, "max")
            dq_s[...] = jnp.zeros(dq_s.shape, f32)

            def chunk(j, carry):
                rows = pl.ds(pl.multiple_of(j * tk, tk), tk)
                k, v = k_ref[rows, :], v_ref[rows, :]
                q_, do_ = qs[...], dos[...]
                p = jnp.exp(_dot(q_, k, "nt") - lse_s[...])
                ds = (p * (_dot(do_, v, "nt") - dl_s[...])).astype(bf16)
                dv_ref[rows, :] += _dot(p.astype(bf16), do_, "tn")
                dk_ref[rows, :] += _dot(ds, q_, "tn")
                dq_s[...] += _dot(ds, k, "nn")
                return carry

            lax.fori_loop(0, nk, chunk, 0)
            dq_ref[:, sl] = _unstack_low(dq_s[...], lo, tq) * SCALE

    grp = lambda: pl.BlockSpec((tq, 2 * LANES), lambda c, i: (i, c))
    kv = lambda: pl.BlockSpec((None, s, LANES), lambda c, i: (c, 0, 0))
    grid = (2, s // tq)
    hosted, xi, xo, xs, xsem = _host_exchange(body, 6, 3, grid, xchg)
    res = pl.pallas_call(
        hosted, name="flash_a_bwd" + ("_x" if xchg else ""), grid=grid,
        in_specs=[grp(), grp(), grp(), grp(), kv(), kv()] + xi, out_specs=[grp(), kv(), kv()] + xo,
        out_shape=[SDS((s, 512), f32), SDS((2, s, LANES), f32), SDS((2, s, LANES), f32)] + xs,
        scratch_shapes=[pltpu.VMEM((2 * tq, LANES), bf16), pltpu.VMEM((2 * tq, LANES), bf16), pltpu.VMEM((2 * tq, 1), f32),
                        pltpu.VMEM((2 * tq, 1), f32), pltpu.VMEM((2 * tq, LANES), f32)] + xsem,
        compiler_params=_cparams())(qa, doa, oa, lse, kd, vd, *(xchg[0] if xchg else []))
    return res[:3], res[3:]


BH = 128
BQ_MAX = 512


class _BandGeom:
    def __init__(self, gi, s):
        self.d = B_GROUP_CFG[gi][1]
        self.l_sub = s // self.d
        self.bq = min(BQ_MAX, self.l_sub)
        self.nb = self.l_sub // self.bq
        self.grid = (self.d, self.nb, 2)
        per, last = self.bq // BH, self.l_sub // BH - 1
        self.out_col = lambda r, pp: r * 2 + pp
        self.in_col = self.out_col
        self.cur = lambda col: pl.BlockSpec((self.bq, LANES), lambda r, n, pp: (n, col(r, pp)))
        self.prev = lambda col: pl.BlockSpec((BH, LANES), lambda r, n, pp: (jnp.maximum(n * per - 1, 0), col(r, pp)))
        self.next = lambda col: pl.BlockSpec((BH, LANES), lambda r, n, pp: (jnp.minimum((n + 1) * per, last), col(r, pp)))
        self.window = lambda col: [self.prev(col), self.cur(col), self.next(col)]
        self.view_out = lambda a: a.reshape(self.l_sub, self.d * 256)
        self.view_in = self.view_out

    def fill_band_bias(self, bias_s, q_rows, q_off, k_off):
        @pl.when((pl.program_id(0) == 0) & (pl.program_id(1) == 0) & (pl.program_id(2) == 0))
        def _():
            r = lax.broadcasted_iota(jnp.int32, bias_s.shape, 0)
            q = jnp.where(r >= q_rows, r - q_rows, r) + q_off
            k = lax.broadcasted_iota(jnp.int32, bias_s.shape, 1) + k_off
            bias_s[...] = jnp.where(jnp.abs(q - k) <= 64, 0.0, NEG_INF)

    def edge_bias(self, first_pos, n, stacked_rows=False):
        if stacked_rows:
            r = lax.broadcasted_iota(jnp.int32, (2 * n, 1), 0)
            idx = jnp.where(r >= n, r - n, r)
        else:
            idx = lax.broadcasted_iota(jnp.int32, (1, n), 1)
        pos = first_pos + idx
        return jnp.where((pos >= 0) & (pos < self.l_sub), 0.0, NEG_INF)


def _cat3(a, b, c):
    return jnp.concatenate([a[...], b[...], c[...]], axis=0)


def _attn_b_fwd(qb, kb, vb, gi, s):
    g = _BandGeom(gi, s)
    bq = g.bq

    def body(q_ref, kp, kc, kn, vp, vc, vn, o_ref, lse_ref, bias_s):
        n = pl.program_id(1)
        g.fill_band_bias(bias_s, bq, 0, -BH)
        lo = _lo_mask((bq, LANES))
        qs = _stack_heads(q_ref[...] * SCALE, lo, bf16)
        k, v = _cat3(kp, kc, kn), _cat3(vp, vc, vn)
        sc = _dot(qs, k, "nt") + bias_s[...] + g.edge_bias(n * bq - BH, bq + 2 * BH)
        m = jnp.max(sc, axis=1, keepdims=True)
        p = jnp.exp(sc - m)
        den = jnp.sum(p, axis=1, keepdims=True)
        o = _dot(p.astype(bf16), v, "nn") / den
        o_ref[...] = jnp.where(lo, o[:bq], o[bq:])
        lse = m + jnp.log(den)
        lse_ref[...] = jnp.where(lo, lse[:bq], lse[bq:])

    out = g.cur(g.out_col)
    qv, kv, vv = g.view_in(qb), g.view_in(kb), g.view_in(vb)
    o, lse = pl.pallas_call(
        body, name=f"attn_b_fwd{gi}", grid=g.grid,
        in_specs=[g.cur(g.in_col)] + g.window(g.in_col) * 2, out_specs=[out, out],
        out_shape=[SDS((g.l_sub, g.d * 256), f32), SDS((g.l_sub, g.d * 256), f32)],
        scratch_shapes=[pltpu.VMEM((2 * bq, bq + 2 * BH), f32)],
        compiler_params=_cparams())(qv, kv, kv, kv, vv, vv, vv)
    return o.reshape(s, 256), lse.reshape(s, 256)


def _attn_b_dq(qb, kb, vb, dob, lse, delta, gi, s):
    g = _BandGeom(gi, s)
    bq = g.bq

    def body(q_ref, kp, kc, kn, vp, vc, vn, do_ref, lse_ref, dl_ref, dq_ref, bias_s):
        n = pl.program_id(1)
        g.fill_band_bias(bias_s, bq, 0, -BH)
        lo = _lo_mask((bq, LANES))
        qs = _stack_heads(q_ref[...] * SCALE, lo, bf16)
        dos = _stack_heads(do_ref[...], lo, bf16)
        k, v = _cat3(kp, kc, kn), _cat3(vp, vc, vn)
        sc = _dot(qs, k, "nt") + bias_s[...] + g.edge_bias(n * bq - BH, bq + 2 * BH)
        p = jnp.exp(sc - _rows_of(lse_ref[...], lo, "max"))
        dp = _dot(dos, v, "nt")
        ds = (p * (dp - _rows_of(dl_ref[...], lo, "sum"))).astype(bf16)
        dqs = _dot(ds, k, "nn")
        dq_ref[...] = jnp.where(lo, dqs[:bq], dqs[bq:]) * SCALE

    out = g.cur(g.out_col)
    qv, kv, vv = g.view_in(qb), g.view_in(kb), g.view_in(vb)
    dq = pl.pallas_call(
        body, name=f"attn_b_dq{gi}", grid=g.grid,
        in_specs=[g.cur(g.in_col)] + g.window(g.in_col) * 2 + [out, out, out], out_specs=out,
        out_shape=SDS((g.l_sub, g.d * 256), f32), scratch_shapes=[pltpu.VMEM((2 * bq, bq + 2 * BH), f32)],
        compiler_params=_cparams())(qv, kv, kv, kv, vv, vv, vv, g.view_out(dob), g.view_out(lse), g.view_out(delta))
    return dq.reshape(s, 256)


def _attn_b_dkv(qb, kb, vb, dob, lse, delta, gi, s):
    g = _BandGeom(gi, s)
    bq = g.bq
    nq = bq + 2 * BH

    def body(k_ref, v_ref, qp, qc, qn, dop, doc, don, lp, lc, ln, dp_, dc_, dn_, dk_ref, dv_ref, bias_s):
        m = pl.program_id(1)
        g.fill_band_bias(bias_s, nq, -BH, 0)
        lo = _lo_mask((nq, LANES))
        qs = _stack_heads(_cat3(qp, qc, qn) * SCALE, lo, bf16)
        dos = _stack_heads(_cat3(dop, doc, don), lo, bf16)
        lses = _rows_of(_cat3(lp, lc, ln), lo, "max")
        dls = _rows_of(_cat3(dp_, dc_, dn_), lo, "sum")
        k, v = k_ref[...], v_ref[...]
        sc = _dot(qs, k, "nt") + bias_s[...] + g.edge_bias(m * bq - BH, nq, stacked_rows=True)
        p = jnp.exp(sc - lses)
        dpv = _dot(dos, v, "nt")
        ds = (p * (dpv - dls)).astype(bf16)
        dv_ref[...] = _dot(p.astype(bf16), dos, "tn")
        dk_ref[...] = _dot(ds, qs, "tn")

    cur = g.cur(g.out_col)
    qv, kv, vv = g.view_in(qb), g.view_in(kb), g.view_in(vb)
    dk, dv = pl.pallas_call(
        body, name=f"attn_b_dkv{gi}", grid=g.grid,
        in_specs=[g.cur(g.in_col)] * 2 + g.window(g.in_col) + g.window(g.out_col) * 3, out_specs=[cur, cur],
        out_shape=[SDS((g.l_sub, g.d * 256), f32), SDS((g.l_sub, g.d * 256), f32)],
        scratch_shapes=[pltpu.VMEM((2 * nq, bq), f32)],
        compiler_params=_cparams())(kv, vv, qv, qv, qv, *([g.view_out(dob)] * 3), *([g.view_out(lse)] * 3),
                                    *([g.view_out(delta)] * 3))
    return dk.reshape(s, 256), dv.reshape(s, 256)


def _merge_b(o_list, lse_list, s):
    ts = 256

    def fn(o0, o1, o2, l0, l1, l2):
        m = jnp.maximum(jnp.maximum(l0, l1), l2)
        w0, w1, w2 = jnp.exp(l0 - m), jnp.exp(l1 - m), jnp.exp(l2 - m)
        den = w0 + w1 + w2
        return (w0 * o0 + w1 * o1 + w2 * o2) / den, m + jnp.log(den)

    row = pl.BlockSpec((ts, 256), lambda i: (i, 0))
    return _ew("merge_b", fn, (s // ts,), [*o_list, *lse_list], [row] * 6,
               [SDS((s, 256), bf16), SDS((s, 256), f32)], [row, row])


def _delta_b(dob, ob, s):
    ts = 256

    def fn(do, o):
        prod = do * o.astype(f32)
        lo = _lo_mask((ts, LANES))
        return jnp.concatenate([_head_stat(prod[:, :LANES], lo), _head_stat(prod[:, LANES:], lo)], axis=1)

    row = pl.BlockSpec((ts, 256), lambda i: (i, 0))
    return _ew("delta_b", fn, (s // ts,), [dob, ob], [row, row], [SDS((s, 256), f32)], [row])[0]


def _band(t0, u0, w, nt, nu, s, transposed):
    t = t0 + lax.broadcasted_iota(jnp.int32, (nt, nu), 0)
    u = u0 + lax.broadcasted_iota(jnp.int32, (nt, nu), 1)
    if transposed:
        lo, hi = jnp.clip(t - w // 2 + 1, 0, s), jnp.clip(t + w // 2 + 1, 0, s)
    else:
        lo, hi = jnp.clip(t - w // 2, 0, s), jnp.clip(t + w - w // 2, 0, s)
    return ((u >= lo) & (u < hi)).astype(f32)


def _pool_cnt(t0, w, nt, s):
    t = t0 + lax.broadcasted_iota(jnp.int32, (nt, 1), 0)
    return (jnp.clip(t + w - w // 2, 0, s) - jnp.clip(t - w // 2, 0, s)).astype(f32)


POOL_HALO = 8


def _halo_specs(tp, s, cb):
    per, last = tp // POOL_HALO, s // POOL_HALO - 1
    return [pl.BlockSpec((POOL_HALO, 512), lambda i: (jnp.maximum(i * per - 1, 0), cb)),
            pl.BlockSpec((tp, 512), lambda i: (i, cb)),
            pl.BlockSpec((POOL_HALO, 512), lambda i: (jnp.minimum((i + 1) * per, last), cb))]


def _pool_fwd(z, lin, scale, s):
    tp = 256
    nt = s // tp

    def body(up, uc, un, lin_ref, sc_ref, pooled_o, pc_o):
        i = pl.program_id(0)
        ext = jnp.concatenate([up[...], uc[...], un[...]], axis=0)
        for g, w in enumerate(POOL_WINDOWS):
            sl = slice(g * LANES, (g + 1) * LANES)
            band = _band(i * tp, i * tp - POOL_HALO, w, tp, tp + 2 * POOL_HALO, s, False)
            sm = jnp.dot(band, ext[:, sl], preferred_element_type=f32, precision=lax.Precision.HIGHEST)
            pooled = (sm / _pool_cnt(i * tp, w, tp, s) - uc[:, sl]).astype(bf16)
            pooled_o[:, sl] = pooled
            mixed = _dot(pooled, lin_ref[g].astype(bf16), "nn")
            pc_o[:, sl] = (mixed * sc_ref[:, sl]).astype(bf16)

    row = pl.BlockSpec((tp, 512), lambda i: (i, 0))
    return pl.pallas_call(
        body, name="pool_fwd", grid=(nt,),
        in_specs=_halo_specs(tp, s, C_UC // 512)
        + [pl.BlockSpec((4, LANES, LANES), lambda i: (0, 0, 0)), pl.BlockSpec((1, 512), lambda i: (0, 0))],
        out_specs=[row, row], out_shape=[SDS((s, 512), bf16), SDS((s, 512), bf16)],
        compiler_params=_cparams())(z, z, z, lin, scale)


def _pool_bwd1(dpc, pooled, lin, scale, s):
    tp = 256

    def body(dpc_ref, pooled_ref, lin_ref, sc_ref, dpn_o, dlin_o, dsc_o):
        i = pl.program_id(0)
        dsc = []
        for g, w in enumerate(POOL_WINDOWS):
            sl = slice(g * LANES, (g + 1) * LANES)
            pooled = pooled_ref[:, sl]
            linb = lin_ref[g].astype(bf16)
            mixed = _dot(pooled, linb, "nn")
            dpc_g = dpc_ref[:, sl]
            dsc.append(jnp.sum(dpc_g * mixed, axis=0, keepdims=True))
            dmixed = (dpc_g * sc_ref[:, sl]).astype(bf16)
            dpn_o[:, sl] = _dot(dmixed, linb, "nt") / _pool_cnt(i * tp, w, tp, s)
            dl = _dot(pooled, dmixed, "tn")

            @pl.when(i == 0)
            def _(g=g, dl=dl):
                dlin_o[g] = dl

            @pl.when(i > 0)
            def _(g=g, dl=dl):
                dlin_o[g] += dl

        dsc = jnp.concatenate(dsc, axis=1)

        @pl.when(i == 0)
        def _():
            dsc_o[...] = dsc

        @pl.when(i > 0)
        def _():
            dsc_o[...] += dsc

    row = pl.BlockSpec((tp, 512), lambda i: (i, 0))
    linspec = pl.BlockSpec((4, LANES, LANES), lambda i: (0, 0, 0))
    one = pl.BlockSpec((1, 512), lambda i: (0, 0))
    return pl.pallas_call(
        body, name="pool_bwd1", grid=(s // tp,), in_specs=[row, row, linspec, one], out_specs=[row, linspec, one],
        out_shape=[SDS((s, 512), f32), SDS((4, LANES, LANES), f32), SDS((1, 512), f32)],
        compiler_params=_cparams())(dpc, pooled, lin, scale)


def _pool_bwd2(dpn, dz, s):
    tp = 256
    nt = s // tp

    def body(dp, dc, dn, dz_in, dz_o):
        i = pl.program_id(0)
        ext = jnp.concatenate([dp[...], dc[...], dn[...]], axis=0)
        for g, w in enumerate(POOL_WINDOWS):
            sl = slice(g * LANES, (g + 1) * LANES)
            band = _band(i * tp, i * tp - POOL_HALO, w, tp, tp + 2 * POOL_HALO, s, True)
            sm = jnp.dot(band, ext[:, sl], preferred_element_type=f32, precision=lax.Precision.HIGHEST)
            dz_o[:, sl] = (sm - dc[:, sl] * _pool_cnt(i * tp, w, tp, s)).astype(bf16)

    return pl.pallas_call(
        body, name="pool_bwd2", grid=(nt,),
        in_specs=_halo_specs(tp, s, 0) + [pl.BlockSpec(memory_space=pl.ANY)],
        out_specs=pl.BlockSpec((tp, 512), lambda i: (i, C_UC // 512)), out_shape=SDS((s, IN_WIDTH), bf16),
        input_output_aliases={3: 0}, compiler_params=_cparams())(dpn, dpn, dpn, dz)


GW = 512


MIX_TM = 256
BR_WIDTHS = (512, 256, 512)
N_GATE_BLOCKS = 3 * D_MODEL // GW


def _gate_specs(ts):
    return [pl.BlockSpec((ts, GW), lambda i, q=q: (i, C_GZ // GW + q)) for q in range(N_GATE_BLOCKS)]


def _rms(xt, gt):
    return xt * lax.rsqrt(jnp.mean(xt * xt, axis=1, keepdims=True) + EPS) * gt


def _mix_out_fwd(acts, z, b_gate, x, norm_gain, w, s):
    ts = MIX_TM

    def body(oa, ob, pc, *rest):
        gz = rest[:N_GATE_BLOCKS]
        bias, x_ref, gain, wa, wb, wc, wo, y3_o, m_o, x1_o, h2_o, y_s = rest[N_GATE_BLOCKS:]
        for b, (act, wt) in enumerate(((oa, wa), (ob, wb), (pc, wc))):
            a = act[...]
            for j in range(N_CHIPS):
                y_s[b, :, j * 256:(j + 1) * 256] = _dot(a, wt[j], "nn")
        y3_o[...] = y_s[...].astype(bf16)
        for h in range(2):
            hs = slice(h * GW, (h + 1) * GW)
            tot = jnp.zeros((ts, GW), f32)
            for b in range(3):
                g = jax.nn.sigmoid(gz[2 * b + h][...] + bias[:, b * D_MODEL + h * GW:b * D_MODEL + (h + 1) * GW])
                tot += g * y_s[b, :, hs]
            m_o[:, hs] = tot.astype(bf16)
        acc = x_ref[...]
        for j in range(N_CHIPS):
            acc += _dot(m_o[:, j * 256:(j + 1) * 256], wo[j], "nn")
        x1_o[...] = acc
        h2_o[...] = _rms(acc, gain[...]).astype(bf16)

    row = lambda width: pl.BlockSpec((ts, width), lambda i: (i, 0))
    res3 = lambda a: pl.BlockSpec(a.shape, lambda i: (0, 0, 0))
    wts = [w["w_branch_a"], w["w_branch_b"], w["w_branch_c"], w["w_out"]]
    return pl.pallas_call(
        body, name="mix_out_fwd", grid=(s // ts,),
        in_specs=[row(BR_WIDTHS[0]), row(BR_WIDTHS[1]), row(BR_WIDTHS[2])] + _gate_specs(ts)
        + [pl.BlockSpec((1, 3 * D_MODEL), lambda i: (0, 0)), row(D_MODEL), pl.BlockSpec((1, D_MODEL), lambda i: (0, 0))]
        + [res3(a) for a in wts],
        out_specs=[pl.BlockSpec((3, ts, D_MODEL), lambda i: (0, i, 0)), row(D_MODEL), row(D_MODEL), row(D_MODEL)],
        out_shape=[SDS((3, s, D_MODEL), bf16), SDS((s, D_MODEL), bf16), SDS((s, D_MODEL), f32), SDS((s, D_MODEL), bf16)],
        scratch_shapes=[pltpu.VMEM((3, ts, D_MODEL), f32)],
        compiler_params=_cparams())(*acts, *([z] * N_GATE_BLOCKS), b_gate, x, norm_gain, *wts)


def _mix_out_bwd(dx1, y3, z, b_gate, w, s):
    ts = MIX_TM

    def body(dx_ref, y3_ref, *rest):
        gz = rest[:N_GATE_BLOCKS]
        bias, wa, wb, wc, wo, dy_o, dz_o, db_o, doa_o, dob_o, dpc_o, dm_s = rest[N_GATE_BLOCKS:]
        i = pl.program_id(0)
        dx = dx_ref[...].astype(bf16)
        for j in range(N_CHIPS):
            dm_s[:, j * 256:(j + 1) * 256] = _dot(dx, wo[j], "nt")
        dz_o[:, 0:C_GZ] = jnp.zeros((ts, C_GZ), bf16)
        dbs = []
        for b in range(3):
            for h in range(2):
                hs = slice(h * GW, (h + 1) * GW)
                g = jax.nn.sigmoid(gz[2 * b + h][...] + bias[:, b * D_MODEL + h * GW:b * D_MODEL + (h + 1) * GW])
                dm = dm_s[:, hs]
                dy_o[b, :, hs] = (dm * g).astype(bf16)
                dgz = dm * y3_ref[b, :, hs].astype(f32) * g * (1.0 - g)
                c0 = C_GZ + b * D_MODEL + h * GW
                dz_o[:, c0:c0 + GW] = dgz.astype(bf16)
                dbs.append(jnp.sum(dgz, axis=0, keepdims=True))
        db = jnp.concatenate(dbs, axis=1)

        @pl.when(i == 0)
        def _():
            db_o[...] = db

        @pl.when(i > 0)
        def _():
            db_o[...] += db

        for b, (wt, out) in enumerate(((wa, doa_o), (wb, dob_o), (wc, dpc_o))):
            acc = _dot(dy_o[b, :, 0:256], wt[0], "nt")
            for j in range(1, N_CHIPS):
                acc += _dot(dy_o[b, :, j * 256:(j + 1) * 256], wt[j], "nt")
            out[...] = acc

    row = lambda width: pl.BlockSpec((ts, width), lambda i: (i, 0))
    res3 = lambda a: pl.BlockSpec(a.shape, lambda i: (0, 0, 0))
    blk3 = pl.BlockSpec((3, ts, D_MODEL), lambda i: (0, i, 0))
    one = pl.BlockSpec((1, 3 * D_MODEL), lambda i: (0, 0))
    wts = [w["w_branch_a"], w["w_branch_b"], w["w_branch_c"], w["w_out"]]
    return pl.pallas_call(
        body, name="mix_out_bwd", grid=(s // ts,),
        in_specs=[row(D_MODEL), blk3] + _gate_specs(ts) + [one] + [res3(a) for a in wts],
        out_specs=[blk3, row(IN_WIDTH), one, row(BR_WIDTHS[0]), row(BR_WIDTHS[1]), row(BR_WIDTHS[2])],
        out_shape=[SDS((3, s, D_MODEL), bf16), SDS((s, IN_WIDTH), bf16), SDS((1, 3 * D_MODEL), f32),
                   SDS((s, BR_WIDTHS[0]), f32), SDS((s, BR_WIDTHS[1]), f32), SDS((s, BR_WIDTHS[2]), f32)],
        scratch_shapes=[pltpu.VMEM((ts, D_MODEL), f32)],
        compiler_params=_cparams())(dx1, y3, *([z] * N_GATE_BLOCKS), b_gate, *wts)


def _loss_grad(y, tgt, s):
    ts = 256

    def fn(yt, tt):
        e = yt - tt
        return e * (1.0 / D_MODEL), jnp.sum(e * e, axis=0, keepdims=True) * (0.5 / D_MODEL)

    row = pl.BlockSpec((ts, D_MODEL), lambda i: (i, 0))
    one = pl.BlockSpec((1, D_MODEL), lambda i: (0, 0))
    return _ew("loss_grad", fn, (s // ts,), [y, tgt], [row, row], [SDS((s, D_MODEL), f32), SDS((1, D_MODEL), f32)],
               [row, one], n_acc=1)


TM = 512


def _layer_fwd(x, w, sm, tabs, s, xchg=None, hb=None, next_gain=None):
    nm = s // TM
    if hb is None:
        hb = _norm_fwd(x, sm["norm_mix"], s)
    z = _mm("mm_z", [hb, w["w_in"]],
            [pl.BlockSpec((TM, D_MODEL), lambda j, i, k: (i, 0)), pl.BlockSpec((None, D_MODEL, 1664), lambda j, i, k: (j, 0, 0))],
            SDS((s, IN_WIDTH), f32), pl.BlockSpec((TM, 1664), lambda j, i, k: (i, j)), (4, nm, 1), 1, "nn", None)
    qa, kd, vd, qb, kb, vb = _prep_fwd(z, tabs, sm["gains"], s)
    (oa, lse_a), landed = _flash_a_fwd(qa, kd, vd, s, xchg)
    if xchg is not None:
        w = dict(w, **dict(zip(REST + ("next_w_in",), landed)))
    ob_parts = [_attn_b_fwd(qb[gi], kb[gi], vb[gi], gi, s) for gi in range(3)]
    ob, lse_b = _merge_b([p[0] for p in ob_parts], [p[1] for p in ob_parts], s)
    pooled, pc = _pool_fwd(z, sm["pool_lin"], sm["pool_scale"], s)

    y3, merged, x1, h2 = _mix_out_fwd([oa, ob, pc], z, sm["b_gate"], x, sm["norm_ffn"], w, s)
    g4, u4, a4 = _ffn_up(h2, w["w_ffn_gate"], w["w_ffn_up"], s)
    x2 = _mm_shard_sum("mm_down", [(a4, w["w_ffn_down"])], "nn", D_MODEL, s, TM, add=x1, norm_gain=next_gain)
    x2, hb_next = x2 if next_gain is not None else (x2, None)
    saved = dict(x=x, hb=hb, z=z, qa=qa, kd=kd, vd=vd, qb=qb, kb=kb, vb=vb, oa=oa, lse_a=lse_a, ob=ob, lse_b=lse_b,
                 pooled=pooled, pc=pc, y3=y3, merged=merged, x1=x1, h2=h2, g4=g4, u4=u4, a4=a4)
    return x2, saved, w, hb_next


def _ffn_up(h2, wg, wu, s):
    nm = s // TM

    def body(h_ref, wg_ref, wu_ref, g_o, u_o, a_o):
        h = h_ref[...]
        g = _dot(h, wg_ref[...], "nn")
        u = _dot(h, wu_ref[...], "nn")
        g_o[...] = g.astype(bf16)
        u_o[...] = u.astype(bf16)
        a_o[...] = (g * jax.nn.sigmoid(g) * u).astype(bf16)

    wsp = pl.BlockSpec((None, D_MODEL, FF_SHARD), lambda j, i: (j, 0, 0))
    osp = pl.BlockSpec((None, TM, FF_SHARD), lambda j, i: (j, i, 0))
    return pl.pallas_call(
        body, name="ffn_up", grid=(4, nm), in_specs=[pl.BlockSpec((TM, D_MODEL), lambda j, i: (i, 0)), wsp, wsp],
        out_specs=[osp, osp, osp],
        out_shape=[SDS((4, s, FF_SHARD), bf16), SDS((4, s, FF_SHARD), bf16), SDS((4, s, FF_SHARD), bf16)],
        compiler_params=_cparams())(h2, wg, wu)


def _ffn_bwd_act(dx2, wd, g4, u4, s):
    def body(dx_ref, wd_ref, g_ref, u_ref, dg_o, du_o):
        dx = dx_ref[...].astype(bf16)
        for j in range(N_CHIPS):
            da = _dot(dx, wd_ref[j], "nt")
            g, u = g_ref[j].astype(f32), u_ref[j].astype(f32)
            sg = jax.nn.sigmoid(g)
            dg_o[j] = (da * u * sg * (1.0 + g * (1.0 - sg))).astype(bf16)
            du_o[j] = (da * g * sg).astype(bf16)

    osp = pl.BlockSpec((N_CHIPS, TM, FF_SHARD), lambda i: (0, i, 0))
    return pl.pallas_call(
        body, name="ffn_bwd_act", grid=(s // TM,),
        in_specs=[pl.BlockSpec((TM, D_MODEL), lambda i: (i, 0)), pl.BlockSpec((N_CHIPS, FF_SHARD, D_MODEL), lambda i: (0, 0, 0)),
                  osp, osp],
        out_specs=[osp, osp], out_shape=[SDS((4, s, FF_SHARD), bf16), SDS((4, s, FF_SHARD), bf16)],
        compiler_params=_cparams())(dx2, wd, g4, u4)


def _mm_shard_sum(name, pairs, dims, out_width, s, tm, add=None, norm_gain=None):
    n = len(pairs)

    def body(*refs):
        a_refs, w_refs = refs[:n], refs[n:2 * n]
        o_ref = refs[-2] if norm_gain is not None else refs[-1]
        acc = refs[2 * n][...] if add is not None else None
        for a_ref, w_ref in zip(a_refs, w_refs):
            for j in range(N_CHIPS):
                if len(a_ref.shape) == 3:
                    a = a_ref[j]
                else:
                    c = a_ref.shape[1] // N_CHIPS
                    a = a_ref[:, j * c:(j + 1) * c]
                part = _dot(a.astype(bf16), w_ref[j], dims)
                acc = part if acc is None else acc + part
        o_ref[...] = acc
        if norm_gain is not None:
            refs[-1][...] = _rms(acc, refs[2 * n + (add is not None)][...]).astype(bf16)

    a_specs = [pl.BlockSpec((N_CHIPS, tm, a.shape[2]), lambda i: (0, i, 0)) if a.ndim == 3
               else pl.BlockSpec((tm, a.shape[1]), lambda i: (i, 0)) for a, _ in pairs]
    w_specs = [pl.BlockSpec(wt.shape, lambda i: (0, 0, 0)) for _, wt in pairs]
    row = pl.BlockSpec((tm, out_width), lambda i: (i, 0))
    extra, extra_specs = ([add], [row]) if add is not None else ([], [])
    if norm_gain is not None:
        extra, extra_specs = extra + [norm_gain], extra_specs + [pl.BlockSpec((1, out_width), lambda i: (0, 0))]
    normed = norm_gain is not None
    return pl.pallas_call(
        body, name=name + ("_norm" if normed else ""), grid=(s // tm,), in_specs=a_specs + w_specs + extra_specs,
        out_specs=[row, row] if normed else row,
        out_shape=[SDS((s, out_width), f32), SDS((s, out_width), bf16)] if normed else SDS((s, out_width), f32),
        compiler_params=_cparams())(*[a for a, _ in pairs], *[wt for _, wt in pairs], *extra)


GRAD_WIRE_DTYPE = bf16


WG_TK = 2048


def _wgrad(name, a, a_spec, b, b_spec, out_shape, out_block, s):
    nk = s // min(WG_TK, s)
    return _mm(name, [a, b], [a_spec, b_spec], SDS(out_shape, GRAD_WIRE_DTYPE),
               pl.BlockSpec((None,) + out_block, lambda j, k: (j, 0, 0)), (4, nk), nk, "tn", out_block)


def _layer_bwd(dx2, sv, w, sm, tabs, s, pending_mixer=None):
    tk = min(WG_TK, s)
    tok = lambda width: pl.BlockSpec((tk, width), lambda j, k: (k, 0))
    g_wd = _wgrad("wg_down", sv["a4"], pl.BlockSpec((None, tk, FF_SHARD), lambda j, k: (j, k, 0)), dx2, tok(D_MODEL),
                  (4, FF_SHARD, D_MODEL), (FF_SHARD, D_MODEL), s)
    dg4, du4 = _ffn_bwd_act(dx2, w["w_ffn_down"], sv["g4"], sv["u4"], s)
    sh704 = pl.BlockSpec((None, tk, FF_SHARD), lambda j, k: (j, k, 0))
    g_wg = _wgrad("wg_gate", sv["h2"], tok(D_MODEL), dg4, sh704, (4, D_MODEL, FF_SHARD), (D_MODEL, FF_SHARD), s)
    g_wu = _wgrad("wg_up", sv["h2"], tok(D_MODEL), du4, sh704, (4, D_MODEL, FF_SHARD), (D_MODEL, FF_SHARD), s)
    dh2 = _mm_shard_sum("ffn_bwd_dh", [(dg4, w["w_ffn_gate"]), (du4, w["w_ffn_up"])], "nt", D_MODEL, s, TM)
    dx1, g_norm_ffn = _norm_bwd(sv["x1"], sm["norm_ffn"], dh2, dx2, s)
    colblk = lambda width: pl.BlockSpec((tk, width), lambda j, k: (k, j))
    g_wo = _wgrad("wg_out", sv["merged"], colblk(256), dx1, tok(D_MODEL), (4, 256, D_MODEL), (256, D_MODEL), s)
    dy3, dz, g_bgate, doa, dob, dpc = _mix_out_bwd(dx1, sv["y3"], sv["z"], sm["b_gate"], w, s)
    br_grads = []
    for b, (nm_, act, kdim) in enumerate((("a", sv["oa"], 512), ("b", sv["ob"], 256), ("c", sv["pc"], 512))):
        br_grads.append(_wgrad("wg_br" + nm_, act, tok(kdim), dy3,
                               pl.BlockSpec((None, tk, 256), lambda j, k, b=b: (b, k, j)), (4, kdim, 256), (kdim, 256), s))
    dpn, g_lin, g_scale = _pool_bwd1(dpc, sv["pooled"], sm["pool_lin"], sm["pool_scale"], s)
    dz = _pool_bwd2(dpn, dz, s)
    delta = _delta_b(dob, sv["ob"], s)
    dqb, dkb, dvb = [], [], []
    for gi in range(3):
        qg, kg, vg = sv["qb"][gi], sv["kb"][gi], sv["vb"][gi]
        dqb.append(_attn_b_dq(qg, kg, vg, dob, sv["lse_b"], delta, gi, s))
        dk, dv = _attn_b_dkv(qg, kg, vg, dob, sv["lse_b"], delta, gi, s)
        dkb.append(dk)
        dvb.append(dv)
    if pending_mixer is not None:
        xchg = (list(pending_mixer) + [g_wg, g_wu, g_wd], lambda ref, peer: ref.at[peer])
    else:
        xchg = None
    (dqa, dkd, dvd), landed = _flash_a_bwd(sv["qa"], sv["kd"], sv["vd"], sv["oa"], sv["lse_a"], doa, s, xchg)
    dz, g_gains = _prep_bwd(sv["z"], tabs, sm["gains"], dqa, dkd, dvd, dqb, dkb, dvb, dz, s)
    g_win = _wgrad("wg_in", sv["hb"], tok(D_MODEL), dz, colblk(1664), (4, D_MODEL, 1664), (D_MODEL, 1664), s)
    dh = _mm_shard_sum("mm_dh", [(dz, w["w_in"])], "nt", D_MODEL, s, 256)
    dx0, g_norm_mix = _norm_bwd(sv["x"], sm["norm_mix"], dh, dx1, s)
    big = dict(w_in=g_win, w_branch_a=br_grads[0], w_branch_b=br_grads[1], w_branch_c=br_grads[2], w_out=g_wo,
               w_ffn_gate=g_wg, w_ffn_up=g_wu, w_ffn_down=g_wd)
    gg = g_gains[0:4, :HEAD_DIM] + g_gains[0:4, HEAD_DIM:]
    small = jnp.concatenate([g_norm_mix.reshape(-1), g_bgate.reshape(-1), gg.reshape(-1), g_lin.reshape(-1),
                             g_scale.reshape(-1), g_norm_ffn.reshape(-1)]).reshape(SMALL_ROWS, LANES)
    return dx0, big, small, landed


def _mesh_pos():
    return lax.axis_index("x"), lax.axis_index("y"), lax.axis_index("c")


def _chip_copies(src, dst, pick_src, send_sems, recv_sems, loc_sems, receiving=True):
    n = len(src)
    x, y, c = _mesh_pos()
    me = 2 * x + y
    local = [pltpu.make_async_copy(pick_src(src[t], me), dst[t].at[me], loc_sems.at[t]) for t in range(n)]
    push, recv = [], []
    for j, (px, py) in enumerate([(1 - x, y), (x, 1 - y), (1 - x, 1 - y)]):
        peer = 2 * px + py
        for t in range(n):
            sems = dict(send_sem=send_sems.at[j * n + t], recv_sem=recv_sems.at[j * n + t],
                        device_id=(px, py, c), device_id_type=MESH)
            push.append(pltpu.make_async_remote_copy(src_ref=pick_src(src[t], peer), dst_ref=dst[t].at[me], **sems))
            if receiving:
                recv.append(pltpu.make_async_remote_copy(src_ref=pick_src(src[t], peer), dst_ref=dst[t].at[peer], **sems))
    return local, push, recv


def _chip_start(copies):
    local, push, _ = copies
    for cp in local + push:
        cp.start()


def _chip_wait(copies):
    local, push, recv = copies
    for cp in push:
        cp.wait_send()
    for cp in recv:
        cp.wait_recv()
    for cp in local:
        cp.wait()


def _chip_sems(n):
    return [pltpu.SemaphoreType.DMA((3 * n,)), pltpu.SemaphoreType.DMA((3 * n,)), pltpu.SemaphoreType.DMA((n,))]


def _chip_landing(srcs):
    return [SDS((N_CHIPS,) + tuple(a.shape[1:]), a.dtype) for a in srcs]


def _chip_exchange(name, srcs, pick_src):
    n = len(srcs)
    any_spec = pl.BlockSpec(memory_space=pl.ANY)

    def body(*refs):
        copies = _chip_copies(refs[:n], refs[n:2 * n], pick_src, *refs[2 * n:])
        _chip_start(copies)
        _chip_wait(copies)

    return pl.pallas_call(
        body, name=name, in_specs=[any_spec] * n, out_specs=[any_spec] * n, out_shape=_chip_landing(srcs),
        scratch_shapes=_chip_sems(n), compiler_params=pltpu.CompilerParams())(*srcs)


def _sibling_exchange(name, srcs):
    n = len(srcs)
    any_spec = pl.BlockSpec(memory_space=pl.ANY)

    def body(*refs):
        src, dst = refs[:n], refs[n:2 * n]
        send_sems, recv_sems = refs[2 * n:]
        x, y, c = _mesh_pos()
        cps = [pltpu.make_async_remote_copy(src_ref=src[t], dst_ref=dst[t], send_sem=send_sems.at[t], recv_sem=recv_sems.at[t],
                                            device_id=(x, y, 1 - c), device_id_type=MESH) for t in range(n)]
        for cp in cps:
            cp.start()
        for cp in cps:
            cp.wait()

    return pl.pallas_call(
        body, name=name, in_specs=[any_spec] * n, out_specs=[any_spec] * n,
        out_shape=[SDS(a.shape, a.dtype) for a in srcs],
        scratch_shapes=[pltpu.SemaphoreType.DMA((n,)), pltpu.SemaphoreType.DMA((n,))],
        compiler_params=pltpu.CompilerParams())(*srcs)


def _all_exchange(name, src):
    any_spec = pl.BlockSpec(memory_space=pl.ANY)

    def body(src_ref, dst_ref, send_sems, recv_sems, loc_sem):
        x, y, c = _mesh_pos()
        me = 4 * x + 2 * y + c
        peers = [(x ^ ((k >> 2) & 1), y ^ ((k >> 1) & 1), c ^ (k & 1)) for k in range(1, N_DEV)]
        local = pltpu.make_async_copy(src_ref, dst_ref.at[me], loc_sem)
        local.start()
        cps = [pltpu.make_async_remote_copy(src_ref=src_ref, dst_ref=dst_ref.at[me], send_sem=send_sems.at[k],
                                            recv_sem=recv_sems.at[k], device_id=p, device_id_type=MESH)
               for k, p in enumerate(peers)]
        for cp in cps:
            cp.start()
        for cp in cps:
            cp.wait_send()
        for k, (px, py, pc) in enumerate(peers):
            pltpu.make_async_remote_copy(src_ref=src_ref, dst_ref=dst_ref.at[4 * px + 2 * py + pc], send_sem=send_sems.at[k],
                                         recv_sem=recv_sems.at[k], device_id=(px, py, pc), device_id_type=MESH).wait_recv()
        local.wait()

    return pl.pallas_call(
        body, name=name, in_specs=[any_spec], out_specs=any_spec, out_shape=SDS((N_DEV,) + src.shape, src.dtype),
        scratch_shapes=[pltpu.SemaphoreType.DMA((N_DEV - 1,)), pltpu.SemaphoreType.DMA((N_DEV - 1,)), pltpu.SemaphoreType.DMA],
        compiler_params=pltpu.CompilerParams())(src)


def _cast_bf16(name, a):
    l, r, c = a.shape
    tr = _row_tile(r, c)
    spec = pl.BlockSpec((None, tr, c), lambda i, j: (i, j, 0))
    return _ew(name, lambda t: t, (l, r // tr), [a], [spec], [SDS(a.shape, bf16)], [spec])[0]


def _sum4(name, land):
    _, r, c = land.shape
    tr = _row_tile(r, c)
    up = lambda t: t.astype(f32)
    return _ew(name, lambda a, b, cc, d: ((up(a) + up(b)) + up(cc)) + up(d), (r // tr,), [land] * 4,
               [pl.BlockSpec((None, tr, c), lambda i, k=k: (k, i, 0)) for k in range(4)],
               [SDS((r, c), f32)], [pl.BlockSpec((tr, c), lambda i: (i, 0))])[0]


def _adam_math(g, w, m, v):
    m2 = ADAM_B1 * m + (1.0 - ADAM_B1) * g
    v2 = ADAM_B2 * v + (1.0 - ADAM_B2) * (g * g)
    m_hat = m2 / (1.0 - ADAM_B1 ** ADAM_STEP)
    v_hat = v2 / (1.0 - ADAM_B2 ** ADAM_STEP)
    delta = -ADAM_LR * (m_hat / (jnp.sqrt(v_hat) + ADAM_EPS) + ADAM_WD * w)
    return delta, m2, v2


def _adamw_big(name, p_own, p_sib, w, m, v):
    l, r, c = w.shape
    tr = _row_tile(r, c)

    def fn(a, b, wt, mt, vt):
        g = a + b
        return (g,) + _adam_math(g, wt, mt, vt)

    spec = pl.BlockSpec((None, tr, c), lambda i, j: (i, j, 0))
    return _ew(name, fn, (l, r // tr), [p_own, p_sib, w, m, v], [spec] * 5, [SDS(w.shape, f32)] * 4, [spec] * 4)


def _adamw_small(land, w, m, v):
    r = w.shape[0]
    tr = _row_tile(r, LANES)

    def fn(*t):
        g = t[0]
        for k in range(1, N_DEV):
            g = g + t[k]
        return (g,) + _adam_math(g, *t[N_DEV:])

    row = pl.BlockSpec((tr, LANES), lambda i: (i, 0))
    return _ew("adamw_small", fn, (r // tr,), [land] * N_DEV + [w, m, v],
               [pl.BlockSpec((None, tr, LANES), lambda i, k=k: (k, i, 0)) for k in range(N_DEV)] + [row] * 3,
               [SDS(w.shape, f32)] * 4, [row] * 4)


def _pack_small(d):
    return jnp.concatenate([d[k].reshape(DEPTH, -1) for k in SMALL], axis=1).reshape(DEPTH * SMALL_ROWS, LANES)


def _unpack_small(a, shapes):
    a = a.reshape(DEPTH, SMALL_ROWS * LANES)
    out, off = {}, 0
    for k, n in zip(SMALL, SMALL_SIZES):
        out[k] = a[:, off:off + n].reshape(shapes[k])
        off += n
    return out


def kernel(x, norm_mix, w_in, b_gate, qn_a, kn_a, qn_b, kn_b, pool_lin, pool_scale, w_branch_a, w_branch_b, w_branch_c, w_out, norm_ffn, w_ffn_gate, w_ffn_up, w_ffn_down, loss_target, m_norm_mix, m_w_in, m_b_gate, m_qn_a, m_kn_a, m_qn_b, m_kn_b, m_pool_lin, m_pool_scale, m_w_branch_a, m_w_branch_b, m_w_branch_c, m_w_out, m_norm_ffn, m_w_ffn_gate, m_w_ffn_up, m_w_ffn_down, v_norm_mix, v_w_in, v_b_gate, v_qn_a, v_kn_a, v_qn_b, v_kn_b, v_pool_lin, v_pool_scale, v_w_branch_a, v_w_branch_b, v_w_branch_c, v_w_out, v_norm_ffn, v_w_ffn_gate, v_w_ffn_up, v_w_ffn_down):
    args = dict(locals())
    s = x.shape[1]
    wts = {k: args[k] for k in WEIGHT_ORDER}
    mom = {k: args["m_" + k] for k in WEIGHT_ORDER}
    var = {k: args["v_" + k] for k in WEIGHT_ORDER}
    tabs = _rope_tables(s)

    shards16 = {k: _cast_bf16("cast_" + k, wts[k]) for k in BIG}
    whole = lambda ref, peer: ref.at[0]
    w_in_l = _chip_exchange("gather_w_in", [shards16["w_in"][0:1]], whole)[0]
    full = [None] * DEPTH

    def small_of(l):
        tile2 = lambda a: jnp.concatenate([a, a])
        gains = jnp.stack([tile2(qn_a[l]), tile2(kn_a[l]), tile2(qn_b[l]), tile2(kn_b[l])])
        return dict(norm_mix=norm_mix[l][None], norm_ffn=norm_ffn[l][None], b_gate=b_gate[l][None], gains=gains,
                    pool_lin=pool_lin[l], pool_scale=pool_scale[l][None])

    xs = x[0]
    saved = []
    hb = None
    for l in range(DEPTH):
        nxt = (l + 1) % DEPTH
        srcs = [shards16[k][l:l + 1] for k in REST] + [shards16["w_in"][nxt:nxt + 1]]
        next_gain = norm_mix[l + 1][None] if l + 1 < DEPTH else None
        xs, sv, full[l], hb = _layer_fwd(xs, dict(w_in=w_in_l), small_of(l), tabs, s, (srcs, whole), hb, next_gain)
        saved.append(sv)
        w_in_l = full[l]["next_w_in"]
    dy, loss_rows = _loss_grad(xs, loss_target[0], s)
    loss = lax.psum(jnp.sum(loss_rows), ("x", "y", "c"))

    part = {k: [None] * DEPTH for k in BIG}
    small_g = [None] * DEPTH
    to_chip = lambda ref, peer: ref.at[peer]
    assert GRAD_WIRE_DTYPE == bf16
    pending = [full[DEPTH - 1][k] for k in MIXER]
    for l in reversed(range(DEPTH)):
        dy, big, small_g[l], landed = _layer_bwd(dy, saved[l], full[l], small_of(l), tabs, s, pending)
        if l + 1 < DEPTH:
            for k, a in zip(MIXER, landed[:len(MIXER)]):
                part[k][l + 1] = _sum4("sum4_" + k, a)
        for k, a in zip(FFN, landed[len(MIXER):]):
            part[k][l] = _sum4("sum4_" + k, a)
        pending = [big[k] for k in MIXER]
    for k, a in zip(MIXER, _chip_exchange("grad_scatter", pending, to_chip)):
        part[k][0] = _sum4("sum4_" + k, a)

    p_own = [jnp.stack(part[k]) for k in BIG]
    p_sib = _sibling_exchange("grad_sibling", p_own)
    outs = {}
    for k, a, b in zip(BIG, p_own, p_sib):
        outs[k] = _adamw_big("adamw_" + k, a, b, wts[k], mom[k], var[k])

    land_s = _all_exchange("small_allgather", jnp.concatenate(small_g, axis=0))
    res_s = _adamw_small(land_s, _pack_small(wts), _pack_small(mom), _pack_small(var))
    shapes = {k: wts[k].shape for k in SMALL}
    small_out = [_unpack_small(r, shapes) for r in res_s]
    for k in SMALL:
        outs[k] = tuple(so[k] for so in small_out)

    flat = [loss, dy[None]]
    for idx in range(4):
        flat += [outs[k][idx] for k in WEIGHT_ORDER]
    return tuple(flat)
```

```python
import functools

import jax
import jax.numpy as jnp
from jax import lax
from jax.experimental import pallas as pl
from jax.experimental.pallas import tpu as pltpu

f32, bf16 = jnp.float32, jnp.bfloat16
SDS = jax.ShapeDtypeStruct

DEPTH = 4
D_MODEL = 1024
HEAD_DIM = 64
LANES = 128
GRID_W = 64
ROPE_THETA = 10000.0
EPS = 1e-6
NEG_INF = -1e30
SCALE = HEAD_DIM ** -0.5
B_GROUP_CFG = ((128, 1), (512, 4), (2048, 16))
POOL_WINDOWS = (2, 4, 8, 16)
N_CHIPS = 4
N_DEV = 8
C_QA, C_KA, C_VA, C_QB, C_KB, C_VB, C_UC, C_GZ = 0, 512, 640, 768, 1536, 2304, 3072, 3584
IN_WIDTH = 6656
FF_SHARD = 704
ADAM_LR, ADAM_B1, ADAM_B2, ADAM_EPS, ADAM_WD, ADAM_STEP = 0.001, 0.9, 0.999, 1e-08, 0.01, 10
VMEM_LIMIT = 56 * 1024 * 1024
MESH = pl.DeviceIdType.MESH

MIXER = ("w_in", "w_branch_a", "w_branch_b", "w_branch_c", "w_out")
FFN = ("w_ffn_gate", "w_ffn_up", "w_ffn_down")
BIG = MIXER + FFN
REST = BIG[1:]
SMALL = ("norm_mix", "b_gate", "qn_a", "kn_a", "qn_b", "kn_b", "pool_lin", "pool_scale", "norm_ffn")
SMALL_SIZES = (1024, 3072, 64, 64, 64, 64, 65536, 512, 1024)
SMALL_ROWS = sum(SMALL_SIZES) // LANES
WEIGHT_ORDER = ("norm_mix", "w_in", "b_gate", "qn_a", "kn_a", "qn_b", "kn_b", "pool_lin", "pool_scale",
                "w_branch_a", "w_branch_b", "w_branch_c", "w_out", "norm_ffn", "w_ffn_gate", "w_ffn_up", "w_ffn_down")


def _cparams():
    return pltpu.CompilerParams(vmem_limit_bytes=VMEM_LIMIT)


def _lo_mask(shape):
    return lax.broadcasted_iota(jnp.int32, shape, len(shape) - 1) < HEAD_DIM


def _dot(a, b, dims):
    dn = {"nn": (((1,), (0,)), ((), ())), "nt": (((1,), (1,)), ((), ())), "tn": (((0,), (0,)), ((), ()))}[dims]
    return lax.dot_general(a, b, dn, preferred_element_type=f32)


def _mm(name, ins, in_specs, out_shape, out_spec, grid, nk, dims, acc_shape, epilogue=None, n_extra=0, aliases=None):
    multi = isinstance(out_shape, (list, tuple))
    n_out = len(out_shape) if multi else 1

    def body(*refs):
        a_ref, b_ref = refs[0], refs[1]
        extra = refs[2:2 + n_extra]
        outs = refs[2 + n_extra:2 + n_extra + n_out]
        k = pl.program_id(len(grid) - 1)
        part = _dot(a_ref[...].astype(bf16), b_ref[...].astype(bf16), dims)

        def finish(acc):
            res = epilogue(acc, *[e[...] for e in extra]) if epilogue is not None else acc
            res = res if isinstance(res, (list, tuple)) else (res,)
            for o, r in zip(outs, res):
                o[...] = r.astype(o.dtype)

        if nk == 1:
            finish(part)
        else:
            acc_ref = refs[-1]

            @pl.when(k == 0)
            def _():
                acc_ref[...] = part

            @pl.when(k > 0)
            def _():
                acc_ref[...] += part

            @pl.when(k == nk - 1)
            def _():
                finish(acc_ref[...])

    return pl.pallas_call(
        body, name=name, grid=grid, in_specs=in_specs,
        out_specs=list(out_spec) if multi else out_spec,
        out_shape=list(out_shape) if multi else out_shape,
        scratch_shapes=[] if nk == 1 else [pltpu.VMEM(acc_shape, f32)],
        input_output_aliases=aliases or {}, compiler_params=_cparams())(*ins)


def _ew(name, fn, grid, ins, in_specs, out_shapes, out_specs, n_acc=0, aliases=None):
    n_in = len(ins)
    n_out = len(out_shapes) - n_acc
    n_alias = len(aliases or {})

    def body(*refs):
        in_refs = refs[:n_in - n_alias] if n_alias else refs[:n_in]
        out_refs = refs[n_in:n_in + n_out]
        acc_refs = refs[n_in + n_out:]
        i = pl.program_id(len(grid) - 1)
        res = fn(*[r[...] for r in in_refs])
        res = res if isinstance(res, (list, tuple)) else (res,)
        for o, r in zip(out_refs, res[:n_out]):
            o[...] = r.astype(o.dtype)
        for a, r in zip(acc_refs, res[n_out:]):
            @pl.when(i == 0)
            def _(a=a, r=r):
                a[...] = r.astype(a.dtype)

            @pl.when(i > 0)
            def _(a=a, r=r):
                a[...] += r.astype(a.dtype)

    return pl.pallas_call(body, name=name, grid=grid, in_specs=in_specs, out_specs=out_specs, out_shape=out_shapes,
                          input_output_aliases=aliases or {}, compiler_params=_cparams())(*ins)


EW_BLOCK_ELEMS = 128 * 1024


def _row_tile(rows, cols):
    step = 16 if rows % 16 == 0 else 8
    return max(t for t in range(step, rows + 1, step) if rows % t == 0 and (t * cols <= EW_BLOCK_ELEMS or t == step))


def _rope_tables(s):
    t = jnp.arange(s)
    inv_ax = ROPE_THETA ** (-jnp.arange(0, HEAD_DIM // 2, 2, dtype=f32) / (HEAD_DIM // 2))
    inv_sq = ROPE_THETA ** (-jnp.arange(0, HEAD_DIM, 2, dtype=f32) / HEAD_DIM)
    ang_row = (t // GRID_W).astype(f32)[:, None] * inv_ax[None, :]
    ang_col = (t % GRID_W).astype(f32)[:, None] * inv_ax[None, :]
    ang_seq = t.astype(f32)[:, None] * inv_sq[None, :]
    a_ax = jnp.concatenate([ang_row, ang_row, ang_col, ang_col], axis=1)
    sg_ax = jnp.concatenate([-jnp.ones(16), jnp.ones(16), -jnp.ones(16), jnp.ones(16)]).astype(f32)
    a_sq = jnp.concatenate([ang_seq, ang_seq], axis=1)
    sg_sq = jnp.concatenate([-jnp.ones(32), jnp.ones(32)]).astype(f32)
    two = lambda a: jnp.concatenate([a, a], axis=1)
    return (two(jnp.cos(a_ax)), two(jnp.sin(a_ax) * sg_ax), two(jnp.cos(a_sq)), two(jnp.sin(a_sq) * sg_sq))


def _head_stat(v, lo):
    r = lax.broadcasted_iota(jnp.int32, (LANES, LANES), 0) < HEAD_DIM
    c = lax.broadcasted_iota(jnp.int32, (LANES, LANES), 1) < HEAD_DIM
    same_head = (r == c).astype(f32)
    return jnp.dot(v, same_head, preferred_element_type=f32, precision=lax.Precision.HIGHEST)


def _partner(y, off):
    lane = lax.broadcasted_iota(jnp.int32, y.shape, 1)
    first = (lane & (2 * off - 1)) < off
    return jnp.where(first, pltpu.roll(y, LANES - off, 1), pltpu.roll(y, off, 1))


def _normrope(xc, g, cos, sin, off):
    lo = _lo_mask(xc.shape)
    r = lax.rsqrt(_head_stat(xc * xc, lo) * (1.0 / HEAD_DIM) + EPS)
    y = xc * r * g
    return y * cos + _partner(y, off) * sin


def _normrope_bwd(xc, g, cos, sin, off, drot):
    lo = _lo_mask(xc.shape)
    r = lax.rsqrt(_head_stat(xc * xc, lo) * (1.0 / HEAD_DIM) + EPS)
    n = xc * r
    dy = drot * cos + _partner(drot * sin, off)
    dg = jnp.sum(dy * n, axis=0, keepdims=True)
    dn = dy * g
    dx = r * (dn - n * (_head_stat(dn * n, lo) * (1.0 / HEAD_DIM)))
    return dx, dg


def _norm_fwd(x, g, s):
    ts = 256

    def fn(xt, gt):
        r = lax.rsqrt(jnp.mean(xt * xt, axis=1, keepdims=True) + EPS)
        return xt * r * gt

    return _ew("norm_fwd", fn, (s // ts,), [x, g],
               [pl.BlockSpec((ts, D_MODEL), lambda i: (i, 0)), pl.BlockSpec((1, D_MODEL), lambda i: (0, 0))],
               [SDS((s, D_MODEL), bf16)], [pl.BlockSpec((ts, D_MODEL), lambda i: (i, 0))])[0]


def _prep_fwd(z, tabs, gains, s):
    ts = 256
    cos_a, sin_a, cos_b, sin_b = tabs

    def body(za_ref, zq_ref, zk_ref, zv_ref, ca, sa, cb, sb, g_ref, qa_o, kd_o, vd_o, *grp_o):
        lo = _lo_mask((ts, LANES))
        g = g_ref[...]
        ca_, sa_, cb_, sb_ = ca[...], sa[...], cb[...], sb[...]
        for c in range(4):
            qa_o[:, c * LANES:(c + 1) * LANES] = _normrope(za_ref[:, c * LANES:(c + 1) * LANES], g[0:1], ca_, sa_, 16).astype(bf16)
        k = _normrope(za_ref[:, C_KA:C_KA + LANES], g[1:2], ca_, sa_, 16)
        kr = pltpu.roll(k, HEAD_DIM, 1)
        kd_o[0] = k.astype(bf16)
        kd_o[1] = kr.astype(bf16)
        v = za_ref[:, C_VA:C_VA + LANES]
        ones_col = (lax.broadcasted_iota(jnp.int32, v.shape, 1) == HEAD_DIM).astype(f32)
        vd_o[0] = jnp.where(lo, v, ones_col).astype(bf16)
        vd_o[1] = jnp.where(lo, pltpu.roll(v, HEAD_DIM, 1), ones_col).astype(bf16)
        for c in range(6):
            sl = slice(c * LANES, (c + 1) * LANES)
            gi, gsl = c // 2, slice((c % 2) * LANES, (c % 2 + 1) * LANES)
            grp_o[gi][:, gsl] = _normrope(zq_ref[:, sl], g[2:3], cb_, sb_, 32).astype(bf16)
            grp_o[3 + gi][:, gsl] = _normrope(zk_ref[:, sl], g[3:4], cb_, sb_, 32).astype(bf16)
            grp_o[6 + gi][:, gsl] = zv_ref[:, sl].astype(bf16)

    w = 768
    grp = pl.BlockSpec((ts, 256), lambda i: (i, 0))
    zspec = lambda cb: pl.BlockSpec((ts, w), lambda i: (i, cb))
    tab = pl.BlockSpec((ts, LANES), lambda i: (i, 0))
    dup = pl.BlockSpec((2, ts, LANES), lambda i: (0, i, 0))
    res = pl.pallas_call(
        body, name="prep_fwd", grid=(s // ts,),
        in_specs=[zspec(0), zspec(1), zspec(2), zspec(3), tab, tab, tab, tab, pl.BlockSpec((4, LANES), lambda i: (0, 0))],
        out_specs=[pl.BlockSpec((ts, 512), lambda i: (i, 0)), dup, dup] + [grp] * 9,
        out_shape=[SDS((s, 512), bf16), SDS((2, s, LANES), bf16), SDS((2, s, LANES), bf16)] + [SDS((s, 256), bf16)] * 9,
        compiler_params=_cparams())(z, z, z, z, cos_a, sin_a, cos_b, sin_b, gains)
    return res[0], res[1], res[2], res[3:6], res[6:9], res[9:12]


def _prep_bwd(z, tabs, gains, dqa, dkd, dvd, dqb, dkb, dvb, dz, s):
    ts = 256
    cos_a, sin_a, cos_b, sin_b = tabs

    def body(za_ref, zq_ref, zk_ref, ca, sa, cb, sb, g_ref, dqa_r, dkd_r, dvd_r,
             dq0, dq1, dq2, dk0, dk1, dk2, dv0, dv1, dv2, dz_in, dz_o, dg_o):
        i = pl.program_id(0)
        lo = _lo_mask((ts, LANES))
        g = g_ref[...]
        ca_, sa_, cb_, sb_ = ca[...], sa[...], cb[...], sb[...]
        dg = [jnp.zeros((1, LANES), f32) for _ in range(4)]
        for c in range(4):
            sl = slice(c * LANES, (c + 1) * LANES)
            dx, d = _normrope_bwd(za_ref[:, sl], g[0:1], ca_, sa_, 16, dqa_r[:, sl])
            dz_o[:, sl] = dx.astype(bf16)
            dg[0] += d
        dk = jnp.where(lo, dkd_r[0], pltpu.roll(dkd_r[1], HEAD_DIM, 1))
        dx, d = _normrope_bwd(za_ref[:, C_KA:C_KA + LANES], g[1:2], ca_, sa_, 16, dk)
        dz_o[:, C_KA:C_KA + LANES] = dx.astype(bf16)
        dg[1] += d
        dz_o[:, C_VA:C_VA + LANES] = jnp.where(lo, dvd_r[0], pltpu.roll(dvd_r[1], HEAD_DIM, 1)).astype(bf16)
        dqs, dks, dvs = (dq0, dq1, dq2), (dk0, dk1, dk2), (dv0, dv1, dv2)
        for c in range(6):
            sl = slice(c * LANES, (c + 1) * LANES)
            gsl = slice((c % 2) * LANES, (c % 2 + 1) * LANES)
            dx, d = _normrope_bwd(zq_ref[:, sl], g[2:3], cb_, sb_, 32, dqs[c // 2][:, gsl])
            dz_o[:, C_QB + c * LANES:C_QB + (c + 1) * LANES] = dx.astype(bf16)
            dg[2] += d
            dx, d = _normrope_bwd(zk_ref[:, sl], g[3:4], cb_, sb_, 32, dks[c // 2][:, gsl])
            dz_o[:, C_KB + c * LANES:C_KB + (c + 1) * LANES] = dx.astype(bf16)
            dg[3] += d
            dz_o[:, C_VB + c * LANES:C_VB + (c + 1) * LANES] = dvs[c // 2][:, gsl].astype(bf16)
        dgs = jnp.concatenate(dg + [jnp.zeros((4, LANES), f32)], axis=0)

        @pl.when(i == 0)
        def _():
            dg_o[...] = dgs

        @pl.when(i > 0)
        def _():
            dg_o[...] += dgs

    w = 768
    zspec = lambda cb: pl.BlockSpec((ts, w), lambda i: (i, cb))
    tab = pl.BlockSpec((ts, LANES), lambda i: (i, 0))
    dup = pl.BlockSpec((2, ts, LANES), lambda i: (0, i, 0))
    grp = pl.BlockSpec((ts, 256), lambda i: (i, 0))
    dzo, dgo = pl.pallas_call(
        body, name="prep_bwd", grid=(s // ts,),
        in_specs=[zspec(0), zspec(1), zspec(2), tab, tab, tab, tab, pl.BlockSpec((4, LANES), lambda i: (0, 0)),
                  pl.BlockSpec((ts, 512), lambda i: (i, 0)), dup, dup] + [grp] * 9 + [pl.BlockSpec(memory_space=pl.ANY)],
        out_specs=[pl.BlockSpec((ts, C_UC), lambda i: (i, 0)), pl.BlockSpec((8, LANES), lambda i: (0, 0))],
        out_shape=[SDS((s, IN_WIDTH), bf16), SDS((8, LANES), f32)],
        input_output_aliases={20: 0}, compiler_params=_cparams())(
            z, z, z, cos_a, sin_a, cos_b, sin_b, gains, dqa, dkd, dvd, *dqb, *dkb, *dvb, dz)
    return dzo, dgo


def _stack_heads(x, lo, dtype):
    z = jnp.zeros_like(x)
    return jnp.concatenate([jnp.where(lo, x, z), jnp.where(lo, z, x)], axis=0).astype(dtype)


def _rows_of(v, lo, kind):
    if kind == "max":
        a = jnp.max(jnp.where(lo, v, NEG_INF * 10), axis=1, keepdims=True)
        b = jnp.max(jnp.where(lo, NEG_INF * 10, v), axis=1, keepdims=True)
    else:
        a = jnp.sum(jnp.where(lo, v, 0.0), axis=1, keepdims=True) * (1.0 / HEAD_DIM)
        b = jnp.sum(jnp.where(lo, 0.0, v), axis=1, keepdims=True) * (1.0 / HEAD_DIM)
    return jnp.concatenate([a, b], axis=0)


def _stack_low(x, lo, dtype, scale=None):
    x = x.astype(f32)
    if scale is not None:
        x = x * scale
    z = jnp.zeros_like(x)
    return jnp.concatenate([jnp.where(lo, x, z), jnp.where(lo, pltpu.roll(x, HEAD_DIM, 1), z)], axis=0).astype(dtype)


def _unstack_low(xs, lo, rows):
    return jnp.where(lo, xs[:rows], pltpu.roll(xs[rows:], HEAD_DIM, 1))


FA_TQ, FA_TK = 256, 1024


def _host_exchange(body, n_in, n_out, grid, xchg):
    if xchg is None:
        return body, [], [], [], []
    srcs, pick_src = xchg
    n = len(srcs)
    any_spec = pl.BlockSpec(memory_space=pl.ANY)

    def hosted(*refs):
        ins, src = refs[:n_in], refs[n_in:n_in + n]
        outs, dst = refs[n_in + n:n_in + n + n_out], refs[n_in + n + n_out:n_in + 2 * n + n_out]
        scratch, sems = refs[n_in + 2 * n + n_out:-3], refs[-3:]
        ids = [pl.program_id(a) for a in range(len(grid))]
        first = functools.reduce(jnp.logical_and, [i == 0 for i in ids])
        last = functools.reduce(jnp.logical_and, [i == g - 1 for i, g in zip(ids, grid)])

        @pl.when(first)
        def _():
            _chip_start(_chip_copies(src, dst, pick_src, *sems, receiving=False))

        body(*ins, *outs, *scratch)

        @pl.when(last)
        def _():
            _chip_wait(_chip_copies(src, dst, pick_src, *sems))

    return hosted, [any_spec] * n, [any_spec] * n, _chip_landing(srcs), _chip_sems(n)


def _flash_a_fwd(qa, kd, vd, s, xchg=None):
    tq, tk = FA_TQ, min(4 * FA_TK, s)
    nk = s // tk

    def body(q_ref, k_ref, v_ref, o_ref, lse_ref, qs, m_s, acc):
        lo = _lo_mask((tq, LANES))
        for pp in range(2):
            qs[pp] = _stack_low(q_ref[:, pp * LANES:(pp + 1) * LANES], lo, bf16, SCALE)
        m_s[...] = jnp.full(m_s.shape, NEG_INF, f32)
        acc[...] = jnp.zeros(acc.shape, f32)

        def chunk(j, carry):
            rows = pl.ds(pl.multiple_of(j * tk, tk), tk)
            k, v = k_ref[rows, :], v_ref[rows, :]
            for pp in range(2):
                sc = _dot(qs[pp], k, "nt")
                m_prev = m_s[pp]
                m_new = jnp.maximum(m_prev, jnp.max(sc, axis=1, keepdims=True))
                p = jnp.exp(sc - m_new).astype(bf16)
                acc[pp] = acc[pp] * jnp.exp(m_prev - m_new) + _dot(p, v, "nn")
                m_s[pp] = m_new
            return carry

        lax.fori_loop(0, nk, chunk, 0)
        for pp in range(2):
            sl = slice(pp * LANES, (pp + 1) * LANES)
            a = acc[pp]
            lane = lax.broadcasted_iota(jnp.int32, a.shape, 1)
            l = jnp.sum(jnp.where(lane == HEAD_DIM, a, 0.0), axis=1, keepdims=True)
            o_ref[:, sl] = _unstack_low(a / l, lo, tq).astype(bf16)
            lse = m_s[pp] + jnp.log(l)
            lse_ref[:, sl] = jnp.where(lo, lse[:tq], lse[tq:])

    kv = pl.BlockSpec((None, s, LANES), lambda c, i: (c, 0, 0))
    qo = pl.BlockSpec((tq, 2 * LANES), lambda c, i: (i, c))
    grid = (2, s // tq)
    hosted, xi, xo, xs, xsem = _host_exchange(body, 3, 2, grid, xchg)
    res = pl.pallas_call(
        hosted, name="flash_a_fwd" + ("_x" if xchg else ""), grid=grid, in_specs=[qo, kv, kv] + xi, out_specs=[qo, qo] + xo,
        out_shape=[SDS((s, 512), bf16), SDS((s, 512), f32)] + xs,
        scratch_shapes=[pltpu.VMEM((2, 2 * tq, LANES), bf16), pltpu.VMEM((2, 2 * tq, 1), f32),
                        pltpu.VMEM((2, 2 * tq, LANES), f32)] + xsem,
        compiler_params=_cparams())(qa, kd, vd, *(xchg[0] if xchg else []))
    return res[:2], res[2:]


def _flash_a_bwd(qa, kd, vd, oa, lse, doa, s, xchg=None):
    tq, tk = FA_TQ, 2 * FA_TK
    nk = s // tk

    def body(q_ref, do_ref, o_ref, lse_ref, k_ref, v_ref, dq_ref, dk_ref, dv_ref, qs, dos, lse_s, dl_s, dq_s):
        i = pl.program_id(1)
        lo = _lo_mask((tq, LANES))

        @pl.when(i == 0)
        def _():
            dk_ref[...] = jnp.zeros(dk_ref.shape, f32)
            dv_ref[...] = jnp.zeros(dv_ref.shape, f32)

        for pp in range(2):
            sl = slice(pp * LANES, (pp + 1) * LANES)
            do = do_ref[:, sl]
            qs[...] = _stack_low(q_ref[:, sl], lo, bf16, SCALE)
            dos[...] = _stack_low(do, lo, bf16)
            dl_s[...] = _rows_of(do * o_ref[:, sl].astype(f32), lo, "sum") * HEAD_DIM
            lse_s[...] = _rows_of(lse_ref[:, sl], lo, "max")
            dq_s[...] = jnp.zeros(dq_s.shape, f32)

            def chunk(j, carry):
                rows = pl.ds(pl.multiple_of(j * tk, tk), tk)
                k, v = k_ref[rows, :], v_ref[rows, :]
                q_, do_ = qs[...], dos[...]
                p = jnp.exp(_dot(q_, k, "nt") - lse_s[...])
                ds = (p * (_dot(do_, v, "nt") - dl_s[...])).astype(bf16)
                dv_ref[rows, :] += _dot(p.astype(bf16), do_, "tn")
                dk_ref[rows, :] += _dot(ds, q_, "tn")
                dq_s[...] += _dot(ds, k, "nn")
                return carry

            lax.fori_loop(0, nk, chunk, 0)
            dq_ref[:, sl] = _unstack_low(dq_s[...], lo, tq) * SCALE

    grp = lambda: pl.BlockSpec((tq, 2 * LANES), lambda c, i: (i, c))
    kv = lambda: pl.BlockSpec((None, s, LANES), lambda c, i: (c, 0, 0))
    grid = (2, s // tq)
    hosted, xi, xo, xs, xsem = _host_exchange(body, 6, 3, grid, xchg)
    res = pl.pallas_call(
        hosted, name="flash_a_bwd" + ("_x" if xchg else ""), grid=grid,
        in_specs=[grp(), grp(), grp(), grp(), kv(), kv()] + xi, out_specs=[grp(), kv(), kv()] + xo,
        out_shape=[SDS((s, 512), f32), SDS((2, s, LANES), f32), SDS((2, s, LANES), f32)] + xs,
        scratch_shapes=[pltpu.VMEM((2 * tq, LANES), bf16), pltpu.VMEM((2 * tq, LANES), bf16), pltpu.VMEM((2 * tq, 1), f32),
                        pltpu.VMEM((2 * tq, 1), f32), pltpu.VMEM((2 * tq, LANES), f32)] + xsem,
        compiler_params=_cparams())(qa, doa, oa, lse, kd, vd, *(xchg[0] if xchg else []))
    return res[:3], res[3:]


BH = 128
BQ_MAX = 512


class _BandGeom:
    def __init__(self, gi, s):
        self.d = B_GROUP_CFG[gi][1]
        self.l_sub = s // self.d
        self.bq = min(BQ_MAX, self.l_sub)
        self.nb = self.l_sub // self.bq
        self.grid = (self.d, self.nb, 2)
        per, last = self.bq // BH, self.l_sub // BH - 1
        self.out_col = lambda r, pp: r * 2 + pp
        self.in_col = self.out_col
        self.cur = lambda col: pl.BlockSpec((self.bq, LANES), lambda r, n, pp: (n, col(r, pp)))
        self.prev = lambda col: pl.BlockSpec((BH, LANES), lambda r, n, pp: (jnp.maximum(n * per - 1, 0), col(r, pp)))
        self.next = lambda col: pl.BlockSpec((BH, LANES), lambda r, n, pp: (jnp.minimum((n + 1) * per, last), col(r, pp)))
        self.window = lambda col: [self.prev(col), self.cur(col), self.next(col)]
        self.view_out = lambda a: a.reshape(self.l_sub, self.d * 256)
        self.view_in = self.view_out

    def fill_band_bias(self, bias_s, q_rows, q_off, k_off):
        @pl.when((pl.program_id(0) == 0) & (pl.program_id(1) == 0) & (pl.program_id(2) == 0))
        def _():
            r = lax.broadcasted_iota(jnp.int32, bias_s.shape, 0)
            q = jnp.where(r >= q_rows, r - q_rows, r) + q_off
            k = lax.broadcasted_iota(jnp.int32, bias_s.shape, 1) + k_off
            bias_s[...] = jnp.where(jnp.abs(q - k) <= 64, 0.0, NEG_INF)

    def edge_bias(self, first_pos, n, stacked_rows=False):
        if stacked_rows:
            r = lax.broadcasted_iota(jnp.int32, (2 * n, 1), 0)
            idx = jnp.where(r >= n, r - n, r)
        else:
            idx = lax.broadcasted_iota(jnp.int32, (1, n), 1)
        pos = first_pos + idx
        return jnp.where((pos >= 0) & (pos < self.l_sub), 0.0, NEG_INF)


def _cat3(a, b, c):
    return jnp.concatenate([a[...], b[...], c[...]], axis=0)


def _attn_b_fwd(qb, kb, vb, gi, s):
    g = _BandGeom(gi, s)
    bq = g.bq

    def body(q_ref, kp, kc, kn, vp, vc, vn, o_ref, lse_ref, bias_s):
        n = pl.program_id(1)
        g.fill_band_bias(bias_s, bq, 0, -BH)
        lo = _lo_mask((bq, LANES))
        qs = _stack_heads(q_ref[...] * SCALE, lo, bf16)
        k, v = _cat3(kp, kc, kn), _cat3(vp, vc, vn)
        sc = _dot(qs, k, "nt") + bias_s[...] + g.edge_bias(n * bq - BH, bq + 2 * BH)
        m = jnp.max(sc, axis=1, keepdims=True)
        p = jnp.exp(sc - m)
        den = jnp.sum(p, axis=1, keepdims=True)
        o = _dot(p.astype(bf16), v, "nn") / den
        o_ref[...] = jnp.where(lo, o[:bq], o[bq:])
        lse = m + jnp.log(den)
        lse_ref[...] = jnp.where(lo, lse[:bq], lse[bq:])

    out = g.cur(g.out_col)
    qv, kv, vv = g.view_in(qb), g.view_in(kb), g.view_in(vb)
    o, lse = pl.pallas_call(
        body, name=f"attn_b_fwd{gi}", grid=g.grid,
        in_specs=[g.cur(g.in_col)] + g.window(g.in_col) * 2, out_specs=[out, out],
        out_shape=[SDS((g.l_sub, g.d * 256), f32), SDS((g.l_sub, g.d * 256), f32)],
        scratch_shapes=[pltpu.VMEM((2 * bq, bq + 2 * BH), f32)],
        compiler_params=_cparams())(qv, kv, kv, kv, vv, vv, vv)
    return o.reshape(s, 256), lse.reshape(s, 256)


def _attn_b_dq(qb, kb, vb, dob, lse, delta, gi, s):
    g = _BandGeom(gi, s)
    bq = g.bq

    def body(q_ref, kp, kc, kn, vp, vc, vn, do_ref, lse_ref, dl_ref, dq_ref, bias_s):
        n = pl.program_id(1)
        g.fill_band_bias(bias_s, bq, 0, -BH)
        lo = _lo_mask((bq, LANES))
        qs = _stack_heads(q_ref[...] * SCALE, lo, bf16)
        dos = _stack_heads(do_ref[...], lo, bf16)
        k, v = _cat3(kp, kc, kn), _cat3(vp, vc, vn)
        sc = _dot(qs, k, "nt") + bias_s[...] + g.edge_bias(n * bq - BH, bq + 2 * BH)
        p = jnp.exp(sc - _rows_of(lse_ref[...], lo, "max"))
        dp = _dot(dos, v, "nt")
        ds = (p * (dp - _rows_of(dl_ref[...], lo, "sum"))).astype(bf16)
        dqs = _dot(ds, k, "nn")
        dq_ref[...] = jnp.where(lo, dqs[:bq], dqs[bq:]) * SCALE

    out = g.cur(g.out_col)
    qv, kv, vv = g.view_in(qb), g.view_in(kb), g.view_in(vb)
    dq = pl.pallas_call(
        body, name=f"attn_b_dq{gi}", grid=g.grid,
        in_specs=[g.cur(g.in_col)] + g.window(g.in_col) * 2 + [out, out, out], out_specs=out,
        out_shape=SDS((g.l_sub, g.d * 256), f32), scratch_shapes=[pltpu.VMEM((2 * bq, bq + 2 * BH), f32)],
        compiler_params=_cparams())(qv, kv, kv, kv, vv, vv, vv, g.view_out(dob), g.view_out(lse), g.view_out(delta))
    return dq.reshape(s, 256)


def _attn_b_dkv(qb, kb, vb, dob, lse, delta, gi, s):
    g = _BandGeom(gi, s)
    bq = g.bq
    nq = bq + 2 * BH

    def body(k_ref, v_ref, qp, qc, qn, dop, doc, don, lp, lc, ln, dp_, dc_, dn_, dk_ref, dv_ref, bias_s):
        m = pl.program_id(1)
        g.fill_band_bias(bias_s, nq, -BH, 0)
        lo = _lo_mask((nq, LANES))
        qs = _stack_heads(_cat3(qp, qc, qn) * SCALE, lo, bf16)
        dos = _stack_heads(_cat3(dop, doc, don), lo, bf16)
        lses = _rows_of(_cat3(lp, lc, ln), lo, "max")
        dls = _rows_of(_cat3(dp_, dc_, dn_), lo, "sum")
        k, v = k_ref[...], v_ref[...]
        sc = _dot(qs, k, "nt") + bias_s[...] + g.edge_bias(m * bq - BH, nq, stacked_rows=True)
        p = jnp.exp(sc - lses)
        dpv = _dot(dos, v, "nt")
        ds = (p * (dpv - dls)).astype(bf16)
        dv_ref[...] = _dot(p.astype(bf16), dos, "tn")
        dk_ref[...] = _dot(ds, qs, "tn")

    cur = g.cur(g.out_col)
    qv, kv, vv = g.view_in(qb), g.view_in(kb), g.view_in(vb)
    dk, dv = pl.pallas_call(
        body, name=f"attn_b_dkv{gi}", grid=g.grid,
        in_specs=[g.cur(g.in_col)] * 2 + g.window(g.in_col) + g.window(g.out_col) * 3, out_specs=[cur, cur],
        out_shape=[SDS((g.l_sub, g.d * 256), f32), SDS((g.l_sub, g.d * 256), f32)],
        scratch_shapes=[pltpu.VMEM((2 * nq, bq), f32)],
        compiler_params=_cparams())(kv, vv, qv, qv, qv, *([g.view_out(dob)] * 3), *([g.view_out(lse)] * 3),
                                    *([g.view_out(delta)] * 3))
    return dk.reshape(s, 256), dv.reshape(s, 256)


def _merge_b(o_list, lse_list, s):
    ts = 256

    def fn(o0, o1, o2, l0, l1, l2):
        m = jnp.maximum(jnp.maximum(l0, l1), l2)
        w0, w1, w2 = jnp.exp(l0 - m), jnp.exp(l1 - m), jnp.exp(l2 - m)
        den = w0 + w1 + w2
        return (w0 * o0 + w1 * o1 + w2 * o2) / den, m + jnp.log(den)

    row = pl.BlockSpec((ts, 256), lambda i: (i, 0))
    return _ew("merge_b", fn, (s // ts,), [*o_list, *lse_list], [row] * 6,
               [SDS((s, 256), bf16), SDS((s, 256), f32)], [row, row])


def _delta_b(dob, ob, s):
    ts = 256

    def fn(do, o):
        prod = do * o.astype(f32)
        lo = _lo_mask((ts, LANES))
        return jnp.concatenate([_head_stat(prod[:, :LANES], lo), _head_stat(prod[:, LANES:], lo)], axis=1)

    row = pl.BlockSpec((ts, 256), lambda i: (i, 0))
    return _ew("delta_b", fn, (s // ts,), [dob, ob], [row, row], [SDS((s, 256), f32)], [row])[0]


def _band(t0, u0, w, nt, nu, s, transposed):
    t = t0 + lax.broadcasted_iota(jnp.int32, (nt, nu), 0)
    u = u0 + lax.broadcasted_iota(jnp.int32, (nt, nu), 1)
    if transposed:
        lo, hi = jnp.clip(t - w // 2 + 1, 0, s), jnp.clip(t + w // 2 + 1, 0, s)
    else:
        lo, hi = jnp.clip(t - w // 2, 0, s), jnp.clip(t + w - w // 2, 0, s)
    return ((u >= lo) & (u < hi)).astype(f32)


def _pool_cnt(t0, w, nt, s):
    t = t0 + lax.broadcasted_iota(jnp.int32, (nt, 1), 0)
    return (jnp.clip(t + w - w // 2, 0, s) - jnp.clip(t - w // 2, 0, s)).astype(f32)


POOL_HALO = 8


def _halo_specs(tp, s, cb):
    per, last = tp // POOL_HALO, s // POOL_HALO - 1
    return [pl.BlockSpec((POOL_HALO, 512), lambda i: (jnp.maximum(i * per - 1, 0), cb)),
            pl.BlockSpec((tp, 512), lambda i: (i, cb)),
            pl.BlockSpec((POOL_HALO, 512), lambda i: (jnp.minimum((i + 1) * per, last), cb))]


def _pool_fwd(z, lin, scale, s):
    tp = 256
    nt = s // tp

    def body(up, uc, un, lin_ref, sc_ref, pooled_o, pc_o):
        i = pl.program_id(0)
        ext = jnp.concatenate([up[...], uc[...], un[...]], axis=0)
        for g, w in enumerate(POOL_WINDOWS):
            sl = slice(g * LANES, (g + 1) * LANES)
            band = _band(i * tp, i * tp - POOL_HALO, w, tp, tp + 2 * POOL_HALO, s, False)
            sm = jnp.dot(band, ext[:, sl], preferred_element_type=f32, precision=lax.Precision.HIGHEST)
            pooled = (sm / _pool_cnt(i * tp, w, tp, s) - uc[:, sl]).astype(bf16)
            pooled_o[:, sl] = pooled
            mixed = _dot(pooled, lin_ref[g].astype(bf16), "nn")
            pc_o[:, sl] = (mixed * sc_ref[:, sl]).astype(bf16)

    row = pl.BlockSpec((tp, 512), lambda i: (i, 0))
    return pl.pallas_call(
        body, name="pool_fwd", grid=(nt,),
        in_specs=_halo_specs(tp, s, C_UC // 512)
        + [pl.BlockSpec((4, LANES, LANES), lambda i: (0, 0, 0)), pl.BlockSpec((1, 512), lambda i: (0, 0))],
        out_specs=[row, row], out_shape=[SDS((s, 512), bf16), SDS((s, 512), bf16)],
        compiler_params=_cparams())(z, z, z, lin, scale)


def _pool_bwd1(dpc, pooled, lin, scale, s):
    tp = 256

    def body(dpc_ref, pooled_ref, lin_ref, sc_ref, dpn_o, dlin_o, dsc_o):
        i = pl.program_id(0)
        dsc = []
        for g, w in enumerate(POOL_WINDOWS):
            sl = slice(g * LANES, (g + 1) * LANES)
            pooled = pooled_ref[:, sl]
            linb = lin_ref[g].astype(bf16)
            mixed = _dot(pooled, linb, "nn")
            dpc_g = dpc_ref[:, sl]
            dsc.append(jnp.sum(dpc_g * mixed, axis=0, keepdims=True))
            dmixed = (dpc_g * sc_ref[:, sl]).astype(bf16)
            dpn_o[:, sl] = _dot(dmixed, linb, "nt") / _pool_cnt(i * tp, w, tp, s)
            dl = _dot(pooled, dmixed, "tn")

            @pl.when(i == 0)
            def _(g=g, dl=dl):
                dlin_o[g] = dl

            @pl.when(i > 0)
            def _(g=g, dl=dl):
                dlin_o[g] += dl

        dsc = jnp.concatenate(dsc, axis=1)

        @pl.when(i == 0)
        def _():
            dsc_o[...] = dsc

        @pl.when(i > 0)
        def _():
            dsc_o[...] += dsc

    row = pl.BlockSpec((tp, 512), lambda i: (i, 0))
    linspec = pl.BlockSpec((4, LANES, LANES), lambda i: (0, 0, 0))
    one = pl.BlockSpec((1, 512), lambda i: (0, 0))
    return pl.pallas_call(
        body, name="pool_bwd1", grid=(s // tp,), in_specs=[row, row, linspec, one], out_specs=[row, linspec, one],
        out_shape=[SDS((s, 512), f32), SDS((4, LANES, LANES), f32), SDS((1, 512), f32)],
        compiler_params=_cparams())(dpc, pooled, lin, scale)


def _pool_bwd2(dpn, dz, s):
    tp = 256
    nt = s // tp

    def body(dp, dc, dn, dz_in, dz_o):
        i = pl.program_id(0)
        ext = jnp.concatenate([dp[...], dc[...], dn[...]], axis=0)
        for g, w in enumerate(POOL_WINDOWS):
            sl = slice(g * LANES, (g + 1) * LANES)
            band = _band(i * tp, i * tp - POOL_HALO, w, tp, tp + 2 * POOL_HALO, s, True)
            sm = jnp.dot(band, ext[:, sl], preferred_element_type=f32, precision=lax.Precision.HIGHEST)
            dz_o[:, sl] = (sm - dc[:, sl] * _pool_cnt(i * tp, w, tp, s)).astype(bf16)

    return pl.pallas_call(
        body, name="pool_bwd2", grid=(nt,),
        in_specs=_halo_specs(tp, s, 0) + [pl.BlockSpec(memory_space=pl.ANY)],
        out_specs=pl.BlockSpec((tp, 512), lambda i: (i, C_UC // 512)), out_shape=SDS((s, IN_WIDTH), bf16),
        input_output_aliases={3: 0}, compiler_params=_cparams())(dpn, dpn, dpn, dz)


GW = 512


MIX_TM = 256
BR_WIDTHS = (512, 256, 512)
N_GATE_BLOCKS = 3 * D_MODEL // GW


def _gate_specs(ts):
    return [pl.BlockSpec((ts, GW), lambda i, q=q: (i, C_GZ // GW + q)) for q in range(N_GATE_BLOCKS)]


def _rms(xt, gt):
    return xt * lax.rsqrt(jnp.mean(xt * xt, axis=1, keepdims=True) + EPS) * gt


def _mix_out_fwd(acts, z, b_gate, x, norm_gain, w, s):
    ts = MIX_TM

    def body(oa, ob, pc, *rest):
        gz = rest[:N_GATE_BLOCKS]
        bias, x_ref, gain, wa, wb, wc, wo, y3_o, m_o, x1_o, h2_o, y_s = rest[N_GATE_BLOCKS:]
        for b, (act, wt) in enumerate(((oa, wa), (ob, wb), (pc, wc))):
            a = act[...]
            for j in range(N_CHIPS):
                y_s[b, :, j * 256:(j + 1) * 256] = _dot(a, wt[j], "nn")
        y3_o[...] = y_s[...].astype(bf16)
        for h in range(2):
            hs = slice(h * GW, (h + 1) * GW)
            tot = jnp.zeros((ts, GW), f32)
            for b in range(3):
                g = jax.nn.sigmoid(gz[2 * b + h][...] + bias[:, b * D_MODEL + h * GW:b * D_MODEL + (h + 1) * GW])
                tot += g * y_s[b, :, hs]
            m_o[:, hs] = tot.astype(bf16)
        acc = x_ref[...]
        for j in range(N_CHIPS):
            acc += _dot(m_o[:, j * 256:(j + 1) * 256], wo[j], "nn")
        x1_o[...] = acc
        h2_o[...] = _rms(acc, gain[...]).astype(bf16)

    row = lambda width: pl.BlockSpec((ts, width), lambda i: (i, 0))
    res3 = lambda a: pl.BlockSpec(a.shape, lambda i: (0, 0, 0))
    wts = [w["w_branch_a"], w["w_branch_b"], w["w_branch_c"], w["w_out"]]
    return pl.pallas_call(
        body, name="mix_out_fwd", grid=(s // ts,),
        in_specs=[row(BR_WIDTHS[0]), row(BR_WIDTHS[1]), row(BR_WIDTHS[2])] + _gate_specs(ts)
        + [pl.BlockSpec((1, 3 * D_MODEL), lambda i: (0, 0)), row(D_MODEL), pl.BlockSpec((1, D_MODEL), lambda i: (0, 0))]
        + [res3(a) for a in wts],
        out_specs=[pl.BlockSpec((3, ts, D_MODEL), lambda i: (0, i, 0)), row(D_MODEL), row(D_MODEL), row(D_MODEL)],
        out_shape=[SDS((3, s, D_MODEL), bf16), SDS((s, D_MODEL), bf16), SDS((s, D_MODEL), f32), SDS((s, D_MODEL), bf16)],
        scratch_shapes=[pltpu.VMEM((3, ts, D_MODEL), f32)],
        compiler_params=_cparams())(*acts, *([z] * N_GATE_BLOCKS), b_gate, x, norm_gain, *wts)


def _mix_out_bwd(dx1, y3, z, b_gate, w, s):
    ts = MIX_TM

    def body(dx_ref, y3_ref, *rest):
        gz = rest[:N_GATE_BLOCKS]
        bias, wa, wb, wc, wo, dy_o, dz_o, db_o, doa_o, dob_o, dpc_o, dm_s = rest[N_GATE_BLOCKS:]
        i = pl.program_id(0)
        dx = dx_ref[...].astype(bf16)
        for j in range(N_CHIPS):
            dm_s[:, j * 256:(j + 1) * 256] = _dot(dx, wo[j], "nt")
        dz_o[:, 0:C_GZ] = jnp.zeros((ts, C_GZ), bf16)
        dbs = []
        for b in range(3):
            for h in range(2):
                hs = slice(h * GW, (h + 1) * GW)
                g = jax.nn.sigmoid(gz[2 * b + h][...] + bias[:, b * D_MODEL + h * GW:b * D_MODEL + (h + 1) * GW])
                dm = dm_s[:, hs]
                dy_o[b, :, hs] = (dm * g).astype(bf16)
                dgz = dm * y3_ref[b, :, hs].astype(f32) * g * (1.0 - g)
                c0 = C_GZ + b * D_MODEL + h * GW
                dz_o[:, c0:c0 + GW] = dgz.astype(bf16)
                dbs.append(jnp.sum(dgz, axis=0, keepdims=True))
        db = jnp.concatenate(dbs, axis=1)

        @pl.when(i == 0)
        def _():
            db_o[...] = db

        @pl.when(i > 0)
        def _():
            db_o[...] += db

        for b, (wt, out) in enumerate(((wa, doa_o), (wb, dob_o), (wc, dpc_o))):
            acc = _dot(dy_o[b, :, 0:256], wt[0], "nt")
            for j in range(1, N_CHIPS):
                acc += _dot(dy_o[b, :, j * 256:(j + 1) * 256], wt[j], "nt")
            out[...] = acc

    row = lambda width: pl.BlockSpec((ts, width), lambda i: (i, 0))
    res3 = lambda a: pl.BlockSpec(a.shape, lambda i: (0, 0, 0))
    blk3 = pl.BlockSpec((3, ts, D_MODEL), lambda i: (0, i, 0))
    one = pl.BlockSpec((1, 3 * D_MODEL), lambda i: (0, 0))
    wts = [w["w_branch_a"], w["w_branch_b"], w["w_branch_c"], w["w_out"]]
    return pl.pallas_call(
        body, name="mix_out_bwd", grid=(s // ts,),
        in_specs=[row(D_MODEL), blk3] + _gate_specs(ts) + [one] + [res3(a) for a in wts],
        out_specs=[blk3, row(IN_WIDTH), one, row(BR_WIDTHS[0]), row(BR_WIDTHS[1]), row(BR_WIDTHS[2])],
        out_shape=[SDS((3, s, D_MODEL), bf16), SDS((s, IN_WIDTH), bf16), SDS((1, 3 * D_MODEL), f32),
                   SDS((s, BR_WIDTHS[0]), f32), SDS((s, BR_WIDTHS[1]), f32), SDS((s, BR_WIDTHS[2]), f32)],
        scratch_shapes=[pltpu.VMEM((ts, D_MODEL), f32)],
        compiler_params=_cparams())(dx1, y3, *([z] * N_GATE_BLOCKS), b_gate, *wts)


def _loss_grad(y, tgt, s):
    ts = 256

    def fn(yt, tt):
        e = yt - tt
        return e * (1.0 / D_MODEL), jnp.sum(e * e, axis=0, keepdims=True) * (0.5 / D_MODEL)

    row = pl.BlockSpec((ts, D_MODEL), lambda i: (i, 0))
    one = pl.BlockSpec((1, D_MODEL), lambda i: (0, 0))
    return _ew("loss_grad", fn, (s // ts,), [y, tgt], [row, row], [SDS((s, D_MODEL), f32), SDS((1, D_MODEL), f32)],
               [row, one], n_acc=1)


TM = 512


def _layer_fwd(x, w, sm, tabs, s, xchg=None, hb=None, next_gain=None):
    nm = s // TM
    if hb is None:
        hb = _norm_fwd(x, sm["norm_mix"], s)
    z = _mm("mm_z", [hb, w["w_in"]],
            [pl.BlockSpec((TM, D_MODEL), lambda j, i, k: (i, 0)), pl.BlockSpec((None, D_MODEL, 1664), lambda j, i, k: (j, 0, 0))],
            SDS((s, IN_WIDTH), f32), pl.BlockSpec((TM, 1664), lambda j, i, k: (i, j)), (4, nm, 1), 1, "nn", None)
    qa, kd, vd, qb, kb, vb = _prep_fwd(z, tabs, sm["gains"], s)
    (oa, lse_a), landed = _flash_a_fwd(qa, kd, vd, s, xchg)
    if xchg is not None:
        w = dict(w, **dict(zip(REST + ("next_w_in",), landed)))
    ob_parts = [_attn_b_fwd(qb[gi], kb[gi], vb[gi], gi, s) for gi in range(3)]
    ob, lse_b = _merge_b([p[0] for p in ob_parts], [p[1] for p in ob_parts], s)
    pooled, pc = _pool_fwd(z, sm["pool_lin"], sm["pool_scale"], s)

    y3, merged, x1, h2 = _mix_out_fwd([oa, ob, pc], z, sm["b_gate"], x, sm["norm_ffn"], w, s)
    g4, u4, a4 = _ffn_up(h2, w["w_ffn_gate"], w["w_ffn_up"], s)
    x2 = _mm_shard_sum("mm_down", [(a4, w["w_ffn_down"])], "nn", D_MODEL, s, TM, add=x1, norm_gain=next_gain)
    x2, hb_next = x2 if next_gain is not None else (x2, None)
    saved = dict(x=x, hb=hb, z=z, qa=qa, kd=kd, vd=vd, qb=qb, kb=kb, vb=vb, oa=oa, lse_a=lse_a, ob=ob, lse_b=lse_b,
                 pooled=pooled, pc=pc, y3=y3, merged=merged, x1=x1, h2=h2, g4=g4, u4=u4, a4=a4)
    return x2, saved, w, hb_next


def _ffn_up(h2, wg, wu, s):
    nm = s // TM

    def body(h_ref, wg_ref, wu_ref, g_o, u_o, a_o):
        h = h_ref[...]
        g = _dot(h, wg_ref[...], "nn")
        u = _dot(h, wu_ref[...], "nn")
        g_o[...] = g.astype(bf16)
        u_o[...] = u.astype(bf16)
        a_o[...] = (g * jax.nn.sigmoid(g) * u).astype(bf16)

    wsp = pl.BlockSpec((None, D_MODEL, FF_SHARD), lambda j, i: (j, 0, 0))
    osp = pl.BlockSpec((None, TM, FF_SHARD), lambda j, i: (j, i, 0))
    return pl.pallas_call(
        body, name="ffn_up", grid=(4, nm), in_specs=[pl.BlockSpec((TM, D_MODEL), lambda j, i: (i, 0)), wsp, wsp],
        out_specs=[osp, osp, osp],
        out_shape=[SDS((4, s, FF_SHARD), bf16), SDS((4, s, FF_SHARD), bf16), SDS((4, s, FF_SHARD), bf16)],
        compiler_params=_cparams())(h2, wg, wu)


def _ffn_bwd_act(dx2, wd, g4, u4, s):
    def body(dx_ref, wd_ref, g_ref, u_ref, dg_o, du_o):
        dx = dx_ref[...].astype(bf16)
        for j in range(N_CHIPS):
            da = _dot(dx, wd_ref[j], "nt")
            g, u = g_ref[j].astype(f32), u_ref[j].astype(f32)
            sg = jax.nn.sigmoid(g)
            dg_o[j] = (da * u * sg * (1.0 + g * (1.0 - sg))).astype(bf16)
            du_o[j] = (da * g * sg).astype(bf16)

    osp = pl.BlockSpec((N_CHIPS, TM, FF_SHARD), lambda i: (0, i, 0))
    return pl.pallas_call(
        body, name="ffn_bwd_act", grid=(s // TM,),
        in_specs=[pl.BlockSpec((TM, D_MODEL), lambda i: (i, 0)), pl.BlockSpec((N_CHIPS, FF_SHARD, D_MODEL), lambda i: (0, 0, 0)),
                  osp, osp],
        out_specs=[osp, osp], out_shape=[SDS((4, s, FF_SHARD), bf16), SDS((4, s, FF_SHARD), bf16)],
        compiler_params=_cparams())(dx2, wd, g4, u4)


def _shard_acc(a_refs, w_refs, dims, acc=None):
    for a_ref, w_ref in zip(a_refs, w_refs):
        for j in range(N_CHIPS):
            if len(a_ref.shape) == 3:
                a = a_ref[j]
            else:
                c = a_ref.shape[1] // N_CHIPS
                a = a_ref[:, j * c:(j + 1) * c]
            part = _dot(a.astype(bf16), w_ref[j], dims)
            acc = part if acc is None else acc + part
    return acc


def _shard_specs(pairs, tm):
    a_specs = [pl.BlockSpec((N_CHIPS, tm, a.shape[2]), lambda i: (0, i, 0)) if a.ndim == 3
               else pl.BlockSpec((tm, a.shape[1]), lambda i: (i, 0)) for a, _ in pairs]
    return a_specs + [pl.BlockSpec(wt.shape, lambda i: (0, 0, 0)) for _, wt in pairs]


def _mm_shard_sum_norm_bwd(name, pairs, dims, s, tm, x, gain, dres):
    n = len(pairs)

    def body(*refs):
        x_ref, g_ref, dr_ref, dx_o, dg_o = refs[2 * n:]
        i = pl.program_id(0)
        dh = _shard_acc(refs[:n], refs[n:2 * n], dims)
        xt = x_ref[...]
        r = lax.rsqrt(jnp.mean(xt * xt, axis=1, keepdims=True) + EPS)
        nrm = xt * r
        dn = dh * g_ref[...]
        dx_o[...] = dr_ref[...] + r * (dn - nrm * jnp.mean(dn * nrm, axis=1, keepdims=True))
        dg = jnp.sum(dh * nrm, axis=0, keepdims=True)

        @pl.when(i == 0)
        def _():
            dg_o[...] = dg

        @pl.when(i > 0)
        def _():
            dg_o[...] += dg

    row = pl.BlockSpec((tm, D_MODEL), lambda i: (i, 0))
    one = pl.BlockSpec((1, D_MODEL), lambda i: (0, 0))
    return pl.pallas_call(
        body, name=name, grid=(s // tm,), in_specs=_shard_specs(pairs, tm) + [row, one, row], out_specs=[row, one],
        out_shape=[SDS((s, D_MODEL), f32), SDS((1, D_MODEL), f32)], compiler_params=_cparams())(
            *[a for a, _ in pairs], *[wt for _, wt in pairs], x, gain, dres)


def _mm_shard_sum(name, pairs, dims, out_width, s, tm, add=None, norm_gain=None):
    n = len(pairs)

    def body(*refs):
        o_ref = refs[-2] if norm_gain is not None else refs[-1]
        acc = _shard_acc(refs[:n], refs[n:2 * n], dims, refs[2 * n][...] if add is not None else None)
        o_ref[...] = acc
        if norm_gain is not None:
            refs[-1][...] = _rms(acc, refs[2 * n + (add is not None)][...]).astype(bf16)

    a_specs = [pl.BlockSpec((N_CHIPS, tm, a.shape[2]), lambda i: (0, i, 0)) if a.ndim == 3
               else pl.BlockSpec((tm, a.shape[1]), lambda i: (i, 0)) for a, _ in pairs]
    w_specs = [pl.BlockSpec(wt.shape, lambda i: (0, 0, 0)) for _, wt in pairs]
    row = pl.BlockSpec((tm, out_width), lambda i: (i, 0))
    extra, extra_specs = ([add], [row]) if add is not None else ([], [])
    if norm_gain is not None:
        extra, extra_specs = extra + [norm_gain], extra_specs + [pl.BlockSpec((1, out_width), lambda i: (0, 0))]
    normed = norm_gain is not None
    return pl.pallas_call(
        body, name=name + ("_norm" if normed else ""), grid=(s // tm,), in_specs=a_specs + w_specs + extra_specs,
        out_specs=[row, row] if normed else row,
        out_shape=[SDS((s, out_width), f32), SDS((s, out_width), bf16)] if normed else SDS((s, out_width), f32),
        compiler_params=_cparams())(*[a for a, _ in pairs], *[wt for _, wt in pairs], *extra)


GRAD_WIRE_DTYPE = bf16


WG_TK = 2048


def _wgrad(name, a, a_spec, b, b_spec, out_shape, out_block, s):
    nk = s // min(WG_TK, s)
    return _mm(name, [a, b], [a_spec, b_spec], SDS(out_shape, GRAD_WIRE_DTYPE),
               pl.BlockSpec((None,) + out_block, lambda j, k: (j, 0, 0)), (4, nk), nk, "tn", out_block)


def _layer_bwd(dx2, sv, w, sm, tabs, s, pending_mixer=None):
    tk = min(WG_TK, s)
    tok = lambda width: pl.BlockSpec((tk, width), lambda j, k: (k, 0))
    g_wd = _wgrad("wg_down", sv["a4"], pl.BlockSpec((None, tk, FF_SHARD), lambda j, k: (j, k, 0)), dx2, tok(D_MODEL),
                  (4, FF_SHARD, D_MODEL), (FF_SHARD, D_MODEL), s)
    dg4, du4 = _ffn_bwd_act(dx2, w["w_ffn_down"], sv["g4"], sv["u4"], s)
    sh704 = pl.BlockSpec((None, tk, FF_SHARD), lambda j, k: (j, k, 0))
    g_wg = _wgrad("wg_gate", sv["h2"], tok(D_MODEL), dg4, sh704, (4, D_MODEL, FF_SHARD), (D_MODEL, FF_SHARD), s)
    g_wu = _wgrad("wg_up", sv["h2"], tok(D_MODEL), du4, sh704, (4, D_MODEL, FF_SHARD), (D_MODEL, FF_SHARD), s)
    dx1, g_norm_ffn = _mm_shard_sum_norm_bwd("ffn_bwd_dh", [(dg4, w["w_ffn_gate"]), (du4, w["w_ffn_up"])], "nt", s, TM,
                                             sv["x1"], sm["norm_ffn"], dx2)
    colblk = lambda width: pl.BlockSpec((tk, width), lambda j, k: (k, j))
    g_wo = _wgrad("wg_out", sv["merged"], colblk(256), dx1, tok(D_MODEL), (4, 256, D_MODEL), (256, D_MODEL), s)
    dy3, dz, g_bgate, doa, dob, dpc = _mix_out_bwd(dx1, sv["y3"], sv["z"], sm["b_gate"], w, s)
    br_grads = []
    for b, (nm_, act, kdim) in enumerate((("a", sv["oa"], 512), ("b", sv["ob"], 256), ("c", sv["pc"], 512))):
        br_grads.append(_wgrad("wg_br" + nm_, act, tok(kdim), dy3,
                               pl.BlockSpec((None, tk, 256), lambda j, k, b=b: (b, k, j)), (4, kdim, 256), (kdim, 256), s))
    dpn, g_lin, g_scale = _pool_bwd1(dpc, sv["pooled"], sm["pool_lin"], sm["pool_scale"], s)
    dz = _pool_bwd2(dpn, dz, s)
    delta = _delta_b(dob, sv["ob"], s)
    dqb, dkb, dvb = [], [], []
    for gi in range(3):
        qg, kg, vg = sv["qb"][gi], sv["kb"][gi], sv["vb"][gi]
        dqb.append(_attn_b_dq(qg, kg, vg, dob, sv["lse_b"], delta, gi, s))
        dk, dv = _attn_b_dkv(qg, kg, vg, dob, sv["lse_b"], delta, gi, s)
        dkb.append(dk)
        dvb.append(dv)
    if pending_mixer is not None:
        xchg = (list(pending_mixer) + [g_wg, g_wu, g_wd], lambda ref, peer: ref.at[peer])
    else:
        xchg = None
    (dqa, dkd, dvd), landed = _flash_a_bwd(sv["qa"], sv["kd"], sv["vd"], sv["oa"], sv["lse_a"], doa, s, xchg)
    dz, g_gains = _prep_bwd(sv["z"], tabs, sm["gains"], dqa, dkd, dvd, dqb, dkb, dvb, dz, s)
    g_win = _wgrad("wg_in", sv["hb"], tok(D_MODEL), dz, colblk(1664), (4, D_MODEL, 1664), (D_MODEL, 1664), s)
    dx0, g_norm_mix = _mm_shard_sum_norm_bwd("mm_dh", [(dz, w["w_in"])], "nt", s, 256, sv["x"], sm["norm_mix"], dx1)
    big = dict(w_in=g_win, w_branch_a=br_grads[0], w_branch_b=br_grads[1], w_branch_c=br_grads[2], w_out=g_wo,
               w_ffn_gate=g_wg, w_ffn_up=g_wu, w_ffn_down=g_wd)
    gg = g_gains[0:4, :HEAD_DIM] + g_gains[0:4, HEAD_DIM:]
    small = jnp.concatenate([g_norm_mix.reshape(-1), g_bgate.reshape(-1), gg.reshape(-1), g_lin.reshape(-1),
                             g_scale.reshape(-1), g_norm_ffn.reshape(-1)]).reshape(SMALL_ROWS, LANES)
    return dx0, big, small, landed


def _mesh_pos():
    return lax.axis_index("x"), lax.axis_index("y"), lax.axis_index("c")


def _chip_copies(src, dst, pick_src, send_sems, recv_sems, loc_sems, receiving=True):
    n = len(src)
    x, y, c = _mesh_pos()
    me = 2 * x + y
    local = [pltpu.make_async_copy(pick_src(src[t], me), dst[t].at[me], loc_sems.at[t]) for t in range(n)]
    push, recv = [], []
    for j, (px, py) in enumerate([(1 - x, y), (x, 1 - y), (1 - x, 1 - y)]):
        peer = 2 * px + py
        for t in range(n):
            sems = dict(send_sem=send_sems.at[j * n + t], recv_sem=recv_sems.at[j * n + t],
                        device_id=(px, py, c), device_id_type=MESH)
            push.append(pltpu.make_async_remote_copy(src_ref=pick_src(src[t], peer), dst_ref=dst[t].at[me], **sems))
            if receiving:
                recv.append(pltpu.make_async_remote_copy(src_ref=pick_src(src[t], peer), dst_ref=dst[t].at[peer], **sems))
    return local, push, recv


def _chip_start(copies):
    local, push, _ = copies
    for cp in local + push:
        cp.start()


def _chip_wait(copies):
    local, push, recv = copies
    for cp in push:
        cp.wait_send()
    for cp in recv:
        cp.wait_recv()
    for cp in local:
        cp.wait()


def _chip_sems(n):
    return [pltpu.SemaphoreType.DMA((3 * n,)), pltpu.SemaphoreType.DMA((3 * n,)), pltpu.SemaphoreType.DMA((n,))]


def _chip_landing(srcs):
    return [SDS((N_CHIPS,) + tuple(a.shape[1:]), a.dtype) for a in srcs]


def _chip_exchange(name, srcs, pick_src):
    n = len(srcs)
    any_spec = pl.BlockSpec(memory_space=pl.ANY)

    def body(*refs):
        copies = _chip_copies(refs[:n], refs[n:2 * n], pick_src, *refs[2 * n:])
        _chip_start(copies)
        _chip_wait(copies)

    return pl.pallas_call(
        body, name=name, in_specs=[any_spec] * n, out_specs=[any_spec] * n, out_shape=_chip_landing(srcs),
        scratch_shapes=_chip_sems(n), compiler_params=pltpu.CompilerParams())(*srcs)


def _sibling_exchange(name, srcs):
    n = len(srcs)
    any_spec = pl.BlockSpec(memory_space=pl.ANY)

    def body(*refs):
        src, dst = refs[:n], refs[n:2 * n]
        send_sems, recv_sems = refs[2 * n:]
        x, y, c = _mesh_pos()
        cps = [pltpu.make_async_remote_copy(src_ref=src[t], dst_ref=dst[t], send_sem=send_sems.at[t], recv_sem=recv_sems.at[t],
                                            device_id=(x, y, 1 - c), device_id_type=MESH) for t in range(n)]
        for cp in cps:
            cp.start()
        for cp in cps:
            cp.wait()

    return pl.pallas_call(
        body, name=name, in_specs=[any_spec] * n, out_specs=[any_spec] * n,
        out_shape=[SDS(a.shape, a.dtype) for a in srcs],
        scratch_shapes=[pltpu.SemaphoreType.DMA((n,)), pltpu.SemaphoreType.DMA((n,))],
        compiler_params=pltpu.CompilerParams())(*srcs)


def _all_exchange(name, src):
    any_spec = pl.BlockSpec(memory_space=pl.ANY)

    def body(src_ref, dst_ref, send_sems, recv_sems, loc_sem):
        x, y, c = _mesh_pos()
        me = 4 * x + 2 * y + c
        peers = [(x ^ ((k >> 2) & 1), y ^ ((k >> 1) & 1), c ^ (k & 1)) for k in range(1, N_DEV)]
        local = pltpu.make_async_copy(src_ref, dst_ref.at[me], loc_sem)
        local.start()
        cps = [pltpu.make_async_remote_copy(src_ref=src_ref, dst_ref=dst_ref.at[me], send_sem=send_sems.at[k],
                                            recv_sem=recv_sems.at[k], device_id=p, device_id_type=MESH)
               for k, p in enumerate(peers)]
        for cp in cps:
            cp.start()
        for cp in cps:
            cp.wait_send()
        for k, (px, py, pc) in enumerate(peers):
            pltpu.make_async_remote_copy(src_ref=src_ref, dst_ref=dst_ref.at[4 * px + 2 * py + pc], send_sem=send_sems.at[k],
                                         recv_sem=recv_sems.at[k], device_id=(px, py, pc), device_id_type=MESH).wait_recv()
        local.wait()

    return pl.pallas_call(
        body, name=name, in_specs=[any_spec], out_specs=any_spec, out_shape=SDS((N_DEV,) + src.shape, src.dtype),
        scratch_shapes=[pltpu.SemaphoreType.DMA((N_DEV - 1,)), pltpu.SemaphoreType.DMA((N_DEV - 1,)), pltpu.SemaphoreType.DMA],
        compiler_params=pltpu.CompilerParams())(src)


def _cast_bf16(name, a):
    l, r, c = a.shape
    tr = _row_tile(r, c)
    spec = pl.BlockSpec((None, tr, c), lambda i, j: (i, j, 0))
    return _ew(name, lambda t: t, (l, r // tr), [a], [spec], [SDS(a.shape, bf16)], [spec])[0]


def _sum4(name, land):
    _, r, c = land.shape
    tr = _row_tile(r, c)
    up = lambda t: t.astype(f32)
    return _ew(name, lambda a, b, cc, d: ((up(a) + up(b)) + up(cc)) + up(d), (r // tr,), [land] * 4,
               [pl.BlockSpec((None, tr, c), lambda i, k=k: (k, i, 0)) for k in range(4)],
               [SDS((r, c), f32)], [pl.BlockSpec((tr, c), lambda i: (i, 0))])[0]


def _adam_math(g, w, m, v):
    m2 = ADAM_B1 * m + (1.0 - ADAM_B1) * g
    v2 = ADAM_B2 * v + (1.0 - ADAM_B2) * (g * g)
    m_hat = m2 / (1.0 - ADAM_B1 ** ADAM_STEP)
    v_hat = v2 / (1.0 - ADAM_B2 ** ADAM_STEP)
    delta = -ADAM_LR * (m_hat / (jnp.sqrt(v_hat) + ADAM_EPS) + ADAM_WD * w)
    return delta, m2, v2


def _adamw_big(name, p_own, p_sib, w, m, v):
    l, r, c = w.shape
    tr = _row_tile(r, c)

    def fn(a, b, wt, mt, vt):
        g = a + b
        return (g,) + _adam_math(g, wt, mt, vt)

    spec = pl.BlockSpec((None, tr, c), lambda i, j: (i, j, 0))
    return _ew(name, fn, (l, r // tr), [p_own, p_sib, w, m, v], [spec] * 5, [SDS(w.shape, f32)] * 4, [spec] * 4)


def _adamw_small(land, w, m, v):
    r = w.shape[0]
    tr = _row_tile(r, LANES)

    def fn(*t):
        g = t[0]
        for k in range(1, N_DEV):
            g = g + t[k]
        return (g,) + _adam_math(g, *t[N_DEV:])

    row = pl.BlockSpec((tr, LANES), lambda i: (i, 0))
    return _ew("adamw_small", fn, (r // tr,), [land] * N_DEV + [w, m, v],
               [pl.BlockSpec((None, tr, LANES), lambda i, k=k: (k, i, 0)) for k in range(N_DEV)] + [row] * 3,
               [SDS(w.shape, f32)] * 4, [row] * 4)


def _pack_small(d):
    return jnp.concatenate([d[k].reshape(DEPTH, -1) for k in SMALL], axis=1).reshape(DEPTH * SMALL_ROWS, LANES)


def _unpack_small(a, shapes):
    a = a.reshape(DEPTH, SMALL_ROWS * LANES)
    out, off = {}, 0
    for k, n in zip(SMALL, SMALL_SIZES):
        out[k] = a[:, off:off + n].reshape(shapes[k])
        off += n
    return out


def kernel(x, norm_mix, w_in, b_gate, qn_a, kn_a, qn_b, kn_b, pool_lin, pool_scale, w_branch_a, w_branch_b, w_branch_c, w_out, norm_ffn, w_ffn_gate, w_ffn_up, w_ffn_down, loss_target, m_norm_mix, m_w_in, m_b_gate, m_qn_a, m_kn_a, m_qn_b, m_kn_b, m_pool_lin, m_pool_scale, m_w_branch_a, m_w_branch_b, m_w_branch_c, m_w_out, m_norm_ffn, m_w_ffn_gate, m_w_ffn_up, m_w_ffn_down, v_norm_mix, v_w_in, v_b_gate, v_qn_a, v_kn_a, v_qn_b, v_kn_b, v_pool_lin, v_pool_scale, v_w_branch_a, v_w_branch_b, v_w_branch_c, v_w_out, v_norm_ffn, v_w_ffn_gate, v_w_ffn_up, v_w_ffn_down):
    args = dict(locals())
    s = x.shape[1]
    wts = {k: args[k] for k in WEIGHT_ORDER}
    mom = {k: args["m_" + k] for k in WEIGHT_ORDER}
    var = {k: args["v_" + k] for k in WEIGHT_ORDER}
    tabs = _rope_tables(s)

    shards16 = {k: _cast_bf16("cast_" + k, wts[k]) for k in BIG}
    whole = lambda ref, peer: ref.at[0]
    w_in_l = _chip_exchange("gather_w_in", [shards16["w_in"][0:1]], whole)[0]
    full = [None] * DEPTH

    def small_of(l):
        tile2 = lambda a: jnp.concatenate([a, a])
        gains = jnp.stack([tile2(qn_a[l]), tile2(kn_a[l]), tile2(qn_b[l]), tile2(kn_b[l])])
        return dict(norm_mix=norm_mix[l][None], norm_ffn=norm_ffn[l][None], b_gate=b_gate[l][None], gains=gains,
                    pool_lin=pool_lin[l], pool_scale=pool_scale[l][None])

    xs = x[0]
    saved = []
    hb = None
    for l in range(DEPTH):
        nxt = (l + 1) % DEPTH
        srcs = [shards16[k][l:l + 1] for k in REST] + [shards16["w_in"][nxt:nxt + 1]]
        next_gain = norm_mix[l + 1][None] if l + 1 < DEPTH else None
        xs, sv, full[l], hb = _layer_fwd(xs, dict(w_in=w_in_l), small_of(l), tabs, s, (srcs, whole), hb, next_gain)
        saved.append(sv)
        w_in_l = full[l]["next_w_in"]
    dy, loss_rows = _loss_grad(xs, loss_target[0], s)
    loss = lax.psum(jnp.sum(loss_rows), ("x", "y", "c"))

    part = {k: [None] * DEPTH for k in BIG}
    small_g = [None] * DEPTH
    to_chip = lambda ref, peer: ref.at[peer]
    assert GRAD_WIRE_DTYPE == bf16
    pending = [full[DEPTH - 1][k] for k in MIXER]
    for l in reversed(range(DEPTH)):
        dy, big, small_g[l], landed = _layer_bwd(dy, saved[l], full[l], small_of(l), tabs, s, pending)
        if l + 1 < DEPTH:
            for k, a in zip(MIXER, landed[:len(MIXER)]):
                part[k][l + 1] = _sum4("sum4_" + k, a)
        for k, a in zip(FFN, landed[len(MIXER):]):
            part[k][l] = _sum4("sum4_" + k, a)
        pending = [big[k] for k in MIXER]
    for k, a in zip(MIXER, _chip_exchange("grad_scatter", pending, to_chip)):
        part[k][0] = _sum4("sum4_" + k, a)

    p_own = [jnp.stack(part[k]) for k in BIG]
    p_sib = _sibling_exchange("grad_sibling", p_own)
    outs = {}
    for k, a, b in zip(BIG, p_own, p_sib):
        outs[k] = _adamw_big("adamw_" + k, a, b, wts[k], mom[k], var[k])

    land_s = _all_exchange("small_allgather", jnp.concatenate(small_g, axis=0))
    res_s = _adamw_small(land_s, _pack_small(wts), _pack_small(mom), _pack_small(var))
    shapes = {k: wts[k].shape for k in SMALL}
    small_out = [_unpack_small(r, shapes) for r in res_s]
    for k in SMALL:
        outs[k] = tuple(so[k] for so in small_out)

    flat = [loss, dy[None]]
    for idx in range(4):
        flat += [outs[k][idx] for k in WEIGHT_ORDER]
    return tuple(flat)
```

```python
import functools

import jax
import jax.numpy as jnp
from jax import lax
from jax.experimental import pallas as pl
from jax.experimental.pallas import tpu as pltpu

f32, bf16 = jnp.float32, jnp.bfloat16
SDS = jax.ShapeDtypeStruct

DEPTH = 4
D_MODEL = 1024
HEAD_DIM = 64
LANES = 128
GRID_W = 64
ROPE_THETA = 10000.0
EPS = 1e-6
NEG_INF = -1e30
SCALE = HEAD_DIM ** -0.5
B_GROUP_CFG = ((128, 1), (512, 4), (2048, 16))
POOL_WINDOWS = (2, 4, 8, 16)
N_CHIPS = 4
N_DEV = 8
C_QA, C_KA, C_VA, C_QB, C_KB, C_VB, C_UC, C_GZ = 0, 512, 640, 768, 1536, 2304, 3072, 3584
IN_WIDTH = 6656
FF_SHARD = 704
ADAM_LR, ADAM_B1, ADAM_B2, ADAM_EPS, ADAM_WD, ADAM_STEP = 0.001, 0.9, 0.999, 1e-08, 0.01, 10
VMEM_LIMIT = 56 * 1024 * 1024
MESH = pl.DeviceIdType.MESH

MIXER = ("w_in", "w_branch_a", "w_branch_b", "w_branch_c", "w_out")
FFN = ("w_ffn_gate", "w_ffn_up", "w_ffn_down")
BIG = MIXER + FFN
REST = BIG[1:]
SMALL = ("norm_mix", "b_gate", "qn_a", "kn_a", "qn_b", "kn_b", "pool_lin", "pool_scale", "norm_ffn")
SMALL_SIZES = (1024, 3072, 64, 64, 64, 64, 65536, 512, 1024)
SMALL_ROWS = sum(SMALL_SIZES) // LANES
WEIGHT_ORDER = ("norm_mix", "w_in", "b_gate", "qn_a", "kn_a", "qn_b", "kn_b", "pool_lin", "pool_scale",
                "w_branch_a", "w_branch_b", "w_branch_c", "w_out", "norm_ffn", "w_ffn_gate", "w_ffn_up", "w_ffn_down")


def _cparams():
    return pltpu.CompilerParams(vmem_limit_bytes=VMEM_LIMIT)


def _lo_mask(shape):
    return lax.broadcasted_iota(jnp.int32, shape, len(shape) - 1) < HEAD_DIM


def _dot(a, b, dims):
    dn = {"nn": (((1,), (0,)), ((), ())), "nt": (((1,), (1,)), ((), ())), "tn": (((0,), (0,)), ((), ()))}[dims]
    return lax.dot_general(a, b, dn, preferred_element_type=f32)


def _mm(name, ins, in_specs, out_shape, out_spec, grid, nk, dims, acc_shape, epilogue=None, n_extra=0, aliases=None):
    multi = isinstance(out_shape, (list, tuple))
    n_out = len(out_shape) if multi else 1

    def body(*refs):
        a_ref, b_ref = refs[0], refs[1]
        extra = refs[2:2 + n_extra]
        outs = refs[2 + n_extra:2 + n_extra + n_out]
        k = pl.program_id(len(grid) - 1)
        part = _dot(a_ref[...].astype(bf16), b_ref[...].astype(bf16), dims)

        def finish(acc):
            res = epilogue(acc, *[e[...] for e in extra]) if epilogue is not None else acc
            res = res if isinstance(res, (list, tuple)) else (res,)
            for o, r in zip(outs, res):
                o[...] = r.astype(o.dtype)

        if nk == 1:
            finish(part)
        else:
            acc_ref = refs[-1]

            @pl.when(k == 0)
            def _():
                acc_ref[...] = part

            @pl.when(k > 0)
            def _():
                acc_ref[...] += part

            @pl.when(k == nk - 1)
            def _():
                finish(acc_ref[...])

    return pl.pallas_call(
        body, name=name, grid=grid, in_specs=in_specs,
        out_specs=list(out_spec) if multi else out_spec,
        out_shape=list(out_shape) if multi else out_shape,
        scratch_shapes=[] if nk == 1 else [pltpu.VMEM(acc_shape, f32)],
        input_output_aliases=aliases or {}, compiler_params=_cparams())(*ins)


def _ew(name, fn, grid, ins, in_specs, out_shapes, out_specs, n_acc=0, aliases=None):
    n_in = len(ins)
    n_out = len(out_shapes) - n_acc
    n_alias = len(aliases or {})

    def body(*refs):
        in_refs = refs[:n_in - n_alias] if n_alias else refs[:n_in]
        out_refs = refs[n_in:n_in + n_out]
        acc_refs = refs[n_in + n_out:]
        i = pl.program_id(len(grid) - 1)
        res = fn(*[r[...] for r in in_refs])
        res = res if isinstance(res, (list, tuple)) else (res,)
        for o, r in zip(out_refs, res[:n_out]):
            o[...] = r.astype(o.dtype)
        for a, r in zip(acc_refs, res[n_out:]):
            @pl.when(i == 0)
            def _(a=a, r=r):
                a[...] = r.astype(a.dtype)

            @pl.when(i > 0)
            def _(a=a, r=r):
                a[...] += r.astype(a.dtype)

    return pl.pallas_call(body, name=name, grid=grid, in_specs=in_specs, out_specs=out_specs, out_shape=out_shapes,
                          input_output_aliases=aliases or {}, compiler_params=_cparams())(*ins)


EW_BLOCK_ELEMS = 128 * 1024


def _row_tile(rows, cols):
    step = 16 if rows % 16 == 0 else 8
    return max(t for t in range(step, rows + 1, step) if rows % t == 0 and (t * cols <= EW_BLOCK_ELEMS or t == step))


def _rope_tables(s):
    t = jnp.arange(s)
    inv_ax = ROPE_THETA ** (-jnp.arange(0, HEAD_DIM // 2, 2, dtype=f32) / (HEAD_DIM // 2))
    inv_sq = ROPE_THETA ** (-jnp.arange(0, HEAD_DIM, 2, dtype=f32) / HEAD_DIM)
    ang_row = (t // GRID_W).astype(f32)[:, None] * inv_ax[None, :]
    ang_col = (t % GRID_W).astype(f32)[:, None] * inv_ax[None, :]
    ang_seq = t.astype(f32)[:, None] * inv_sq[None, :]
    a_ax = jnp.concatenate([ang_row, ang_row, ang_col, ang_col], axis=1)
    sg_ax = jnp.concatenate([-jnp.ones(16), jnp.ones(16), -jnp.ones(16), jnp.ones(16)]).astype(f32)
    a_sq = jnp.concatenate([ang_seq, ang_seq], axis=1)
    sg_sq = jnp.concatenate([-jnp.ones(32), jnp.ones(32)]).astype(f32)
    two = lambda a: jnp.concatenate([a, a], axis=1)
    return (two(jnp.cos(a_ax)), two(jnp.sin(a_ax) * sg_ax), two(jnp.cos(a_sq)), two(jnp.sin(a_sq) * sg_sq))


def _head_stat(v, lo):
    r = lax.broadcasted_iota(jnp.int32, (LANES, LANES), 0) < HEAD_DIM
    c = lax.broadcasted_iota(jnp.int32, (LANES, LANES), 1) < HEAD_DIM
    same_head = (r == c).astype(f32)
    return jnp.dot(v, same_head, preferred_element_type=f32, precision=lax.Precision.HIGHEST)


def _partner(y, off):
    lane = lax.broadcasted_iota(jnp.int32, y.shape, 1)
    first = (lane & (2 * off - 1)) < off
    return jnp.where(first, pltpu.roll(y, LANES - off, 1), pltpu.roll(y, off, 1))


def _normrope(xc, g, cos, sin, off):
    lo = _lo_mask(xc.shape)
    r = lax.rsqrt(_head_stat(xc * xc, lo) * (1.0 / HEAD_DIM) + EPS)
    y = xc * r * g
    return y * cos + _partner(y, off) * sin


def _normrope_bwd(xc, g, cos, sin, off, drot):
    lo = _lo_mask(xc.shape)
    r = lax.rsqrt(_head_stat(xc * xc, lo) * (1.0 / HEAD_DIM) + EPS)
    n = xc * r
    dy = drot * cos + _partner(drot * sin, off)
    dg = jnp.sum(dy * n, axis=0, keepdims=True)
    dn = dy * g
    dx = r * (dn - n * (_head_stat(dn * n, lo) * (1.0 / HEAD_DIM)))
    return dx, dg


def _norm_fwd(x, g, s):
    ts = 256

    def fn(xt, gt):
        r = lax.rsqrt(jnp.mean(xt * xt, axis=1, keepdims=True) + EPS)
        return xt * r * gt

    return _ew("norm_fwd", fn, (s // ts,), [x, g],
               [pl.BlockSpec((ts, D_MODEL), lambda i: (i, 0)), pl.BlockSpec((1, D_MODEL), lambda i: (0, 0))],
               [SDS((s, D_MODEL), bf16)], [pl.BlockSpec((ts, D_MODEL), lambda i: (i, 0))])[0]


def _prep_fwd(z, tabs, gains, s):
    ts = 256
    cos_a, sin_a, cos_b, sin_b = tabs

    def body(za_ref, zq_ref, zk_ref, zv_ref, ca, sa, cb, sb, g_ref, qa_o, kd_o, vd_o, *grp_o):
        lo = _lo_mask((ts, LANES))
        g = g_ref[...]
        ca_, sa_, cb_, sb_ = ca[...], sa[...], cb[...], sb[...]
        for c in range(4):
            qa_o[:, c * LANES:(c + 1) * LANES] = _normrope(za_ref[:, c * LANES:(c + 1) * LANES], g[0:1], ca_, sa_, 16).astype(bf16)
        k = _normrope(za_ref[:, C_KA:C_KA + LANES], g[1:2], ca_, sa_, 16)
        kr = pltpu.roll(k, HEAD_DIM, 1)
        kd_o[0] = k.astype(bf16)
        kd_o[1] = kr.astype(bf16)
        v = za_ref[:, C_VA:C_VA + LANES]
        ones_col = (lax.broadcasted_iota(jnp.int32, v.shape, 1) == HEAD_DIM).astype(f32)
        vd_o[0] = jnp.where(lo, v, ones_col).astype(bf16)
        vd_o[1] = jnp.where(lo, pltpu.roll(v, HEAD_DIM, 1), ones_col).astype(bf16)
        for c in range(6):
            sl = slice(c * LANES, (c + 1) * LANES)
            gi, gsl = c // 2, slice((c % 2) * LANES, (c % 2 + 1) * LANES)
            grp_o[gi][:, gsl] = _normrope(zq_ref[:, sl], g[2:3], cb_, sb_, 32).astype(bf16)
            grp_o[3 + gi][:, gsl] = _normrope(zk_ref[:, sl], g[3:4], cb_, sb_, 32).astype(bf16)
            grp_o[6 + gi][:, gsl] = zv_ref[:, sl].astype(bf16)

    w = 768
    grp = pl.BlockSpec((ts, 256), lambda i: (i, 0))
    zspec = lambda cb: pl.BlockSpec((ts, w), lambda i: (i, cb))
    tab = pl.BlockSpec((ts, LANES), lambda i: (i, 0))
    dup = pl.BlockSpec((2, ts, LANES), lambda i: (0, i, 0))
    res = pl.pallas_call(
        body, name="prep_fwd", grid=(s // ts,),
        in_specs=[zspec(0), zspec(1), zspec(2), zspec(3), tab, tab, tab, tab, pl.BlockSpec((4, LANES), lambda i: (0, 0))],
        out_specs=[pl.BlockSpec((ts, 512), lambda i: (i, 0)), dup, dup] + [grp] * 9,
        out_shape=[SDS((s, 512), bf16), SDS((2, s, LANES), bf16), SDS((2, s, LANES), bf16)] + [SDS((s, 256), bf16)] * 9,
        compiler_params=_cparams())(z, z, z, z, cos_a, sin_a, cos_b, sin_b, gains)
    return res[0], res[1], res[2], res[3:6], res[6:9], res[9:12]


def _prep_bwd(z, tabs, gains, dqa, dkd, dvd, dqb, dkb, dvb, dz, s):
    ts = 256
    cos_a, sin_a, cos_b, sin_b = tabs

    def body(za_ref, zq_ref, zk_ref, ca, sa, cb, sb, g_ref, dqa_r, dkd_r, dvd_r,
             dq0, dq1, dq2, dk0, dk1, dk2, dv0, dv1, dv2, dz_in, dz_o, dg_o):
        i = pl.program_id(0)
        lo = _lo_mask((ts, LANES))
        g = g_ref[...]
        ca_, sa_, cb_, sb_ = ca[...], sa[...], cb[...], sb[...]
        dg = [jnp.zeros((1, LANES), f32) for _ in range(4)]
        for c in range(4):
            sl = slice(c * LANES, (c + 1) * LANES)
            dx, d = _normrope_bwd(za_ref[:, sl], g[0:1], ca_, sa_, 16, dqa_r[:, sl])
            dz_o[:, sl] = dx.astype(bf16)
            dg[0] += d
        dk = jnp.where(lo, dkd_r[0], pltpu.roll(dkd_r[1], HEAD_DIM, 1))
        dx, d = _normrope_bwd(za_ref[:, C_KA:C_KA + LANES], g[1:2], ca_, sa_, 16, dk)
        dz_o[:, C_KA:C_KA + LANES] = dx.astype(bf16)
        dg[1] += d
        dz_o[:, C_VA:C_VA + LANES] = jnp.where(lo, dvd_r[0], pltpu.roll(dvd_r[1], HEAD_DIM, 1)).astype(bf16)
        dqs, dks, dvs = (dq0, dq1, dq2), (dk0, dk1, dk2), (dv0, dv1, dv2)
        for c in range(6):
            sl = slice(c * LANES, (c + 1) * LANES)
            gsl = slice((c % 2) * LANES, (c % 2 + 1) * LANES)
            dx, d = _normrope_bwd(zq_ref[:, sl], g[2:3], cb_, sb_, 32, dqs[c // 2][:, gsl])
            dz_o[:, C_QB + c * LANES:C_QB + (c + 1) * LANES] = dx.astype(bf16)
            dg[2] += d
            dx, d = _normrope_bwd(zk_ref[:, sl], g[3:4], cb_, sb_, 32, dks[c // 2][:, gsl])
            dz_o[:, C_KB + c * LANES:C_KB + (c + 1) * LANES] = dx.astype(bf16)
            dg[3] += d
            dz_o[:, C_VB + c * LANES:C_VB + (c + 1) * LANES] = dvs[c // 2][:, gsl].astype(bf16)
        dgs = jnp.concatenate(dg + [jnp.zeros((4, LANES), f32)], axis=0)

        @pl.when(i == 0)
        def _():
            dg_o[...] = dgs

        @pl.when(i > 0)
        def _():
            dg_o[...] += dgs

    w = 768
    zspec = lambda cb: pl.BlockSpec((ts, w), lambda i: (i, cb))
    tab = pl.BlockSpec((ts, LANES), lambda i: (i, 0))
    dup = pl.BlockSpec((2, ts, LANES), lambda i: (0, i, 0))
    grp = pl.BlockSpec((ts, 256), lambda i: (i, 0))
    dzo, dgo = pl.pallas_call(
        body, name="prep_bwd", grid=(s // ts,),
        in_specs=[zspec(0), zspec(1), zspec(2), tab, tab, tab, tab, pl.BlockSpec((4, LANES), lambda i: (0, 0)),
                  pl.BlockSpec((ts, 512), lambda i: (i, 0)), dup, dup] + [grp] * 9 + [pl.BlockSpec(memory_space=pl.ANY)],
        out_specs=[pl.BlockSpec((ts, C_UC), lambda i: (i, 0)), pl.BlockSpec((8, LANES), lambda i: (0, 0))],
        out_shape=[SDS((s, IN_WIDTH), bf16), SDS((8, LANES), f32)],
        input_output_aliases={20: 0}, compiler_params=_cparams())(
            z, z, z, cos_a, sin_a, cos_b, sin_b, gains, dqa, dkd, dvd, *dqb, *dkb, *dvb, dz)
    return dzo, dgo


def _stack_heads(x, lo, dtype):
    z = jnp.zeros_like(x)
    return jnp.concatenate([jnp.where(lo, x, z), jnp.where(lo, z, x)], axis=0).astype(dtype)


def _rows_of(v, lo, kind):
    if kind == "max":
        a = jnp.max(jnp.where(lo, v, NEG_INF * 10), axis=1, keepdims=True)
        b = jnp.max(jnp.where(lo, NEG_INF * 10, v), axis=1, keepdims=True)
    else:
        a = jnp.sum(jnp.where(lo, v, 0.0), axis=1, keepdims=True) * (1.0 / HEAD_DIM)
        b = jnp.sum(jnp.where(lo, 0.0, v), axis=1, keepdims=True) * (1.0 / HEAD_DIM)
    return jnp.concatenate([a, b], axis=0)


def _stack_low(x, lo, dtype, scale=None):
    x = x.astype(f32)
    if scale is not None:
        x = x * scale
    z = jnp.zeros_like(x)
    return jnp.concatenate([jnp.where(lo, x, z), jnp.where(lo, pltpu.roll(x, HEAD_DIM, 1), z)], axis=0).astype(dtype)


def _unstack_low(xs, lo, rows):
    return jnp.where(lo, xs[:rows], pltpu.roll(xs[rows:], HEAD_DIM, 1))


FA_TQ, FA_TK = 256, 1024


def _host_exchange(body, n_in, n_out, grid, xchg):
    if xchg is None:
        return body, [], [], [], []
    srcs, pick_src = xchg
    n = len(srcs)
    any_spec = pl.BlockSpec(memory_space=pl.ANY)

    def hosted(*refs):
        ins, src = refs[:n_in], refs[n_in:n_in + n]
        outs, dst = refs[n_in + n:n_in + n + n_out], refs[n_in + n + n_out:n_in + 2 * n + n_out]
        scratch, sems = refs[n_in + 2 * n + n_out:-3], refs[-3:]
        ids = [pl.program_id(a) for a in range(len(grid))]
        first = functools.reduce(jnp.logical_and, [i == 0 for i in ids])
        last = functools.reduce(jnp.logical_and, [i == g - 1 for i, g in zip(ids, grid)])

        @pl.when(first)
        def _():
            _chip_start(_chip_copies(src, dst, pick_src, *sems, receiving=False))

        body(*ins, *outs, *scratch)

        @pl.when(last)
        def _():
            _chip_wait(_chip_copies(src, dst, pick_src, *sems))

    return hosted, [any_spec] * n, [any_spec] * n, _chip_landing(srcs), _chip_sems(n)


def _flash_a_fwd(qa, kd, vd, s, xchg=None):
    tq, tk = FA_TQ, min(4 * FA_TK, s)
    nk = s // tk

    def body(q_ref, k_ref, v_ref, o_ref, lse_ref, qs, m_s, acc):
        lo = _lo_mask((tq, LANES))
        for pp in range(2):
            qs[pp] = _stack_low(q_ref[:, pp * LANES:(pp + 1) * LANES], lo, bf16, SCALE)
        m_s[...] = jnp.full(m_s.shape, NEG_INF, f32)
        acc[...] = jnp.zeros(acc.shape, f32)

        def chunk(j, carry):
            rows = pl.ds(pl.multiple_of(j * tk, tk), tk)
            k, v = k_ref[rows, :], v_ref[rows, :]
            for pp in range(2):
                sc = _dot(qs[pp], k, "nt")
                m_prev = m_s[pp]
                m_new = jnp.maximum(m_prev, jnp.max(sc, axis=1, keepdims=True))
                p = jnp.exp(sc - m_new).astype(bf16)
                acc[pp] = acc[pp] * jnp.exp(m_prev - m_new) + _dot(p, v, "nn")
                m_s[pp] = m_new
            return carry

        lax.fori_loop(0, nk, chunk, 0)
        for pp in range(2):
            sl = slice(pp * LANES, (pp + 1) * LANES)
            a = acc[pp]
            lane = lax.broadcasted_iota(jnp.int32, a.shape, 1)
            l = jnp.sum(jnp.where(lane == HEAD_DIM, a, 0.0), axis=1, keepdims=True)
            o_ref[:, sl] = _unstack_low(a / l, lo, tq).astype(bf16)
            lse = m_s[pp] + jnp.log(l)
            lse_ref[:, sl] = jnp.where(lo, lse[:tq], lse[tq:])

    kv = pl.BlockSpec((None, s, LANES), lambda c, i: (c, 0, 0))
    qo = pl.BlockSpec((tq, 2 * LANES), lambda c, i: (i, c))
    grid = (2, s // tq)
    hosted, xi, xo, xs, xsem = _host_exchange(body, 3, 2, grid, xchg)
    res = pl.pallas_call(
        hosted, name="flash_a_fwd" + ("_x" if xchg else ""), grid=grid, in_specs=[qo, kv, kv] + xi, out_specs=[qo, qo] + xo,
        out_shape=[SDS((s, 512), bf16), SDS((s, 512), f32)] + xs,
        scratch_shapes=[pltpu.VMEM((2, 2 * tq, LANES), bf16), pltpu.VMEM((2, 2 * tq, 1), f32),
                        pltpu.VMEM((2, 2 * tq, LANES), f32)] + xsem,
        compiler_params=_cparams())(qa, kd, vd, *(xchg[0] if xchg else []))
    return res[:2], res[2:]


def _flash_a_bwd(qa, kd, vd, oa, lse, doa, s, xchg=None):
    tq, tk = FA_TQ, 2 * FA_TK
    nk = s // tk

    def body(q_ref, do_ref, o_ref, lse_ref, k_ref, v_ref, dq_ref, dk_ref, dv_ref, qs, dos, lse_s, dl_s, dq_s):
        i = pl.program_id(1)
        lo = _lo_mask((tq, LANES))

        @pl.when(i == 0)
        def _():
            dk_ref[...] = jnp.zeros(dk_ref.shape, f32)
            dv_ref[...] = jnp.zeros(dv_ref.shape, f32)

        for pp in range(2):
            sl = slice(pp * LANES, (pp + 1) * LANES)
            do = do_ref[:, sl]
            qs[...] = _stack_low(q_ref[:, sl], lo, bf16, SCALE)
            dos[...] = _stack_low(do, lo, bf16)
            dl_s[...] = _rows_of(do * o_ref[:, sl].astype(f32), lo, "sum") * HEAD_DIM
            lse_s[...] = _rows_of(lse_ref[:, sl], lo, "max")
            dq_s[...] = jnp.zeros(dq_s.shape, f32)

            def chunk(j, carry):
                rows = pl.ds(pl.multiple_of(j * tk, tk), tk)
                k, v = k_ref[rows, :], v_ref[rows, :]
                q_, do_ = qs[...], dos[...]
                p = jnp.exp(_dot(q_, k, "nt") - lse_s[...])
                ds = (p * (_dot(do_, v, "nt") - dl_s[...])).astype(bf16)
                dv_ref[rows, :] += _dot(p.astype(bf16), do_, "tn")
                dk_ref[rows, :] += _dot(ds, q_, "tn")
                dq_s[...] += _dot(ds, k, "nn")
                return carry

            lax.fori_loop(0, nk, chunk, 0)
            dq_ref[:, sl] = _unstack_low(dq_s[...], lo, tq) * SCALE

    grp = lambda: pl.BlockSpec((tq, 2 * LANES), lambda c, i: (i, c))
    kv = lambda: pl.BlockSpec((None, s, LANES), lambda c, i: (c, 0, 0))
    grid = (2, s // tq)
    hosted, xi, xo, xs, xsem = _host_exchange(body, 6, 3, grid, xchg)
    res = pl.pallas_call(
        hosted, name="flash_a_bwd" + ("_x" if xchg else ""), grid=grid,
        in_specs=[grp(), grp(), grp(), grp(), kv(), kv()] + xi, out_specs=[grp(), kv(), kv()] + xo,
        out_shape=[SDS((s, 512), f32), SDS((2, s, LANES), f32), SDS((2, s, LANES), f32)] + xs,
        scratch_shapes=[pltpu.VMEM((2 * tq, LANES), bf16), pltpu.VMEM((2 * tq, LANES), bf16), pltpu.VMEM((2 * tq, 1), f32),
                        pltpu.VMEM((2 * tq, 1), f32), pltpu.VMEM((2 * tq, LANES), f32)] + xsem,
        compiler_params=_cparams())(qa, doa, oa, lse, kd, vd, *(xchg[0] if xchg else []))
    return res[:3], res[3:]


BH = 128
BQ_MAX = 512


class _BandGeom:
    def __init__(self, gi, s):
        self.d = B_GROUP_CFG[gi][1]
        self.l_sub = s // self.d
        self.bq = min(BQ_MAX, self.l_sub)
        self.nb = self.l_sub // self.bq
        self.grid = (self.d, self.nb, 2)
        per, last = self.bq // BH, self.l_sub // BH - 1
        self.out_col = lambda r, pp: r * 2 + pp
        self.in_col = self.out_col
        self.cur = lambda col: pl.BlockSpec((self.bq, LANES), lambda r, n, pp: (n, col(r, pp)))
        self.prev = lambda col: pl.BlockSpec((BH, LANES), lambda r, n, pp: (jnp.maximum(n * per - 1, 0), col(r, pp)))
        self.next = lambda col: pl.BlockSpec((BH, LANES), lambda r, n, pp: (jnp.minimum((n + 1) * per, last), col(r, pp)))
        self.window = lambda col: [self.prev(col), self.cur(col), self.next(col)]
        self.view_out = lambda a: a.reshape(self.l_sub, self.d * 256)
        self.view_in = self.view_out

    def fill_band_bias(self, bias_s, q_rows, q_off, k_off):
        @pl.when((pl.program_id(0) == 0) & (pl.program_id(1) == 0) & (pl.program_id(2) == 0))
        def _():
            r = lax.broadcasted_iota(jnp.int32, bias_s.shape, 0)
            q = jnp.where(r >= q_rows, r - q_rows, r) + q_off
            k = lax.broadcasted_iota(jnp.int32, bias_s.shape, 1) + k_off
            bias_s[...] = jnp.where(jnp.abs(q - k) <= 64, 0.0, NEG_INF)

    def edge_bias(self, first_pos, n, stacked_rows=False):
        if stacked_rows:
            r = lax.broadcasted_iota(jnp.int32, (2 * n, 1), 0)
            idx = jnp.where(r >= n, r - n, r)
        else:
            idx = lax.broadcasted_iota(jnp.int32, (1, n), 1)
        pos = first_pos + idx
        return jnp.where((pos >= 0) & (pos < self.l_sub), 0.0, NEG_INF)


def _cat3(a, b, c):
    return jnp.concatenate([a[...], b[...], c[...]], axis=0)


def _attn_b_fwd(qb, kb, vb, gi, s):
    g = _BandGeom(gi, s)
    bq = g.bq

    def body(q_ref, kp, kc, kn, vp, vc, vn, o_ref, lse_ref, bias_s):
        n = pl.program_id(1)
        g.fill_band_bias(bias_s, bq, 0, -BH)
        lo = _lo_mask((bq, LANES))
        qs = _stack_heads(q_ref[...] * SCALE, lo, bf16)
        k, v = _cat3(kp, kc, kn), _cat3(vp, vc, vn)
        sc = _dot(qs, k, "nt") + bias_s[...] + g.edge_bias(n * bq - BH, bq + 2 * BH)
        m = jnp.max(sc, axis=1, keepdims=True)
        p = jnp.exp(sc - m)
        den = jnp.sum(p, axis=1, keepdims=True)
        o = _dot(p.astype(bf16), v, "nn") / den
        o_ref[...] = jnp.where(lo, o[:bq], o[bq:])
        lse = m + jnp.log(den)
        lse_ref[...] = jnp.where(lo, lse[:bq], lse[bq:])

    out = g.cur(g.out_col)
    qv, kv, vv = g.view_in(qb), g.view_in(kb), g.view_in(vb)
    o, lse = pl.pallas_call(
        body, name=f"attn_b_fwd{gi}", grid=g.grid,
        in_specs=[g.cur(g.in_col)] + g.window(g.in_col) * 2, out_specs=[out, out],
        out_shape=[SDS((g.l_sub, g.d * 256), f32), SDS((g.l_sub, g.d * 256), f32)],
        scratch_shapes=[pltpu.VMEM((2 * bq, bq + 2 * BH), f32)],
        compiler_params=_cparams())(qv, kv, kv, kv, vv, vv, vv)
    return o.reshape(s, 256), lse.reshape(s, 256)


def _attn_b_dq(qb, kb, vb, dob, lse, delta, gi, s):
    g = _BandGeom(gi, s)
    bq = g.bq

    def body(q_ref, kp, kc, kn, vp, vc, vn, do_ref, lse_ref, dl_ref, dq_ref, bias_s):
        n = pl.program_id(1)
        g.fill_band_bias(bias_s, bq, 0, -BH)
        lo = _lo_mask((bq, LANES))
        qs = _stack_heads(q_ref[...] * SCALE, lo, bf16)
        dos = _stack_heads(do_ref[...], lo, bf16)
        k, v = _cat3(kp, kc, kn), _cat3(vp, vc, vn)
        sc = _dot(qs, k, "nt") + bias_s[...] + g.edge_bias(n * bq - BH, bq + 2 * BH)
        p = jnp.exp(sc - _rows_of(lse_ref[...], lo, "max"))
        dp = _dot(dos, v, "nt")
        ds = (p * (dp - _rows_of(dl_ref[...], lo, "sum"))).astype(bf16)
        dqs = _dot(ds, k, "nn")
        dq_ref[...] = jnp.where(lo, dqs[:bq], dqs[bq:]) * SCALE

    out = g.cur(g.out_col)
    qv, kv, vv = g.view_in(qb), g.view_in(kb), g.view_in(vb)
    dq = pl.pallas_call(
        body, name=f"attn_b_dq{gi}", grid=g.grid,
        in_specs=[g.cur(g.in_col)] + g.window(g.in_col) * 2 + [out, out, out], out_specs=out,
        out_shape=SDS((g.l_sub, g.d * 256), f32), scratch_shapes=[pltpu.VMEM((2 * bq, bq + 2 * BH), f32)],
        compiler_params=_cparams())(qv, kv, kv, kv, vv, vv, vv, g.view_out(dob), g.view_out(lse), g.view_out(delta))
    return dq.reshape(s, 256)


def _attn_b_dkv(qb, kb, vb, dob, lse, delta, gi, s):
    g = _BandGeom(gi, s)
    bq = g.bq
    nq = bq + 2 * BH

    def body(k_ref, v_ref, qp, qc, qn, dop, doc, don, lp, lc, ln, dp_, dc_, dn_, dk_ref, dv_ref, bias_s):
        m = pl.program_id(1)
        g.fill_band_bias(bias_s, nq, -BH, 0)
        lo = _lo_mask((nq, LANES))
        qs = _stack_heads(_cat3(qp, qc, qn) * SCALE, lo, bf16)
        dos = _stack_heads(_cat3(dop, doc, don), lo, bf16)
        lses = _rows_of(_cat3(lp, lc, ln), lo, "max")
        dls = _rows_of(_cat3(dp_, dc_, dn_), lo, "sum")
        k, v = k_ref[...], v_ref[...]
        sc = _dot(qs, k, "nt") + bias_s[...] + g.edge_bias(m * bq - BH, nq, stacked_rows=True)
        p = jnp.exp(sc - lses)
        dpv = _dot(dos, v, "nt")
        ds = (p * (dpv - dls)).astype(bf16)
        dv_ref[...] = _dot(p.astype(bf16), dos, "tn")
        dk_ref[...] = _dot(ds, qs, "tn")

    cur = g.cur(g.out_col)
    qv, kv, vv = g.view_in(qb), g.view_in(kb), g.view_in(vb)
    dk, dv = pl.pallas_call(
        body, name=f"attn_b_dkv{gi}", grid=g.grid,
        in_specs=[g.cur(g.in_col)] * 2 + g.window(g.in_col) + g.window(g.out_col) * 3, out_specs=[cur, cur],
        out_shape=[SDS((g.l_sub, g.d * 256), f32), SDS((g.l_sub, g.d * 256), f32)],
        scratch_shapes=[pltpu.VMEM((2 * nq, bq), f32)],
        compiler_params=_cparams())(kv, vv, qv, qv, qv, *([g.view_out(dob)] * 3), *([g.view_out(lse)] * 3),
                                    *([g.view_out(delta)] * 3))
    return dk.reshape(s, 256), dv.reshape(s, 256)


def _merge_b(o_list, lse_list, s):
    ts = 256

    def fn(o0, o1, o2, l0, l1, l2):
        m = jnp.maximum(jnp.maximum(l0, l1), l2)
        w0, w1, w2 = jnp.exp(l0 - m), jnp.exp(l1 - m), jnp.exp(l2 - m)
        den = w0 + w1 + w2
        return (w0 * o0 + w1 * o1 + w2 * o2) / den, m + jnp.log(den)

    row = pl.BlockSpec((ts, 256), lambda i: (i, 0))
    return _ew("merge_b", fn, (s // ts,), [*o_list, *lse_list], [row] * 6,
               [SDS((s, 256), bf16), SDS((s, 256), f32)], [row, row])


def _delta_b(dob, ob, s):
    ts = 256

    def fn(do, o):
        prod = do * o.astype(f32)
        lo = _lo_mask((ts, LANES))
        return jnp.concatenate([_head_stat(prod[:, :LANES], lo), _head_stat(prod[:, LANES:], lo)], axis=1)

    row = pl.BlockSpec((ts, 256), lambda i: (i, 0))
    return _ew("delta_b", fn, (s // ts,), [dob, ob], [row, row], [SDS((s, 256), f32)], [row])[0]


def _band(t0, u0, w, nt, nu, s, transposed):
    t = t0 + lax.broadcasted_iota(jnp.int32, (nt, nu), 0)
    u = u0 + lax.broadcasted_iota(jnp.int32, (nt, nu), 1)
    if transposed:
        lo, hi = jnp.clip(t - w // 2 + 1, 0, s), jnp.clip(t + w // 2 + 1, 0, s)
    else:
        lo, hi = jnp.clip(t - w // 2, 0, s), jnp.clip(t + w - w // 2, 0, s)
    return ((u >= lo) & (u < hi)).astype(f32)


def _pool_cnt(t0, w, nt, s):
    t = t0 + lax.broadcasted_iota(jnp.int32, (nt, 1), 0)
    return (jnp.clip(t + w - w // 2, 0, s) - jnp.clip(t - w // 2, 0, s)).astype(f32)


POOL_HALO = 8


def _halo_specs(tp, s, cb):
    per, last = tp // POOL_HALO, s // POOL_HALO - 1
    return [pl.BlockSpec((POOL_HALO, 512), lambda i: (jnp.maximum(i * per - 1, 0), cb)),
            pl.BlockSpec((tp, 512), lambda i: (i, cb)),
            pl.BlockSpec((POOL_HALO, 512), lambda i: (jnp.minimum((i + 1) * per, last), cb))]


def _pool_fwd(z, lin, scale, s):
    tp = 256
    nt = s // tp

    def body(up, uc, un, lin_ref, sc_ref, pooled_o, pc_o):
        i = pl.program_id(0)
        ext = jnp.concatenate([up[...], uc[...], un[...]], axis=0)
        for g, w in enumerate(POOL_WINDOWS):
            sl = slice(g * LANES, (g + 1) * LANES)
            band = _band(i * tp, i * tp - POOL_HALO, w, tp, tp + 2 * POOL_HALO, s, False)
            sm = jnp.dot(band, ext[:, sl], preferred_element_type=f32, precision=lax.Precision.HIGHEST)
            pooled = (sm / _pool_cnt(i * tp, w, tp, s) - uc[:, sl]).astype(bf16)
            pooled_o[:, sl] = pooled
            mixed = _dot(pooled, lin_ref[g].astype(bf16), "nn")
            pc_o[:, sl] = (mixed * sc_ref[:, sl]).astype(bf16)

    row = pl.BlockSpec((tp, 512), lambda i: (i, 0))
    return pl.pallas_call(
        body, name="pool_fwd", grid=(nt,),
        in_specs=_halo_specs(tp, s, C_UC // 512)
        + [pl.BlockSpec((4, LANES, LANES), lambda i: (0, 0, 0)), pl.BlockSpec((1, 512), lambda i: (0, 0))],
        out_specs=[row, row], out_shape=[SDS((s, 512), bf16), SDS((s, 512), bf16)],
        compiler_params=_cparams())(z, z, z, lin, scale)


def _pool_bwd1(dpc, pooled, lin, scale, s):
    tp = 256

    def body(dpc_ref, pooled_ref, lin_ref, sc_ref, dpn_o, dlin_o, dsc_o):
        i = pl.program_id(0)
        dsc = []
        for g, w in enumerate(POOL_WINDOWS):
            sl = slice(g * LANES, (g + 1) * LANES)
            pooled = pooled_ref[:, sl]
            linb = lin_ref[g].astype(bf16)
            mixed = _dot(pooled, linb, "nn")
            dpc_g = dpc_ref[:, sl]
            dsc.append(jnp.sum(dpc_g * mixed, axis=0, keepdims=True))
            dmixed = (dpc_g * sc_ref[:, sl]).astype(bf16)
            dpn_o[:, sl] = _dot(dmixed, linb, "nt") / _pool_cnt(i * tp, w, tp, s)
            dl = _dot(pooled, dmixed, "tn")

            @pl.when(i == 0)
            def _(g=g, dl=dl):
                dlin_o[g] = dl

            @pl.when(i > 0)
            def _(g=g, dl=dl):
                dlin_o[g] += dl

        dsc = jnp.concatenate(dsc, axis=1)

        @pl.when(i == 0)
        def _():
            dsc_o[...] = dsc

        @pl.when(i > 0)
        def _():
            dsc_o[...] += dsc

    row = pl.BlockSpec((tp, 512), lambda i: (i, 0))
    linspec = pl.BlockSpec((4, LANES, LANES), lambda i: (0, 0, 0))
    one = pl.BlockSpec((1, 512), lambda i: (0, 0))
    return pl.pallas_call(
        body, name="pool_bwd1", grid=(s // tp,), in_specs=[row, row, linspec, one], out_specs=[row, linspec, one],
        out_shape=[SDS((s, 512), f32), SDS((4, LANES, LANES), f32), SDS((1, 512), f32)],
        compiler_params=_cparams())(dpc, pooled, lin, scale)


def _pool_bwd2(dpn, dz, s):
    tp = 256
    nt = s // tp

    def body(dp, dc, dn, dz_in, dz_o):
        i = pl.program_id(0)
        ext = jnp.concatenate([dp[...], dc[...], dn[...]], axis=0)
        for g, w in enumerate(POOL_WINDOWS):
            sl = slice(g * LANES, (g + 1) * LANES)
            band = _band(i * tp, i * tp - POOL_HALO, w, tp, tp + 2 * POOL_HALO, s, True)
            sm = jnp.dot(band, ext[:, sl], preferred_element_type=f32, precision=lax.Precision.HIGHEST)
            dz_o[:, sl] = (sm - dc[:, sl] * _pool_cnt(i * tp, w, tp, s)).astype(bf16)

    return pl.pallas_call(
        body, name="pool_bwd2", grid=(nt,),
        in_specs=_halo_specs(tp, s, 0) + [pl.BlockSpec(memory_space=pl.ANY)],
        out_specs=pl.BlockSpec((tp, 512), lambda i: (i, C_UC // 512)), out_shape=SDS((s, IN_WIDTH), bf16),
        input_output_aliases={3: 0}, compiler_params=_cparams())(dpn, dpn, dpn, dz)


GW = 512


MIX_TM = 256
BR_WIDTHS = (512, 256, 512)
N_GATE_BLOCKS = 3 * D_MODEL // GW


def _gate_specs(ts):
    return [pl.BlockSpec((ts, GW), lambda i, q=q: (i, C_GZ // GW + q)) for q in range(N_GATE_BLOCKS)]


def _rms(xt, gt):
    return xt * lax.rsqrt(jnp.mean(xt * xt, axis=1, keepdims=True) + EPS) * gt


def _mix_out_fwd(acts, z, b_gate, x, norm_gain, w, s):
    ts = MIX_TM

    def body(oa, ob, pc, *rest):
        gz = rest[:N_GATE_BLOCKS]
        bias, x_ref, gain, wa, wb, wc, wo, y3_o, m_o, x1_o, h2_o, y_s = rest[N_GATE_BLOCKS:]
        for b, (act, wt) in enumerate(((oa, wa), (ob, wb), (pc, wc))):
            a = act[...]
            for j in range(N_CHIPS):
                y_s[b, :, j * 256:(j + 1) * 256] = _dot(a, wt[j], "nn")
        y3_o[...] = y_s[...].astype(bf16)
        for h in range(2):
            hs = slice(h * GW, (h + 1) * GW)
            tot = jnp.zeros((ts, GW), f32)
            for b in range(3):
                g = jax.nn.sigmoid(gz[2 * b + h][...] + bias[:, b * D_MODEL + h * GW:b * D_MODEL + (h + 1) * GW])
                tot += g * y_s[b, :, hs]
            m_o[:, hs] = tot.astype(bf16)
        acc = x_ref[...]
        for j in range(N_CHIPS):
            acc += _dot(m_o[:, j * 256:(j + 1) * 256], wo[j], "nn")
        x1_o[...] = acc
        h2_o[...] = _rms(acc, gain[...]).astype(bf16)

    row = lambda width: pl.BlockSpec((ts, width), lambda i: (i, 0))
    res3 = lambda a: pl.BlockSpec(a.shape, lambda i: (0, 0, 0))
    wts = [w["w_branch_a"], w["w_branch_b"], w["w_branch_c"], w["w_out"]]
    return pl.pallas_call(
        body, name="mix_out_fwd", grid=(s // ts,),
        in_specs=[row(BR_WIDTHS[0]), row(BR_WIDTHS[1]), row(BR_WIDTHS[2])] + _gate_specs(ts)
        + [pl.BlockSpec((1, 3 * D_MODEL), lambda i: (0, 0)), row(D_MODEL), pl.BlockSpec((1, D_MODEL), lambda i: (0, 0))]
        + [res3(a) for a in wts],
        out_specs=[pl.BlockSpec((3, ts, D_MODEL), lambda i: (0, i, 0)), row(D_MODEL), row(D_MODEL), row(D_MODEL)],
        out_shape=[SDS((3, s, D_MODEL), bf16), SDS((s, D_MODEL), bf16), SDS((s, D_MODEL), f32), SDS((s, D_MODEL), bf16)],
        scratch_shapes=[pltpu.VMEM((3, ts, D_MODEL), f32)],
        compiler_params=_cparams())(*acts, *([z] * N_GATE_BLOCKS), b_gate, x, norm_gain, *wts)


def _mix_out_bwd(dx1, y3, z, b_gate, w, s):
    ts = MIX_TM

    def body(dx_ref, y3_ref, *rest):
        gz = rest[:N_GATE_BLOCKS]
        bias, wa, wb, wc, wo, dy_o, dz_o, db_o, doa_o, dob_o, dpc_o, dm_s = rest[N_GATE_BLOCKS:]
        i = pl.program_id(0)
        dx = dx_ref[...].astype(bf16)
        for j in range(N_CHIPS):
            dm_s[:, j * 256:(j + 1) * 256] = _dot(dx, wo[j], "nt")
        dz_o[:, 0:C_GZ] = jnp.zeros((ts, C_GZ), bf16)
        dbs = []
        for b in range(3):
            for h in range(2):
                hs = slice(h * GW, (h + 1) * GW)
                g = jax.nn.sigmoid(gz[2 * b + h][...] + bias[:, b * D_MODEL + h * GW:b * D_MODEL + (h + 1) * GW])
                dm = dm_s[:, hs]
                dy_o[b, :, hs] = (dm * g).astype(bf16)
                dgz = dm * y3_ref[b, :, hs].astype(f32) * g * (1.0 - g)
                c0 = C_GZ + b * D_MODEL + h * GW
                dz_o[:, c0:c0 + GW] = dgz.astype(bf16)
                dbs.append(jnp.sum(dgz, axis=0, keepdims=True))
        db = jnp.concatenate(dbs, axis=1)

        @pl.when(i == 0)
        def _():
            db_o[...] = db

        @pl.when(i > 0)
        def _():
            db_o[...] += db

        for b, (wt, out) in enumerate(((wa, doa_o), (wb, dob_o), (wc, dpc_o))):
            acc = _dot(dy_o[b, :, 0:256], wt[0], "nt")
            for j in range(1, N_CHIPS):
                acc += _dot(dy_o[b, :, j * 256:(j + 1) * 256], wt[j], "nt")
            out[...] = acc

    row = lambda width: pl.BlockSpec((ts, width), lambda i: (i, 0))
    res3 = lambda a: pl.BlockSpec(a.shape, lambda i: (0, 0, 0))
    blk3 = pl.BlockSpec((3, ts, D_MODEL), lambda i: (0, i, 0))
    one = pl.BlockSpec((1, 3 * D_MODEL), lambda i: (0, 0))
    wts = [w["w_branch_a"], w["w_branch_b"], w["w_branch_c"], w["w_out"]]
    return pl.pallas_call(
        body, name="mix_out_bwd", grid=(s // ts,),
        in_specs=[row(D_MODEL), blk3] + _gate_specs(ts) + [one] + [res3(a) for a in wts],
        out_specs=[blk3, row(IN_WIDTH), one, row(BR_WIDTHS[0]), row(BR_WIDTHS[1]), row(BR_WIDTHS[2])],
        out_shape=[SDS((3, s, D_MODEL), bf16), SDS((s, IN_WIDTH), bf16), SDS((1, 3 * D_MODEL), f32),
                   SDS((s, BR_WIDTHS[0]), f32), SDS((s, BR_WIDTHS[1]), f32), SDS((s, BR_WIDTHS[2]), f32)],
        scratch_shapes=[pltpu.VMEM((ts, D_MODEL), f32)],
        compiler_params=_cparams())(dx1, y3, *([z] * N_GATE_BLOCKS), b_gate, *wts)


def _loss_grad(y, tgt, s):
    ts = 256

    def fn(yt, tt):
        e = yt - tt
        return e * (1.0 / D_MODEL), jnp.sum(e * e, axis=0, keepdims=True) * (0.5 / D_MODEL)

    row = pl.BlockSpec((ts, D_MODEL), lambda i: (i, 0))
    one = pl.BlockSpec((1, D_MODEL), lambda i: (0, 0))
    return _ew("loss_grad", fn, (s // ts,), [y, tgt], [row, row], [SDS((s, D_MODEL), f32), SDS((1, D_MODEL), f32)],
               [row, one], n_acc=1)


TM = 512


def _layer_fwd(x, w, sm, tabs, s, xchg=None, hb=None, next_gain=None):
    nm = s // TM
    if hb is None:
        hb = _norm_fwd(x, sm["norm_mix"], s)
    z = _mm("mm_z", [hb, w["w_in"]],
            [pl.BlockSpec((TM, D_MODEL), lambda j, i, k: (i, 0)), pl.BlockSpec((None, D_MODEL, 1664), lambda j, i, k: (j, 0, 0))],
            SDS((s, IN_WIDTH), f32), pl.BlockSpec((TM, 1664), lambda j, i, k: (i, j)), (4, nm, 1), 1, "nn", None)
    qa, kd, vd, qb, kb, vb = _prep_fwd(z, tabs, sm["gains"], s)
    (oa, lse_a), landed = _flash_a_fwd(qa, kd, vd, s, xchg)
    if xchg is not None:
        w = dict(w, **dict(zip(REST + ("next_w_in",), landed)))
    ob_parts = [_attn_b_fwd(qb[gi], kb[gi], vb[gi], gi, s) for gi in range(3)]
    ob, lse_b = _merge_b([p[0] for p in ob_parts], [p[1] for p in ob_parts], s)
    pooled, pc = _pool_fwd(z, sm["pool_lin"], sm["pool_scale"], s)

    y3, merged, x1, h2 = _mix_out_fwd([oa, ob, pc], z, sm["b_gate"], x, sm["norm_ffn"], w, s)
    g4, u4, a4 = _ffn_up(h2, w["w_ffn_gate"], w["w_ffn_up"], s)
    x2 = _mm_shard_sum("mm_down", [(a4, w["w_ffn_down"])], "nn", D_MODEL, s, TM, add=x1, norm_gain=next_gain)
    x2, hb_next = x2 if next_gain is not None else (x2, None)
    saved = dict(x=x, hb=hb, z=z, qa=qa, kd=kd, vd=vd, qb=qb, kb=kb, vb=vb, oa=oa, lse_a=lse_a, ob=ob, lse_b=lse_b,
                 pooled=pooled, pc=pc, y3=y3, merged=merged, x1=x1, h2=h2, g4=g4, u4=u4, a4=a4)
    return x2, saved, w, hb_next


def _ffn_up(h2, wg, wu, s):
    nm = s // TM

    def body(h_ref, wg_ref, wu_ref, g_o, u_o, a_o):
        h = h_ref[...]
        g = _dot(h, wg_ref[...], "nn")
        u = _dot(h, wu_ref[...], "nn")
        g_o[...] = g.astype(bf16)
        u_o[...] = u.astype(bf16)
        a_o[...] = (g * jax.nn.sigmoid(g) * u).astype(bf16)

    wsp = pl.BlockSpec((None, D_MODEL, FF_SHARD), lambda j, i: (j, 0, 0))
    osp = pl.BlockSpec((None, TM, FF_SHARD), lambda j, i: (j, i, 0))
    return pl.pallas_call(
        body, name="ffn_up", grid=(4, nm), in_specs=[pl.BlockSpec((TM, D_MODEL), lambda j, i: (i, 0)), wsp, wsp],
        out_specs=[osp, osp, osp],
        out_shape=[SDS((4, s, FF_SHARD), bf16), SDS((4, s, FF_SHARD), bf16), SDS((4, s, FF_SHARD), bf16)],
        compiler_params=_cparams())(h2, wg, wu)


def _ffn_bwd_act(dx2, wd, g4, u4, s):
    def body(dx_ref, wd_ref, g_ref, u_ref, dg_o, du_o):
        dx = dx_ref[...].astype(bf16)
        for j in range(N_CHIPS):
            da = _dot(dx, wd_ref[j], "nt")
            g, u = g_ref[j].astype(f32), u_ref[j].astype(f32)
            sg = jax.nn.sigmoid(g)
            dg_o[j] = (da * u * sg * (1.0 + g * (1.0 - sg))).astype(bf16)
            du_o[j] = (da * g * sg).astype(bf16)

    osp = pl.BlockSpec((N_CHIPS, TM, FF_SHARD), lambda i: (0, i, 0))
    return pl.pallas_call(
        body, name="ffn_bwd_act", grid=(s // TM,),
        in_specs=[pl.BlockSpec((TM, D_MODEL), lambda i: (i, 0)), pl.BlockSpec((N_CHIPS, FF_SHARD, D_MODEL), lambda i: (0, 0, 0)),
                  osp, osp],
        out_specs=[osp, osp], out_shape=[SDS((4, s, FF_SHARD), bf16), SDS((4, s, FF_SHARD), bf16)],
        compiler_params=_cparams())(dx2, wd, g4, u4)


def _shard_acc(a_refs, w_refs, dims, acc=None):
    for a_ref, w_ref in zip(a_refs, w_refs):
        for j in range(N_CHIPS):
            if len(a_ref.shape) == 3:
                a = a_ref[j]
            else:
                c = a_ref.shape[1] // N_CHIPS
                a = a_ref[:, j * c:(j + 1) * c]
            part = _dot(a.astype(bf16), w_ref[j], dims)
            acc = part if acc is None else acc + part
    return acc


def _shard_specs(pairs, tm):
    a_specs = [pl.BlockSpec((N_CHIPS, tm, a.shape[2]), lambda i: (0, i, 0)) if a.ndim == 3
               else pl.BlockSpec((tm, a.shape[1]), lambda i: (i, 0)) for a, _ in pairs]
    return a_specs + [pl.BlockSpec(wt.shape, lambda i: (0, 0, 0)) for _, wt in pairs]


def _mm_shard_sum_norm_bwd(name, pairs, dims, s, tm, x, gain, dres):
    n = len(pairs)

    def body(*refs):
        x_ref, g_ref, dr_ref, dx_o, dg_o = refs[2 * n:]
        i = pl.program_id(0)
        dh = _shard_acc(refs[:n], refs[n:2 * n], dims)
        xt = x_ref[...]
        r = lax.rsqrt(jnp.mean(xt * xt, axis=1, keepdims=True) + EPS)
        nrm = xt * r
        dn = dh * g_ref[...]
        dx_o[...] = dr_ref[...] + r * (dn - nrm * jnp.mean(dn * nrm, axis=1, keepdims=True))
        dg = jnp.sum(dh * nrm, axis=0, keepdims=True)

        @pl.when(i == 0)
        def _():
            dg_o[...] = dg

        @pl.when(i > 0)
        def _():
            dg_o[...] += dg

    row = pl.BlockSpec((tm, D_MODEL), lambda i: (i, 0))
    one = pl.BlockSpec((1, D_MODEL), lambda i: (0, 0))
    return pl.pallas_call(
        body, name=name, grid=(s // tm,), in_specs=_shard_specs(pairs, tm) + [row, one, row], out_specs=[row, one],
        out_shape=[SDS((s, D_MODEL), f32), SDS((1, D_MODEL), f32)], compiler_params=_cparams())(
            *[a for a, _ in pairs], *[wt for _, wt in pairs], x, gain, dres)


def _mm_shard_sum(name, pairs, dims, out_width, s, tm, add=None, norm_gain=None):
    n = len(pairs)

    def body(*refs):
        o_ref = refs[-2] if norm_gain is not None else refs[-1]
        acc = _shard_acc(refs[:n], refs[n:2 * n], dims, refs[2 * n][...] if add is not None else None)
        o_ref[...] = acc
        if norm_gain is not None:
            refs[-1][...] = _rms(acc, refs[2 * n + (add is not None)][...]).astype(bf16)

    a_specs = [pl.BlockSpec((N_CHIPS, tm, a.shape[2]), lambda i: (0, i, 0)) if a.ndim == 3
               else pl.BlockSpec((tm, a.shape[1]), lambda i: (i, 0)) for a, _ in pairs]
    w_specs = [pl.BlockSpec(wt.shape, lambda i: (0, 0, 0)) for _, wt in pairs]
    row = pl.BlockSpec((tm, out_width), lambda i: (i, 0))
    extra, extra_specs = ([add], [row]) if add is not None else ([], [])
    if norm_gain is not None:
        extra, extra_specs = extra + [norm_gain], extra_specs + [pl.BlockSpec((1, out_width), lambda i: (0, 0))]
    normed = norm_gain is not None
    return pl.pallas_call(
        body, name=name + ("_norm" if normed else ""), grid=(s // tm,), in_specs=a_specs + w_specs + extra_specs,
        out_specs=[row, row] if normed else row,
        out_shape=[SDS((s, out_width), f32), SDS((s, out_width), bf16)] if normed else SDS((s, out_width), f32),
        compiler_params=_cparams())(*[a for a, _ in pairs], *[wt for _, wt in pairs], *extra)


GRAD_WIRE_DTYPE = bf16


WG_TK = 2048


def _wgrad(name, a, a_spec, b, b_spec, out_shape, out_block, s):
    nk = s // min(WG_TK, s)
    return _mm(name, [a, b], [a_spec, b_spec], SDS(out_shape, GRAD_WIRE_DTYPE),
               pl.BlockSpec((None,) + out_block, lambda j, k: (j, 0, 0)), (4, nk), nk, "tn", out_block)


def _layer_bwd(dx2, sv, w, sm, tabs, s, pending_mixer=None):
    tk = min(WG_TK, s)
    tok = lambda width: pl.BlockSpec((tk, width), lambda j, k: (k, 0))
    g_wd = _wgrad("wg_down", sv["a4"], pl.BlockSpec((None, tk, FF_SHARD), lambda j, k: (j, k, 0)), dx2, tok(D_MODEL),
                  (4, FF_SHARD, D_MODEL), (FF_SHARD, D_MODEL), s)
    dg4, du4 = _ffn_bwd_act(dx2, w["w_ffn_down"], sv["g4"], sv["u4"], s)
    sh704 = pl.BlockSpec((None, tk, FF_SHARD), lambda j, k: (j, k, 0))
    g_wg = _wgrad("wg_gate", sv["h2"], tok(D_MODEL), dg4, sh704, (4, D_MODEL, FF_SHARD), (D_MODEL, FF_SHARD), s)
    g_wu = _wgrad("wg_up", sv["h2"], tok(D_MODEL), du4, sh704, (4, D_MODEL, FF_SHARD), (D_MODEL, FF_SHARD), s)
    dx1, g_norm_ffn = _mm_shard_sum_norm_bwd("ffn_bwd_dh", [(dg4, w["w_ffn_gate"]), (du4, w["w_ffn_up"])], "nt", s, TM,
                                             sv["x1"], sm["norm_ffn"], dx2)
    colblk = lambda width: pl.BlockSpec((tk, width), lambda j, k: (k, j))
    g_wo = _wgrad("wg_out", sv["merged"], colblk(256), dx1, tok(D_MODEL), (4, 256, D_MODEL), (256, D_MODEL), s)
    dy3, dz, g_bgate, doa, dob, dpc = _mix_out_bwd(dx1, sv["y3"], sv["z"], sm["b_gate"], w, s)
    br_grads = []
    for b, (nm_, act, kdim) in enumerate((("a", sv["oa"], 512), ("b", sv["ob"], 256), ("c", sv["pc"], 512))):
        br_grads.append(_wgrad("wg_br" + nm_, act, tok(kdim), dy3,
                               pl.BlockSpec((None, tk, 256), lambda j, k, b=b: (b, k, j)), (4, kdim, 256), (kdim, 256), s))
    dpn, g_lin, g_scale = _pool_bwd1(dpc, sv["pooled"], sm["pool_lin"], sm["pool_scale"], s)
    dz = _pool_bwd2(dpn, dz, s)
    delta = _delta_b(dob, sv["ob"], s)
    dqb, dkb, dvb = [], [], []
    for gi in range(3):
        qg, kg, vg = sv["qb"][gi], sv["kb"][gi], sv["vb"][gi]
        dqb.append(_attn_b_dq(qg, kg, vg, dob, sv["lse_b"], delta, gi, s))
        dk, dv = _attn_b_dkv(qg, kg, vg, dob, sv["lse_b"], delta, gi, s)
        dkb.append(dk)
        dvb.append(dv)
    if pending_mixer is not None:
        xchg = (list(pending_mixer) + [g_wg, g_wu, g_wd], lambda ref, peer: ref.at[peer])
    else:
        xchg = None
    (dqa, dkd, dvd), landed = _flash_a_bwd(sv["qa"], sv["kd"], sv["vd"], sv["oa"], sv["lse_a"], doa, s, xchg)
    dz, g_gains = _prep_bwd(sv["z"], tabs, sm["gains"], dqa, dkd, dvd, dqb, dkb, dvb, dz, s)
    g_win = _wgrad("wg_in", sv["hb"], tok(D_MODEL), dz, colblk(1664), (4, D_MODEL, 1664), (D_MODEL, 1664), s)
    dx0, g_norm_mix = _mm_shard_sum_norm_bwd("mm_dh", [(dz, w["w_in"])], "nt", s, 256, sv["x"], sm["norm_mix"], dx1)
    big = dict(w_in=g_win, w_branch_a=br_grads[0], w_branch_b=br_grads[1], w_branch_c=br_grads[2], w_out=g_wo,
               w_ffn_gate=g_wg, w_ffn_up=g_wu, w_ffn_down=g_wd)
    gg = g_gains[0:4, :HEAD_DIM] + g_gains[0:4, HEAD_DIM:]
    small = jnp.concatenate([g_norm_mix.reshape(-1), g_bgate.reshape(-1), gg.reshape(-1), g_lin.reshape(-1),
                             g_scale.reshape(-1), g_norm_ffn.reshape(-1)]).reshape(SMALL_ROWS, LANES)
    return dx0, big, small, landed


def _mesh_pos():
    return lax.axis_index("x"), lax.axis_index("y"), lax.axis_index("c")


def _chip_copies(src, dst, pick_src, send_sems, recv_sems, loc_sems, receiving=True):
    n = len(src)
    x, y, c = _mesh_pos()
    me = 2 * x + y
    local = [pltpu.make_async_copy(pick_src(src[t], me), dst[t].at[me], loc_sems.at[t]) for t in range(n)]
    push, recv = [], []
    for j, (px, py) in enumerate([(1 - x, y), (x, 1 - y), (1 - x, 1 - y)]):
        peer = 2 * px + py
        for t in range(n):
            sems = dict(send_sem=send_sems.at[j * n + t], recv_sem=recv_sems.at[j * n + t],
                        device_id=(px, py, c), device_id_type=MESH)
            push.append(pltpu.make_async_remote_copy(src_ref=pick_src(src[t], peer), dst_ref=dst[t].at[me], **sems))
            if receiving:
                recv.append(pltpu.make_async_remote_copy(src_ref=pick_src(src[t], peer), dst_ref=dst[t].at[peer], **sems))
    return local, push, recv


def _chip_start(copies):
    local, push, _ = copies
    for cp in push:
        cp.start()
    for cp in local:
        cp.start(priority=1)


def _chip_wait(copies):
    local, push, recv = copies
    for cp in push:
        cp.wait_send()
    for cp in recv:
        cp.wait_recv()
    for cp in local:
        cp.wait()


def _chip_sems(n):
    return [pltpu.SemaphoreType.DMA((3 * n,)), pltpu.SemaphoreType.DMA((3 * n,)), pltpu.SemaphoreType.DMA((n,))]


def _chip_landing(srcs):
    return [SDS((N_CHIPS,) + tuple(a.shape[1:]), a.dtype) for a in srcs]


def _chip_exchange(name, srcs, pick_src):
    n = len(srcs)
    any_spec = pl.BlockSpec(memory_space=pl.ANY)

    def body(*refs):
        copies = _chip_copies(refs[:n], refs[n:2 * n], pick_src, *refs[2 * n:])
        _chip_start(copies)
        _chip_wait(copies)

    return pl.pallas_call(
        body, name=name, in_specs=[any_spec] * n, out_specs=[any_spec] * n, out_shape=_chip_landing(srcs),
        scratch_shapes=_chip_sems(n), compiler_params=pltpu.CompilerParams())(*srcs)


def _sibling_exchange(name, srcs):
    n = len(srcs)
    any_spec = pl.BlockSpec(memory_space=pl.ANY)

    def body(*refs):
        src, dst = refs[:n], refs[n:2 * n]
        send_sems, recv_sems = refs[2 * n:]
        x, y, c = _mesh_pos()
        cps = [pltpu.make_async_remote_copy(src_ref=src[t], dst_ref=dst[t], send_sem=send_sems.at[t], recv_sem=recv_sems.at[t],
                                            device_id=(x, y, 1 - c), device_id_type=MESH) for t in range(n)]
        for cp in cps:
            cp.start()
        for cp in cps:
            cp.wait()

    return pl.pallas_call(
        body, name=name, in_specs=[any_spec] * n, out_specs=[any_spec] * n,
        out_shape=[SDS(a.shape, a.dtype) for a in srcs],
        scratch_shapes=[pltpu.SemaphoreType.DMA((n,)), pltpu.SemaphoreType.DMA((n,))],
        compiler_params=pltpu.CompilerParams())(*srcs)


def _all_exchange(name, src):
    any_spec = pl.BlockSpec(memory_space=pl.ANY)

    def body(src_ref, dst_ref, send_sems, recv_sems, loc_sem):
        x, y, c = _mesh_pos()
        me = 4 * x + 2 * y + c
        peers = [(x ^ ((k >> 2) & 1), y ^ ((k >> 1) & 1), c ^ (k & 1)) for k in range(1, N_DEV)]
        local = pltpu.make_async_copy(src_ref, dst_ref.at[me], loc_sem)
        local.start()
        cps = [pltpu.make_async_remote_copy(src_ref=src_ref, dst_ref=dst_ref.at[me], send_sem=send_sems.at[k],
                                            recv_sem=recv_sems.at[k], device_id=p, device_id_type=MESH)
               for k, p in enumerate(peers)]
        for cp in cps:
            cp.start()
        for cp in cps:
            cp.wait_send()
        for k, (px, py, pc) in enumerate(peers):
            pltpu.make_async_remote_copy(src_ref=src_ref, dst_ref=dst_ref.at[4 * px + 2 * py + pc], send_sem=send_sems.at[k],
                                         recv_sem=recv_sems.at[k], device_id=(px, py, pc), device_id_type=MESH).wait_recv()
        local.wait()

    return pl.pallas_call(
        body, name=name, in_specs=[any_spec], out_specs=any_spec, out_shape=SDS((N_DEV,) + src.shape, src.dtype),
        scratch_shapes=[pltpu.SemaphoreType.DMA((N_DEV - 1,)), pltpu.SemaphoreType.DMA((N_DEV - 1,)), pltpu.SemaphoreType.DMA],
        compiler_params=pltpu.CompilerParams())(src)


def _cast_bf16(name, a):
    l, r, c = a.shape
    tr = _row_tile(r, c)
    spec = pl.BlockSpec((None, tr, c), lambda i, j: (i, j, 0))
    return _ew(name, lambda t: t, (l, r // tr), [a], [spec], [SDS(a.shape, bf16)], [spec])[0]


def _sum4(name, land):
    _, r, c = land.shape
    tr = _row_tile(r, c)
    up = lambda t: t.astype(f32)
    return _ew(name, lambda a, b, cc, d: ((up(a) + up(b)) + up(cc)) + up(d), (r // tr,), [land] * 4,
               [pl.BlockSpec((None, tr, c), lambda i, k=k: (k, i, 0)) for k in range(4)],
               [SDS((r, c), f32)], [pl.BlockSpec((tr, c), lambda i: (i, 0))])[0]


def _adam_math(g, w, m, v):
    m2 = ADAM_B1 * m + (1.0 - ADAM_B1) * g
    v2 = ADAM_B2 * v + (1.0 - ADAM_B2) * (g * g)
    m_hat = m2 / (1.0 - ADAM_B1 ** ADAM_STEP)
    v_hat = v2 / (1.0 - ADAM_B2 ** ADAM_STEP)
    delta = -ADAM_LR * (m_hat / (jnp.sqrt(v_hat) + ADAM_EPS) + ADAM_WD * w)
    return delta, m2, v2


def _adamw_big(name, p_own, p_sib, w, m, v):
    l, r, c = w.shape
    tr = _row_tile(r, c)

    def fn(a, b, wt, mt, vt):
        g = a + b
        return (g,) + _adam_math(g, wt, mt, vt)

    spec = pl.BlockSpec((None, tr, c), lambda i, j: (i, j, 0))
    return _ew(name, fn, (l, r // tr), [p_own, p_sib, w, m, v], [spec] * 5, [SDS(w.shape, f32)] * 4, [spec] * 4)


def _adamw_small(land, w, m, v):
    r = w.shape[0]
    tr = _row_tile(r, LANES)

    def fn(*t):
        g = t[0]
        for k in range(1, N_DEV):
            g = g + t[k]
        return (g,) + _adam_math(g, *t[N_DEV:])

    row = pl.BlockSpec((tr, LANES), lambda i: (i, 0))
    return _ew("adamw_small", fn, (r // tr,), [land] * N_DEV + [w, m, v],
               [pl.BlockSpec((None, tr, LANES), lambda i, k=k: (k, i, 0)) for k in range(N_DEV)] + [row] * 3,
               [SDS(w.shape, f32)] * 4, [row] * 4)


def _pack_small(d):
    return jnp.concatenate([d[k].reshape(DEPTH, -1) for k in SMALL], axis=1).reshape(DEPTH * SMALL_ROWS, LANES)


def _unpack_small(a, shapes):
    a = a.reshape(DEPTH, SMALL_ROWS * LANES)
    out, off = {}, 0
    for k, n in zip(SMALL, SMALL_SIZES):
        out[k] = a[:, off:off + n].reshape(shapes[k])
        off += n
    return out


def kernel(x, norm_mix, w_in, b_gate, qn_a, kn_a, qn_b, kn_b, pool_lin, pool_scale, w_branch_a, w_branch_b, w_branch_c, w_out, norm_ffn, w_ffn_gate, w_ffn_up, w_ffn_down, loss_target, m_norm_mix, m_w_in, m_b_gate, m_qn_a, m_kn_a, m_qn_b, m_kn_b, m_pool_lin, m_pool_scale, m_w_branch_a, m_w_branch_b, m_w_branch_c, m_w_out, m_norm_ffn, m_w_ffn_gate, m_w_ffn_up, m_w_ffn_down, v_norm_mix, v_w_in, v_b_gate, v_qn_a, v_kn_a, v_qn_b, v_kn_b, v_pool_lin, v_pool_scale, v_w_branch_a, v_w_branch_b, v_w_branch_c, v_w_out, v_norm_ffn, v_w_ffn_gate, v_w_ffn_up, v_w_ffn_down):
    args = dict(locals())
    s = x.shape[1]
    wts = {k: args[k] for k in WEIGHT_ORDER}
    mom = {k: args["m_" + k] for k in WEIGHT_ORDER}
    var = {k: args["v_" + k] for k in WEIGHT_ORDER}
    tabs = _rope_tables(s)

    shards16 = {k: _cast_bf16("cast_" + k, wts[k]) for k in BIG}
    whole = lambda ref, peer: ref.at[0]
    w_in_l = _chip_exchange("gather_w_in", [shards16["w_in"][0:1]], whole)[0]
    full = [None] * DEPTH

    def small_of(l):
        tile2 = lambda a: jnp.concatenate([a, a])
        gains = jnp.stack([tile2(qn_a[l]), tile2(kn_a[l]), tile2(qn_b[l]), tile2(kn_b[l])])
        return dict(norm_mix=norm_mix[l][None], norm_ffn=norm_ffn[l][None], b_gate=b_gate[l][None], gains=gains,
                    pool_lin=pool_lin[l], pool_scale=pool_scale[l][None])

    xs = x[0]
    saved = []
    hb = None
    for l in range(DEPTH):
        nxt = (l + 1) % DEPTH
        srcs = [shards16[k][l:l + 1] for k in REST] + [shards16["w_in"][nxt:nxt + 1]]
        next_gain = norm_mix[l + 1][None] if l + 1 < DEPTH else None
        xs, sv, full[l], hb = _layer_fwd(xs, dict(w_in=w_in_l), small_of(l), tabs, s, (srcs, whole), hb, next_gain)
        saved.append(sv)
        w_in_l = full[l]["next_w_in"]
    dy, loss_rows = _loss_grad(xs, loss_target[0], s)
    loss = lax.psum(jnp.sum(loss_rows), ("x", "y", "c"))

    part = {k: [None] * DEPTH for k in BIG}
    small_g = [None] * DEPTH
    to_chip = lambda ref, peer: ref.at[peer]
    assert GRAD_WIRE_DTYPE == bf16
    pending = [full[DEPTH - 1][k] for k in MIXER]
    for l in reversed(range(DEPTH)):
        dy, big, small_g[l], landed = _layer_bwd(dy, saved[l], full[l], small_of(l), tabs, s, pending)
        if l + 1 < DEPTH:
            for k, a in zip(MIXER, landed[:len(MIXER)]):
                part[k][l + 1] = _sum4("sum4_" + k, a)
        for k, a in zip(FFN, landed[len(MIXER):]):
            part[k][l] = _sum4("sum4_" + k, a)
        pending = [big[k] for k in MIXER]
    for k, a in zip(MIXER, _chip_exchange("grad_scatter", pending, to_chip)):
        part[k][0] = _sum4("sum4_" + k, a)

    p_own = [jnp.stack(part[k]) for k in BIG]
    p_sib = _sibling_exchange("grad_sibling", p_own)
    outs = {}
    for k, a, b in zip(BIG, p_own, p_sib):
        outs[k] = _adamw_big("adamw_" + k, a, b, wts[k], mom[k], var[k])

    land_s = _all_exchange("small_allgather", jnp.concatenate(small_g, axis=0))
    res_s = _adamw_small(land_s, _pack_small(wts), _pack_small(mom), _pack_small(var))
    shapes = {k: wts[k].shape for k in SMALL}
    small_out = [_unpack_small(r, shapes) for r in res_s]
    for k in SMALL:
        outs[k] = tuple(so[k] for so in small_out)

    flat = [loss, dy[None]]
    for idx in range(4):
        flat += [outs[k][idx] for k in WEIGHT_ORDER]
    return tuple(flat)
```
